```python
import math
import jax
import jax.numpy as jnp
from jax import lax
import numpy as np

D_MODEL = 1024
BATCH = 8
SEQ = 4096
DEPTH = 2
DEC_BATCH = 32
DEC_SEQ = 8
PAST_LEN = 16384
PAGE_SIZE = 128

N_HEADS = 16
HEAD_DIM = D_MODEL // N_HEADS
N_KV_HEADS = 4
GROUP = N_HEADS // N_KV_HEADS
ATTN_DIM = N_HEADS * HEAD_DIM
KV_DIM = N_KV_HEADS * HEAD_DIM
IDX_HEADS = 8
IDX_DIM = 64
IDX_TOPK = 256
MOBA_BLOCK = 256
MOBA_TOPK = 3
REL_BUCKETS = 32
REL_MAX_DIST = 128
PEER_HEADS = 8
PEER_NKEYS = 128
PEER_EXPERTS = PEER_NKEYS * PEER_NKEYS
PEER_QDIM = 256
PEER_HALF = PEER_QDIM // 2
PEER_TOPK = 16
PLE_DIM = 256
DN_ALPHA = (2 * DEPTH) ** 0.25
DN_BETA = (8 * DEPTH) ** -0.25
LN_EPS = 1e-5
Q_BLOCK = 64
PEER_BLOCK = 512
N_DSA_LAYERS = (DEPTH + 1) // 2
N_MOBA_LAYERS = DEPTH // 2
DSA_IN = ATTN_DIM + 2 * KV_DIM + IDX_HEADS * IDX_DIM + IDX_DIM + IDX_HEADS
MOBA_IN = ATTN_DIM + 2 * KV_DIM

kernel_name = 'hybrid_dsa_moba_peer_decoder_step'


def layer_norm(x, g, b):
    xf = x.astype(jnp.float32)
    mu = jnp.mean(xf, axis=-1, keepdims=True)
    xc = xf - mu
    var = jnp.mean(xc * xc, axis=-1, keepdims=True)
    y = xc * lax.rsqrt(var + LN_EPS) * g.astype(jnp.float32) + b.astype(jnp.float32)
    return y.astype(x.dtype)


def rel_bucket(dist):
    n = jnp.maximum(dist, 0)
    exact = REL_BUCKETS // 2
    nf = jnp.maximum(n, 1).astype(jnp.float32)
    large = exact + (jnp.log(nf / exact) / math.log(REL_MAX_DIST / exact) * (REL_BUCKETS - exact)).astype(jnp.int32)
    return jnp.where(n < exact, n, jnp.minimum(large, REL_BUCKETS - 1))


def to_blocks(a, nb, qb):
    return jnp.moveaxis(a.reshape(a.shape[0], nb, qb, *a.shape[2:]), 1, 0)


def from_blocks(a):
    a = jnp.moveaxis(a, 0, 1)
    return a.reshape(a.shape[0], a.shape[1] * a.shape[2], *a.shape[3:])


def gather_pages(pool, page_table):
    g = pool[page_table]
    return g.reshape(g.shape[0], g.shape[1] * g.shape[2], *g.shape[3:])


def make_fetch(k_new, v_new, pool_k=None, pool_v=None, page_table=None):
    B, n_new = k_new.shape[:2]
    bidx = jnp.arange(B)[:, None, None, None]
    gidx = jnp.arange(N_KV_HEADS)[None, None, :, None]
    if pool_k is None:
        def fetch(pos):
            pn = jnp.clip(pos, 0, n_new - 1)
            return k_new[bidx, pn, gidx], v_new[bidx, pn, gidx]
        return fetch
    page = pool_k.shape[1]
    past_len = page_table.shape[1] * page
    flat_k = pool_k.reshape(-1, N_KV_HEADS, HEAD_DIM)
    flat_v = pool_v.reshape(-1, N_KV_HEADS, HEAD_DIM)

    def fetch(pos):
        pp = jnp.clip(pos, 0, past_len - 1)
        phys = page_table[bidx, pp // page] * page + pp % page
        pn = jnp.clip(pos - past_len, 0, n_new - 1)
        in_past = (pos < past_len)[..., None]
        k = jnp.where(in_past, flat_k[phys, gidx], k_new[bidx, pn, gidx])
        v = jnp.where(in_past, flat_v[phys, gidx], v_new[bidx, pn, gidx])
        return k, v
    return fetch


def dsa_attention(q, qi, wi, ki_all, fetch, q_pos, rel_bias):
    B, Q = q.shape[:2]
    L = ki_all.shape[1]
    k_sel = min(IDX_TOPK, L // 4)
    qb = min(Q_BLOCK, Q)
    nb = Q // qb
    key_pos = jnp.arange(L)
    ki_f = ki_all.astype(jnp.float32)
    tg = rel_bias.astype(jnp.float32).reshape(REL_BUCKETS, N_KV_HEADS, GROUP)
    scale = HEAD_DIM ** -0.5

    def one_block(args):
        qc, qic, wic, pc = args
        s = jnp.einsum('bqhd,bsd->bqhs', qic.astype(jnp.float32), ki_f)
        score = jnp.einsum('bqhs,bqh->bqs', jax.nn.relu(s), wic.astype(jnp.float32))
        score = jnp.where(key_pos[None, None, :] <= pc[None, :, None], score, -jnp.inf)
        _, sel = lax.top_k(score, k_sel)
        kg, vg = fetch(sel[:, :, None, :])
        qh = qc.reshape(B, qb, N_KV_HEADS, GROUP, HEAD_DIM)
        logits = jnp.einsum('bqgjd,bqgnd->bqgjn', qh, kg).astype(jnp.float32) * scale
        dist = pc[None, :, None] - sel
        bias = jnp.transpose(tg[rel_bucket(dist)], (0, 1, 3, 4, 2))
        valid = (dist >= 0)[:, :, None, None, :]
        probs = jax.nn.softmax(jnp.where(valid, logits + bias, -jnp.inf), axis=-1)
        o = jnp.einsum('bqgjn,bqgnd->bqgjd', probs.astype(vg.dtype), vg)
        return o.reshape(B, qb, ATTN_DIM)

    out = lax.map(one_block, (to_blocks(q, nb, qb), to_blocks(qi, nb, qb), to_blocks(wi, nb, qb), q_pos.reshape(nb, qb)))
    return from_blocks(out)


def moba_attention(q, k_all, fetch, q_pos, rel_bias):
    B, Q = q.shape[:2]
    L = k_all.shape[1]
    n_blk = -(-L // MOBA_BLOCK)
    k_pad = jnp.pad(k_all.astype(jnp.float32), ((0, 0), (0, n_blk * MOBA_BLOCK - L), (0, 0), (0, 0)))
    k_mean = k_pad.reshape(B, n_blk, MOBA_BLOCK, N_KV_HEADS, HEAD_DIM).mean(axis=2)
    top = min(MOBA_TOPK, n_blk)
    qb = min(Q_BLOCK, Q)
    nb = Q // qb
    blk_ids = jnp.arange(n_blk)
    rows = jnp.arange(MOBA_BLOCK)
    tgm = jnp.transpose(rel_bias.astype(jnp.float32).reshape(REL_BUCKETS, N_KV_HEADS, GROUP), (1, 0, 2))
    gidx = jnp.arange(N_KV_HEADS)[None, None, :, None]
    scale = HEAD_DIM ** -0.5

    def one_block(args):
        qc, pc = args
        own = pc // MOBA_BLOCK
        qh = qc.reshape(B, qb, N_KV_HEADS, GROUP, HEAD_DIM)
        gate = jnp.einsum('bqgd,bngd->bqgn', qh.astype(jnp.float32).sum(axis=3), k_mean)
        gate = jnp.where((blk_ids[None, :] < own[:, None])[None, :, None, :], gate, -jnp.inf)
        _, sel = lax.top_k(gate, top)
        own_b = jnp.broadcast_to(own[None, :, None, None], (B, qb, N_KV_HEADS, 1))
        blocks = jnp.concatenate([sel, own_b], axis=-1)
        blk_ok = jnp.concatenate([sel < own_b, jnp.ones_like(own_b, dtype=bool)], axis=-1)
        kpos = (blocks[..., None] * MOBA_BLOCK + rows).reshape(B, qb, N_KV_HEADS, -1)
        kg, vg = fetch(kpos)
        logits = jnp.einsum('bqgjd,bqgnd->bqgjn', qh, kg).astype(jnp.float32) * scale
        dist = pc[None, :, None, None] - kpos
        bias = jnp.transpose(tgm[gidx, rel_bucket(dist)], (0, 1, 2, 4, 3))
        valid = (jnp.repeat(blk_ok, MOBA_BLOCK, axis=-1) & (dist >= 0))[:, :, :, None, :]
        probs = jax.nn.softmax(jnp.where(valid, logits + bias, -jnp.inf), axis=-1)
        o = jnp.einsum('bqgjn,bqgnd->bqgjd', probs.astype(vg.dtype), vg)
        return o.reshape(B, qb, ATTN_DIM)

    out = lax.map(one_block, (to_blocks(q, nb, qb), q_pos.reshape(nb, qb)))
    return from_blocks(out)


def peer_ffn(x, wq, keys, u, v):
    shape = x.shape
    t = x.reshape(-1, shape[-1])
    T = t.shape[0]
    blk = min(PEER_BLOCK, T)
    n = -(-T // blk)
    t = jnp.pad(t, ((0, n * blk - T), (0, 0))).reshape(n, blk, shape[-1])

    def one(xc):
        qry = (xc @ wq).reshape(blk, PEER_HEADS, 2, PEER_HALF)
        s = jnp.einsum('thcd,hcnd->thcn', qry, keys).astype(jnp.float32)
        sv, si = lax.top_k(s, PEER_TOPK)
        cand = (sv[:, :, 0, :, None] + sv[:, :, 1, None, :]).reshape(blk, PEER_HEADS, PEER_TOPK * PEER_TOPK)
        cidx = (si[:, :, 0, :, None] * PEER_NKEYS + si[:, :, 1, None, :]).reshape(blk, PEER_HEADS, PEER_TOPK * PEER_TOPK)
        gv, gi = lax.top_k(cand, PEER_TOPK)
        eidx = jnp.take_along_axis(cidx, gi, axis=-1)
        g = jax.nn.softmax(gv, axis=-1)
        act = jax.nn.gelu(jnp.einsum('thkd,td->thk', u[eidx], xc).astype(jnp.float32), approximate=False)
        return jnp.einsum('thk,thkd->td', (g * act).astype(v.dtype), v[eidx])

    out = lax.map(one, t).reshape(n * blk, shape[-1])[:T]
    return out.reshape(shape)


def run_trunk(x, p, past, prm):
    B, Q, _ = x.shape
    if past is None:
        past_len = 0
    else:
        cache_k, cache_v, cache_ki, page_table = past
        past_len = page_table.shape[1] * cache_k.shape[2]
    q_pos = past_len + jnp.arange(Q, dtype=jnp.int32)
    ks, vs, kis = [], [], []
    for l in range(DEPTH):
        if l % 2 == 0:
            proj = x @ prm['w_in_dsa'][l // 2]
        else:
            proj = x @ prm['w_in_moba'][l // 2]
        q = proj[..., :ATTN_DIM].reshape(B, Q, N_HEADS, HEAD_DIM)
        k = proj[..., ATTN_DIM:ATTN_DIM + KV_DIM].reshape(B, Q, N_KV_HEADS, HEAD_DIM)
        v = proj[..., ATTN_DIM + KV_DIM:ATTN_DIM + 2 * KV_DIM].reshape(B, Q, N_KV_HEADS, HEAD_DIM)
        if past is None:
            fetch = make_fetch(k, v)
        else:
            fetch = make_fetch(k, v, cache_k[l], cache_v[l], page_table)
        if l % 2 == 0:
            off = ATTN_DIM + 2 * KV_DIM
            qi = proj[..., off:off + IDX_HEADS * IDX_DIM].reshape(B, Q, IDX_HEADS, IDX_DIM)
            off = off + IDX_HEADS * IDX_DIM
            ki = proj[..., off:off + IDX_DIM]
            wi = proj[..., off + IDX_DIM:off + IDX_DIM + IDX_HEADS]
            if past is None:
                ki_all = ki
            else:
                ki_all = jnp.concatenate([gather_pages(cache_ki[l // 2], page_table), ki], axis=1)
            o = dsa_attention(q, qi, wi, ki_all, fetch, q_pos, prm['rel_bias'])
            kis.append(ki)
        else:
            if past is None:
                k_all = k
            else:
                k_all = jnp.concatenate([gather_pages(cache_k[l], page_table), k], axis=1)
            o = moba_attention(q, k_all, fetch, q_pos, prm['rel_bias'])
        ks.append(k)
        vs.append(v)
        x = layer_norm(DN_ALPHA * x + o @ prm['w_o'][l], prm['ln_g'][l, 0], prm['ln_b'][l, 0])
        f = peer_ffn(x, prm['peer_wq'][l], prm['peer_keys'][l], prm['peer_u'][l], prm['peer_v'][l])
        x = layer_norm(DN_ALPHA * x + f, prm['ln_g'][l, 1], prm['ln_b'][l, 1])
        gate = jax.nn.sigmoid(x @ prm['gate_w'][l] + prm['gate_b'][l])
        x = x + gate * (p[l] @ prm['ple_w'][l])
    return x, jnp.stack(ks), jnp.stack(vs), jnp.stack(kis)


def setup_inputs(seed: int = 0) -> dict:
    key = jax.random.key(seed)
    ks = jax.random.split(key, 24)
    n_pages = PAST_LEN // PAGE_SIZE
    n_used = DEC_BATCH * n_pages
    n_pool = n_used + max(1, n_used // 4)

    def nrm(k, shape, s):
        return jax.random.normal(k, shape, jnp.float32) * s

    page_table = jax.random.permutation(ks[5], n_pool)[:n_used].reshape(DEC_BATCH, n_pages).astype(jnp.int32)
    return {
        'x_prompt': nrm(ks[0], (BATCH, SEQ, D_MODEL), 1.0),
        'x_sample': nrm(ks[1], (DEC_BATCH, DEC_SEQ, D_MODEL), 1.0),
        'cache_k': nrm(ks[2], (DEPTH, n_pool, PAGE_SIZE, N_KV_HEADS, HEAD_DIM), 1.0),
        'cache_v': nrm(ks[3], (DEPTH, n_pool, PAGE_SIZE, N_KV_HEADS, HEAD_DIM), 1.0),
        'cache_ki': nrm(ks[4], (N_DSA_LAYERS, n_pool, PAGE_SIZE, IDX_DIM), 1.0),
        'page_table': page_table,
        'p_prompt': nrm(ks[6], (DEPTH, BATCH, SEQ, PLE_DIM), 1.0),
        'p_sample': nrm(ks[7], (DEPTH, DEC_BATCH, DEC_SEQ, PLE_DIM), 1.0),
        'w_in_dsa': nrm(ks[8], (N_DSA_LAYERS, D_MODEL, DSA_IN), D_MODEL ** -0.5),
        'w_in_moba': nrm(ks[9], (N_MOBA_LAYERS, D_MODEL, MOBA_IN), D_MODEL ** -0.5),
        'w_o': nrm(ks[10], (DEPTH, ATTN_DIM, D_MODEL), DN_BETA * ATTN_DIM ** -0.5),
        'rel_bias': nrm(ks[11], (REL_BUCKETS, N_HEADS), 0.5),
        'ln_g': 1.0 + nrm(ks[12], (DEPTH, 2, D_MODEL), 0.05),
        'ln_b': nrm(ks[13], (DEPTH, 2, D_MODEL), 0.02),
        'peer_wq': nrm(ks[14], (DEPTH, D_MODEL, PEER_HEADS * PEER_QDIM), D_MODEL ** -0.5),
        'peer_keys': nrm(ks[15], (DEPTH, PEER_HEADS, 2, PEER_NKEYS, PEER_HALF), PEER_HALF ** -0.5),
        'peer_u': nrm(ks[16], (DEPTH, PEER_EXPERTS, D_MODEL), D_MODEL ** -0.5),
        'peer_v': nrm(ks[17], (DEPTH, PEER_EXPERTS, D_MODEL), DN_BETA),
        'gate_w': nrm(ks[18], (DEPTH, D_MODEL, D_MODEL), D_MODEL ** -0.5),
        'gate_b': nrm(ks[19], (DEPTH, D_MODEL), 0.02),
        'ple_w': nrm(ks[20], (DEPTH, PLE_DIM, D_MODEL), PLE_DIM ** -0.5),
    }


def reference(x_prompt, x_sample, cache_k, cache_v, cache_ki, page_table, p_prompt, p_sample,
              w_in_dsa, w_in_moba, w_o, rel_bias, ln_g, ln_b, peer_wq, peer_keys, peer_u, peer_v,
              gate_w, gate_b, ple_w):
    prm = {
        'w_in_dsa': w_in_dsa, 'w_in_moba': w_in_moba, 'w_o': w_o, 'rel_bias': rel_bias,
        'ln_g': ln_g, 'ln_b': ln_b, 'peer_wq': peer_wq, 'peer_keys': peer_keys,
        'peer_u': peer_u, 'peer_v': peer_v, 'gate_w': gate_w, 'gate_b': gate_b, 'ple_w': ple_w,
    }
    y_prompt, k_prompt, v_prompt, ki_prompt = run_trunk(x_prompt, p_prompt, None, prm)
    y_sample, k_sample, v_sample, ki_sample = run_trunk(
        x_sample, p_sample, (cache_k, cache_v, cache_ki, page_table), prm)
    return (y_prompt, y_sample, k_prompt, v_prompt, ki_prompt, k_sample, v_sample, ki_sample)
```

```python
import functools
import math

import numpy as np
import jax
import jax.numpy as jnp
from jax import lax
from jax.experimental import pallas as pl
from jax.experimental.pallas import tpu as pltpu

F32 = jnp.float32
BF16 = jnp.bfloat16
I32 = jnp.int32

N_HEADS = 16
HEAD_DIM = 64
N_KV = 4
GROUP = 4
IDX_HEADS = 8
IDX_DIM = 64
IDX_TOPK = 256
MOBA_BLOCK = 256
MOBA_TOPK = 3
REL_BUCKETS = 32
REL_MAX_DIST = 128
PEER_HEADS = 8
PEER_NKEYS = 128
PEER_HALF = 128
PEER_TOPK = 16
PEER_PAIRS = PEER_HEADS * PEER_TOPK
LN_EPS = 1e-5
PAGE = 128

KEY_CHUNK = 256
STRIP_W = 768
STRIP_ORIGIN = 512
NEG = -1e30
INT_MIN = -2147483648
VMEM_LIMIT = 56 * 1024 * 1024


def _cparams(sem):
    return pltpu.CompilerParams(dimension_semantics=sem, vmem_limit_bytes=VMEM_LIMIT)


def _mm_kernel(a_ref, b_ref, o_ref):
    o_ref[...] = jnp.dot(a_ref[...].astype(BF16), b_ref[...], preferred_element_type=F32)


def _matmul(a, b_bf16, tm):
    m, k = a.shape
    n = b_bf16.shape[1]
    return pl.pallas_call(
        _mm_kernel,
        grid=(m // tm,),
        in_specs=[pl.BlockSpec((tm, k), lambda i: (i, 0)),
                  pl.BlockSpec((k, n), lambda i: (0, 0))],
        out_specs=pl.BlockSpec((tm, n), lambda i: (i, 0)),
        out_shape=jax.ShapeDtypeStruct((m, n), F32),
        compiler_params=_cparams(("parallel",)),
        name="in_proj",
    )(a, b_bf16)


def _layer_norm(y, g, b):
    mu = jnp.mean(y, axis=-1, keepdims=True)
    yc = y - mu
    var = jnp.mean(yc * yc, axis=-1, keepdims=True)
    return yc * lax.rsqrt(var + LN_EPS) * g + b


def _attn_out_kernel(alpha, o_ref, w_ref, x_ref, g_ref, b_ref, y_ref):
    y = alpha * x_ref[...] + jnp.dot(o_ref[...].astype(BF16), w_ref[...],
                                     preferred_element_type=F32)
    y_ref[...] = _layer_norm(y, g_ref[...], b_ref[...])


def _attn_out(o, w_bf16, x, g, b, alpha, tm):
    m, d = x.shape
    row = pl.BlockSpec((tm, d), lambda i: (i, 0))
    vec = pl.BlockSpec((1, d), lambda i: (0, 0))
    return pl.pallas_call(
        functools.partial(_attn_out_kernel, alpha),
        grid=(m // tm,),
        in_specs=[row, pl.BlockSpec((d, d), lambda i: (0, 0)), row, vec, vec],
        out_specs=row,
        out_shape=jax.ShapeDtypeStruct((m, d), F32),
        compiler_params=_cparams(("parallel",)),
        name="attn_out_ln",
    )(o, w_bf16, x, g, b)


def _layer_tail_kernel(alpha, x_ref, f_ref, g_ref, b_ref, gw_ref, gb_ref, p_ref, pw_ref, y_ref):
    x2 = _layer_norm(alpha * x_ref[...] + f_ref[...], g_ref[...], b_ref[...])
    z = jnp.dot(x2.astype(BF16), gw_ref[...], preferred_element_type=F32) + gb_ref[...]
    gate = 1.0 / (1.0 + jnp.exp(-z))
    e = jnp.dot(p_ref[...].astype(BF16), pw_ref[...], preferred_element_type=F32)
    y_ref[...] = x2 + gate * e


def _layer_tail(x, f, g, b, gw_bf16, gb, p, pw_bf16, alpha, tm):
    m, d = x.shape
    pd = p.shape[1]
    row = pl.BlockSpec((tm, d), lambda i: (i, 0))
    vec = pl.BlockSpec((1, d), lambda i: (0, 0))
    return pl.pallas_call(
        functools.partial(_layer_tail_kernel, alpha),
        grid=(m // tm,),
        in_specs=[row, row, vec, vec, pl.BlockSpec((d, d), lambda i: (0, 0)), vec,
                  pl.BlockSpec((tm, pd), lambda i: (i, 0)),
                  pl.BlockSpec((pd, d), lambda i: (0, 0))],
        out_specs=row,
        out_shape=jax.ShapeDtypeStruct((m, d), F32),
        compiler_params=_cparams(("parallel",)),
        name="layer_tail",
    )(x, f, g, b, gw_bf16, gb, p, pw_bf16)


def _relayout_kernel(n_pool_blocks, has_new, *refs):
    if has_new:
        _, pa_ref, pb_ref, new_ref, xt_ref, x_ref, s_ref = refs
    else:
        _, pa_ref, pb_ref, xt_ref, x_ref, s_ref = refs
        new_ref = None
    j = pl.program_id(1)

    def emit(blk):
        xt_ref[...] = blk.T.astype(BF16)
        x_ref[...] = blk.astype(BF16)
        s_ref[...] = jnp.sum(blk, axis=0, keepdims=True)

    if has_new:
        @pl.when(j < n_pool_blocks)
        def _():
            emit(jnp.concatenate([pa_ref[...], pb_ref[...]], axis=0))

        @pl.when(j >= n_pool_blocks)
        def _():
            emit(new_ref[...])
    else:
        emit(jnp.concatenate([pa_ref[...], pb_ref[...]], axis=0))


def _relayout(pool, table, new):
    bsz, n_pages = table.shape
    w = pool.shape[2]
    nb_pool = n_pages // 2
    nb = nb_pool + (1 if new is not None else 0)
    lp = nb * KEY_CHUNK
    last = nb_pool - 1

    def pa_map(b, j, tbl):
        return (tbl[b, 2 * jnp.minimum(j, last)], 0, 0)

    def pb_map(b, j, tbl):
        return (tbl[b, 2 * jnp.minimum(j, last) + 1], 0, 0)

    in_specs = [pl.BlockSpec((None, PAGE, w), pa_map), pl.BlockSpec((None, PAGE, w), pb_map)]
    args = [pool, pool]
    if new is not None:
        in_specs.append(pl.BlockSpec((None, KEY_CHUNK, w), lambda b, j, tbl: (b, 0, 0)))
        args.append(new)
    xt, x, s = pl.pallas_call(
        functools.partial(_relayout_kernel, nb_pool, new is not None),
        grid_spec=pltpu.PrefetchScalarGridSpec(
            num_scalar_prefetch=1,
            grid=(bsz, nb),
            in_specs=in_specs,
            out_specs=[pl.BlockSpec((None, w, KEY_CHUNK), lambda b, j, tbl: (b, 0, j)),
                       pl.BlockSpec((None, KEY_CHUNK, w), lambda b, j, tbl: (b, j, 0)),
                       pl.BlockSpec((None, None, 1, w), lambda b, j, tbl: (b, j, 0, 0))]),
        out_shape=[jax.ShapeDtypeStruct((bsz, w, lp), BF16),
                   jax.ShapeDtypeStruct((bsz, lp, w), BF16),
                   jax.ShapeDtypeStruct((bsz, nb, 1, w), F32)],
        compiler_params=_cparams(("parallel", "arbitrary")),
        name="kv_relayout",
    )(table, *args)
    return xt, x, s[:, :, 0, :]


def _bucket_table():
    n = np.arange(REL_MAX_DIST + 1)
    exact = REL_BUCKETS // 2
    nf = np.maximum(n, 1).astype(np.float32)
    large = exact + (np.log(nf / np.float32(exact)) / np.float32(math.log(REL_MAX_DIST / exact))
                     * np.float32(REL_BUCKETS - exact)).astype(np.int32)
    return np.where(n < exact, n, np.minimum(large, REL_BUCKETS - 1)).astype(np.int32)


def _bias_strip_kernel(bk_ref, rb_ref, o_ref):
    bk = bk_ref[...]
    for h in range(N_HEADS):
        acc = jnp.zeros(bk.shape, F32)
        for bkt in range(REL_BUCKETS):
            acc = jnp.where(bk == bkt, rb_ref[bkt, h], acc)
        o_ref[h] = acc


def _bias_strip(rel_bias, qb):
    qi = np.arange(qb)[:, None]
    z = np.arange(STRIP_W)[None, :]
    dist = np.clip(qi - z + STRIP_ORIGIN, 0, REL_MAX_DIST)
    bk = jnp.asarray(_bucket_table()[dist])
    out = pl.pallas_call(
        _bias_strip_kernel,
        in_specs=[pl.BlockSpec(memory_space=pltpu.VMEM), pl.BlockSpec(memory_space=pltpu.SMEM)],
        out_specs=pl.BlockSpec(memory_space=pltpu.VMEM),
        out_shape=jax.ShapeDtypeStruct((N_HEADS, qb, STRIP_W), F32),
        name="bias_strip",
    )(bk, rel_bias)
    return out.reshape(N_KV, GROUP * qb, STRIP_W)


def _attn_kernel(mode, qb, lp, q_pos0, k_sel, *refs):
    if mode == "dsa":
        (q_ref, kt_ref, v_ref, strip_ref, qi_ref, wi_ref, kit_ref, o_ref,
         qs_s, m_s, l_s, acc_s, key_s) = refs
    else:
        (q_ref, kt_ref, v_ref, strip_ref, kmt_ref, o_ref,
         qs_s, m_s, l_s, acc_s) = refs
    rows = GROUP * qb
    i = pl.program_id(1)
    q0 = q_pos0 + i * qb
    c_max = (q0 + qb - 1) // KEY_CHUNK
    n_chunks = c_max + 1
    qpos = q0 + lax.broadcasted_iota(I32, (qb, 1), 0)
    lane = lax.broadcasted_iota(I32, (1, KEY_CHUNK), 1)
    scale = HEAD_DIM ** -0.5

    for g in range(N_KV):
        qs_s[g] = (q_ref[g] * scale).astype(BF16)
    m_s[...] = jnp.full(m_s.shape, NEG, F32)
    l_s[...] = jnp.zeros(l_s.shape, F32)
    acc_s[...] = jnp.zeros(acc_s.shape, F32)

    if mode == "dsa":
        qi = qi_ref[...].astype(BF16)
        wi = wi_ref[...]

        def score_chunk(c, carry):
            k0 = pl.multiple_of(c * KEY_CHUNK, KEY_CHUNK)
            s = jnp.dot(qi, kit_ref[:, pl.ds(k0, KEY_CHUNK)], preferred_element_type=F32)
            s = jnp.maximum(s, 0.0) * wi
            sc = s[0:qb]
            for h in range(1, IDX_HEADS):
                sc = sc + s[h * qb:(h + 1) * qb]
            sc = jnp.where(k0 + lane <= qpos, sc, -jnp.inf)
            bits = pltpu.bitcast(sc, I32)
            bits = jnp.where(bits == INT_MIN, 0, bits)
            key_s[:, pl.ds(k0, KEY_CHUNK)] = jnp.where(bits < 0, bits ^ 0x7FFFFFFF, bits)
            return carry

        lax.fori_loop(0, n_chunks, score_chunk, 0)

        def count(pred):
            def body(c, acc):
                k0 = pl.multiple_of(c * KEY_CHUNK, KEY_CHUNK)
                return acc + pred(key_s[:, pl.ds(k0, KEY_CHUNK)], k0).astype(I32)
            acc = lax.fori_loop(0, n_chunks, body, jnp.zeros((qb, KEY_CHUNK), I32))
            return jnp.sum(acc, axis=1, keepdims=True)

        def bit_step(bi, thr):
            cand = thr + jnp.left_shift(jnp.int32(1), 31 - bi)
            cnt = count(lambda k, k0: k >= cand)
            return jnp.where(cnt >= k_sel, cand, thr)

        thr = lax.fori_loop(0, 32, bit_step, jnp.full((qb, 1), INT_MIN, I32))
        n_gt = count(lambda k, k0: k > thr)
        n_ge = count(lambda k, k0: k >= thr)
        need = k_sel - n_gt

        def tie_limit():
            n_bits = max(1, (lp - 1).bit_length())

            def idx_step(bi, lo):
                cand = lo + jnp.left_shift(jnp.int32(1), n_bits - 1 - bi)
                cnt = count(lambda k, k0: (k == thr) & (k0 + lane <= cand))
                return jnp.where(cnt < need, cand, lo)

            lo = lax.fori_loop(0, n_bits, idx_step, jnp.full((qb, 1), -1, I32))
            return lo + 1

        tie_idx = lax.cond(jnp.max(n_ge - n_gt - need) > 0, tie_limit,
                           lambda: jnp.full((qb, 1), lp, I32))
    else:
        blk = lax.broadcasted_iota(I32, (qb, kmt_ref.shape[2]), 1)
        sel_blocks = []
        for g in range(N_KV):
            qg = q_ref[g]
            qsum = qg[0:qb]
            for j in range(1, GROUP):
                qsum = qsum + qg[j * qb:(j + 1) * qb]
            gate = jnp.dot(qsum, kmt_ref[g], preferred_element_type=F32,
                           precision=lax.Precision.HIGHEST)
            gate = jnp.where(blk < c_max, gate, -jnp.inf)
            sel = jnp.zeros(blk.shape, jnp.bool_)
            for _ in range(MOBA_TOPK):
                mx = jnp.max(gate, axis=1, keepdims=True)
                first = jnp.min(jnp.where(gate == mx, blk, blk.shape[1]), axis=1, keepdims=True)
                hit = blk == first
                sel = sel | (hit & (blk < c_max))
                gate = jnp.where(hit, -jnp.inf, gate)
            sel_blocks.append(sel.astype(I32))

    def attend(c, carry):
        k0 = pl.multiple_of(c * KEY_CHUNK, KEY_CHUNK)
        kidx = k0 + lane
        causal = kidx <= qpos
        off = pl.multiple_of(jnp.clip(STRIP_ORIGIN - (q0 - k0), 0, STRIP_ORIGIN), 128)
        if mode == "dsa":
            k = key_s[:, pl.ds(k0, KEY_CHUNK)]
            mask1 = ((k > thr) | ((k == thr) & (kidx <= tie_idx))) & causal
            mask = jnp.concatenate([mask1] * GROUP, axis=0)
        vblk = v_ref[pl.ds(k0, KEY_CHUNK), :]
        for g in range(N_KV):
            if mode != "dsa":
                picked = jnp.sum(jnp.where(blk == c, sel_blocks[g], 0), axis=1, keepdims=True)
                mask1 = jnp.where(c == c_max, causal.astype(I32), picked) > 0
                mask = jnp.concatenate([mask1] * GROUP, axis=0)
            s = jnp.dot(qs_s[g], kt_ref[g, :, pl.ds(k0, KEY_CHUNK)], preferred_element_type=F32)
            s = s + strip_ref[g, :, pl.ds(off, KEY_CHUNK)]
            s = jnp.where(mask, s, NEG)
            m_old = m_s[g]
            m_new = jnp.maximum(m_old, jnp.max(s, axis=1, keepdims=True))
            alpha = jnp.exp(m_old - m_new)
            p = jnp.where(mask, jnp.exp(s - m_new), 0.0)
            l_s[g] = alpha * l_s[g] + jnp.sum(p, axis=1, keepdims=True)
            acc_s[g] = alpha * acc_s[g] + jnp.dot(p.astype(BF16), vblk, preferred_element_type=F32)
            m_s[g] = m_new
        return carry

    lax.fori_loop(0, n_chunks, attend, 0)
    for g in range(N_KV):
        o_ref[g] = acc_s[g][:, g * HEAD_DIM:(g + 1) * HEAD_DIM] / l_s[g]


def _sparse_attention(mode, q_st, kt, v, strip, extra, qb, q_pos0, k_sel=0):
    bsz, nq = q_st.shape[:2]
    lp = v.shape[1]
    rows = GROUP * qb
    per_q = lambda *tail: pl.BlockSpec((None, None) + tail, lambda b, i: (b, i) + (0,) * len(tail))
    per_b = lambda *tail: pl.BlockSpec((None,) + tail, lambda b, i: (b,) + (0,) * len(tail))
    in_specs = [per_q(N_KV, rows, HEAD_DIM), per_b(N_KV, HEAD_DIM, lp), per_b(lp, N_KV * HEAD_DIM),
                pl.BlockSpec((N_KV, rows, STRIP_W), lambda b, i: (0, 0, 0))]
    scratch = [pltpu.VMEM((N_KV, rows, HEAD_DIM), BF16),
               pltpu.VMEM((N_KV, rows, 1), F32),
               pltpu.VMEM((N_KV, rows, 1), F32),
               pltpu.VMEM((N_KV, rows, N_KV * HEAD_DIM), F32)]
    if mode == "dsa":
        qi_st, wi_st, kit = extra
        in_specs += [per_q(IDX_HEADS * qb, IDX_DIM), per_q(IDX_HEADS * qb, 1), per_b(IDX_DIM, lp)]
        scratch.append(pltpu.VMEM((qb, lp), I32))
    else:
        (kmt,) = extra
        in_specs.append(per_b(N_KV, HEAD_DIM, kmt.shape[3]))
    return pl.pallas_call(
        functools.partial(_attn_kernel, mode, qb, lp, q_pos0, k_sel),
        grid=(bsz, nq),
        in_specs=in_specs,
        out_specs=per_q(N_KV, rows, HEAD_DIM),
        out_shape=jax.ShapeDtypeStruct((bsz, nq, N_KV, rows, HEAD_DIM), F32),
        scratch_shapes=scratch,
        compiler_params=_cparams(("parallel", "arbitrary")),
        name=mode + "_attention",
    )(q_st, kt, v, strip, *extra)


def _top_rows(s, k, payload=None):
    n = s.shape[0]
    rid = lax.broadcasted_iota(I32, s.shape, 0)
    vals, ids = [], []
    for _ in range(k):
        mx = jnp.max(s, axis=0, keepdims=True)
        first = jnp.min(jnp.where(s == mx, rid, n), axis=0, keepdims=True)
        hit = rid == first
        vals.append(mx)
        if payload is None:
            ids.append(first)
        else:
            ids.append(jnp.max(jnp.where(hit, payload, -1), axis=0, keepdims=True))
        s = jnp.where(hit, -jnp.inf, s)
    return jnp.concatenate(vals, axis=0), jnp.concatenate(ids, axis=0)


def _peer_route_kernel(x_ref, wqt_ref, keys_ref, row_ref, sh_ref, g_ref):
    xb = x_ref[...].astype(BF16)
    qt = lax.dot_general(wqt_ref[...], xb, (((1,), (1,)), ((), ())),
                         preferred_element_type=F32)
    e_rows, g_rows = [], []
    for h in range(PEER_HEADS):
        sv, si = [], []
        for c in range(2):
            hc = 2 * h + c
            qhc = qt[hc * PEER_HALF:(hc + 1) * PEER_HALF].astype(BF16)
            s = jnp.dot(keys_ref[hc], qhc, preferred_element_type=F32)
            v_, i_ = _top_rows(s, PEER_TOPK)
            sv.append(v_)
            si.append(i_)
        cand = jnp.concatenate([sv[0][a:a + 1] + sv[1] for a in range(PEER_TOPK)], axis=0)
        cidx = jnp.concatenate([si[0][a:a + 1] * PEER_NKEYS + si[1] for a in range(PEER_TOPK)], axis=0)
        gv, ge = _top_rows(cand, PEER_TOPK, payload=cidx)
        ex = jnp.exp(gv - gv[0:1])
        g_rows.append(ex / jnp.sum(ex, axis=0, keepdims=True))
        e_rows.append(ge)
    e_t = jnp.concatenate(e_rows, axis=0).T
    row_ref[...] = jnp.right_shift(e_t, 1)
    sh_ref[...] = (1 - (e_t & 1)) * 16
    g_ref[...] = jnp.concatenate(g_rows, axis=0).T


def _peer_route(x, wqt_bf16, keys_bf16, tb):
    m, d = x.shape
    out = pl.BlockSpec((tb, PEER_PAIRS), lambda i: (i, 0))
    return pl.pallas_call(
        _peer_route_kernel,
        grid=(m // tb,),
        in_specs=[pl.BlockSpec((tb, d), lambda i: (i, 0)),
                  pl.BlockSpec(wqt_bf16.shape, lambda i: (0, 0)),
                  pl.BlockSpec(keys_bf16.shape, lambda i: (0, 0, 0))],
        out_specs=[out, out, out],
        out_shape=[jax.ShapeDtypeStruct((m, PEER_PAIRS), I32),
                   jax.ShapeDtypeStruct((m, PEER_PAIRS), I32),
                   jax.ShapeDtypeStruct((m, PEER_PAIRS), F32)],
        compiler_params=_cparams(("parallel",)),
        name="peer_route",
    )(x, wqt_bf16, keys_bf16)


def _pack_table(w):
    n, d = w.shape
    bits = lax.bitcast_convert_type(w.astype(BF16), jnp.uint16).astype(jnp.uint32)
    bits = bits.reshape(n // 2, 2, d // 128, 128)
    return (bits[:, 0] | (bits[:, 1] << 16)).astype(jnp.uint32)


def _expert_tile(tbl_ref, r, sh):
    w = tbl_ref[r]
    return pltpu.bitcast(jnp.left_shift(w, sh.astype(jnp.uint32)) & jnp.uint32(0xFFFF0000), F32)


def _peer_up_kernel(tb, row_ref, sh_ref, x_ref, g_ref, fold_ref, tbl_ref, c_ref, prod_s):
    diag = (lax.broadcasted_iota(I32, (PEER_PAIRS, PEER_PAIRS), 0)
            == lax.broadcasted_iota(I32, (PEER_PAIRS, PEER_PAIRS), 1))

    def token(t, carry):
        xt = x_ref[t]
        for p in range(PEER_PAIRS):
            prod_s[p] = _expert_tile(tbl_ref, row_ref[t, p], sh_ref[t, p]) * xt
        part = jnp.dot(fold_ref[...], prod_s[...].reshape(PEER_PAIRS * 8, 128).astype(BF16),
                       preferred_element_type=F32)
        col = jnp.sum(part, axis=1, keepdims=True)
        a = jnp.sum(jnp.where(diag, col, 0.0), axis=0, keepdims=True)
        act = 0.5 * a * (1.0 + lax.erf(a * (2.0 ** -0.5)))
        c_ref[pl.ds(t, 1), :] = g_ref[pl.ds(t, 1), :] * act
        return carry

    lax.fori_loop(0, tb, token, 0)


def _peer_down_kernel(tb, row_ref, sh_ref, c_ref, tbl_ref, f_ref):
    def token(t, carry):
        accs = [jnp.zeros((8, 128), F32) for _ in range(4)]
        for p in range(PEER_PAIRS):
            accs[p % 4] = accs[p % 4] + c_ref[t, p] * _expert_tile(tbl_ref, row_ref[t, p], sh_ref[t, p])
        f_ref[t] = (accs[0] + accs[1]) + (accs[2] + accs[3])
        return carry

    lax.fori_loop(0, tb, token, 0)


def _peer_experts(x, row, sh, g, u_tbl, v_tbl, tb):
    m, d = x.shape
    x3 = x.reshape(m, d // 128, 128)
    smem = pl.BlockSpec((tb, PEER_PAIRS), lambda i: (i, 0), memory_space=pltpu.SMEM)
    vrow = pl.BlockSpec((tb, PEER_PAIRS), lambda i: (i, 0))
    tile = pl.BlockSpec((tb, d // 128, 128), lambda i: (i, 0, 0))
    resident = pl.BlockSpec(memory_space=pltpu.VMEM)
    fold = jnp.asarray(np.arange(PEER_PAIRS * 8)[None, :] // 8 == np.arange(PEER_PAIRS)[:, None], BF16)
    c = pl.pallas_call(
        functools.partial(_peer_up_kernel, tb),
        grid=(m // tb,),
        in_specs=[smem, smem, tile, vrow, pl.BlockSpec(fold.shape, lambda i: (0, 0)), resident],
        out_specs=vrow,
        out_shape=jax.ShapeDtypeStruct((m, PEER_PAIRS), F32),
        scratch_shapes=[pltpu.VMEM((PEER_PAIRS, 8, 128), F32)],
        compiler_params=_cparams(("arbitrary",)),
        name="peer_up",
    )(row, sh, x3, g, fold, u_tbl)
    f = pl.pallas_call(
        functools.partial(_peer_down_kernel, tb),
        grid=(m // tb,),
        in_specs=[smem, smem, smem, resident],
        out_specs=tile,
        out_shape=jax.ShapeDtypeStruct((m, d // 128, 128), F32),
        compiler_params=_cparams(("arbitrary",)),
        name="peer_down",
    )(row, sh, c, v_tbl)
    return f.reshape(m, d)


def _stack_heads(a, qb, n_lead):
    bsz, q, n, hd = a.shape
    n_in = n // n_lead
    a = a.reshape(bsz, q // qb, qb, n_lead, n_in, hd)
    return a.transpose(0, 1, 3, 4, 2, 5).reshape(bsz, q // qb, n_lead, n_in * qb, hd)


def _unstack_heads(o, qb):
    bsz, nq = o.shape[:2]
    o = o.reshape(bsz, nq, N_KV, GROUP, qb, HEAD_DIM).transpose(0, 1, 4, 2, 3, 5)
    return o.reshape(bsz * nq * qb, N_HEADS * HEAD_DIM)


def _run_trunk(x, p, past, prm, qb, tm, tb_route, tb_exp):
    bsz, q_len, d = x.shape
    depth = prm["w_o"].shape[0]
    m = bsz * q_len
    tm, tb_route, tb_exp = min(tm, m), min(tb_route, m), min(tb_exp, m)
    alpha = (2 * depth) ** 0.25
    attn_dim = N_HEADS * HEAD_DIM
    kv_dim = N_KV * HEAD_DIM
    if past is None:
        q_pos0 = 0
        n_pages = q_len // PAGE
        table = jnp.arange(bsz * n_pages, dtype=I32).reshape(bsz, n_pages)
    else:
        cache_k, cache_v, cache_ki, table = past
        q_pos0 = table.shape[1] * PAGE
    strip = _bias_strip(prm["rel_bias"], qb)
    xf = x.reshape(m, d)
    ks, vs, kis = [], [], []

    def relayout(new_rows, pool):
        if past is None:
            return _relayout(new_rows.reshape(bsz * n_pages, PAGE, new_rows.shape[2]), table, None)
        pad = jnp.zeros((bsz, KEY_CHUNK - q_len, new_rows.shape[2]), F32)
        return _relayout(pool, table, jnp.concatenate([new_rows, pad], axis=1))

    for l in range(depth):
        dsa = l % 2 == 0
        w_in = prm["w_in_dsa"][l // 2] if dsa else prm["w_in_moba"][l // 2]
        n_in = w_in.shape[1]
        n_pad = -(-n_in // 128) * 128
        w_in = jnp.pad(w_in, ((0, 0), (0, n_pad - n_in))).astype(BF16)
        proj = _matmul(xf, w_in, tm)
        qh = proj[:, :attn_dim].reshape(bsz, q_len, N_HEADS, HEAD_DIM)
        k = proj[:, attn_dim:attn_dim + kv_dim].reshape(bsz, q_len, kv_dim)
        v = proj[:, attn_dim + kv_dim:attn_dim + 2 * kv_dim].reshape(bsz, q_len, kv_dim)
        ks.append(k.reshape(bsz, q_len, N_KV, HEAD_DIM))
        vs.append(v.reshape(bsz, q_len, N_KV, HEAD_DIM))
        kt, _, ksum = relayout(k, None if past is None else cache_k[l].reshape(-1, PAGE, kv_dim))
        _, vr, _ = relayout(v, None if past is None else cache_v[l].reshape(-1, PAGE, kv_dim))
        kt = kt.reshape(bsz, N_KV, HEAD_DIM, kt.shape[2])
        q_st = _stack_heads(qh, qb, N_KV)
        if dsa:
            off = attn_dim + 2 * kv_dim
            qi = proj[:, off:off + IDX_HEADS * IDX_DIM].reshape(bsz, q_len, IDX_HEADS, IDX_DIM)
            off += IDX_HEADS * IDX_DIM
            ki = proj[:, off:off + IDX_DIM].reshape(bsz, q_len, IDX_DIM)
            wi = proj[:, off + IDX_DIM:off + IDX_DIM + IDX_HEADS].reshape(bsz, q_len, IDX_HEADS, 1)
            kis.append(ki)
            kit, _, _ = relayout(ki, None if past is None else cache_ki[l // 2])
            qi_st = _stack_heads(qi, qb, 1)[:, :, 0]
            wi_st = _stack_heads(wi, qb, 1)[:, :, 0]
            k_sel = min(IDX_TOPK, (q_pos0 + q_len) // 4)
            o = _sparse_attention("dsa", q_st, kt, vr, strip, (qi_st, wi_st, kit), qb, q_pos0, k_sel)
        else:
            nb = ksum.shape[1]
            kmean = (ksum * (1.0 / MOBA_BLOCK)).reshape(bsz, nb, N_KV, HEAD_DIM)
            kmt = jnp.pad(kmean.transpose(0, 2, 3, 1), ((0, 0), (0, 0), (0, 0), (0, -nb % 128)))
            o = _sparse_attention("moba", q_st, kt, vr, strip, (kmt,), qb, q_pos0)
        o = _unstack_heads(o, qb)
        xf = _attn_out(o, prm["w_o"][l].astype(BF16), xf, prm["ln_g"][l, 0][None], prm["ln_b"][l, 0][None],
                       alpha, tm)
        row, sh, g = _peer_route(xf, prm["peer_wq"][l].T.astype(BF16),
                                 prm["peer_keys"][l].reshape(2 * PEER_HEADS, PEER_NKEYS, PEER_HALF).astype(BF16),
                                 tb_route)
        f = _peer_experts(xf, row, sh, g, _pack_table(prm["peer_u"][l]), _pack_table(prm["peer_v"][l]), tb_exp)
        xf = _layer_tail(xf, f, prm["ln_g"][l, 1][None], prm["ln_b"][l, 1][None],
                         prm["gate_w"][l].astype(BF16), prm["gate_b"][l][None],
                         p[l].reshape(m, -1), prm["ple_w"][l].astype(BF16), alpha, tm)
    return xf.reshape(bsz, q_len, d), jnp.stack(ks), jnp.stack(vs), jnp.stack(kis)


def kernel(x_prompt, x_sample, cache_k, cache_v, cache_ki, page_table, p_prompt, p_sample, w_in_dsa, w_in_moba, w_o, rel_bias, ln_g, ln_b, peer_wq, peer_keys, peer_u, peer_v, gate_w, gate_b, ple_w):
    prm = {
        "w_in_dsa": w_in_dsa, "w_in_moba": w_in_moba, "w_o": w_o, "rel_bias": rel_bias,
        "ln_g": ln_g, "ln_b": ln_b, "peer_wq": peer_wq, "peer_keys": peer_keys,
        "peer_u": peer_u, "peer_v": peer_v, "gate_w": gate_w, "gate_b": gate_b, "ple_w": ple_w,
    }
    q_sample = x_sample.shape[1]
    y_prompt, k_prompt, v_prompt, ki_prompt = _run_trunk(
        x_prompt, p_prompt, None, prm, qb=128, tm=512, tb_route=256, tb_exp=64)
    y_sample, k_sample, v_sample, ki_sample = _run_trunk(
        x_sample, p_sample, (cache_k, cache_v, cache_ki, page_table), prm,
        qb=q_sample, tm=256, tb_route=256, tb_exp=64)
    return (y_prompt, y_sample, k_prompt, v_prompt, ki_prompt, k_sample, v_sample, ki_sample)
```

```python
import functools
import math

import numpy as np
import jax
import jax.numpy as jnp
from jax import lax
from jax.experimental import pallas as pl
from jax.experimental.pallas import tpu as pltpu

F32 = jnp.float32
BF16 = jnp.bfloat16
I32 = jnp.int32

N_HEADS = 16
HEAD_DIM = 64
N_KV = 4
GROUP = 4
IDX_HEADS = 8
IDX_DIM = 64
IDX_TOPK = 256
MOBA_BLOCK = 256
MOBA_TOPK = 3
REL_BUCKETS = 32
REL_MAX_DIST = 128
PEER_HEADS = 8
PEER_NKEYS = 128
PEER_HALF = 128
PEER_TOPK = 16
PEER_PAIRS = PEER_HEADS * PEER_TOPK
LN_EPS = 1e-5
PAGE = 128

KEY_CHUNK = 256
STRIP_W = 768
STRIP_ORIGIN = 512
NEG = -1e30
INT_MIN = -2147483648
VMEM_LIMIT = 56 * 1024 * 1024


def _cparams(sem):
    return pltpu.CompilerParams(dimension_semantics=sem, vmem_limit_bytes=VMEM_LIMIT)


def _mm_kernel(a_ref, b_ref, o_ref):
    o_ref[...] = jnp.dot(a_ref[...].astype(BF16), b_ref[...], preferred_element_type=F32)


def _matmul(a, b_bf16, tm):
    m, k = a.shape
    n = b_bf16.shape[1]
    return pl.pallas_call(
        _mm_kernel,
        grid=(m // tm,),
        in_specs=[pl.BlockSpec((tm, k), lambda i: (i, 0)),
                  pl.BlockSpec((k, n), lambda i: (0, 0))],
        out_specs=pl.BlockSpec((tm, n), lambda i: (i, 0)),
        out_shape=jax.ShapeDtypeStruct((m, n), F32),
        compiler_params=_cparams(("parallel",)),
        name="in_proj",
    )(a, b_bf16)


def _layer_norm(y, g, b):
    mu = jnp.mean(y, axis=-1, keepdims=True)
    yc = y - mu
    var = jnp.mean(yc * yc, axis=-1, keepdims=True)
    return yc * lax.rsqrt(var + LN_EPS) * g + b


def _attn_out_kernel(alpha, o_ref, w_ref, x_ref, g_ref, b_ref, y_ref):
    y = alpha * x_ref[...] + jnp.dot(o_ref[...].astype(BF16), w_ref[...],
                                     preferred_element_type=F32)
    y_ref[...] = _layer_norm(y, g_ref[...], b_ref[...])


def _attn_out(o, w_bf16, x, g, b, alpha, tm):
    m, d = x.shape
    row = pl.BlockSpec((tm, d), lambda i: (i, 0))
    vec = pl.BlockSpec((1, d), lambda i: (0, 0))
    return pl.pallas_call(
        functools.partial(_attn_out_kernel, alpha),
        grid=(m // tm,),
        in_specs=[row, pl.BlockSpec((d, d), lambda i: (0, 0)), row, vec, vec],
        out_specs=row,
        out_shape=jax.ShapeDtypeStruct((m, d), F32),
        compiler_params=_cparams(("parallel",)),
        name="attn_out_ln",
    )(o, w_bf16, x, g, b)


def _layer_tail_kernel(alpha, x_ref, f_ref, g_ref, b_ref, gw_ref, gb_ref, p_ref, pw_ref, y_ref):
    x2 = _layer_norm(alpha * x_ref[...] + f_ref[...], g_ref[...], b_ref[...])
    z = jnp.dot(x2.astype(BF16), gw_ref[...], preferred_element_type=F32) + gb_ref[...]
    gate = 1.0 / (1.0 + jnp.exp(-z))
    e = jnp.dot(p_ref[...].astype(BF16), pw_ref[...], preferred_element_type=F32)
    y_ref[...] = x2 + gate * e


def _layer_tail(x, f, g, b, gw_bf16, gb, p, pw_bf16, alpha, tm):
    m, d = x.shape
    pd = p.shape[1]
    row = pl.BlockSpec((tm, d), lambda i: (i, 0))
    vec = pl.BlockSpec((1, d), lambda i: (0, 0))
    return pl.pallas_call(
        functools.partial(_layer_tail_kernel, alpha),
        grid=(m // tm,),
        in_specs=[row, row, vec, vec, pl.BlockSpec((d, d), lambda i: (0, 0)), vec,
                  pl.BlockSpec((tm, pd), lambda i: (i, 0)),
                  pl.BlockSpec((pd, d), lambda i: (0, 0))],
        out_specs=row,
        out_shape=jax.ShapeDtypeStruct((m, d), F32),
        compiler_params=_cparams(("parallel",)),
        name="layer_tail",
    )(x, f, g, b, gw_bf16, gb, p, pw_bf16)


STEP_PAGES = 8
STEP_KEYS = STEP_PAGES * PAGE
STEP_CHUNKS = STEP_KEYS // KEY_CHUNK


def _relayout_kernel(n_pool_steps, has_new, *refs):
    page_refs = refs[1:1 + STEP_PAGES]
    new_ref = refs[1 + STEP_PAGES] if has_new else None
    xt_ref, x_ref, s_ref = refs[-3:]
    j = pl.program_id(1)

    def emit(blk):
        xt_ref[...] = blk.T.astype(BF16)
        x_ref[...] = blk.astype(BF16)
        for i in range(STEP_CHUNKS):
            s_ref[pl.ds(i, 1), :] = jnp.sum(blk[i * KEY_CHUNK:(i + 1) * KEY_CHUNK], axis=0, keepdims=True)

    def from_pages():
        emit(jnp.concatenate([r[...] for r in page_refs], axis=0))

    if has_new:
        pl.when(j < n_pool_steps)(from_pages)

        @pl.when(j >= n_pool_steps)
        def _():
            emit(new_ref[...])
    else:
        from_pages()


def _relayout(pool, table, new):
    bsz, n_pages = table.shape
    w = pool.shape[2]
    n_pool_steps = n_pages // STEP_PAGES
    n_steps = n_pool_steps + (1 if new is not None else 0)
    lp = n_steps * STEP_KEYS
    last = n_pool_steps - 1

    def page_map(i):
        return lambda b, j, tbl: (tbl[b, STEP_PAGES * jnp.minimum(j, last) + i], 0, 0)

    in_specs = [pl.BlockSpec((None, PAGE, w), page_map(i)) for i in range(STEP_PAGES)]
    args = [pool] * STEP_PAGES
    if new is not None:
        in_specs.append(pl.BlockSpec((None, STEP_KEYS, w), lambda b, j, tbl: (b, 0, 0)))
        args.append(new)
    xt, x, s = pl.pallas_call(
        functools.partial(_relayout_kernel, n_pool_steps, new is not None),
        grid_spec=pltpu.PrefetchScalarGridSpec(
            num_scalar_prefetch=1,
            grid=(bsz, n_steps),
            in_specs=in_specs,
            out_specs=[pl.BlockSpec((None, w, STEP_KEYS), lambda b, j, tbl: (b, 0, j)),
                       pl.BlockSpec((None, STEP_KEYS, w), lambda b, j, tbl: (b, j, 0)),
                       pl.BlockSpec((None, None, STEP_CHUNKS, w), lambda b, j, tbl: (b, j, 0, 0))]),
        out_shape=[jax.ShapeDtypeStruct((bsz, w, lp), BF16),
                   jax.ShapeDtypeStruct((bsz, lp, w), BF16),
                   jax.ShapeDtypeStruct((bsz, n_steps, STEP_CHUNKS, w), F32)],
        compiler_params=_cparams(("parallel", "arbitrary")),
        name="kv_relayout",
    )(table, *args)
    return xt, x, s.reshape(bsz, n_steps * STEP_CHUNKS, w)


def _bucket_table():
    n = np.arange(REL_MAX_DIST + 1)
    exact = REL_BUCKETS // 2
    nf = np.maximum(n, 1).astype(np.float32)
    large = exact + (np.log(nf / np.float32(exact)) / np.float32(math.log(REL_MAX_DIST / exact))
                     * np.float32(REL_BUCKETS - exact)).astype(np.int32)
    return np.where(n < exact, n, np.minimum(large, REL_BUCKETS - 1)).astype(np.int32)


def _bias_strip_kernel(bk_ref, rb_ref, o_ref):
    bk = bk_ref[...]
    for h in range(N_HEADS):
        acc = jnp.zeros(bk.shape, F32)
        for bkt in range(REL_BUCKETS):
            acc = jnp.where(bk == bkt, rb_ref[bkt, h], acc)
        o_ref[h] = acc


def _bias_strip(rel_bias, qb):
    qi = np.arange(qb)[:, None]
    z = np.arange(STRIP_W)[None, :]
    dist = np.clip(qi - z + STRIP_ORIGIN, 0, REL_MAX_DIST)
    bk = jnp.asarray(_bucket_table()[dist])
    out = pl.pallas_call(
        _bias_strip_kernel,
        in_specs=[pl.BlockSpec(memory_space=pltpu.VMEM), pl.BlockSpec(memory_space=pltpu.SMEM)],
        out_specs=pl.BlockSpec(memory_space=pltpu.VMEM),
        out_shape=jax.ShapeDtypeStruct((N_HEADS, qb, STRIP_W), F32),
        name="bias_strip",
    )(bk, rel_bias)
    return out


def _attn_kernel(mode, qb, lp, q_pos0, k_sel, *refs):
    if mode == "dsa":
        (q_ref, kt_ref, v_ref, strip_ref, qi_ref, wi_ref, kit_ref, o_ref,
         qs_s, m_s, l_s, acc_s, key_s) = refs
    else:
        (q_ref, kt_ref, v_ref, strip_ref, kmt_ref, o_ref,
         qs_s, m_s, l_s, acc_s) = refs
    rows = GROUP * qb
    i = pl.program_id(1)
    q0 = q_pos0 + i * qb
    c_max = (q0 + qb - 1) // KEY_CHUNK
    n_chunks = c_max + 1
    qpos = q0 + lax.broadcasted_iota(I32, (qb, 1), 0)
    lane = lax.broadcasted_iota(I32, (1, KEY_CHUNK), 1)
    scale = HEAD_DIM ** -0.5

    for g in range(N_KV):
        qs_s[g] = (q_ref[g] * scale).astype(BF16)
    m_s[...] = jnp.full(m_s.shape, NEG, F32)
    l_s[...] = jnp.zeros(l_s.shape, F32)
    acc_s[...] = jnp.zeros(acc_s.shape, F32)

    if mode == "dsa":
        qi = qi_ref[...].astype(BF16)
        wi = wi_ref[...]

        def score_chunk(c, carry):
            k0 = pl.multiple_of(c * KEY_CHUNK, KEY_CHUNK)
            s = jnp.dot(qi, kit_ref[:, pl.ds(k0, KEY_CHUNK)], preferred_element_type=F32)
            s = jnp.maximum(s, 0.0) * wi
            sc = s[0:qb]
            for h in range(1, IDX_HEADS):
                sc = sc + s[h * qb:(h + 1) * qb]
            sc = jnp.where(k0 + lane <= qpos, sc, -jnp.inf)
            bits = pltpu.bitcast(sc, I32)
            bits = jnp.where(bits == INT_MIN, 0, bits)
            key_s[:, pl.ds(k0, KEY_CHUNK)] = jnp.where(bits < 0, bits ^ 0x7FFFFFFF, bits)
            return carry

        lax.fori_loop(0, n_chunks, score_chunk, 0)

        def count(pred):
            def body(c, acc):
                k0 = pl.multiple_of(c * KEY_CHUNK, KEY_CHUNK)
                return acc + pred(key_s[:, pl.ds(k0, KEY_CHUNK)], k0).astype(I32)
            acc = lax.fori_loop(0, n_chunks, body, jnp.zeros((qb, KEY_CHUNK), I32))
            return jnp.sum(acc, axis=1, keepdims=True)

        def bit_step(bi, thr):
            cand = thr + jnp.left_shift(jnp.int32(1), 31 - bi)
            cnt = count(lambda k, k0: k >= cand)
            return jnp.where(cnt >= k_sel, cand, thr)

        thr = lax.fori_loop(0, 32, bit_step, jnp.full((qb, 1), INT_MIN, I32))
        n_gt = count(lambda k, k0: k > thr)
        n_ge = count(lambda k, k0: k >= thr)
        need = k_sel - n_gt

        def tie_limit():
            n_bits = max(1, (lp - 1).bit_length())

            def idx_step(bi, lo):
                cand = lo + jnp.left_shift(jnp.int32(1), n_bits - 1 - bi)
                cnt = count(lambda k, k0: (k == thr) & (k0 + lane <= cand))
                return jnp.where(cnt < need, cand, lo)

            lo = lax.fori_loop(0, n_bits, idx_step, jnp.full((qb, 1), -1, I32))
            return lo + 1

        tie_idx = lax.cond(jnp.max(n_ge - n_gt - need) > 0, tie_limit,
                           lambda: jnp.full((qb, 1), lp, I32))
    else:
        blk = lax.broadcasted_iota(I32, (qb, kmt_ref.shape[2]), 1)
        sel_blocks = []
        for g in range(N_KV):
            qg = q_ref[g]
            qsum = qg[0:qb]
            for j in range(1, GROUP):
                qsum = qsum + qg[j * qb:(j + 1) * qb]
            gate = jnp.dot(qsum, kmt_ref[g], preferred_element_type=F32,
                           precision=lax.Precision.HIGHEST)
            gate = jnp.where(blk < c_max, gate, -jnp.inf)
            sel = jnp.zeros(blk.shape, jnp.bool_)
            for _ in range(MOBA_TOPK):
                mx = jnp.max(gate, axis=1, keepdims=True)
                first = jnp.min(jnp.where(gate == mx, blk, blk.shape[1]), axis=1, keepdims=True)
                hit = blk == first
                sel = sel | (hit & (blk < c_max))
                gate = jnp.where(hit, -jnp.inf, gate)
            sel_blocks.append(sel.astype(I32))

    def attend(c, carry):
        k0 = pl.multiple_of(c * KEY_CHUNK, KEY_CHUNK)
        kidx = k0 + lane
        causal = kidx <= qpos
        off = pl.multiple_of(jnp.clip(STRIP_ORIGIN - (q0 - k0), 0, STRIP_ORIGIN), 128)
        if mode == "dsa":
            k = key_s[:, pl.ds(k0, KEY_CHUNK)]
            mask1 = ((k > thr) | ((k == thr) & (kidx <= tie_idx))) & causal
            mask = jnp.concatenate([mask1] * GROUP, axis=0)
        vblk = v_ref[pl.ds(k0, KEY_CHUNK), :]
        for g in range(N_KV):
            if mode != "dsa":
                picked = jnp.sum(jnp.where(blk == c, sel_blocks[g], 0), axis=1, keepdims=True)
                mask1 = jnp.where(c == c_max, causal.astype(I32), picked) > 0
                mask = jnp.concatenate([mask1] * GROUP, axis=0)
            s = jnp.dot(qs_s[g], kt_ref[g, :, pl.ds(k0, KEY_CHUNK)], preferred_element_type=F32)
            s = s + strip_ref[g, :, pl.ds(off, KEY_CHUNK)]
            s = jnp.where(mask, s, NEG)
            m_old = m_s[g]
            m_new = jnp.maximum(m_old, jnp.max(s, axis=1, keepdims=True))
            alpha = jnp.exp(m_old - m_new)
            p = jnp.where(mask, jnp.exp(s - m_new), 0.0)
            l_s[g] = alpha * l_s[g] + jnp.sum(p, axis=1, keepdims=True)
            acc_s[g] = alpha * acc_s[g] + jnp.dot(p.astype(BF16), vblk, preferred_element_type=F32)
            m_s[g] = m_new
        return carry

    lax.fori_loop(0, n_chunks, attend, 0)
    for g in range(N_KV):
        o_ref[g] = acc_s[g][:, g * HEAD_DIM:(g + 1) * HEAD_DIM] / l_s[g]


def _sparse_attention(mode, q_st, kt, v, strip, extra, qb, q_pos0, k_sel=0):
    bsz, nq = q_st.shape[:2]
    lp = v.shape[1]
    rows = GROUP * qb
    per_q = lambda *tail: pl.BlockSpec((None, None) + tail, lambda b, i: (b, i) + (0,) * len(tail))
    per_b = lambda *tail: pl.BlockSpec((None,) + tail, lambda b, i: (b,) + (0,) * len(tail))
    in_specs = [per_q(N_KV, rows, HEAD_DIM), per_b(N_KV, HEAD_DIM, lp), per_b(lp, N_KV * HEAD_DIM),
                pl.BlockSpec((N_KV, rows, STRIP_W), lambda b, i: (0, 0, 0))]
    scratch = [pltpu.VMEM((N_KV, rows, HEAD_DIM), BF16),
               pltpu.VMEM((N_KV, rows, 1), F32),
               pltpu.VMEM((N_KV, rows, 1), F32),
               pltpu.VMEM((N_KV, rows, N_KV * HEAD_DIM), F32)]
    if mode == "dsa":
        qi_st, wi_st, kit = extra
        in_specs += [per_q(IDX_HEADS * qb, IDX_DIM), per_q(IDX_HEADS * qb, 1), per_b(IDX_DIM, lp)]
        scratch.append(pltpu.VMEM((qb, lp), I32))
    else:
        (kmt,) = extra
        in_specs.append(per_b(N_KV, HEAD_DIM, kmt.shape[3]))
    return pl.pallas_call(
        functools.partial(_attn_kernel, mode, qb, lp, q_pos0, k_sel),
        grid=(bsz, nq),
        in_specs=in_specs,
        out_specs=per_q(N_KV, rows, HEAD_DIM),
        out_shape=jax.ShapeDtypeStruct((bsz, nq, N_KV, rows, HEAD_DIM), F32),
        scratch_shapes=scratch,
        compiler_params=_cparams(("parallel", "arbitrary")),
        name=mode + "_attention",
    )(q_st, kt, v, strip, *extra)


QT = 128


def _attn_t_kernel(mode, lp, k_sel, *refs):
    if mode == "dsa":
        (qt_ref, k_ref, vt_ref, strip_ref, qit_ref, wit_ref, ki_ref, o_ref,
         qbd_s, m_s, l_s, acc_s, key_s) = refs
    else:
        (qt_ref, k_ref, vt_ref, strip_ref, km_ref, o_ref,
         qbd_s, m_s, l_s, acc_s) = refs
    kv_dim = N_KV * HEAD_DIM
    cols = GROUP * QT
    i = pl.program_id(1)
    q0 = i * QT
    c_max = (q0 + QT - 1) // KEY_CHUNK
    n_chunks = c_max + 1
    qpos = q0 + lax.broadcasted_iota(I32, (1, QT), 1)
    krow = lax.broadcasted_iota(I32, (KEY_CHUNK, 1), 0)
    scale = HEAD_DIM ** -0.5

    qbd_s[...] = jnp.zeros(qbd_s.shape, BF16)
    for g in range(N_KV):
        qbd_s[g * HEAD_DIM:(g + 1) * HEAD_DIM, g * cols:(g + 1) * cols] = (
            qt_ref[g * HEAD_DIM:(g + 1) * HEAD_DIM, :] * scale).astype(BF16)
    m_s[...] = jnp.full(m_s.shape, NEG, F32)
    l_s[...] = jnp.zeros(l_s.shape, F32)
    acc_s[...] = jnp.zeros(acc_s.shape, F32)

    if mode == "dsa":
        qit = qit_ref[...].astype(BF16)
        wit = wit_ref[...]

        def score_chunk(c, carry):
            k0 = pl.multiple_of(c * KEY_CHUNK, KEY_CHUNK)
            kib = ki_ref[pl.ds(k0, KEY_CHUNK), :]
            sc = jnp.zeros((KEY_CHUNK, QT), F32)
            for hp in range(IDX_HEADS // 2):
                lanes = slice(2 * hp * QT, 2 * (hp + 1) * QT)
                s = jnp.maximum(jnp.dot(kib, qit[:, lanes], preferred_element_type=F32), 0.0) * wit[:, lanes]
                sc = sc + s[:, :QT] + s[:, QT:]
            sc = jnp.where(k0 + krow <= qpos, sc, -jnp.inf)
            bits = pltpu.bitcast(sc, I32)
            bits = jnp.where(bits == INT_MIN, 0, bits)
            key_s[pl.ds(k0, KEY_CHUNK), :] = jnp.where(bits < 0, bits ^ 0x7FFFFFFF, bits)
            return carry

        lax.fori_loop(0, n_chunks, score_chunk, 0)

        def count(pred):
            def body(c, acc):
                k0 = pl.multiple_of(c * KEY_CHUNK, KEY_CHUNK)
                hit = pred(key_s[pl.ds(k0, KEY_CHUNK), :], k0).astype(I32)
                return acc + jnp.sum(hit.reshape(KEY_CHUNK // 8, 8, QT), axis=0)
            acc = lax.fori_loop(0, n_chunks, body, jnp.zeros((8, QT), I32))
            return jnp.sum(acc, axis=0, keepdims=True)

        def bit_step(bi, thr):
            cand = thr + jnp.left_shift(jnp.int32(1), 31 - bi)
            cnt = count(lambda k, k0: k >= cand)
            return jnp.where(cnt >= k_sel, cand, thr)

        thr = lax.fori_loop(0, 32, bit_step, jnp.full((1, QT), INT_MIN, I32))
        n_gt = count(lambda k, k0: k > thr)
        n_ge = count(lambda k, k0: k >= thr)
        need = k_sel - n_gt

        def tie_limit():
            n_bits = max(1, (lp - 1).bit_length())

            def idx_step(bi, lo):
                cand = lo + jnp.left_shift(jnp.int32(1), n_bits - 1 - bi)
                cnt = count(lambda k, k0: (k == thr) & (k0 + krow <= cand))
                return jnp.where(cnt < need, cand, lo)

            lo = lax.fori_loop(0, n_bits, idx_step, jnp.full((1, QT), -1, I32))
            return lo + 1

        tie_idx = lax.cond(jnp.max(n_ge - n_gt - need) > 0, tie_limit,
                           lambda: jnp.full((1, QT), lp, I32))
    else:
        nbp = km_ref.shape[1]
        blk = lax.broadcasted_iota(I32, (nbp, QT), 0)
        sel_blocks = []
        for g in range(N_KV):
            qg = qt_ref[g * HEAD_DIM:(g + 1) * HEAD_DIM, :]
            qsum = qg[:, 0:QT]
            for j in range(1, GROUP):
                qsum = qsum + qg[:, j * QT:(j + 1) * QT]
            gate = jnp.dot(km_ref[g], qsum, preferred_element_type=F32, precision=lax.Precision.HIGHEST)
            gate = jnp.where(blk < c_max, gate, -jnp.inf)
            sel = jnp.zeros(blk.shape, I32)
            for _ in range(MOBA_TOPK):
                mx = jnp.max(gate, axis=0, keepdims=True)
                first = jnp.min(jnp.where(gate == mx, blk, nbp), axis=0, keepdims=True)
                hit = blk == first
                sel = jnp.where(hit & (blk < c_max), 1, sel)
                gate = jnp.where(hit, -jnp.inf, gate)
            sel_blocks.append(sel)

    def attend(c, carry):
        k0 = pl.multiple_of(c * KEY_CHUNK, KEY_CHUNK)
        kidx = k0 + krow
        causal = kidx <= qpos
        off = pl.multiple_of(jnp.clip(STRIP_ORIGIN - (q0 - k0), 0, STRIP_ORIGIN), 128)
        if mode == "dsa":
            k = key_s[pl.ds(k0, KEY_CHUNK), :]
            mask = ((k > thr) | ((k == thr) & (kidx <= tie_idx))) & causal
        kblk = k_ref[pl.ds(k0, KEY_CHUNK), :]
        for g in range(N_KV):
            if mode != "dsa":
                picked = jnp.sum(jnp.where(blk == c, sel_blocks[g], 0), axis=0, keepdims=True)
                mask = jnp.where(c == c_max, causal.astype(I32), picked) > 0
            s_all = jnp.dot(kblk, qbd_s[:, g * cols:(g + 1) * cols], preferred_element_type=F32)
            probs, alphas = [], []
            for j in range(GROUP):
                h = g * GROUP + j
                lanes = slice(j * QT, (j + 1) * QT)
                s = s_all[:, lanes] + strip_ref[g, pl.ds(off, KEY_CHUNK), lanes]
                s = jnp.where(mask, s, NEG)
                m_old = m_s[pl.ds(h, 1), :]
                m_new = jnp.maximum(m_old, jnp.max(s, axis=0, keepdims=True))
                alpha = jnp.exp(m_old - m_new)
                p = jnp.where(mask, jnp.exp(s - m_new), 0.0)
                l_s[pl.ds(h, 1), :] = alpha * l_s[pl.ds(h, 1), :] + jnp.sum(p, axis=0, keepdims=True)
                m_s[pl.ds(h, 1), :] = m_new
                probs.append(p.astype(BF16))
                alphas.append(alpha)
            pv = jnp.dot(vt_ref[g * HEAD_DIM:(g + 1) * HEAD_DIM, pl.ds(k0, KEY_CHUNK)],
                         jnp.concatenate(probs, axis=1), preferred_element_type=F32)
            acc_s[g] = jnp.concatenate(alphas, axis=1) * acc_s[g] + pv
        return carry

    lax.fori_loop(0, n_chunks, attend, 0)
    for g in range(N_KV):
        inv = jnp.concatenate([1.0 / l_s[pl.ds(g * GROUP + j, 1), :] for j in range(GROUP)], axis=1)
        o_ref[g * HEAD_DIM:(g + 1) * HEAD_DIM, :] = acc_s[g] * inv


def _sparse_attention_t(mode, qt, k, vt, strip_t, extra, k_sel=0):
    bsz, nq = qt.shape[:2]
    lp = k.shape[1]
    kv_dim = N_KV * HEAD_DIM
    cols = GROUP * QT
    per_q = lambda *tail: pl.BlockSpec((None, None) + tail, lambda b, i: (b, i) + (0,) * len(tail))
    per_b = lambda *tail: pl.BlockSpec((None,) + tail, lambda b, i: (b,) + (0,) * len(tail))
    in_specs = [per_q(kv_dim, cols), per_b(lp, kv_dim), per_b(kv_dim, lp),
                pl.BlockSpec((N_KV, STRIP_W, cols), lambda b, i: (0, 0, 0))]
    scratch = [pltpu.VMEM((kv_dim, N_KV * cols), BF16),
               pltpu.VMEM((N_HEADS, QT), F32),
               pltpu.VMEM((N_HEADS, QT), F32),
               pltpu.VMEM((N_KV, HEAD_DIM, cols), F32)]
    if mode == "dsa":
        qit, wit, ki = extra
        in_specs += [per_q(IDX_DIM, IDX_HEADS * QT), per_q(1, IDX_HEADS * QT), per_b(lp, IDX_DIM)]
        scratch.append(pltpu.VMEM((lp, QT), I32))
    else:
        (km,) = extra
        in_specs.append(per_b(N_KV, km.shape[2], HEAD_DIM))
    return pl.pallas_call(
        functools.partial(_attn_t_kernel, mode, lp, k_sel),
        grid=(bsz, nq),
        in_specs=in_specs,
        out_specs=per_q(kv_dim, cols),
        out_shape=jax.ShapeDtypeStruct((bsz, nq, kv_dim, cols), F32),
        scratch_shapes=scratch,
        compiler_params=_cparams(("parallel", "arbitrary")),
        name=mode + "_attention_t",
    )(qt, k, vt, strip_t, *extra)


def _top_rows(s, k, payload=None):
    n = s.shape[0]
    rid = lax.broadcasted_iota(I32, s.shape, 0)
    vals, ids = [], []
    for _ in range(k):
        mx = jnp.max(s, axis=0, keepdims=True)
        first = jnp.min(jnp.where(s == mx, rid, n), axis=0, keepdims=True)
        hit = rid == first
        vals.append(mx)
        if payload is None:
            ids.append(first)
        else:
            ids.append(jnp.max(jnp.where(hit, payload, -1), axis=0, keepdims=True))
        s = jnp.where(hit, -jnp.inf, s)
    return jnp.concatenate(vals, axis=0), jnp.concatenate(ids, axis=0)


def _peer_route_kernel(x_ref, wqt_ref, keys_ref, row_ref, par_ref, g_ref):
    xb = x_ref[...].astype(BF16)
    qt = lax.dot_general(wqt_ref[...], xb, (((1,), (1,)), ((), ())),
                         preferred_element_type=F32)
    e_rows, g_rows = [], []
    for h in range(PEER_HEADS):
        sv, si = [], []
        for c in range(2):
            hc = 2 * h + c
            qhc = qt[hc * PEER_HALF:(hc + 1) * PEER_HALF].astype(BF16)
            s = jnp.dot(keys_ref[hc], qhc, preferred_element_type=F32)
            v_, i_ = _top_rows(s, PEER_TOPK)
            sv.append(v_)
            si.append(i_)
        cand = jnp.concatenate([sv[0][a:a + 1] + sv[1] for a in range(PEER_TOPK)], axis=0)
        cidx = jnp.concatenate([si[0][a:a + 1] * PEER_NKEYS + si[1] for a in range(PEER_TOPK)], axis=0)
        gv, ge = _top_rows(cand, PEER_TOPK, payload=cidx)
        ex = jnp.exp(gv - gv[0:1])
        g_rows.append(ex / jnp.sum(ex, axis=0, keepdims=True))
        e_rows.append(ge)
    e_t = jnp.concatenate(e_rows, axis=0).T
    row_ref[...] = jnp.right_shift(e_t, 1) * 8
    par_ref[...] = e_t & 1
    g_ref[...] = jnp.concatenate(g_rows, axis=0).T


def _peer_route(x, wqt_bf16, keys_bf16, tb):
    m, d = x.shape
    out = pl.BlockSpec((tb, PEER_PAIRS), lambda i: (i, 0))
    return pl.pallas_call(
        _peer_route_kernel,
        grid=(m // tb,),
        in_specs=[pl.BlockSpec((tb, d), lambda i: (i, 0)),
                  pl.BlockSpec(wqt_bf16.shape, lambda i: (0, 0)),
                  pl.BlockSpec(keys_bf16.shape, lambda i: (0, 0, 0))],
        out_specs=[out, out, out],
        out_shape=[jax.ShapeDtypeStruct((m, PEER_PAIRS), I32),
                   jax.ShapeDtypeStruct((m, PEER_PAIRS), I32),
                   jax.ShapeDtypeStruct((m, PEER_PAIRS), F32)],
        compiler_params=_cparams(("parallel",)),
        name="peer_route",
    )(x, wqt_bf16, keys_bf16)


FOLD_PAIRS = 32
UP_SLOTS = 4
TILE_ROWS = 8
HI_MASK = 0xFFFF0000


def _pack_table(w):
    n, d = w.shape
    bits = lax.bitcast_convert_type(w.astype(BF16), jnp.uint16).astype(jnp.uint32)
    bits = bits.reshape(n // 2, 2, d // 128, 128)
    return (bits[:, 0] | (bits[:, 1] << 16)).astype(jnp.uint32).reshape(n // 2 * (d // 128), 128)


def _load_tile(tbl_ref, row_ref, idx):
    return tbl_ref[pl.ds(pl.multiple_of(row_ref[idx], TILE_ROWS), TILE_ROWS), :]


def _peer_up_kernel(tb, row_ref, par_ref, x_ref, g_ref, fold_ref, tbl_ref, c_ref, prod_s):
    diag = (lax.broadcasted_iota(I32, (PEER_PAIRS, PEER_PAIRS), 0)
            == lax.broadcasted_iota(I32, (PEER_PAIRS, PEER_PAIRS), 1))

    def gather(t, slot):
        xt = x_ref[t]
        rows_t = row_ref.at[pl.ds(t * PEER_PAIRS, PEER_PAIRS)]
        for p in range(0, PEER_PAIRS, 2):
            w0 = _load_tile(tbl_ref, rows_t, p)
            w1 = _load_tile(tbl_ref, rows_t, p + 1)
            even = [pltpu.bitcast(jnp.left_shift(w, jnp.uint32(16)), F32) * xt for w in (w0, w1)]
            odd = [pltpu.bitcast(w & jnp.uint32(HI_MASK), F32) * xt for w in (w0, w1)]
            rows = pl.ds(p * TILE_ROWS, 2 * TILE_ROWS)
            prod_s[slot, rows, 0:128] = jnp.concatenate(even, axis=0).astype(BF16)
            prod_s[slot, rows, 128:256] = jnp.concatenate(odd, axis=0).astype(BF16)

    def reduce(t, slot):
        depth = FOLD_PAIRS * TILE_ROWS
        parts = [jnp.dot(fold_ref[...], prod_s[slot, pl.ds(i * depth, depth), :], preferred_element_type=F32)
                 for i in range(PEER_PAIRS // FOLD_PAIRS)]
        part = jnp.concatenate(parts, axis=0)
        rows = []
        for half in range(2):
            col = jnp.sum(part[:, half * 128:(half + 1) * 128], axis=1, keepdims=True)
            rows.append(jnp.sum(jnp.where(diag, col, 0.0), axis=0, keepdims=True))
        a = jnp.where(par_ref[pl.ds(t, 1), :] == 1, rows[1], rows[0])
        act = 0.5 * a * (1.0 + lax.erf(a * (2.0 ** -0.5)))
        c_ref[pl.ds(t, 1), :] = g_ref[pl.ds(t, 1), :] * act

    gather(0, 0)

    def step(k, carry):
        for j in range(UP_SLOTS):
            gather(jnp.minimum(UP_SLOTS * k + j + 1, tb - 1), (j + 1) % UP_SLOTS)
            reduce(UP_SLOTS * k + j, j)
        return carry

    lax.fori_loop(0, tb // UP_SLOTS, step, 0)


DOWN_GROUP = 8


def _peer_down_kernel(tb, row_ref, par_ref, c_ref, spread_ref, tbl_ref, f_ref, tile_s, coef_s):
    width = PEER_PAIRS * 2 * TILE_ROWS
    lane = lax.broadcasted_iota(I32, (TILE_ROWS, width), 1)
    own_row = ((lane & (2 * TILE_ROWS - 1)) >> 1) == lax.broadcasted_iota(I32, (TILE_ROWS, width), 0)
    lane_par = (lax.broadcasted_iota(I32, (DOWN_GROUP, width), 1) & 1).astype(F32)

    def gather(t, slot):
        rows_t = row_ref.at[pl.ds(t * PEER_PAIRS, PEER_PAIRS)]
        for p in range(PEER_PAIRS):
            tile_s[slot, pl.ds(p * TILE_ROWS, TILE_ROWS), :] = _load_tile(tbl_ref, rows_t, p)

    def apply(t, j, slot):
        tiles = pltpu.bitcast(tile_s[slot], BF16)
        lhs = [jnp.where(own_row, coef_s[pl.ds(part * DOWN_GROUP + j, 1), :], 0.0) for part in range(2)]
        out = jnp.dot(jnp.concatenate(lhs, axis=0).astype(BF16), tiles, preferred_element_type=F32)
        f_ref[t] = out[0:TILE_ROWS] + out[TILE_ROWS:2 * TILE_ROWS]

    def group(gi, carry):
        t0 = pl.multiple_of(gi * DOWN_GROUP, DOWN_GROUP)
        gather(t0, 0)
        c = c_ref[pl.ds(t0, DOWN_GROUP), :]
        c_hi = c.astype(BF16)
        c_lo = (c - c_hi.astype(F32)).astype(BF16)
        par = par_ref[pl.ds(t0, DOWN_GROUP), :].astype(BF16)
        rep = jnp.dot(jnp.concatenate([c_hi, c_lo, par], axis=0), spread_ref[...],
                      preferred_element_type=F32)
        keep = rep[2 * DOWN_GROUP:] == lane_par
        coef_s[0:DOWN_GROUP, :] = jnp.where(keep, rep[0:DOWN_GROUP], 0.0)
        coef_s[DOWN_GROUP:2 * DOWN_GROUP, :] = jnp.where(keep, rep[DOWN_GROUP:2 * DOWN_GROUP], 0.0)
        for j in range(DOWN_GROUP):
            if j + 1 < DOWN_GROUP:
                gather(t0 + j + 1, (j + 1) % 2)
            apply(t0 + j, j, j % 2)
        return carry

    lax.fori_loop(0, tb // DOWN_GROUP, group, 0)


def _peer_experts(x, row, par, g, u_tbl, v_tbl, tb):
    m, d = x.shape
    x3 = x.reshape(m, d // 128, 128)
    flat = row.reshape(m * PEER_PAIRS)
    smem = pl.BlockSpec((tb * PEER_PAIRS,), lambda i: (i,), memory_space=pltpu.SMEM)
    vrow = pl.BlockSpec((tb, PEER_PAIRS), lambda i: (i, 0))
    tile = pl.BlockSpec((tb, d // 128, 128), lambda i: (i, 0, 0))
    resident = pl.BlockSpec(memory_space=pltpu.VMEM)
    const = lambda a: pl.BlockSpec(a.shape, lambda i: (0, 0))
    fold = jnp.asarray(np.arange(FOLD_PAIRS * TILE_ROWS)[None, :] // TILE_ROWS == np.arange(FOLD_PAIRS)[:, None], BF16)
    width = PEER_PAIRS * 2 * TILE_ROWS
    spread = jnp.asarray(np.arange(width)[None, :] // (2 * TILE_ROWS) == np.arange(PEER_PAIRS)[:, None], BF16)
    c = pl.pallas_call(
        functools.partial(_peer_up_kernel, tb),
        grid=(m // tb,),
        in_specs=[smem, vrow, tile, vrow, const(fold), resident],
        out_specs=vrow,
        out_shape=jax.ShapeDtypeStruct((m, PEER_PAIRS), F32),
        scratch_shapes=[pltpu.VMEM((UP_SLOTS, PEER_PAIRS * TILE_ROWS, 256), BF16)],
        compiler_params=_cparams(("arbitrary",)),
        name="peer_up",
    )(flat, par, x3, g, fold, u_tbl)
    f = pl.pallas_call(
        functools.partial(_peer_down_kernel, tb),
        grid=(m // tb,),
        in_specs=[smem, vrow, vrow, const(spread), resident],
        out_specs=tile,
        out_shape=jax.ShapeDtypeStruct((m, d // 128, 128), F32),
        scratch_shapes=[pltpu.VMEM((2, PEER_PAIRS * TILE_ROWS, 128), jnp.uint32),
                        pltpu.VMEM((2 * DOWN_GROUP, width), F32)],
        compiler_params=_cparams(("arbitrary",)),
        name="peer_down",
    )(flat, par, c, spread, v_tbl)
    return f.reshape(m, d)


def _stack_heads(a, qb, n_lead):
    bsz, q, n, hd = a.shape
    n_in = n // n_lead
    a = a.reshape(bsz, q // qb, qb, n_lead, n_in, hd)
    return a.transpose(0, 1, 3, 4, 2, 5).reshape(bsz, q // qb, n_lead, n_in * qb, hd)


def _unstack_heads(o, qb):
    bsz, nq = o.shape[:2]
    o = o.reshape(bsz, nq, N_KV, GROUP, qb, HEAD_DIM).transpose(0, 1, 4, 2, 3, 5)
    return o.reshape(bsz * nq * qb, N_HEADS * HEAD_DIM)


def _run_trunk(x, p, past, prm, qb, tm, tb_route, tb_exp):
    bsz, q_len, d = x.shape
    depth = prm["w_o"].shape[0]
    m = bsz * q_len
    tm, tb_route, tb_exp = min(tm, m), min(tb_route, m), min(tb_exp, m)
    alpha = (2 * depth) ** 0.25
    attn_dim = N_HEADS * HEAD_DIM
    kv_dim = N_KV * HEAD_DIM
    if past is None:
        q_pos0 = 0
        n_pages = q_len // PAGE
        table = jnp.arange(bsz * n_pages, dtype=I32).reshape(bsz, n_pages)
    else:
        cache_k, cache_v, cache_ki, table = past
        q_pos0 = table.shape[1] * PAGE
    on_lanes = past is None and q_len % QT == 0
    if on_lanes:
        qb = QT
    nq = q_len // qb
    strip = _bias_strip(prm["rel_bias"], qb).reshape(N_KV, GROUP, qb, STRIP_W)
    if on_lanes:
        strip = strip.transpose(0, 3, 1, 2).reshape(N_KV, STRIP_W, GROUP * qb)
    else:
        strip = strip.reshape(N_KV, GROUP * qb, STRIP_W)
    xf = x.reshape(m, d)
    ks, vs, kis = [], [], []

    def relayout(new_rows, pool):
        if past is None:
            return _relayout(new_rows.reshape(bsz * n_pages, PAGE, new_rows.shape[2]), table, None)
        pad = jnp.zeros((bsz, STEP_KEYS - q_len, new_rows.shape[2]), F32)
        return _relayout(pool, table, jnp.concatenate([new_rows, pad], axis=1))

    for l in range(depth):
        dsa = l % 2 == 0
        w_in = prm["w_in_dsa"][l // 2] if dsa else prm["w_in_moba"][l // 2]
        n_in = w_in.shape[1]
        n_pad = -(-n_in // 128) * 128
        w_in = jnp.pad(w_in, ((0, 0), (0, n_pad - n_in))).astype(BF16)
        proj = _matmul(xf, w_in, tm)
        qh = proj[:, :attn_dim].reshape(bsz, q_len, N_HEADS, HEAD_DIM)
        k = proj[:, attn_dim:attn_dim + kv_dim].reshape(bsz, q_len, kv_dim)
        v = proj[:, attn_dim + kv_dim:attn_dim + 2 * kv_dim].reshape(bsz, q_len, kv_dim)
        ks.append(k.reshape(bsz, q_len, N_KV, HEAD_DIM))
        vs.append(v.reshape(bsz, q_len, N_KV, HEAD_DIM))
        kt, kr, ksum = relayout(k, None if past is None else cache_k[l].reshape(-1, PAGE, kv_dim))
        vt, vr, _ = relayout(v, None if past is None else cache_v[l].reshape(-1, PAGE, kv_dim))
        if on_lanes:
            q_in = qh.reshape(bsz, nq, qb, N_KV, GROUP, HEAD_DIM).transpose(0, 1, 3, 5, 4, 2)
            q_in = q_in.reshape(bsz, nq, kv_dim, GROUP * qb)
        else:
            kt = kt.reshape(bsz, N_KV, HEAD_DIM, kt.shape[2])
            q_in = _stack_heads(qh, qb, N_KV)
        if dsa:
            off = attn_dim + 2 * kv_dim
            qi = proj[:, off:off + IDX_HEADS * IDX_DIM].reshape(bsz, q_len, IDX_HEADS, IDX_DIM)
            off += IDX_HEADS * IDX_DIM
            ki = proj[:, off:off + IDX_DIM].reshape(bsz, q_len, IDX_DIM)
            wi = proj[:, off + IDX_DIM:off + IDX_DIM + IDX_HEADS].reshape(bsz, q_len, IDX_HEADS, 1)
            kis.append(ki)
            kit, kir, _ = relayout(ki, None if past is None else cache_ki[l // 2])
            k_sel = min(IDX_TOPK, (q_pos0 + q_len) // 4)
            if on_lanes:
                qi_t = qi.reshape(bsz, nq, qb, IDX_HEADS, IDX_DIM).transpose(0, 1, 4, 3, 2)
                qi_t = qi_t.reshape(bsz, nq, IDX_DIM, IDX_HEADS * qb)
                wi_t = wi.reshape(bsz, nq, qb, IDX_HEADS).transpose(0, 1, 3, 2).reshape(bsz, nq, 1, IDX_HEADS * qb)
                o = _sparse_attention_t("dsa", q_in, kr, vt, strip, (qi_t, wi_t, kir), k_sel)
            else:
                qi_st = _stack_heads(qi, qb, 1)[:, :, 0]
                wi_st = _stack_heads(wi, qb, 1)[:, :, 0]
                o = _sparse_attention("dsa", q_in, kt, vr, strip, (qi_st, wi_st, kit), qb, q_pos0, k_sel)
        else:
            nb = ksum.shape[1]
            kmean = (ksum * (1.0 / MOBA_BLOCK)).reshape(bsz, nb, N_KV, HEAD_DIM)
            if on_lanes:
                km = jnp.pad(kmean.transpose(0, 2, 1, 3), ((0, 0), (0, 0), (0, -nb % 8), (0, 0)))
                o = _sparse_attention_t("moba", q_in, kr, vt, strip, (km,))
            else:
                kmt = jnp.pad(kmean.transpose(0, 2, 3, 1), ((0, 0), (0, 0), (0, 0), (0, -nb % 128)))
                o = _sparse_attention("moba", q_in, kt, vr, strip, (kmt,), qb, q_pos0)
        if on_lanes:
            o = o.reshape(bsz, nq, N_KV, HEAD_DIM, GROUP, qb).transpose(0, 1, 5, 2, 4, 3)
            o = o.reshape(m, attn_dim)
        else:
            o = _unstack_heads(o, qb)
        xf = _attn_out(o, prm["w_o"][l].astype(BF16), xf, prm["ln_g"][l, 0][None], prm["ln_b"][l, 0][None],
                       alpha, tm)
        row, par, g = _peer_route(xf, prm["peer_wq"][l].T.astype(BF16),
                                 prm["peer_keys"][l].reshape(2 * PEER_HEADS, PEER_NKEYS, PEER_HALF).astype(BF16),
                                 tb_route)
        f = _peer_experts(xf, row, par, g, _pack_table(prm["peer_u"][l]), _pack_table(prm["peer_v"][l]), tb_exp)
        xf = _layer_tail(xf, f, prm["ln_g"][l, 1][None], prm["ln_b"][l, 1][None],
                         prm["gate_w"][l].astype(BF16), prm["gate_b"][l][None],
                         p[l].reshape(m, -1), prm["ple_w"][l].astype(BF16), alpha, tm)
    return xf.reshape(bsz, q_len, d), jnp.stack(ks), jnp.stack(vs), jnp.stack(kis)


def kernel(x_prompt, x_sample, cache_k, cache_v, cache_ki, page_table, p_prompt, p_sample, w_in_dsa, w_in_moba, w_o, rel_bias, ln_g, ln_b, peer_wq, peer_keys, peer_u, peer_v, gate_w, gate_b, ple_w):
    prm = {
        "w_in_dsa": w_in_dsa, "w_in_moba": w_in_moba, "w_o": w_o, "rel_bias": rel_bias,
        "ln_g": ln_g, "ln_b": ln_b, "peer_wq": peer_wq, "peer_keys": peer_keys,
        "peer_u": peer_u, "peer_v": peer_v, "gate_w": gate_w, "gate_b": gate_b, "ple_w": ple_w,
    }
    q_sample = x_sample.shape[1]
    y_prompt, k_prompt, v_prompt, ki_prompt = _run_trunk(
        x_prompt, p_prompt, None, prm, qb=128, tm=512, tb_route=256, tb_exp=64)
    y_sample, k_sample, v_sample, ki_sample = _run_trunk(
        x_sample, p_sample, (cache_k, cache_v, cache_ki, page_table), prm,
        qb=q_sample, tm=256, tb_route=256, tb_exp=64)
    return (y_prompt, y_sample, k_prompt, v_prompt, ki_prompt, k_sample, v_sample, ki_sample)
```

```python
import functools
import math

import numpy as np
import jax
import jax.numpy as jnp
from jax import lax
from jax.experimental import pallas as pl
from jax.experimental.pallas import tpu as pltpu

F32 = jnp.float32
BF16 = jnp.bfloat16
I32 = jnp.int32

N_HEADS = 16
HEAD_DIM = 64
N_KV = 4
GROUP = 4
IDX_HEADS = 8
IDX_DIM = 64
IDX_TOPK = 256
MOBA_BLOCK = 256
MOBA_TOPK = 3
REL_BUCKETS = 32
REL_MAX_DIST = 128
PEER_HEADS = 8
PEER_NKEYS = 128
PEER_HALF = 128
PEER_TOPK = 16
PEER_PAIRS = PEER_HEADS * PEER_TOPK
LN_EPS = 1e-5
PAGE = 128

KEY_CHUNK = 256
STRIP_W = 768
STRIP_ORIGIN = 512
NEG = -1e30
INT_MIN = -2147483648
VMEM_LIMIT = 56 * 1024 * 1024


def _cparams(sem):
    return pltpu.CompilerParams(dimension_semantics=sem, vmem_limit_bytes=VMEM_LIMIT)


def _mm_kernel(a_ref, b_ref, o_ref):
    o_ref[...] = jnp.dot(a_ref[...].astype(BF16), b_ref[...], preferred_element_type=F32)


def _matmul(a, b_bf16, tm):
    m, k = a.shape
    n = b_bf16.shape[1]
    return pl.pallas_call(
        _mm_kernel,
        grid=(m // tm,),
        in_specs=[pl.BlockSpec((tm, k), lambda i: (i, 0)),
                  pl.BlockSpec((k, n), lambda i: (0, 0))],
        out_specs=pl.BlockSpec((tm, n), lambda i: (i, 0)),
        out_shape=jax.ShapeDtypeStruct((m, n), F32),
        compiler_params=_cparams(("parallel",)),
        name="in_proj",
    )(a, b_bf16)


def _layer_norm(y, g, b):
    mu = jnp.mean(y, axis=-1, keepdims=True)
    yc = y - mu
    var = jnp.mean(yc * yc, axis=-1, keepdims=True)
    return yc * lax.rsqrt(var + LN_EPS) * g + b


def _attn_out_kernel(alpha, o_ref, w_ref, x_ref, g_ref, b_ref, y_ref):
    y = alpha * x_ref[...] + jnp.dot(o_ref[...].astype(BF16), w_ref[...],
                                     preferred_element_type=F32)
    y_ref[...] = _layer_norm(y, g_ref[...], b_ref[...])


def _attn_out(o, w_bf16, x, g, b, alpha, tm):
    m, d = x.shape
    row = pl.BlockSpec((tm, d), lambda i: (i, 0))
    vec = pl.BlockSpec((1, d), lambda i: (0, 0))
    return pl.pallas_call(
        functools.partial(_attn_out_kernel, alpha),
        grid=(m // tm,),
        in_specs=[row, pl.BlockSpec((d, d), lambda i: (0, 0)), row, vec, vec],
        out_specs=row,
        out_shape=jax.ShapeDtypeStruct((m, d), F32),
        compiler_params=_cparams(("parallel",)),
        name="attn_out_ln",
    )(o, w_bf16, x, g, b)


def _layer_tail_kernel(alpha, x_ref, f_ref, g_ref, b_ref, gw_ref, gb_ref, p_ref, pw_ref, y_ref):
    x2 = _layer_norm(alpha * x_ref[...] + f_ref[...], g_ref[...], b_ref[...])
    z = jnp.dot(x2.astype(BF16), gw_ref[...], preferred_element_type=F32) + gb_ref[...]
    gate = 1.0 / (1.0 + jnp.exp(-z))
    e = jnp.dot(p_ref[...].astype(BF16), pw_ref[...], preferred_element_type=F32)
    y_ref[...] = x2 + gate * e


def _layer_tail(x, f, g, b, gw_bf16, gb, p, pw_bf16, alpha, tm):
    m, d = x.shape
    pd = p.shape[1]
    row = pl.BlockSpec((tm, d), lambda i: (i, 0))
    vec = pl.BlockSpec((1, d), lambda i: (0, 0))
    return pl.pallas_call(
        functools.partial(_layer_tail_kernel, alpha),
        grid=(m // tm,),
        in_specs=[row, row, vec, vec, pl.BlockSpec((d, d), lambda i: (0, 0)), vec,
                  pl.BlockSpec((tm, pd), lambda i: (i, 0)),
                  pl.BlockSpec((pd, d), lambda i: (0, 0))],
        out_specs=row,
        out_shape=jax.ShapeDtypeStruct((m, d), F32),
        compiler_params=_cparams(("parallel",)),
        name="layer_tail",
    )(x, f, g, b, gw_bf16, gb, p, pw_bf16)


STEP_PAGES = 8
STEP_KEYS = STEP_PAGES * PAGE
STEP_CHUNKS = STEP_KEYS // KEY_CHUNK


def _relayout_kernel(n_pool_steps, has_new, *refs):
    page_refs = refs[1:1 + STEP_PAGES]
    new_ref = refs[1 + STEP_PAGES] if has_new else None
    xt_ref, x_ref, s_ref = refs[-3:]
    j = pl.program_id(1)

    def emit(blk):
        xt_ref[...] = blk.T.astype(BF16)
        x_ref[...] = blk.astype(BF16)
        for i in range(STEP_CHUNKS):
            s_ref[pl.ds(i, 1), :] = jnp.sum(blk[i * KEY_CHUNK:(i + 1) * KEY_CHUNK], axis=0, keepdims=True)

    def from_pages():
        emit(jnp.concatenate([r[...] for r in page_refs], axis=0))

    if has_new:
        pl.when(j < n_pool_steps)(from_pages)

        @pl.when(j >= n_pool_steps)
        def _():
            emit(new_ref[...])
    else:
        from_pages()


def _relayout(pool, table, new):
    bsz, n_pages = table.shape
    w = pool.shape[2]
    n_pool_steps = n_pages // STEP_PAGES
    n_steps = n_pool_steps + (1 if new is not None else 0)
    lp = n_steps * STEP_KEYS
    last = n_pool_steps - 1

    def page_map(i):
        return lambda b, j, tbl: (tbl[b, STEP_PAGES * jnp.minimum(j, last) + i], 0, 0)

    in_specs = [pl.BlockSpec((None, PAGE, w), page_map(i)) for i in range(STEP_PAGES)]
    args = [pool] * STEP_PAGES
    if new is not None:
        in_specs.append(pl.BlockSpec((None, STEP_KEYS, w), lambda b, j, tbl: (b, 0, 0)))
        args.append(new)
    xt, x, s = pl.pallas_call(
        functools.partial(_relayout_kernel, n_pool_steps, new is not None),
        grid_spec=pltpu.PrefetchScalarGridSpec(
            num_scalar_prefetch=1,
            grid=(bsz, n_steps),
            in_specs=in_specs,
            out_specs=[pl.BlockSpec((None, w, STEP_KEYS), lambda b, j, tbl: (b, 0, j)),
                       pl.BlockSpec((None, STEP_KEYS, w), lambda b, j, tbl: (b, j, 0)),
                       pl.BlockSpec((None, None, STEP_CHUNKS, w), lambda b, j, tbl: (b, j, 0, 0))]),
        out_shape=[jax.ShapeDtypeStruct((bsz, w, lp), BF16),
                   jax.ShapeDtypeStruct((bsz, lp, w), BF16),
                   jax.ShapeDtypeStruct((bsz, n_steps, STEP_CHUNKS, w), F32)],
        compiler_params=_cparams(("parallel", "arbitrary")),
        name="kv_relayout",
    )(table, *args)
    return xt, x, s.reshape(bsz, n_steps * STEP_CHUNKS, w)


def _bucket_table():
    n = np.arange(REL_MAX_DIST + 1)
    exact = REL_BUCKETS // 2
    nf = np.maximum(n, 1).astype(np.float32)
    large = exact + (np.log(nf / np.float32(exact)) / np.float32(math.log(REL_MAX_DIST / exact))
                     * np.float32(REL_BUCKETS - exact)).astype(np.int32)
    return np.where(n < exact, n, np.minimum(large, REL_BUCKETS - 1)).astype(np.int32)


def _bias_strip_kernel(bk_ref, rb_ref, o_ref):
    bk = bk_ref[...]
    for h in range(N_HEADS):
        acc = jnp.zeros(bk.shape, F32)
        for bkt in range(REL_BUCKETS):
            acc = jnp.where(bk == bkt, rb_ref[bkt, h], acc)
        o_ref[h] = acc


def _bias_strip(rel_bias, qb):
    qi = np.arange(qb)[:, None]
    z = np.arange(STRIP_W)[None, :]
    dist = np.clip(qi - z + STRIP_ORIGIN, 0, REL_MAX_DIST)
    bk = jnp.asarray(_bucket_table()[dist])
    out = pl.pallas_call(
        _bias_strip_kernel,
        in_specs=[pl.BlockSpec(memory_space=pltpu.VMEM), pl.BlockSpec(memory_space=pltpu.SMEM)],
        out_specs=pl.BlockSpec(memory_space=pltpu.VMEM),
        out_shape=jax.ShapeDtypeStruct((N_HEADS, qb, STRIP_W), F32),
        name="bias_strip",
    )(bk, rel_bias)
    return out


def _attn_kernel(mode, qb, lp, q_pos0, k_sel, *refs):
    if mode == "dsa":
        (q_ref, kt_ref, v_ref, strip_ref, qi_ref, wi_ref, kit_ref, o_ref,
         qs_s, m_s, l_s, acc_s, key_s) = refs
    else:
        (q_ref, kt_ref, v_ref, strip_ref, kmt_ref, o_ref,
         qs_s, m_s, l_s, acc_s) = refs
    rows = GROUP * qb
    i = pl.program_id(1)
    q0 = q_pos0 + i * qb
    c_max = (q0 + qb - 1) // KEY_CHUNK
    n_chunks = c_max + 1
    qpos = q0 + lax.broadcasted_iota(I32, (qb, 1), 0)
    lane = lax.broadcasted_iota(I32, (1, KEY_CHUNK), 1)
    scale = HEAD_DIM ** -0.5

    for g in range(N_KV):
        qs_s[g] = (q_ref[g] * scale).astype(BF16)
    m_s[...] = jnp.full(m_s.shape, NEG, F32)
    l_s[...] = jnp.zeros(l_s.shape, F32)
    acc_s[...] = jnp.zeros(acc_s.shape, F32)

    if mode == "dsa":
        qi = qi_ref[...].astype(BF16)
        wi = wi_ref[...]

        def score_chunk(c, carry):
            k0 = pl.multiple_of(c * KEY_CHUNK, KEY_CHUNK)
            s = jnp.dot(qi, kit_ref[:, pl.ds(k0, KEY_CHUNK)], preferred_element_type=F32)
            s = jnp.maximum(s, 0.0) * wi
            sc = s[0:qb]
            for h in range(1, IDX_HEADS):
                sc = sc + s[h * qb:(h + 1) * qb]
            sc = jnp.where(k0 + lane <= qpos, sc, -jnp.inf)
            bits = pltpu.bitcast(sc, I32)
            bits = jnp.where(bits == INT_MIN, 0, bits)
            key_s[:, pl.ds(k0, KEY_CHUNK)] = jnp.where(bits < 0, bits ^ 0x7FFFFFFF, bits)
            return carry

        lax.fori_loop(0, n_chunks, score_chunk, 0)

        def count(pred):
            def body(c, acc):
                k0 = pl.multiple_of(c * KEY_CHUNK, KEY_CHUNK)
                return acc + pred(key_s[:, pl.ds(k0, KEY_CHUNK)], k0).astype(I32)
            acc = lax.fori_loop(0, n_chunks, body, jnp.zeros((qb, KEY_CHUNK), I32))
            return jnp.sum(acc, axis=1, keepdims=True)

        def bit_step(bi, thr):
            cand = thr + jnp.left_shift(jnp.int32(1), 31 - bi)
            cnt = count(lambda k, k0: k >= cand)
            return jnp.where(cnt >= k_sel, cand, thr)

        thr = lax.fori_loop(0, 32, bit_step, jnp.full((qb, 1), INT_MIN, I32))
        n_gt = count(lambda k, k0: k > thr)
        n_ge = count(lambda k, k0: k >= thr)
        need = k_sel - n_gt

        def tie_limit():
            n_bits = max(1, (lp - 1).bit_length())

            def idx_step(bi, lo):
                cand = lo + jnp.left_shift(jnp.int32(1), n_bits - 1 - bi)
                cnt = count(lambda k, k0: (k == thr) & (k0 + lane <= cand))
                return jnp.where(cnt < need, cand, lo)

            lo = lax.fori_loop(0, n_bits, idx_step, jnp.full((qb, 1), -1, I32))
            return lo + 1

        tie_idx = lax.cond(jnp.max(n_ge - n_gt - need) > 0, tie_limit,
                           lambda: jnp.full((qb, 1), lp, I32))
    else:
        blk = lax.broadcasted_iota(I32, (qb, kmt_ref.shape[2]), 1)
        sel_blocks = []
        for g in range(N_KV):
            qg = q_ref[g]
            qsum = qg[0:qb]
            for j in range(1, GROUP):
                qsum = qsum + qg[j * qb:(j + 1) * qb]
            gate = jnp.dot(qsum, kmt_ref[g], preferred_element_type=F32,
                           precision=lax.Precision.HIGHEST)
            gate = jnp.where(blk < c_max, gate, -jnp.inf)
            sel = jnp.zeros(blk.shape, jnp.bool_)
            for _ in range(MOBA_TOPK):
                mx = jnp.max(gate, axis=1, keepdims=True)
                first = jnp.min(jnp.where(gate == mx, blk, blk.shape[1]), axis=1, keepdims=True)
                hit = blk == first
                sel = sel | (hit & (blk < c_max))
                gate = jnp.where(hit, -jnp.inf, gate)
            sel_blocks.append(sel.astype(I32))

    def attend(c, carry):
        k0 = pl.multiple_of(c * KEY_CHUNK, KEY_CHUNK)
        kidx = k0 + lane
        causal = kidx <= qpos
        off = pl.multiple_of(jnp.clip(STRIP_ORIGIN - (q0 - k0), 0, STRIP_ORIGIN), 128)
        if mode == "dsa":
            k = key_s[:, pl.ds(k0, KEY_CHUNK)]
            mask1 = ((k > thr) | ((k == thr) & (kidx <= tie_idx))) & causal
            mask = jnp.concatenate([mask1] * GROUP, axis=0)
        vblk = v_ref[pl.ds(k0, KEY_CHUNK), :]
        for g in range(N_KV):
            if mode != "dsa":
                picked = jnp.sum(jnp.where(blk == c, sel_blocks[g], 0), axis=1, keepdims=True)
                mask1 = jnp.where(c == c_max, causal.astype(I32), picked) > 0
                mask = jnp.concatenate([mask1] * GROUP, axis=0)
            s = jnp.dot(qs_s[g], kt_ref[g, :, pl.ds(k0, KEY_CHUNK)], preferred_element_type=F32)
            s = s + strip_ref[g, :, pl.ds(off, KEY_CHUNK)]
            s = jnp.where(mask, s, NEG)
            m_old = m_s[g]
            m_new = jnp.maximum(m_old, jnp.max(s, axis=1, keepdims=True))
            alpha = jnp.exp(m_old - m_new)
            p = jnp.where(mask, jnp.exp(s - m_new), 0.0)
            l_s[g] = alpha * l_s[g] + jnp.sum(p, axis=1, keepdims=True)
            acc_s[g] = alpha * acc_s[g] + jnp.dot(p.astype(BF16), vblk, preferred_element_type=F32)
            m_s[g] = m_new
        return carry

    lax.fori_loop(0, n_chunks, attend, 0)
    for g in range(N_KV):
        o_ref[g] = acc_s[g][:, g * HEAD_DIM:(g + 1) * HEAD_DIM] / l_s[g]


def _sparse_attention(mode, q_st, kt, v, strip, extra, qb, q_pos0, k_sel=0):
    bsz, nq = q_st.shape[:2]
    lp = v.shape[1]
    rows = GROUP * qb
    per_q = lambda *tail: pl.BlockSpec((None, None) + tail, lambda b, i: (b, i) + (0,) * len(tail))
    per_b = lambda *tail: pl.BlockSpec((None,) + tail, lambda b, i: (b,) + (0,) * len(tail))
    in_specs = [per_q(N_KV, rows, HEAD_DIM), per_b(N_KV, HEAD_DIM, lp), per_b(lp, N_KV * HEAD_DIM),
                pl.BlockSpec((N_KV, rows, STRIP_W), lambda b, i: (0, 0, 0))]
    scratch = [pltpu.VMEM((N_KV, rows, HEAD_DIM), BF16),
               pltpu.VMEM((N_KV, rows, 1), F32),
               pltpu.VMEM((N_KV, rows, 1), F32),
               pltpu.VMEM((N_KV, rows, N_KV * HEAD_DIM), F32)]
    if mode == "dsa":
        qi_st, wi_st, kit = extra
        in_specs += [per_q(IDX_HEADS * qb, IDX_DIM), per_q(IDX_HEADS * qb, 1), per_b(IDX_DIM, lp)]
        scratch.append(pltpu.VMEM((qb, lp), I32))
    else:
        (kmt,) = extra
        in_specs.append(per_b(N_KV, HEAD_DIM, kmt.shape[3]))
    return pl.pallas_call(
        functools.partial(_attn_kernel, mode, qb, lp, q_pos0, k_sel),
        grid=(bsz, nq),
        in_specs=in_specs,
        out_specs=per_q(N_KV, rows, HEAD_DIM),
        out_shape=jax.ShapeDtypeStruct((bsz, nq, N_KV, rows, HEAD_DIM), F32),
        scratch_shapes=scratch,
        compiler_params=_cparams(("parallel", "arbitrary")),
        name=mode + "_attention",
    )(q_st, kt, v, strip, *extra)


QT = 128


def _attn_t_kernel(mode, lp, k_sel, *refs):
    if mode == "dsa":
        (qt_ref, k_ref, vt_ref, strip_ref, qit_ref, wit_ref, ki_ref, o_ref,
         qbd_s, m_s, l_s, acc_s, key_s) = refs
    else:
        (qt_ref, k_ref, vt_ref, strip_ref, km_ref, o_ref,
         qbd_s, m_s, l_s, acc_s) = refs
    kv_dim = N_KV * HEAD_DIM
    cols = GROUP * QT
    i = pl.program_id(1)
    q0 = i * QT
    c_max = (q0 + QT - 1) // KEY_CHUNK
    n_chunks = c_max + 1
    qpos = q0 + lax.broadcasted_iota(I32, (1, QT), 1)
    krow = lax.broadcasted_iota(I32, (KEY_CHUNK, 1), 0)
    scale = HEAD_DIM ** -0.5

    qbd_s[...] = jnp.zeros(qbd_s.shape, BF16)
    for g in range(N_KV):
        qbd_s[g * HEAD_DIM:(g + 1) * HEAD_DIM, g * cols:(g + 1) * cols] = (
            qt_ref[g * HEAD_DIM:(g + 1) * HEAD_DIM, :] * scale).astype(BF16)
    m_s[...] = jnp.full(m_s.shape, NEG, F32)
    l_s[...] = jnp.zeros(l_s.shape, F32)
    acc_s[...] = jnp.zeros(acc_s.shape, F32)

    if mode == "dsa":
        qit = qit_ref[...].astype(BF16)
        wit = wit_ref[...]

        def score_chunk(c, carry):
            k0 = pl.multiple_of(c * KEY_CHUNK, KEY_CHUNK)
            kib = ki_ref[pl.ds(k0, KEY_CHUNK), :]
            sc = jnp.zeros((KEY_CHUNK, QT), F32)
            for hp in range(IDX_HEADS // 2):
                lanes = slice(2 * hp * QT, 2 * (hp + 1) * QT)
                s = jnp.maximum(jnp.dot(kib, qit[:, lanes], preferred_element_type=F32), 0.0) * wit[:, lanes]
                sc = sc + s[:, :QT] + s[:, QT:]
            sc = jnp.where(k0 + krow <= qpos, sc, -jnp.inf)
            bits = pltpu.bitcast(sc, I32)
            bits = jnp.where(bits == INT_MIN, 0, bits)
            key_s[pl.ds(k0, KEY_CHUNK), :] = jnp.where(bits < 0, bits ^ 0x7FFFFFFF, bits)
            return carry

        lax.fori_loop(0, n_chunks, score_chunk, 0)

        def count(pred):
            def body(c, acc):
                k0 = pl.multiple_of(c * KEY_CHUNK, KEY_CHUNK)
                hit = pred(key_s[pl.ds(k0, KEY_CHUNK), :], k0).astype(I32)
                return acc + jnp.sum(hit.reshape(KEY_CHUNK // 8, 8, QT), axis=0)
            acc = lax.fori_loop(0, n_chunks, body, jnp.zeros((8, QT), I32))
            return jnp.sum(acc, axis=0, keepdims=True)

        def bit_step(bi, thr):
            cand = thr + jnp.left_shift(jnp.int32(1), 31 - bi)
            cnt = count(lambda k, k0: k >= cand)
            return jnp.where(cnt >= k_sel, cand, thr)

        thr = lax.fori_loop(0, 32, bit_step, jnp.full((1, QT), INT_MIN, I32))
        n_gt = count(lambda k, k0: k > thr)
        n_ge = count(lambda k, k0: k >= thr)
        need = k_sel - n_gt

        def tie_limit():
            n_bits = max(1, (lp - 1).bit_length())

            def idx_step(bi, lo):
                cand = lo + jnp.left_shift(jnp.int32(1), n_bits - 1 - bi)
                cnt = count(lambda k, k0: (k == thr) & (k0 + krow <= cand))
                return jnp.where(cnt < need, cand, lo)

            lo = lax.fori_loop(0, n_bits, idx_step, jnp.full((1, QT), -1, I32))
            return lo + 1

        tie_idx = lax.cond(jnp.max(n_ge - n_gt - need) > 0, tie_limit,
                           lambda: jnp.full((1, QT), lp, I32))
    else:
        nbp = km_ref.shape[1]
        blk = lax.broadcasted_iota(I32, (nbp, QT), 0)
        sel_blocks = []
        for g in range(N_KV):
            qg = qt_ref[g * HEAD_DIM:(g + 1) * HEAD_DIM, :]
            qsum = qg[:, 0:QT]
            for j in range(1, GROUP):
                qsum = qsum + qg[:, j * QT:(j + 1) * QT]
            gate = jnp.dot(km_ref[g], qsum, preferred_element_type=F32, precision=lax.Precision.HIGHEST)
            gate = jnp.where(blk < c_max, gate, -jnp.inf)
            sel = jnp.zeros(blk.shape, I32)
            for _ in range(MOBA_TOPK):
                mx = jnp.max(gate, axis=0, keepdims=True)
                first = jnp.min(jnp.where(gate == mx, blk, nbp), axis=0, keepdims=True)
                hit = blk == first
                sel = jnp.where(hit & (blk < c_max), 1, sel)
                gate = jnp.where(hit, -jnp.inf, gate)
            sel_blocks.append(sel)

    def attend(c, carry):
        k0 = pl.multiple_of(c * KEY_CHUNK, KEY_CHUNK)
        kidx = k0 + krow
        causal = kidx <= qpos
        off = pl.multiple_of(jnp.clip(STRIP_ORIGIN - (q0 - k0), 0, STRIP_ORIGIN), 128)
        if mode == "dsa":
            k = key_s[pl.ds(k0, KEY_CHUNK), :]
            mask = ((k > thr) | ((k == thr) & (kidx <= tie_idx))) & causal
        kblk = k_ref[pl.ds(k0, KEY_CHUNK), :]

        def logits(g):
            return jnp.dot(kblk, qbd_s[:, g * cols:(g + 1) * cols], preferred_element_type=F32)

        s_next = logits(0)
        for g in range(N_KV):
            if mode != "dsa":
                picked = jnp.sum(jnp.where(blk == c, sel_blocks[g], 0), axis=0, keepdims=True)
                mask = jnp.where(c == c_max, causal.astype(I32), picked) > 0
            s_all = s_next
            if g + 1 < N_KV:
                s_next = logits(g + 1)
            probs, alphas = [], []
            for j in range(GROUP):
                h = g * GROUP + j
                lanes = slice(j * QT, (j + 1) * QT)
                s = s_all[:, lanes] + strip_ref[g, pl.ds(off, KEY_CHUNK), lanes]
                s = jnp.where(mask, s, NEG)
                m_old = m_s[pl.ds(h, 1), :]
                m_new = jnp.maximum(m_old, jnp.max(s, axis=0, keepdims=True))
                alpha = jnp.exp(m_old - m_new)
                p = jnp.where(mask, jnp.exp(s - m_new), 0.0)
                l_s[pl.ds(h, 1), :] = alpha * l_s[pl.ds(h, 1), :] + jnp.sum(p, axis=0, keepdims=True)
                m_s[pl.ds(h, 1), :] = m_new
                probs.append(p.astype(BF16))
                alphas.append(alpha)
            pv = jnp.dot(vt_ref[g * HEAD_DIM:(g + 1) * HEAD_DIM, pl.ds(k0, KEY_CHUNK)],
                         jnp.concatenate(probs, axis=1), preferred_element_type=F32)
            acc_s[g] = jnp.concatenate(alphas, axis=1) * acc_s[g] + pv
        return carry

    lax.fori_loop(0, n_chunks, attend, 0)
    for g in range(N_KV):
        inv = jnp.concatenate([1.0 / l_s[pl.ds(g * GROUP + j, 1), :] for j in range(GROUP)], axis=1)
        o_ref[g * HEAD_DIM:(g + 1) * HEAD_DIM, :] = acc_s[g] * inv


def _sparse_attention_t(mode, qt, k, vt, strip_t, extra, k_sel=0):
    bsz, nq = qt.shape[:2]
    lp = k.shape[1]
    kv_dim = N_KV * HEAD_DIM
    cols = GROUP * QT
    per_q = lambda *tail: pl.BlockSpec((None, None) + tail, lambda b, i: (b, i) + (0,) * len(tail))
    per_b = lambda *tail: pl.BlockSpec((None,) + tail, lambda b, i: (b,) + (0,) * len(tail))
    in_specs = [per_q(kv_dim, cols), per_b(lp, kv_dim), per_b(kv_dim, lp),
                pl.BlockSpec((N_KV, STRIP_W, cols), lambda b, i: (0, 0, 0))]
    scratch = [pltpu.VMEM((kv_dim, N_KV * cols), BF16),
               pltpu.VMEM((N_HEADS, QT), F32),
               pltpu.VMEM((N_HEADS, QT), F32),
               pltpu.VMEM((N_KV, HEAD_DIM, cols), F32)]
    if mode == "dsa":
        qit, wit, ki = extra
        in_specs += [per_q(IDX_DIM, IDX_HEADS * QT), per_q(1, IDX_HEADS * QT), per_b(lp, IDX_DIM)]
        scratch.append(pltpu.VMEM((lp, QT), I32))
    else:
        (km,) = extra
        in_specs.append(per_b(N_KV, km.shape[2], HEAD_DIM))
    return pl.pallas_call(
        functools.partial(_attn_t_kernel, mode, lp, k_sel),
        grid=(bsz, nq),
        in_specs=in_specs,
        out_specs=per_q(kv_dim, cols),
        out_shape=jax.ShapeDtypeStruct((bsz, nq, kv_dim, cols), F32),
        scratch_shapes=scratch,
        compiler_params=_cparams(("parallel", "arbitrary")),
        name=mode + "_attention_t",
    )(qt, k, vt, strip_t, *extra)


def _top_rows(s, k, payload=None, rank=None):
    rid = lax.broadcasted_iota(I32, s.shape, 0) if rank is None else rank
    vals, ids = [], []
    for _ in range(k):
        mx = jnp.max(s, axis=0, keepdims=True)
        first = jnp.min(jnp.where(s == mx, rid, jnp.iinfo(jnp.int32).max), axis=0, keepdims=True)
        hit = rid == first
        vals.append(mx)
        if payload is None:
            ids.append(first)
        else:
            ids.append(jnp.max(jnp.where(hit, payload, -1), axis=0, keepdims=True))
        s = jnp.where(hit, -jnp.inf, s)
    return jnp.concatenate(vals, axis=0), jnp.concatenate(ids, axis=0)


def _peer_route_kernel(x_ref, wqt_ref, keys_ref, row_ref, par_ref, g_ref):
    xb = x_ref[...].astype(BF16)
    qt = lax.dot_general(wqt_ref[...], xb, (((1,), (1,)), ((), ())),
                         preferred_element_type=F32)
    n_t = x_ref.shape[0]
    piece = lax.broadcasted_iota(I32, (11 * 8, n_t), 0) // 8
    within = lax.broadcasted_iota(I32, (11 * 8, n_t), 0) % 8
    by_b = (piece >= 2) & (piece < 10)
    ca = jnp.where(piece < 2, 0, jnp.where(by_b, within, 8 + within))
    cb = jnp.where(piece == 0, within, jnp.where(piece == 1, 8 + within, jnp.where(by_b, piece - 2, 0)))
    cand_ok = ((ca + 1) * (cb + 1) <= PEER_TOPK) & jnp.logical_not(by_b & (within == 0))
    cand_rank = ca * PEER_TOPK + cb

    def pieces(first, second, combine):
        out = [combine(first[0:1], second[0:8]), combine(first[0:1], second[8:16])]
        out += [combine(first[0:8], second[b:b + 1]) for b in range(8)]
        out.append(combine(first[8:16], second[0:1]))
        return jnp.concatenate(out, axis=0)

    e_rows, g_rows = [], []
    for h in range(PEER_HEADS):
        sv, si = [], []
        for c in range(2):
            hc = 2 * h + c
            qhc = qt[hc * PEER_HALF:(hc + 1) * PEER_HALF].astype(BF16)
            s = jnp.dot(keys_ref[hc], qhc, preferred_element_type=F32)
            v_, i_ = _top_rows(s, PEER_TOPK)
            sv.append(v_)
            si.append(i_)
        cand = jnp.where(cand_ok, pieces(sv[0], sv[1], lambda u, w: u + w), -jnp.inf)
        cidx = pieces(si[0], si[1], lambda u, w: u * PEER_NKEYS + w)
        gv, ge = _top_rows(cand, PEER_TOPK, payload=cidx, rank=cand_rank)
        ex = jnp.exp(gv - gv[0:1])
        g_rows.append(ex / jnp.sum(ex, axis=0, keepdims=True))
        e_rows.append(ge)
    e_t = jnp.concatenate(e_rows, axis=0).T
    row_ref[...] = jnp.right_shift(e_t, 1) * 8
    par_ref[...] = e_t & 1
    g_ref[...] = jnp.concatenate(g_rows, axis=0).T


def _peer_route(x, wqt_bf16, keys_bf16, tb):
    m, d = x.shape
    out = pl.BlockSpec((tb, PEER_PAIRS), lambda i: (i, 0))
    return pl.pallas_call(
        _peer_route_kernel,
        grid=(m // tb,),
        in_specs=[pl.BlockSpec((tb, d), lambda i: (i, 0)),
                  pl.BlockSpec(wqt_bf16.shape, lambda i: (0, 0)),
                  pl.BlockSpec(keys_bf16.shape, lambda i: (0, 0, 0))],
        out_specs=[out, out, out],
        out_shape=[jax.ShapeDtypeStruct((m, PEER_PAIRS), I32),
                   jax.ShapeDtypeStruct((m, PEER_PAIRS), I32),
                   jax.ShapeDtypeStruct((m, PEER_PAIRS), F32)],
        compiler_params=_cparams(("parallel",)),
        name="peer_route",
    )(x, wqt_bf16, keys_bf16)


FOLD_PAIRS = 32
UP_SLOTS = 4
TILE_ROWS = 8
HI_MASK = 0xFFFF0000


def _pack_table(w):
    n, d = w.shape
    bits = lax.bitcast_convert_type(w.astype(BF16), jnp.uint16).astype(jnp.uint32)
    bits = bits.reshape(n // 2, 2, d // 128, 128)
    return (bits[:, 0] | (bits[:, 1] << 16)).astype(jnp.uint32).reshape(n // 2 * (d // 128), 128)


def _load_tile(tbl_ref, row_ref, idx):
    return tbl_ref[pl.ds(pl.multiple_of(row_ref[idx], TILE_ROWS), TILE_ROWS), :]


def _peer_up_kernel(tb, row_ref, par_ref, x_ref, g_ref, fold_ref, tbl_ref, c_ref, prod_s):
    diag = (lax.broadcasted_iota(I32, (PEER_PAIRS, PEER_PAIRS), 0)
            == lax.broadcasted_iota(I32, (PEER_PAIRS, PEER_PAIRS), 1))

    def gather(t, slot):
        xt = x_ref[t]
        rows_t = row_ref.at[pl.ds(t * PEER_PAIRS, PEER_PAIRS)]
        for p in range(0, PEER_PAIRS, 2):
            w0 = _load_tile(tbl_ref, rows_t, p)
            w1 = _load_tile(tbl_ref, rows_t, p + 1)
            even = [pltpu.bitcast(jnp.left_shift(w, jnp.uint32(16)), F32) * xt for w in (w0, w1)]
            odd = [pltpu.bitcast(w & jnp.uint32(HI_MASK), F32) * xt for w in (w0, w1)]
            rows = pl.ds(p * TILE_ROWS, 2 * TILE_ROWS)
            prod_s[slot, rows, 0:128] = jnp.concatenate(even, axis=0).astype(BF16)
            prod_s[slot, rows, 128:256] = jnp.concatenate(odd, axis=0).astype(BF16)

    def reduce(t, slot):
        depth = FOLD_PAIRS * TILE_ROWS
        parts = [jnp.dot(fold_ref[...], prod_s[slot, pl.ds(i * depth, depth), :], preferred_element_type=F32)
                 for i in range(PEER_PAIRS // FOLD_PAIRS)]
        part = jnp.concatenate(parts, axis=0)
        rows = []
        for half in range(2):
            col = jnp.sum(part[:, half * 128:(half + 1) * 128], axis=1, keepdims=True)
            rows.append(jnp.sum(jnp.where(diag, col, 0.0), axis=0, keepdims=True))
        a = jnp.where(par_ref[pl.ds(t, 1), :] == 1, rows[1], rows[0])
        act = 0.5 * a * (1.0 + lax.erf(a * (2.0 ** -0.5)))
        c_ref[pl.ds(t, 1), :] = g_ref[pl.ds(t, 1), :] * act

    prod_s[UP_SLOTS:2 * UP_SLOTS] = jnp.zeros((UP_SLOTS,) + prod_s.shape[1:], BF16)

    def step(k, carry):
        t0 = 2 * UP_SLOTS * k
        for j in range(UP_SLOTS):
            gather(t0 + j, j)
            reduce(jnp.maximum(t0 - UP_SLOTS + j, 0), UP_SLOTS + j)
        for j in range(UP_SLOTS):
            gather(t0 + UP_SLOTS + j, UP_SLOTS + j)
            reduce(t0 + j, j)
        return carry

    lax.fori_loop(0, tb // (2 * UP_SLOTS), step, 0)
    for j in range(UP_SLOTS):
        reduce(tb - UP_SLOTS + j, UP_SLOTS + j)


DOWN_GROUP = 8


def _peer_down_kernel(tb, row_ref, par_ref, c_ref, spread_ref, tbl_ref, f_ref, tile_s, coef_s):
    width = PEER_PAIRS * 2 * TILE_ROWS
    lane = lax.broadcasted_iota(I32, (TILE_ROWS, width), 1)
    own_row = ((lane & (2 * TILE_ROWS - 1)) >> 1) == lax.broadcasted_iota(I32, (TILE_ROWS, width), 0)
    lane_par = (lax.broadcasted_iota(I32, (DOWN_GROUP, width), 1) & 1).astype(F32)

    def gather(t, slot):
        rows_t = row_ref.at[pl.ds(t * PEER_PAIRS, PEER_PAIRS)]
        for p in range(PEER_PAIRS):
            tile_s[slot, pl.ds(p * TILE_ROWS, TILE_ROWS), :] = _load_tile(tbl_ref, rows_t, p)

    def apply(t, j, slot):
        tiles = pltpu.bitcast(tile_s[slot], BF16)
        lhs = [jnp.where(own_row, coef_s[pl.ds(part * DOWN_GROUP + j, 1), :], 0.0) for part in range(2)]
        out = jnp.dot(jnp.concatenate(lhs, axis=0).astype(BF16), tiles, preferred_element_type=F32)
        f_ref[t] = out[0:TILE_ROWS] + out[TILE_ROWS:2 * TILE_ROWS]

    def group(gi, carry):
        t0 = pl.multiple_of(gi * DOWN_GROUP, DOWN_GROUP)
        gather(t0, 0)
        c = c_ref[pl.ds(t0, DOWN_GROUP), :]
        c_hi = c.astype(BF16)
        c_lo = (c - c_hi.astype(F32)).astype(BF16)
        par = par_ref[pl.ds(t0, DOWN_GROUP), :].astype(BF16)
        rep = jnp.dot(jnp.concatenate([c_hi, c_lo, par], axis=0), spread_ref[...],
                      preferred_element_type=F32)
        keep = rep[2 * DOWN_GROUP:] == lane_par
        coef_s[0:DOWN_GROUP, :] = jnp.where(keep, rep[0:DOWN_GROUP], 0.0)
        coef_s[DOWN_GROUP:2 * DOWN_GROUP, :] = jnp.where(keep, rep[DOWN_GROUP:2 * DOWN_GROUP], 0.0)
        for j in range(DOWN_GROUP):
            if j + 1 < DOWN_GROUP:
                gather(t0 + j + 1, (j + 1) % 2)
            apply(t0 + j, j, j % 2)
        return carry

    lax.fori_loop(0, tb // DOWN_GROUP, group, 0)


def _peer_experts(x, row, par, g, u_tbl, v_tbl, tb):
    m, d = x.shape
    x3 = x.reshape(m, d // 128, 128)
    flat = row.reshape(m * PEER_PAIRS)
    smem = pl.BlockSpec((tb * PEER_PAIRS,), lambda i: (i,), memory_space=pltpu.SMEM)
    vrow = pl.BlockSpec((tb, PEER_PAIRS), lambda i: (i, 0))
    tile = pl.BlockSpec((tb, d // 128, 128), lambda i: (i, 0, 0))
    resident = pl.BlockSpec(memory_space=pltpu.VMEM)
    const = lambda a: pl.BlockSpec(a.shape, lambda i: (0, 0))
    fold = jnp.asarray(np.arange(FOLD_PAIRS * TILE_ROWS)[None, :] // TILE_ROWS == np.arange(FOLD_PAIRS)[:, None], BF16)
    width = PEER_PAIRS * 2 * TILE_ROWS
    spread = jnp.asarray(np.arange(width)[None, :] // (2 * TILE_ROWS) == np.arange(PEER_PAIRS)[:, None], BF16)
    c = pl.pallas_call(
        functools.partial(_peer_up_kernel, tb),
        grid=(m // tb,),
        in_specs=[smem, vrow, tile, vrow, const(fold), resident],
        out_specs=vrow,
        out_shape=jax.ShapeDtypeStruct((m, PEER_PAIRS), F32),
        scratch_shapes=[pltpu.VMEM((2 * UP_SLOTS, PEER_PAIRS * TILE_ROWS, 256), BF16)],
        compiler_params=_cparams(("arbitrary",)),
        name="peer_up",
    )(flat, par, x3, g, fold, u_tbl)
    f = pl.pallas_call(
        functools.partial(_peer_down_kernel, tb),
        grid=(m // tb,),
        in_specs=[smem, vrow, vrow, const(spread), resident],
        out_specs=tile,
        out_shape=jax.ShapeDtypeStruct((m, d // 128, 128), F32),
        scratch_shapes=[pltpu.VMEM((2, PEER_PAIRS * TILE_ROWS, 128), jnp.uint32),
                        pltpu.VMEM((2 * DOWN_GROUP, width), F32)],
        compiler_params=_cparams(("arbitrary",)),
        name="peer_down",
    )(flat, par, c, spread, v_tbl)
    return f.reshape(m, d)


def _stack_heads(a, qb, n_lead):
    bsz, q, n, hd = a.shape
    n_in = n // n_lead
    a = a.reshape(bsz, q // qb, qb, n_lead, n_in, hd)
    return a.transpose(0, 1, 3, 4, 2, 5).reshape(bsz, q // qb, n_lead, n_in * qb, hd)


def _unstack_heads(o, qb):
    bsz, nq = o.shape[:2]
    o = o.reshape(bsz, nq, N_KV, GROUP, qb, HEAD_DIM).transpose(0, 1, 4, 2, 3, 5)
    return o.reshape(bsz * nq * qb, N_HEADS * HEAD_DIM)


def _run_trunk(x, p, past, prm, qb, tm, tb_route, tb_exp):
    bsz, q_len, d = x.shape
    depth = prm["w_o"].shape[0]
    m = bsz * q_len
    tm, tb_route, tb_exp = min(tm, m), min(tb_route, m), min(tb_exp, m)
    alpha = (2 * depth) ** 0.25
    attn_dim = N_HEADS * HEAD_DIM
    kv_dim = N_KV * HEAD_DIM
    if past is None:
        q_pos0 = 0
        n_pages = q_len // PAGE
        table = jnp.arange(bsz * n_pages, dtype=I32).reshape(bsz, n_pages)
    else:
        cache_k, cache_v, cache_ki, table = past
        q_pos0 = table.shape[1] * PAGE
    on_lanes = past is None and q_len % QT == 0
    if on_lanes:
        qb = QT
    nq = q_len // qb
    strip = _bias_strip(prm["rel_bias"], qb).reshape(N_KV, GROUP, qb, STRIP_W)
    if on_lanes:
        strip = strip.transpose(0, 3, 1, 2).reshape(N_KV, STRIP_W, GROUP * qb)
    else:
        strip = strip.reshape(N_KV, GROUP * qb, STRIP_W)
    xf = x.reshape(m, d)
    ks, vs, kis = [], [], []

    def relayout(new_rows, pool):
        if past is None:
            return _relayout(new_rows.reshape(bsz * n_pages, PAGE, new_rows.shape[2]), table, None)
        pad = jnp.zeros((bsz, STEP_KEYS - q_len, new_rows.shape[2]), F32)
        return _relayout(pool, table, jnp.concatenate([new_rows, pad], axis=1))

    for l in range(depth):
        dsa = l % 2 == 0
        w_in = prm["w_in_dsa"][l // 2] if dsa else prm["w_in_moba"][l // 2]
        n_in = w_in.shape[1]
        n_pad = -(-n_in // 128) * 128
        w_in = jnp.pad(w_in, ((0, 0), (0, n_pad - n_in))).astype(BF16)
        proj = _matmul(xf, w_in, tm)
        qh = proj[:, :attn_dim].reshape(bsz, q_len, N_HEADS, HEAD_DIM)
        k = proj[:, attn_dim:attn_dim + kv_dim].reshape(bsz, q_len, kv_dim)
        v = proj[:, attn_dim + kv_dim:attn_dim + 2 * kv_dim].reshape(bsz, q_len, kv_dim)
        ks.append(k.reshape(bsz, q_len, N_KV, HEAD_DIM))
        vs.append(v.reshape(bsz, q_len, N_KV, HEAD_DIM))
        kt, kr, ksum = relayout(k, None if past is None else cache_k[l].reshape(-1, PAGE, kv_dim))
        vt, vr, _ = relayout(v, None if past is None else cache_v[l].reshape(-1, PAGE, kv_dim))
        if on_lanes:
            q_in = qh.reshape(bsz, nq, qb, N_KV, GROUP, HEAD_DIM).transpose(0, 1, 3, 5, 4, 2)
            q_in = q_in.reshape(bsz, nq, kv_dim, GROUP * qb)
        else:
            kt = kt.reshape(bsz, N_KV, HEAD_DIM, kt.shape[2])
            q_in = _stack_heads(qh, qb, N_KV)
        if dsa:
            off = attn_dim + 2 * kv_dim
            qi = proj[:, off:off + IDX_HEADS * IDX_DIM].reshape(bsz, q_len, IDX_HEADS, IDX_DIM)
            off += IDX_HEADS * IDX_DIM
            ki = proj[:, off:off + IDX_DIM].reshape(bsz, q_len, IDX_DIM)
            wi = proj[:, off + IDX_DIM:off + IDX_DIM + IDX_HEADS].reshape(bsz, q_len, IDX_HEADS, 1)
            kis.append(ki)
            kit, kir, _ = relayout(ki, None if past is None else cache_ki[l // 2])
            k_sel = min(IDX_TOPK, (q_pos0 + q_len) // 4)
            if on_lanes:
                qi_t = qi.reshape(bsz, nq, qb, IDX_HEADS, IDX_DIM).transpose(0, 1, 4, 3, 2)
                qi_t = qi_t.reshape(bsz, nq, IDX_DIM, IDX_HEADS * qb)
                wi_t = wi.reshape(bsz, nq, qb, IDX_HEADS).transpose(0, 1, 3, 2).reshape(bsz, nq, 1, IDX_HEADS * qb)
                o = _sparse_attention_t("dsa", q_in, kr, vt, strip, (qi_t, wi_t, kir), k_sel)
            else:
                qi_st = _stack_heads(qi, qb, 1)[:, :, 0]
                wi_st = _stack_heads(wi, qb, 1)[:, :, 0]
                o = _sparse_attention("dsa", q_in, kt, vr, strip, (qi_st, wi_st, kit), qb, q_pos0, k_sel)
        else:
            nb = ksum.shape[1]
            kmean = (ksum * (1.0 / MOBA_BLOCK)).reshape(bsz, nb, N_KV, HEAD_DIM)
            if on_lanes:
                km = jnp.pad(kmean.transpose(0, 2, 1, 3), ((0, 0), (0, 0), (0, -nb % 8), (0, 0)))
                o = _sparse_attention_t("moba", q_in, kr, vt, strip, (km,))
            else:
                kmt = jnp.pad(kmean.transpose(0, 2, 3, 1), ((0, 0), (0, 0), (0, 0), (0, -nb % 128)))
                o = _sparse_attention("moba", q_in, kt, vr, strip, (kmt,), qb, q_pos0)
        if on_lanes:
            o = o.reshape(bsz, nq, N_KV, HEAD_DIM, GROUP, qb).transpose(0, 1, 5, 2, 4, 3)
            o = o.reshape(m, attn_dim)
        else:
            o = _unstack_heads(o, qb)
        xf = _attn_out(o, prm["w_o"][l].astype(BF16), xf, prm["ln_g"][l, 0][None], prm["ln_b"][l, 0][None],
                       alpha, tm)
        row, par, g = _peer_route(xf, prm["peer_wq"][l].T.astype(BF16),
                                 prm["peer_keys"][l].reshape(2 * PEER_HEADS, PEER_NKEYS, PEER_HALF).astype(BF16),
                                 tb_route)
        f = _peer_experts(xf, row, par, g, _pack_table(prm["peer_u"][l]), _pack_table(prm["peer_v"][l]), tb_exp)
        xf = _layer_tail(xf, f, prm["ln_g"][l, 1][None], prm["ln_b"][l, 1][None],
                         prm["gate_w"][l].astype(BF16), prm["gate_b"][l][None],
                         p[l].reshape(m, -1), prm["ple_w"][l].astype(BF16), alpha, tm)
    return xf.reshape(bsz, q_len, d), jnp.stack(ks), jnp.stack(vs), jnp.stack(kis)


def kernel(x_prompt, x_sample, cache_k, cache_v, cache_ki, page_table, p_prompt, p_sample, w_in_dsa, w_in_moba, w_o, rel_bias, ln_g, ln_b, peer_wq, peer_keys, peer_u, peer_v, gate_w, gate_b, ple_w):
    prm = {
        "w_in_dsa": w_in_dsa, "w_in_moba": w_in_moba, "w_o": w_o, "rel_bias": rel_bias,
        "ln_g": ln_g, "ln_b": ln_b, "peer_wq": peer_wq, "peer_keys": peer_keys,
        "peer_u": peer_u, "peer_v": peer_v, "gate_w": gate_w, "gate_b": gate_b, "ple_w": ple_w,
    }
    q_sample = x_sample.shape[1]
    y_prompt, k_prompt, v_prompt, ki_prompt = _run_trunk(
        x_prompt, p_prompt, None, prm, qb=128, tm=512, tb_route=256, tb_exp=128)
    y_sample, k_sample, v_sample, ki_sample = _run_trunk(
        x_sample, p_sample, (cache_k, cache_v, cache_ki, page_table), prm,
        qb=q_sample, tm=256, tb_route=256, tb_exp=64)
    return (y_prompt, y_sample, k_prompt, v_prompt, ki_prompt, k_sample, v_sample, ki_sample)
```

```python
import functools
import math

import numpy as np
import jax
import jax.numpy as jnp
from jax import lax
from jax.experimental import pallas as pl
from jax.experimental.pallas import tpu as pltpu

F32 = jnp.float32
BF16 = jnp.bfloat16
I32 = jnp.int32

N_HEADS = 16
HEAD_DIM = 64
N_KV = 4
GROUP = 4
IDX_HEADS = 8
IDX_DIM = 64
IDX_TOPK = 256
MOBA_BLOCK = 256
MOBA_TOPK = 3
REL_BUCKETS = 32
REL_MAX_DIST = 128
PEER_HEADS = 8
PEER_NKEYS = 128
PEER_HALF = 128
PEER_TOPK = 16
PEER_PAIRS = PEER_HEADS * PEER_TOPK
LN_EPS = 1e-5
PAGE = 128

KEY_CHUNK = 256
STRIP_W = 768
STRIP_ORIGIN = 512
NEG = -1e30
INT_MIN = -2147483648
VMEM_LIMIT = 56 * 1024 * 1024


def _cparams(sem):
    return pltpu.CompilerParams(dimension_semantics=sem, vmem_limit_bytes=VMEM_LIMIT)


def _mm_kernel(a_ref, b_ref, o_ref):
    o_ref[...] = jnp.dot(a_ref[...].astype(BF16), b_ref[...], preferred_element_type=F32)


def _matmul(a, b_bf16, tm):
    m, k = a.shape
    n = b_bf16.shape[1]
    return pl.pallas_call(
        _mm_kernel,
        grid=(m // tm,),
        in_specs=[pl.BlockSpec((tm, k), lambda i: (i, 0)),
                  pl.BlockSpec((k, n), lambda i: (0, 0))],
        out_specs=pl.BlockSpec((tm, n), lambda i: (i, 0)),
        out_shape=jax.ShapeDtypeStruct((m, n), F32),
        compiler_params=_cparams(("parallel",)),
        name="in_proj",
    )(a, b_bf16)


def _layer_norm(y, g, b):
    mu = jnp.mean(y, axis=-1, keepdims=True)
    yc = y - mu
    var = jnp.mean(yc * yc, axis=-1, keepdims=True)
    return yc * lax.rsqrt(var + LN_EPS) * g + b


def _attn_out_kernel(alpha, o_ref, w_ref, x_ref, g_ref, b_ref, y_ref):
    y = alpha * x_ref[...] + jnp.dot(o_ref[...].astype(BF16), w_ref[...],
                                     preferred_element_type=F32)
    y_ref[...] = _layer_norm(y, g_ref[...], b_ref[...])


def _attn_out(o, w_bf16, x, g, b, alpha, tm):
    m, d = x.shape
    row = pl.BlockSpec((tm, d), lambda i: (i, 0))
    vec = pl.BlockSpec((1, d), lambda i: (0, 0))
    return pl.pallas_call(
        functools.partial(_attn_out_kernel, alpha),
        grid=(m // tm,),
        in_specs=[row, pl.BlockSpec((d, d), lambda i: (0, 0)), row, vec, vec],
        out_specs=row,
        out_shape=jax.ShapeDtypeStruct((m, d), F32),
        compiler_params=_cparams(("parallel",)),
        name="attn_out_ln",
    )(o, w_bf16, x, g, b)


def _layer_tail_kernel(alpha, x_ref, f_ref, g_ref, b_ref, gw_ref, gb_ref, p_ref, pw_ref, y_ref):
    x2 = _layer_norm(alpha * x_ref[...] + f_ref[...], g_ref[...], b_ref[...])
    z = jnp.dot(x2.astype(BF16), gw_ref[...], preferred_element_type=F32) + gb_ref[...]
    gate = 1.0 / (1.0 + jnp.exp(-z))
    e = jnp.dot(p_ref[...].astype(BF16), pw_ref[...], preferred_element_type=F32)
    y_ref[...] = x2 + gate * e


def _layer_tail(x, f, g, b, gw_bf16, gb, p, pw_bf16, alpha, tm):
    m, d = x.shape
    pd = p.shape[1]
    row = pl.BlockSpec((tm, d), lambda i: (i, 0))
    vec = pl.BlockSpec((1, d), lambda i: (0, 0))
    return pl.pallas_call(
        functools.partial(_layer_tail_kernel, alpha),
        grid=(m // tm,),
        in_specs=[row, row, vec, vec, pl.BlockSpec((d, d), lambda i: (0, 0)), vec,
                  pl.BlockSpec((tm, pd), lambda i: (i, 0)),
                  pl.BlockSpec((pd, d), lambda i: (0, 0))],
        out_specs=row,
        out_shape=jax.ShapeDtypeStruct((m, d), F32),
        compiler_params=_cparams(("parallel",)),
        name="layer_tail",
    )(x, f, g, b, gw_bf16, gb, p, pw_bf16)


STEP_PAGES = 8
STEP_KEYS = STEP_PAGES * PAGE
STEP_CHUNKS = STEP_KEYS // KEY_CHUNK


def _relayout_kernel(n_pool_steps, has_new, pages_t, want, *refs):
    page_refs = refs[1:1 + STEP_PAGES]
    new_ref = refs[1 + STEP_PAGES] if has_new else None
    outs = dict(zip(want, refs[len(refs) - len(want):]))
    j = pl.program_id(1)
    need_rows = "x" in want or "sums" in want

    def emit(rows, cols):
        if "xt" in want:
            outs["xt"][...] = (rows.T if cols is None else cols).astype(BF16)
        if need_rows and rows is None:
            rows = cols.T
        if "x" in want:
            outs["x"][...] = rows.astype(BF16)
        if "sums" in want:
            for i in range(STEP_CHUNKS):
                outs["sums"][pl.ds(i, 1), :] = jnp.sum(rows[i * KEY_CHUNK:(i + 1) * KEY_CHUNK], axis=0, keepdims=True)

    def from_pages():
        if pages_t:
            emit(None, jnp.concatenate([r[...] for r in page_refs], axis=1))
        else:
            emit(jnp.concatenate([r[...] for r in page_refs], axis=0), None)

    if has_new:
        pl.when(j < n_pool_steps)(from_pages)

        @pl.when(j >= n_pool_steps)
        def _():
            emit(new_ref[...], None)
    else:
        from_pages()


def _relayout(pool, table, new, pages_t, want):
    bsz, n_pages = table.shape
    w = pool.shape[1] if pages_t else pool.shape[2]
    n_pool_steps = n_pages // STEP_PAGES
    n_steps = n_pool_steps + (1 if new is not None else 0)
    lp = n_steps * STEP_KEYS
    last = n_pool_steps - 1

    def page_map(i):
        return lambda b, j, tbl: (tbl[b, STEP_PAGES * jnp.minimum(j, last) + i], 0, 0)

    page_block = (None, w, PAGE) if pages_t else (None, PAGE, w)
    in_specs = [pl.BlockSpec(page_block, page_map(i)) for i in range(STEP_PAGES)]
    args = [pool] * STEP_PAGES
    if new is not None:
        in_specs.append(pl.BlockSpec((None, STEP_KEYS, w), lambda b, j, tbl: (b, 0, 0)))
        args.append(new)
    specs = {"xt": (pl.BlockSpec((None, w, STEP_KEYS), lambda b, j, tbl: (b, 0, j)),
                    jax.ShapeDtypeStruct((bsz, w, lp), BF16)),
             "x": (pl.BlockSpec((None, STEP_KEYS, w), lambda b, j, tbl: (b, j, 0)),
                   jax.ShapeDtypeStruct((bsz, lp, w), BF16)),
             "sums": (pl.BlockSpec((None, None, STEP_CHUNKS, w), lambda b, j, tbl: (b, j, 0, 0)),
                      jax.ShapeDtypeStruct((bsz, n_steps, STEP_CHUNKS, w), F32))}
    outs = pl.pallas_call(
        functools.partial(_relayout_kernel, n_pool_steps, new is not None, pages_t, want),
        grid_spec=pltpu.PrefetchScalarGridSpec(
            num_scalar_prefetch=1,
            grid=(bsz, n_steps),
            in_specs=in_specs,
            out_specs=[specs[name][0] for name in want]),
        out_shape=[specs[name][1] for name in want],
        compiler_params=_cparams(("parallel", "arbitrary")),
        name="kv_relayout",
    )(table, *args)
    outs = dict(zip(want, outs))
    if "sums" in outs:
        outs["sums"] = outs["sums"].reshape(bsz, n_steps * STEP_CHUNKS, w)
    return outs


def _bucket_table():
    n = np.arange(REL_MAX_DIST + 1)
    exact = REL_BUCKETS // 2
    nf = np.maximum(n, 1).astype(np.float32)
    large = exact + (np.log(nf / np.float32(exact)) / np.float32(math.log(REL_MAX_DIST / exact))
                     * np.float32(REL_BUCKETS - exact)).astype(np.int32)
    return np.where(n < exact, n, np.minimum(large, REL_BUCKETS - 1)).astype(np.int32)


def _bias_strip_kernel(bk_ref, rb_ref, o_ref):
    bk = bk_ref[...]
    for h in range(N_HEADS):
        acc = jnp.zeros(bk.shape, F32)
        for bkt in range(REL_BUCKETS):
            acc = jnp.where(bk == bkt, rb_ref[bkt, h], acc)
        o_ref[h] = acc


def _bias_strip(rel_bias, qb):
    qi = np.arange(qb)[:, None]
    z = np.arange(STRIP_W)[None, :]
    dist = np.clip(qi - z + STRIP_ORIGIN, 0, REL_MAX_DIST)
    bk = jnp.asarray(_bucket_table()[dist])
    out = pl.pallas_call(
        _bias_strip_kernel,
        in_specs=[pl.BlockSpec(memory_space=pltpu.VMEM), pl.BlockSpec(memory_space=pltpu.SMEM)],
        out_specs=pl.BlockSpec(memory_space=pltpu.VMEM),
        out_shape=jax.ShapeDtypeStruct((N_HEADS, qb, STRIP_W), F32),
        name="bias_strip",
    )(bk, rel_bias)
    return out


def _attn_kernel(mode, qb, lp, q_pos0, k_sel, *refs):
    if mode == "dsa":
        (q_ref, kt_ref, v_ref, strip_ref, qi_ref, wi_ref, kit_ref, o_ref,
         qs_s, m_s, l_s, acc_s, key_s) = refs
    else:
        (q_ref, kt_ref, v_ref, strip_ref, kmt_ref, o_ref,
         qs_s, m_s, l_s, acc_s) = refs
    rows = GROUP * qb
    i = pl.program_id(1)
    q0 = q_pos0 + i * qb
    c_max = (q0 + qb - 1) // KEY_CHUNK
    n_chunks = c_max + 1
    qpos = q0 + lax.broadcasted_iota(I32, (qb, 1), 0)
    lane = lax.broadcasted_iota(I32, (1, KEY_CHUNK), 1)
    scale = HEAD_DIM ** -0.5

    for g in range(N_KV):
        qs_s[g] = (q_ref[g] * scale).astype(BF16)
    m_s[...] = jnp.full(m_s.shape, NEG, F32)
    l_s[...] = jnp.zeros(l_s.shape, F32)
    acc_s[...] = jnp.zeros(acc_s.shape, F32)

    if mode == "dsa":
        qi = qi_ref[...].astype(BF16)
        wi = wi_ref[...]

        def score_chunk(c, carry):
            k0 = pl.multiple_of(c * KEY_CHUNK, KEY_CHUNK)
            s = jnp.dot(qi, kit_ref[:, pl.ds(k0, KEY_CHUNK)], preferred_element_type=F32)
            s = jnp.maximum(s, 0.0) * wi
            sc = s[0:qb]
            for h in range(1, IDX_HEADS):
                sc = sc + s[h * qb:(h + 1) * qb]
            sc = jnp.where(k0 + lane <= qpos, sc, -jnp.inf)
            bits = pltpu.bitcast(sc, I32)
            bits = jnp.where(bits == INT_MIN, 0, bits)
            key_s[:, pl.ds(k0, KEY_CHUNK)] = jnp.where(bits < 0, bits ^ 0x7FFFFFFF, bits)
            return carry

        lax.fori_loop(0, n_chunks, score_chunk, 0)

        def count(pred):
            def body(c, acc):
                k0 = pl.multiple_of(c * KEY_CHUNK, KEY_CHUNK)
                return acc + pred(key_s[:, pl.ds(k0, KEY_CHUNK)], k0).astype(I32)
            acc = lax.fori_loop(0, n_chunks, body, jnp.zeros((qb, KEY_CHUNK), I32))
            return jnp.sum(acc, axis=1, keepdims=True)

        def bit_step(bi, thr):
            cand = thr + jnp.left_shift(jnp.int32(1), 31 - bi)
            cnt = count(lambda k, k0: k >= cand)
            return jnp.where(cnt >= k_sel, cand, thr)

        thr = lax.fori_loop(0, 32, bit_step, jnp.full((qb, 1), INT_MIN, I32))
        n_gt = count(lambda k, k0: k > thr)
        n_ge = count(lambda k, k0: k >= thr)
        need = k_sel - n_gt

        def tie_limit():
            n_bits = max(1, (lp - 1).bit_length())

            def idx_step(bi, lo):
                cand = lo + jnp.left_shift(jnp.int32(1), n_bits - 1 - bi)
                cnt = count(lambda k, k0: (k == thr) & (k0 + lane <= cand))
                return jnp.where(cnt < need, cand, lo)

            lo = lax.fori_loop(0, n_bits, idx_step, jnp.full((qb, 1), -1, I32))
            return lo + 1

        tie_idx = lax.cond(jnp.max(n_ge - n_gt - need) > 0, tie_limit,
                           lambda: jnp.full((qb, 1), lp, I32))
    else:
        blk = lax.broadcasted_iota(I32, (qb, kmt_ref.shape[2]), 1)
        sel_blocks = []
        for g in range(N_KV):
            qg = q_ref[g]
            qsum = qg[0:qb]
            for j in range(1, GROUP):
                qsum = qsum + qg[j * qb:(j + 1) * qb]
            gate = jnp.dot(qsum, kmt_ref[g], preferred_element_type=F32,
                           precision=lax.Precision.HIGHEST)
            gate = jnp.where(blk < c_max, gate, -jnp.inf)
            sel = jnp.zeros(blk.shape, jnp.bool_)
            for _ in range(MOBA_TOPK):
                mx = jnp.max(gate, axis=1, keepdims=True)
                first = jnp.min(jnp.where(gate == mx, blk, blk.shape[1]), axis=1, keepdims=True)
                hit = blk == first
                sel = sel | (hit & (blk < c_max))
                gate = jnp.where(hit, -jnp.inf, gate)
            sel_blocks.append(sel.astype(I32))

    def attend(c, carry):
        k0 = pl.multiple_of(c * KEY_CHUNK, KEY_CHUNK)
        kidx = k0 + lane
        causal = kidx <= qpos
        off = pl.multiple_of(jnp.clip(STRIP_ORIGIN - (q0 - k0), 0, STRIP_ORIGIN), 128)
        if mode == "dsa":
            k = key_s[:, pl.ds(k0, KEY_CHUNK)]
            mask1 = ((k > thr) | ((k == thr) & (kidx <= tie_idx))) & causal
            mask = jnp.concatenate([mask1] * GROUP, axis=0)

        def update(g, mask):
            s = jnp.dot(qs_s[g], kt_ref[g, :, pl.ds(k0, KEY_CHUNK)], preferred_element_type=F32)
            s = s + strip_ref[g, :, pl.ds(off, KEY_CHUNK)]
            s = jnp.where(mask, s, NEG)
            m_old = m_s[g]
            m_new = jnp.maximum(m_old, jnp.max(s, axis=1, keepdims=True))
            alpha = jnp.exp(m_old - m_new)
            p = jnp.where(mask, jnp.exp(s - m_new), 0.0)
            l_s[g] = alpha * l_s[g] + jnp.sum(p, axis=1, keepdims=True)
            acc_s[g] = alpha * acc_s[g] + jnp.dot(p.astype(BF16), v_ref[pl.ds(k0, KEY_CHUNK), :],
                                                  preferred_element_type=F32)
            m_s[g] = m_new

        for g in range(N_KV):
            if mode == "dsa":
                update(g, mask)
            else:
                picked = jnp.sum(jnp.where(blk == c, sel_blocks[g], 0), axis=1, keepdims=True)
                mask1 = jnp.where(c == c_max, causal.astype(I32), picked) > 0

                @pl.when((c == c_max) | (jnp.max(picked) > 0))
                def _():
                    update(g, jnp.concatenate([mask1] * GROUP, axis=0))
        return carry

    lax.fori_loop(0, n_chunks, attend, 0)
    for g in range(N_KV):
        o_ref[g] = acc_s[g][:, g * HEAD_DIM:(g + 1) * HEAD_DIM] / l_s[g]


def _sparse_attention(mode, q_st, kt, v, strip, extra, qb, q_pos0, k_sel=0):
    bsz, nq = q_st.shape[:2]
    lp = v.shape[1]
    rows = GROUP * qb
    per_q = lambda *tail: pl.BlockSpec((None, None) + tail, lambda b, i: (b, i) + (0,) * len(tail))
    per_b = lambda *tail: pl.BlockSpec((None,) + tail, lambda b, i: (b,) + (0,) * len(tail))
    in_specs = [per_q(N_KV, rows, HEAD_DIM), per_b(N_KV, HEAD_DIM, lp), per_b(lp, N_KV * HEAD_DIM),
                pl.BlockSpec((N_KV, rows, STRIP_W), lambda b, i: (0, 0, 0))]
    scratch = [pltpu.VMEM((N_KV, rows, HEAD_DIM), BF16),
               pltpu.VMEM((N_KV, rows, 1), F32),
               pltpu.VMEM((N_KV, rows, 1), F32),
               pltpu.VMEM((N_KV, rows, N_KV * HEAD_DIM), F32)]
    if mode == "dsa":
        qi_st, wi_st, kit = extra
        in_specs += [per_q(IDX_HEADS * qb, IDX_DIM), per_q(IDX_HEADS * qb, 1), per_b(IDX_DIM, lp)]
        scratch.append(pltpu.VMEM((qb, lp), I32))
    else:
        (kmt,) = extra
        in_specs.append(per_b(N_KV, HEAD_DIM, kmt.shape[3]))
    return pl.pallas_call(
        functools.partial(_attn_kernel, mode, qb, lp, q_pos0, k_sel),
        grid=(bsz, nq),
        in_specs=in_specs,
        out_specs=per_q(N_KV, rows, HEAD_DIM),
        out_shape=jax.ShapeDtypeStruct((bsz, nq, N_KV, rows, HEAD_DIM), F32),
        scratch_shapes=scratch,
        compiler_params=_cparams(("parallel", "arbitrary")),
        name=mode + "_attention",
    )(q_st, kt, v, strip, *extra)


QT = 128


def _attn_t_kernel(mode, lp, k_sel, *refs):
    if mode == "dsa":
        (qt_ref, k_ref, vt_ref, strip_ref, qit_ref, wit_ref, ki_ref, o_ref,
         qbd_s, m_s, l_s, acc_s, key_s) = refs
    else:
        (qt_ref, k_ref, vt_ref, strip_ref, km_ref, o_ref,
         qbd_s, m_s, l_s, acc_s) = refs
    kv_dim = N_KV * HEAD_DIM
    cols = GROUP * QT
    i = pl.program_id(1)
    q0 = i * QT
    c_max = (q0 + QT - 1) // KEY_CHUNK
    n_chunks = c_max + 1
    qpos = q0 + lax.broadcasted_iota(I32, (1, QT), 1)
    krow = lax.broadcasted_iota(I32, (KEY_CHUNK, 1), 0)
    scale = HEAD_DIM ** -0.5

    qbd_s[...] = jnp.zeros(qbd_s.shape, BF16)
    for g in range(N_KV):
        qbd_s[g * HEAD_DIM:(g + 1) * HEAD_DIM, g * cols:(g + 1) * cols] = (
            qt_ref[g * HEAD_DIM:(g + 1) * HEAD_DIM, :] * scale).astype(BF16)
    m_s[...] = jnp.full(m_s.shape, NEG, F32)
    l_s[...] = jnp.zeros(l_s.shape, F32)
    acc_s[...] = jnp.zeros(acc_s.shape, F32)

    if mode == "dsa":
        qit = qit_ref[...].astype(BF16)
        wit = wit_ref[...]

        def score_chunk(c, carry):
            k0 = pl.multiple_of(c * KEY_CHUNK, KEY_CHUNK)
            kib = ki_ref[pl.ds(k0, KEY_CHUNK), :]
            sc = jnp.zeros((KEY_CHUNK, QT), F32)
            for hp in range(IDX_HEADS // 2):
                lanes = slice(2 * hp * QT, 2 * (hp + 1) * QT)
                s = jnp.maximum(jnp.dot(kib, qit[:, lanes], preferred_element_type=F32), 0.0) * wit[:, lanes]
                sc = sc + s[:, :QT] + s[:, QT:]
            sc = jnp.where(k0 + krow <= qpos, sc, -jnp.inf)
            bits = pltpu.bitcast(sc, I32)
            bits = jnp.where(bits == INT_MIN, 0, bits)
            key_s[pl.ds(k0, KEY_CHUNK), :] = jnp.where(bits < 0, bits ^ 0x7FFFFFFF, bits)
            return carry

        lax.fori_loop(0, n_chunks, score_chunk, 0)

        def count(pred):
            def body(c, acc):
                k0 = pl.multiple_of(c * KEY_CHUNK, KEY_CHUNK)
                hit = pred(key_s[pl.ds(k0, KEY_CHUNK), :], k0).astype(I32)
                return acc + jnp.sum(hit.reshape(KEY_CHUNK // 8, 8, QT), axis=0)
            acc = lax.fori_loop(0, n_chunks, body, jnp.zeros((8, QT), I32))
            return jnp.sum(acc, axis=0, keepdims=True)

        def bit_step(bi, thr):
            cand = thr + jnp.left_shift(jnp.int32(1), 31 - bi)
            cnt = count(lambda k, k0: k >= cand)
            return jnp.where(cnt >= k_sel, cand, thr)

        thr = lax.fori_loop(0, 32, bit_step, jnp.full((1, QT), INT_MIN, I32))
        n_gt = count(lambda k, k0: k > thr)
        n_ge = count(lambda k, k0: k >= thr)
        need = k_sel - n_gt

        def tie_limit():
            n_bits = max(1, (lp - 1).bit_length())

            def idx_step(bi, lo):
                cand = lo + jnp.left_shift(jnp.int32(1), n_bits - 1 - bi)
                cnt = count(lambda k, k0: (k == thr) & (k0 + krow <= cand))
                return jnp.where(cnt < need, cand, lo)

            lo = lax.fori_loop(0, n_bits, idx_step, jnp.full((1, QT), -1, I32))
            return lo + 1

        tie_idx = lax.cond(jnp.max(n_ge - n_gt - need) > 0, tie_limit,
                           lambda: jnp.full((1, QT), lp, I32))
    else:
        nbp = km_ref.shape[1]
        blk = lax.broadcasted_iota(I32, (nbp, QT), 0)
        sel_blocks = []
        for g in range(N_KV):
            qg = qt_ref[g * HEAD_DIM:(g + 1) * HEAD_DIM, :]
            qsum = qg[:, 0:QT]
            for j in range(1, GROUP):
                qsum = qsum + qg[:, j * QT:(j + 1) * QT]
            gate = jnp.dot(km_ref[g], qsum, preferred_element_type=F32, precision=lax.Precision.HIGHEST)
            gate = jnp.where(blk < c_max, gate, -jnp.inf)
            sel = jnp.zeros(blk.shape, I32)
            for _ in range(MOBA_TOPK):
                mx = jnp.max(gate, axis=0, keepdims=True)
                first = jnp.min(jnp.where(gate == mx, blk, nbp), axis=0, keepdims=True)
                hit = blk == first
                sel = jnp.where(hit & (blk < c_max), 1, sel)
                gate = jnp.where(hit, -jnp.inf, gate)
            sel_blocks.append(sel)

    def attend(c, carry):
        k0 = pl.multiple_of(c * KEY_CHUNK, KEY_CHUNK)
        kidx = k0 + krow
        causal = kidx <= qpos
        off = pl.multiple_of(jnp.clip(STRIP_ORIGIN - (q0 - k0), 0, STRIP_ORIGIN), 128)
        if mode == "dsa":
            k = key_s[pl.ds(k0, KEY_CHUNK), :]
            mask = ((k > thr) | ((k == thr) & (kidx <= tie_idx))) & causal
        kblk = k_ref[pl.ds(k0, KEY_CHUNK), :]

        def logits(g):
            return jnp.dot(kblk, qbd_s[:, g * cols:(g + 1) * cols], preferred_element_type=F32)

        s_next = logits(0)
        for g in range(N_KV):
            if mode != "dsa":
                picked = jnp.sum(jnp.where(blk == c, sel_blocks[g], 0), axis=0, keepdims=True)
                mask = jnp.where(c == c_max, causal.astype(I32), picked) > 0
            s_all = s_next
            if g + 1 < N_KV:
                s_next = logits(g + 1)
            probs, alphas = [], []
            for j in range(GROUP):
                h = g * GROUP + j
                lanes = slice(j * QT, (j + 1) * QT)
                s = s_all[:, lanes] + strip_ref[g, pl.ds(off, KEY_CHUNK), lanes]
                s = jnp.where(mask, s, NEG)
                m_old = m_s[pl.ds(h, 1), :]
                m_new = jnp.maximum(m_old, jnp.max(s, axis=0, keepdims=True))
                alpha = jnp.exp(m_old - m_new)
                p = jnp.where(mask, jnp.exp(s - m_new), 0.0)
                l_s[pl.ds(h, 1), :] = alpha * l_s[pl.ds(h, 1), :] + jnp.sum(p, axis=0, keepdims=True)
                m_s[pl.ds(h, 1), :] = m_new
                probs.append(p.astype(BF16))
                alphas.append(alpha)
            pv = jnp.dot(vt_ref[g * HEAD_DIM:(g + 1) * HEAD_DIM, pl.ds(k0, KEY_CHUNK)],
                         jnp.concatenate(probs, axis=1), preferred_element_type=F32)
            acc_s[g] = jnp.concatenate(alphas, axis=1) * acc_s[g] + pv
        return carry

    lax.fori_loop(0, n_chunks, attend, 0)
    for g in range(N_KV):
        inv = jnp.concatenate([1.0 / l_s[pl.ds(g * GROUP + j, 1), :] for j in range(GROUP)], axis=1)
        o_ref[g * HEAD_DIM:(g + 1) * HEAD_DIM, :] = acc_s[g] * inv


def _sparse_attention_t(mode, qt, k, vt, strip_t, extra, k_sel=0):
    bsz, nq = qt.shape[:2]
    lp = k.shape[1]
    kv_dim = N_KV * HEAD_DIM
    cols = GROUP * QT
    per_q = lambda *tail: pl.BlockSpec((None, None) + tail, lambda b, i: (b, i) + (0,) * len(tail))
    per_b = lambda *tail: pl.BlockSpec((None,) + tail, lambda b, i: (b,) + (0,) * len(tail))
    in_specs = [per_q(kv_dim, cols), per_b(lp, kv_dim), per_b(kv_dim, lp),
                pl.BlockSpec((N_KV, STRIP_W, cols), lambda b, i: (0, 0, 0))]
    scratch = [pltpu.VMEM((kv_dim, N_KV * cols), BF16),
               pltpu.VMEM((N_HEADS, QT), F32),
               pltpu.VMEM((N_HEADS, QT), F32),
               pltpu.VMEM((N_KV, HEAD_DIM, cols), F32)]
    if mode == "dsa":
        qit, wit, ki = extra
        in_specs += [per_q(IDX_DIM, IDX_HEADS * QT), per_q(1, IDX_HEADS * QT), per_b(lp, IDX_DIM)]
        scratch.append(pltpu.VMEM((lp, QT), I32))
    else:
        (km,) = extra
        in_specs.append(per_b(N_KV, km.shape[2], HEAD_DIM))
    return pl.pallas_call(
        functools.partial(_attn_t_kernel, mode, lp, k_sel),
        grid=(bsz, nq),
        in_specs=in_specs,
        out_specs=per_q(kv_dim, cols),
        out_shape=jax.ShapeDtypeStruct((bsz, nq, kv_dim, cols), F32),
        scratch_shapes=scratch,
        compiler_params=_cparams(("parallel", "arbitrary")),
        name=mode + "_attention_t",
    )(qt, k, vt, strip_t, *extra)


def _top_rows(s, k, payload=None, rank=None):
    rid = lax.broadcasted_iota(I32, s.shape, 0) if rank is None else rank
    vals, ids = [], []
    for _ in range(k):
        mx = jnp.max(s, axis=0, keepdims=True)
        first = jnp.min(jnp.where(s == mx, rid, jnp.iinfo(jnp.int32).max), axis=0, keepdims=True)
        hit = rid == first
        vals.append(mx)
        if payload is None:
            ids.append(first)
        else:
            ids.append(jnp.max(jnp.where(hit, payload, -1), axis=0, keepdims=True))
        s = jnp.where(hit, -jnp.inf, s)
    return jnp.concatenate(vals, axis=0), jnp.concatenate(ids, axis=0)


def _peer_route_kernel(x_ref, wqt_ref, keys_ref, row_ref, par_ref, g_ref):
    xb = x_ref[...].astype(BF16)
    qt = lax.dot_general(wqt_ref[...], xb, (((1,), (1,)), ((), ())),
                         preferred_element_type=F32)
    n_t = x_ref.shape[0]
    piece = lax.broadcasted_iota(I32, (11 * 8, n_t), 0) // 8
    within = lax.broadcasted_iota(I32, (11 * 8, n_t), 0) % 8
    by_b = (piece >= 2) & (piece < 10)
    ca = jnp.where(piece < 2, 0, jnp.where(by_b, within, 8 + within))
    cb = jnp.where(piece == 0, within, jnp.where(piece == 1, 8 + within, jnp.where(by_b, piece - 2, 0)))
    cand_ok = ((ca + 1) * (cb + 1) <= PEER_TOPK) & jnp.logical_not(by_b & (within == 0))
    cand_rank = ca * PEER_TOPK + cb

    def pieces(first, second, combine):
        out = [combine(first[0:1], second[0:8]), combine(first[0:1], second[8:16])]
        out += [combine(first[0:8], second[b:b + 1]) for b in range(8)]
        out.append(combine(first[8:16], second[0:1]))
        return jnp.concatenate(out, axis=0)

    e_rows, g_rows = [], []
    for h in range(PEER_HEADS):
        sv, si = [], []
        for c in range(2):
            hc = 2 * h + c
            qhc = qt[hc * PEER_HALF:(hc + 1) * PEER_HALF].astype(BF16)
            s = jnp.dot(keys_ref[hc], qhc, preferred_element_type=F32)
            v_, i_ = _top_rows(s, PEER_TOPK)
            sv.append(v_)
            si.append(i_)
        cand = jnp.where(cand_ok, pieces(sv[0], sv[1], lambda u, w: u + w), -jnp.inf)
        cidx = pieces(si[0], si[1], lambda u, w: u * PEER_NKEYS + w)
        gv, ge = _top_rows(cand, PEER_TOPK, payload=cidx, rank=cand_rank)
        ex = jnp.exp(gv - gv[0:1])
        g_rows.append(ex / jnp.sum(ex, axis=0, keepdims=True))
        e_rows.append(ge)
    e_t = jnp.concatenate(e_rows, axis=0).T
    row_ref[...] = jnp.right_shift(e_t, 1) * 8
    par_ref[...] = e_t & 1
    g_ref[...] = jnp.concatenate(g_rows, axis=0).T


def _peer_route(x, wqt_bf16, keys_bf16, tb):
    m, d = x.shape
    out = pl.BlockSpec((tb, PEER_PAIRS), lambda i: (i, 0))
    return pl.pallas_call(
        _peer_route_kernel,
        grid=(m // tb,),
        in_specs=[pl.BlockSpec((tb, d), lambda i: (i, 0)),
                  pl.BlockSpec(wqt_bf16.shape, lambda i: (0, 0)),
                  pl.BlockSpec(keys_bf16.shape, lambda i: (0, 0, 0))],
        out_specs=[out, out, out],
        out_shape=[jax.ShapeDtypeStruct((m, PEER_PAIRS), I32),
                   jax.ShapeDtypeStruct((m, PEER_PAIRS), I32),
                   jax.ShapeDtypeStruct((m, PEER_PAIRS), F32)],
        compiler_params=_cparams(("parallel",)),
        name="peer_route",
    )(x, wqt_bf16, keys_bf16)


FOLD_PAIRS = 32
UP_SLOTS = 4
TILE_ROWS = 8
HI_MASK = 0xFFFF0000


def _pack_table(w):
    n, d = w.shape
    bits = lax.bitcast_convert_type(w.astype(BF16), jnp.uint16).astype(jnp.uint32)
    bits = bits.reshape(n // 2, 2, d // 128, 128)
    return (bits[:, 0] | (bits[:, 1] << 16)).astype(jnp.uint32).reshape(n // 2 * (d // 128), 128)


def _load_tile(tbl_ref, row_ref, idx):
    return tbl_ref[pl.ds(pl.multiple_of(row_ref[idx], TILE_ROWS), TILE_ROWS), :]


def _peer_up_kernel(tb, row_ref, par_ref, x_ref, g_ref, fold_ref, tbl_ref, c_ref, prod_s):
    diag = (lax.broadcasted_iota(I32, (PEER_PAIRS, PEER_PAIRS), 0)
            == lax.broadcasted_iota(I32, (PEER_PAIRS, PEER_PAIRS), 1))

    def gather(t, slot):
        xt = x_ref[t]
        rows_t = row_ref.at[pl.ds(t * PEER_PAIRS, PEER_PAIRS)]
        for p in range(0, PEER_PAIRS, 2):
            w0 = _load_tile(tbl_ref, rows_t, p)
            w1 = _load_tile(tbl_ref, rows_t, p + 1)
            even = [pltpu.bitcast(jnp.left_shift(w, jnp.uint32(16)), F32) * xt for w in (w0, w1)]
            odd = [pltpu.bitcast(w & jnp.uint32(HI_MASK), F32) * xt for w in (w0, w1)]
            rows = pl.ds(p * TILE_ROWS, 2 * TILE_ROWS)
            prod_s[slot, rows, 0:128] = jnp.concatenate(even, axis=0).astype(BF16)
            prod_s[slot, rows, 128:256] = jnp.concatenate(odd, axis=0).astype(BF16)

    def reduce(t, slot):
        depth = FOLD_PAIRS * TILE_ROWS
        parts = [jnp.dot(fold_ref[...], prod_s[slot, pl.ds(i * depth, depth), :], preferred_element_type=F32)
                 for i in range(PEER_PAIRS // FOLD_PAIRS)]
        part = jnp.concatenate(parts, axis=0)
        rows = []
        for half in range(2):
            col = jnp.sum(part[:, half * 128:(half + 1) * 128], axis=1, keepdims=True)
            rows.append(jnp.sum(jnp.where(diag, col, 0.0), axis=0, keepdims=True))
        a = jnp.where(par_ref[pl.ds(t, 1), :] == 1, rows[1], rows[0])
        act = 0.5 * a * (1.0 + lax.erf(a * (2.0 ** -0.5)))
        c_ref[pl.ds(t, 1), :] = g_ref[pl.ds(t, 1), :] * act

    prod_s[UP_SLOTS:2 * UP_SLOTS] = jnp.zeros((UP_SLOTS,) + prod_s.shape[1:], BF16)

    def step(k, carry):
        t0 = 2 * UP_SLOTS * k
        for j in range(UP_SLOTS):
            gather(t0 + j, j)
            reduce(jnp.maximum(t0 - UP_SLOTS + j, 0), UP_SLOTS + j)
        for j in range(UP_SLOTS):
            gather(t0 + UP_SLOTS + j, UP_SLOTS + j)
            reduce(t0 + j, j)
        return carry

    lax.fori_loop(0, tb // (2 * UP_SLOTS), step, 0)
    for j in range(UP_SLOTS):
        reduce(tb - UP_SLOTS + j, UP_SLOTS + j)


DOWN_GROUP = 8


def _peer_down_kernel(tb, row_ref, par_ref, c_ref, spread_ref, tbl_ref, f_ref, tile_s, coef_s):
    width = PEER_PAIRS * 2 * TILE_ROWS
    lane = lax.broadcasted_iota(I32, (TILE_ROWS, width), 1)
    own_row = ((lane & (2 * TILE_ROWS - 1)) >> 1) == lax.broadcasted_iota(I32, (TILE_ROWS, width), 0)
    lane_par = (lax.broadcasted_iota(I32, (DOWN_GROUP, width), 1) & 1).astype(F32)

    def gather(t, slot):
        rows_t = row_ref.at[pl.ds(t * PEER_PAIRS, PEER_PAIRS)]
        for p in range(PEER_PAIRS):
            tile_s[slot, pl.ds(p * TILE_ROWS, TILE_ROWS), :] = _load_tile(tbl_ref, rows_t, p)

    def apply(t, j, slot):
        tiles = pltpu.bitcast(tile_s[slot], BF16)
        lhs = [jnp.where(own_row, coef_s[pl.ds(part * DOWN_GROUP + j, 1), :], 0.0) for part in range(2)]
        out = jnp.dot(jnp.concatenate(lhs, axis=0).astype(BF16), tiles, preferred_element_type=F32)
        f_ref[t] = out[0:TILE_ROWS] + out[TILE_ROWS:2 * TILE_ROWS]

    def group(gi, carry):
        t0 = pl.multiple_of(gi * DOWN_GROUP, DOWN_GROUP)
        gather(t0, 0)
        c = c_ref[pl.ds(t0, DOWN_GROUP), :]
        c_hi = c.astype(BF16)
        c_lo = (c - c_hi.astype(F32)).astype(BF16)
        par = par_ref[pl.ds(t0, DOWN_GROUP), :].astype(BF16)
        rep = jnp.dot(jnp.concatenate([c_hi, c_lo, par], axis=0), spread_ref[...],
                      preferred_element_type=F32)
        keep = rep[2 * DOWN_GROUP:] == lane_par
        coef_s[0:DOWN_GROUP, :] = jnp.where(keep, rep[0:DOWN_GROUP], 0.0)
        coef_s[DOWN_GROUP:2 * DOWN_GROUP, :] = jnp.where(keep, rep[DOWN_GROUP:2 * DOWN_GROUP], 0.0)
        for j in range(DOWN_GROUP):
            if j + 1 < DOWN_GROUP:
                gather(t0 + j + 1, (j + 1) % 2)
            apply(t0 + j, j, j % 2)
        return carry

    lax.fori_loop(0, tb // DOWN_GROUP, group, 0)


def _peer_experts(x, row, par, g, u_tbl, v_tbl, tb):
    m, d = x.shape
    x3 = x.reshape(m, d // 128, 128)
    flat = row.reshape(m * PEER_PAIRS)
    smem = pl.BlockSpec((tb * PEER_PAIRS,), lambda i: (i,), memory_space=pltpu.SMEM)
    vrow = pl.BlockSpec((tb, PEER_PAIRS), lambda i: (i, 0))
    tile = pl.BlockSpec((tb, d // 128, 128), lambda i: (i, 0, 0))
    resident = pl.BlockSpec(memory_space=pltpu.VMEM)
    const = lambda a: pl.BlockSpec(a.shape, lambda i: (0, 0))
    fold = jnp.asarray(np.arange(FOLD_PAIRS * TILE_ROWS)[None, :] // TILE_ROWS == np.arange(FOLD_PAIRS)[:, None], BF16)
    width = PEER_PAIRS * 2 * TILE_ROWS
    spread = jnp.asarray(np.arange(width)[None, :] // (2 * TILE_ROWS) == np.arange(PEER_PAIRS)[:, None], BF16)
    c = pl.pallas_call(
        functools.partial(_peer_up_kernel, tb),
        grid=(m // tb,),
        in_specs=[smem, vrow, tile, vrow, const(fold), resident],
        out_specs=vrow,
        out_shape=jax.ShapeDtypeStruct((m, PEER_PAIRS), F32),
        scratch_shapes=[pltpu.VMEM((2 * UP_SLOTS, PEER_PAIRS * TILE_ROWS, 256), BF16)],
        compiler_params=_cparams(("arbitrary",)),
        name="peer_up",
    )(flat, par, x3, g, fold, u_tbl)
    f = pl.pallas_call(
        functools.partial(_peer_down_kernel, tb),
        grid=(m // tb,),
        in_specs=[smem, vrow, vrow, const(spread), resident],
        out_specs=tile,
        out_shape=jax.ShapeDtypeStruct((m, d // 128, 128), F32),
        scratch_shapes=[pltpu.VMEM((2, PEER_PAIRS * TILE_ROWS, 128), jnp.uint32),
                        pltpu.VMEM((2 * DOWN_GROUP, width), F32)],
        compiler_params=_cparams(("arbitrary",)),
        name="peer_down",
    )(flat, par, c, spread, v_tbl)
    return f.reshape(m, d)


def _stack_heads(a, qb, n_lead):
    bsz, q, n, hd = a.shape
    n_in = n // n_lead
    a = a.reshape(bsz, q // qb, qb, n_lead, n_in, hd)
    return a.transpose(0, 1, 3, 4, 2, 5).reshape(bsz, q // qb, n_lead, n_in * qb, hd)


def _unstack_heads(o, qb):
    bsz, nq = o.shape[:2]
    o = o.reshape(bsz, nq, N_KV, GROUP, qb, HEAD_DIM).transpose(0, 1, 4, 2, 3, 5)
    return o.reshape(bsz * nq * qb, N_HEADS * HEAD_DIM)


def _run_trunk(x, p, past, prm, qb, tm, tb_route, tb_exp):
    bsz, q_len, d = x.shape
    depth = prm["w_o"].shape[0]
    m = bsz * q_len
    tm, tb_route, tb_exp = min(tm, m), min(tb_route, m), min(tb_exp, m)
    alpha = (2 * depth) ** 0.25
    attn_dim = N_HEADS * HEAD_DIM
    kv_dim = N_KV * HEAD_DIM
    if past is None:
        q_pos0 = 0
        n_pages = q_len // PAGE
        table = jnp.arange(bsz * n_pages, dtype=I32).reshape(bsz, n_pages)
    else:
        cache_k, cache_v, cache_ki, table = past
        q_pos0 = table.shape[1] * PAGE
    on_lanes = past is None and q_len % QT == 0
    if on_lanes:
        qb = QT
    nq = q_len // qb
    strip = _bias_strip(prm["rel_bias"], qb).reshape(N_KV, GROUP, qb, STRIP_W)
    if on_lanes:
        strip = strip.transpose(0, 3, 1, 2).reshape(N_KV, STRIP_W, GROUP * qb)
    else:
        strip = strip.reshape(N_KV, GROUP * qb, STRIP_W)
    xf = x.reshape(m, d)
    ks, vs, kis = [], [], []

    def relayout(new_rows, cache, layer, want):
        w = new_rows.shape[2]
        if past is None:
            return _relayout(new_rows.reshape(bsz * n_pages, PAGE, w), table, None, False, want)
        n_pool = cache.shape[1]
        pool = cache.reshape((cache.shape[0] * n_pool,) + cache.shape[2:])
        pool_t = jnp.moveaxis(pool, 1, -1).reshape(pool.shape[0], w, PAGE)
        pad = jnp.zeros((bsz, STEP_KEYS - q_len, w), F32)
        return _relayout(pool_t, table + layer * n_pool, jnp.concatenate([new_rows, pad], axis=1), True, want)

    for l in range(depth):
        dsa = l % 2 == 0
        w_in = prm["w_in_dsa"][l // 2] if dsa else prm["w_in_moba"][l // 2]
        n_in = w_in.shape[1]
        n_pad = -(-n_in // 128) * 128
        w_in = jnp.pad(w_in, ((0, 0), (0, n_pad - n_in))).astype(BF16)
        proj = _matmul(xf, w_in, tm)
        qh = proj[:, :attn_dim].reshape(bsz, q_len, N_HEADS, HEAD_DIM)
        k = proj[:, attn_dim:attn_dim + kv_dim].reshape(bsz, q_len, kv_dim)
        v = proj[:, attn_dim + kv_dim:attn_dim + 2 * kv_dim].reshape(bsz, q_len, kv_dim)
        ks.append(k.reshape(bsz, q_len, N_KV, HEAD_DIM))
        vs.append(v.reshape(bsz, q_len, N_KV, HEAD_DIM))
        k_lay = relayout(k, None if past is None else cache_k, l,
                         (("x",) if on_lanes else ("xt",)) + (() if dsa else ("sums",)))
        v_lay = relayout(v, None if past is None else cache_v, l, ("xt",) if on_lanes else ("x",))
        if on_lanes:
            q_in = qh.reshape(bsz, nq, qb, N_KV, GROUP, HEAD_DIM).transpose(0, 1, 3, 5, 4, 2)
            q_in = q_in.reshape(bsz, nq, kv_dim, GROUP * qb)
            kr, vt = k_lay["x"], v_lay["xt"]
        else:
            kt = k_lay["xt"].reshape(bsz, N_KV, HEAD_DIM, k_lay["xt"].shape[2])
            vr = v_lay["x"]
            q_in = _stack_heads(qh, qb, N_KV)
        if dsa:
            off = attn_dim + 2 * kv_dim
            qi = proj[:, off:off + IDX_HEADS * IDX_DIM].reshape(bsz, q_len, IDX_HEADS, IDX_DIM)
            off += IDX_HEADS * IDX_DIM
            ki = proj[:, off:off + IDX_DIM].reshape(bsz, q_len, IDX_DIM)
            wi = proj[:, off + IDX_DIM:off + IDX_DIM + IDX_HEADS].reshape(bsz, q_len, IDX_HEADS, 1)
            kis.append(ki)
            ki_lay = relayout(ki, None if past is None else cache_ki, l // 2, ("x",) if on_lanes else ("xt",))
            k_sel = min(IDX_TOPK, (q_pos0 + q_len) // 4)
            if on_lanes:
                qi_t = qi.reshape(bsz, nq, qb, IDX_HEADS, IDX_DIM).transpose(0, 1, 4, 3, 2)
                qi_t = qi_t.reshape(bsz, nq, IDX_DIM, IDX_HEADS * qb)
                wi_t = wi.reshape(bsz, nq, qb, IDX_HEADS).transpose(0, 1, 3, 2).reshape(bsz, nq, 1, IDX_HEADS * qb)
                o = _sparse_attention_t("dsa", q_in, kr, vt, strip, (qi_t, wi_t, ki_lay["x"]), k_sel)
            else:
                qi_st = _stack_heads(qi, qb, 1)[:, :, 0]
                wi_st = _stack_heads(wi, qb, 1)[:, :, 0]
                o = _sparse_attention("dsa", q_in, kt, vr, strip, (qi_st, wi_st, ki_lay["xt"]), qb, q_pos0, k_sel)
        else:
            ksum = k_lay["sums"]
            nb = ksum.shape[1]
            kmean = (ksum * (1.0 / MOBA_BLOCK)).reshape(bsz, nb, N_KV, HEAD_DIM)
            if on_lanes:
                km = jnp.pad(kmean.transpose(0, 2, 1, 3), ((0, 0), (0, 0), (0, -nb % 8), (0, 0)))
                o = _sparse_attention_t("moba", q_in, kr, vt, strip, (km,))
            else:
                kmt = jnp.pad(kmean.transpose(0, 2, 3, 1), ((0, 0), (0, 0), (0, 0), (0, -nb % 128)))
                o = _sparse_attention("moba", q_in, kt, vr, strip, (kmt,), qb, q_pos0)
        if on_lanes:
            o = o.reshape(bsz, nq, N_KV, HEAD_DIM, GROUP, qb).transpose(0, 1, 5, 2, 4, 3)
            o = o.reshape(m, attn_dim)
        else:
            o = _unstack_heads(o, qb)
        xf = _attn_out(o, prm["w_o"][l].astype(BF16), xf, prm["ln_g"][l, 0][None], prm["ln_b"][l, 0][None],
                       alpha, tm)
        row, par, g = _peer_route(xf, prm["peer_wq"][l].T.astype(BF16),
                                 prm["peer_keys"][l].reshape(2 * PEER_HEADS, PEER_NKEYS, PEER_HALF).astype(BF16),
                                 tb_route)
        f = _peer_experts(xf, row, par, g, _pack_table(prm["peer_u"][l]), _pack_table(prm["peer_v"][l]), tb_exp)
        xf = _layer_tail(xf, f, prm["ln_g"][l, 1][None], prm["ln_b"][l, 1][None],
                         prm["gate_w"][l].astype(BF16), prm["gate_b"][l][None],
                         p[l].reshape(m, -1), prm["ple_w"][l].astype(BF16), alpha, tm)
    return xf.reshape(bsz, q_len, d), jnp.stack(ks), jnp.stack(vs), jnp.stack(kis)


def kernel(x_prompt, x_sample, cache_k, cache_v, cache_ki, page_table, p_prompt, p_sample, w_in_dsa, w_in_moba, w_o, rel_bias, ln_g, ln_b, peer_wq, peer_keys, peer_u, peer_v, gate_w, gate_b, ple_w):
    prm = {
        "w_in_dsa": w_in_dsa, "w_in_moba": w_in_moba, "w_o": w_o, "rel_bias": rel_bias,
        "ln_g": ln_g, "ln_b": ln_b, "peer_wq": peer_wq, "peer_keys": peer_keys,
        "peer_u": peer_u, "peer_v": peer_v, "gate_w": gate_w, "gate_b": gate_b, "ple_w": ple_w,
    }
    q_sample = x_sample.shape[1]
    y_prompt, k_prompt, v_prompt, ki_prompt = _run_trunk(
        x_prompt, p_prompt, None, prm, qb=128, tm=512, tb_route=256, tb_exp=128)
    y_sample, k_sample, v_sample, ki_sample = _run_trunk(
        x_sample, p_sample, (cache_k, cache_v, cache_ki, page_table), prm,
        qb=q_sample, tm=256, tb_route=256, tb_exp=64)
    return (y_prompt, y_sample, k_prompt, v_prompt, ki_prompt, k_sample, v_sample, ki_sample)
```

```python
import functools
import math

import numpy as np
import jax
import jax.numpy as jnp
from jax import lax
from jax.experimental import pallas as pl
from jax.experimental.pallas import tpu as pltpu

F32 = jnp.float32
BF16 = jnp.bfloat16
I32 = jnp.int32

N_HEADS = 16
HEAD_DIM = 64
N_KV = 4
GROUP = 4
IDX_HEADS = 8
IDX_DIM = 64
IDX_TOPK = 256
MOBA_BLOCK = 256
MOBA_TOPK = 3
REL_BUCKETS = 32
REL_MAX_DIST = 128
PEER_HEADS = 8
PEER_NKEYS = 128
PEER_HALF = 128
PEER_TOPK = 16
PEER_PAIRS = PEER_HEADS * PEER_TOPK
LN_EPS = 1e-5
PAGE = 128

KEY_CHUNK = 256
STRIP_W = 768
STRIP_ORIGIN = 512
NEG = -1e30
INT_MIN = -2147483648
VMEM_LIMIT = 56 * 1024 * 1024


def _cparams(sem):
    return pltpu.CompilerParams(dimension_semantics=sem, vmem_limit_bytes=VMEM_LIMIT)


def _mm_kernel(a_ref, b_ref, o_ref):
    o_ref[...] = jnp.dot(a_ref[...].astype(BF16), b_ref[...], preferred_element_type=F32)


def _matmul(a, b_bf16, tm):
    m, k = a.shape
    n = b_bf16.shape[1]
    return pl.pallas_call(
        _mm_kernel,
        grid=(m // tm,),
        in_specs=[pl.BlockSpec((tm, k), lambda i: (i, 0)),
                  pl.BlockSpec((k, n), lambda i: (0, 0))],
        out_specs=pl.BlockSpec((tm, n), lambda i: (i, 0)),
        out_shape=jax.ShapeDtypeStruct((m, n), F32),
        compiler_params=_cparams(("parallel",)),
        name="in_proj",
    )(a, b_bf16)


def _layer_norm(y, g, b):
    mu = jnp.mean(y, axis=-1, keepdims=True)
    yc = y - mu
    var = jnp.mean(yc * yc, axis=-1, keepdims=True)
    return yc * lax.rsqrt(var + LN_EPS) * g + b


def _attn_out_kernel(alpha, o_ref, w_ref, x_ref, g_ref, b_ref, y_ref):
    y = alpha * x_ref[...] + jnp.dot(o_ref[...].astype(BF16), w_ref[...],
                                     preferred_element_type=F32)
    y_ref[...] = _layer_norm(y, g_ref[...], b_ref[...])


def _attn_out(o, w_bf16, x, g, b, alpha, tm):
    m, d = x.shape
    row = pl.BlockSpec((tm, d), lambda i: (i, 0))
    vec = pl.BlockSpec((1, d), lambda i: (0, 0))
    return pl.pallas_call(
        functools.partial(_attn_out_kernel, alpha),
        grid=(m // tm,),
        in_specs=[row, pl.BlockSpec((d, d), lambda i: (0, 0)), row, vec, vec],
        out_specs=row,
        out_shape=jax.ShapeDtypeStruct((m, d), F32),
        compiler_params=_cparams(("parallel",)),
        name="attn_out_ln",
    )(o, w_bf16, x, g, b)


def _layer_tail_kernel(alpha, x_ref, f_ref, g_ref, b_ref, gw_ref, gb_ref, p_ref, pw_ref, y_ref):
    x2 = _layer_norm(alpha * x_ref[...] + f_ref[...], g_ref[...], b_ref[...])
    z = jnp.dot(x2.astype(BF16), gw_ref[...], preferred_element_type=F32) + gb_ref[...]
    gate = 1.0 / (1.0 + jnp.exp(-z))
    e = jnp.dot(p_ref[...].astype(BF16), pw_ref[...], preferred_element_type=F32)
    y_ref[...] = x2 + gate * e


def _layer_tail(x, f, g, b, gw_bf16, gb, p, pw_bf16, alpha, tm):
    m, d = x.shape
    pd = p.shape[1]
    row = pl.BlockSpec((tm, d), lambda i: (i, 0))
    vec = pl.BlockSpec((1, d), lambda i: (0, 0))
    return pl.pallas_call(
        functools.partial(_layer_tail_kernel, alpha),
        grid=(m // tm,),
        in_specs=[row, row, vec, vec, pl.BlockSpec((d, d), lambda i: (0, 0)), vec,
                  pl.BlockSpec((tm, pd), lambda i: (i, 0)),
                  pl.BlockSpec((pd, d), lambda i: (0, 0))],
        out_specs=row,
        out_shape=jax.ShapeDtypeStruct((m, d), F32),
        compiler_params=_cparams(("parallel",)),
        name="layer_tail",
    )(x, f, g, b, gw_bf16, gb, p, pw_bf16)


STEP_PAGES = 8
STEP_KEYS = STEP_PAGES * PAGE
STEP_CHUNKS = STEP_KEYS // KEY_CHUNK


def _relayout_kernel(n_pool_steps, has_new, pages_t, want, *refs):
    page_refs = refs[1:1 + STEP_PAGES]
    new_ref = refs[1 + STEP_PAGES] if has_new else None
    outs = dict(zip(want, refs[len(refs) - len(want):]))
    j = pl.program_id(1)
    need_rows = "x" in want or "sums" in want

    def emit(rows, cols):
        if "xt" in want:
            outs["xt"][...] = (rows.T if cols is None else cols).astype(BF16)
        if need_rows and rows is None:
            rows = cols.T
        if "x" in want:
            outs["x"][...] = rows.astype(BF16)
        if "sums" in want:
            for i in range(STEP_CHUNKS):
                outs["sums"][pl.ds(i, 1), :] = jnp.sum(rows[i * KEY_CHUNK:(i + 1) * KEY_CHUNK], axis=0, keepdims=True)

    def from_pages():
        if pages_t:
            emit(None, jnp.concatenate([r[...] for r in page_refs], axis=1))
        else:
            emit(jnp.concatenate([r[...] for r in page_refs], axis=0), None)

    if has_new:
        pl.when(j < n_pool_steps)(from_pages)

        @pl.when(j >= n_pool_steps)
        def _():
            emit(new_ref[...], None)
    else:
        from_pages()


def _relayout(pool, table, new, pages_t, want):
    bsz, n_pages = table.shape
    w = pool.shape[1] if pages_t else pool.shape[2]
    n_pool_steps = n_pages // STEP_PAGES
    n_steps = n_pool_steps + (1 if new is not None else 0)
    lp = n_steps * STEP_KEYS
    last = n_pool_steps - 1

    def page_map(i):
        return lambda b, j, tbl: (tbl[b, STEP_PAGES * jnp.minimum(j, last) + i], 0, 0)

    page_block = (None, w, PAGE) if pages_t else (None, PAGE, w)
    in_specs = [pl.BlockSpec(page_block, page_map(i)) for i in range(STEP_PAGES)]
    args = [pool] * STEP_PAGES
    if new is not None:
        in_specs.append(pl.BlockSpec((None, STEP_KEYS, w), lambda b, j, tbl: (b, 0, 0)))
        args.append(new)
    specs = {"xt": (pl.BlockSpec((None, w, STEP_KEYS), lambda b, j, tbl: (b, 0, j)),
                    jax.ShapeDtypeStruct((bsz, w, lp), BF16)),
             "x": (pl.BlockSpec((None, STEP_KEYS, w), lambda b, j, tbl: (b, j, 0)),
                   jax.ShapeDtypeStruct((bsz, lp, w), BF16)),
             "sums": (pl.BlockSpec((None, None, STEP_CHUNKS, w), lambda b, j, tbl: (b, j, 0, 0)),
                      jax.ShapeDtypeStruct((bsz, n_steps, STEP_CHUNKS, w), F32))}
    outs = pl.pallas_call(
        functools.partial(_relayout_kernel, n_pool_steps, new is not None, pages_t, want),
        grid_spec=pltpu.PrefetchScalarGridSpec(
            num_scalar_prefetch=1,
            grid=(bsz, n_steps),
            in_specs=in_specs,
            out_specs=[specs[name][0] for name in want]),
        out_shape=[specs[name][1] for name in want],
        compiler_params=_cparams(("parallel", "arbitrary")),
        name="kv_relayout",
    )(table, *args)
    outs = dict(zip(want, outs))
    if "sums" in outs:
        outs["sums"] = outs["sums"].reshape(bsz, n_steps * STEP_CHUNKS, w)
    return outs


def _bucket_table():
    n = np.arange(REL_MAX_DIST + 1)
    exact = REL_BUCKETS // 2
    nf = np.maximum(n, 1).astype(np.float32)
    large = exact + (np.log(nf / np.float32(exact)) / np.float32(math.log(REL_MAX_DIST / exact))
                     * np.float32(REL_BUCKETS - exact)).astype(np.int32)
    return np.where(n < exact, n, np.minimum(large, REL_BUCKETS - 1)).astype(np.int32)


def _bias_strip_kernel(bk_ref, rb_ref, o_ref):
    bk = bk_ref[...]
    for h in range(N_HEADS):
        acc = jnp.zeros(bk.shape, F32)
        for bkt in range(REL_BUCKETS):
            acc = jnp.where(bk == bkt, rb_ref[bkt, h], acc)
        o_ref[h] = acc


def _bias_strip(rel_bias, qb):
    qi = np.arange(qb)[:, None]
    z = np.arange(STRIP_W)[None, :]
    dist = np.clip(qi - z + STRIP_ORIGIN, 0, REL_MAX_DIST)
    bk = jnp.asarray(_bucket_table()[dist])
    out = pl.pallas_call(
        _bias_strip_kernel,
        in_specs=[pl.BlockSpec(memory_space=pltpu.VMEM), pl.BlockSpec(memory_space=pltpu.SMEM)],
        out_specs=pl.BlockSpec(memory_space=pltpu.VMEM),
        out_shape=jax.ShapeDtypeStruct((N_HEADS, qb, STRIP_W), F32),
        name="bias_strip",
    )(bk, rel_bias)
    return out


def _attn_kernel(mode, qb, lp, q_pos0, k_sel, *refs):
    if mode == "dsa":
        (q_ref, kt_ref, v_ref, strip_ref, qi_ref, wi_ref, kit_ref, o_ref,
         qs_s, m_s, l_s, acc_s, key_s) = refs
    else:
        (q_ref, kt_ref, v_ref, strip_ref, kmt_ref, o_ref,
         qs_s, m_s, l_s, acc_s) = refs
    rows = GROUP * qb
    i = pl.program_id(1)
    q0 = q_pos0 + i * qb
    c_max = (q0 + qb - 1) // KEY_CHUNK
    n_chunks = c_max + 1
    qpos = q0 + lax.broadcasted_iota(I32, (qb, 1), 0)
    lane = lax.broadcasted_iota(I32, (1, KEY_CHUNK), 1)
    scale = HEAD_DIM ** -0.5

    for g in range(N_KV):
        qs_s[g] = (q_ref[g] * scale).astype(BF16)
    m_s[...] = jnp.full(m_s.shape, NEG, F32)
    l_s[...] = jnp.zeros(l_s.shape, F32)
    acc_s[...] = jnp.zeros(acc_s.shape, F32)

    if mode == "dsa":
        qi = qi_ref[...].astype(BF16)
        wi = wi_ref[...]

        def score_chunk(c, carry):
            k0 = pl.multiple_of(c * KEY_CHUNK, KEY_CHUNK)
            s = jnp.dot(qi, kit_ref[:, pl.ds(k0, KEY_CHUNK)], preferred_element_type=F32)
            s = jnp.maximum(s, 0.0) * wi
            sc = s[0:qb]
            for h in range(1, IDX_HEADS):
                sc = sc + s[h * qb:(h + 1) * qb]
            sc = jnp.where(k0 + lane <= qpos, sc, -jnp.inf)
            bits = pltpu.bitcast(sc, I32)
            bits = jnp.where(bits == INT_MIN, 0, bits)
            key_s[:, pl.ds(k0, KEY_CHUNK)] = jnp.where(bits < 0, bits ^ 0x7FFFFFFF, bits)
            return carry

        scan = STEP_KEYS if qb <= 32 else KEY_CHUNK
        n_scan = (n_chunks * KEY_CHUNK + scan - 1) // scan
        lax.fori_loop(0, n_scan * (scan // KEY_CHUNK), score_chunk, 0)
        scan_lane = lax.broadcasted_iota(I32, (1, scan), 1)

        def count(pred):
            def body(c, acc):
                k0 = pl.multiple_of(c * scan, scan)
                return acc + pred(key_s[:, pl.ds(k0, scan)], k0 + scan_lane).astype(I32)
            acc = lax.fori_loop(0, n_scan, body, jnp.zeros((qb, scan), I32))
            return jnp.sum(acc, axis=1, keepdims=True)

        def bit_step(bi, thr):
            cand = thr + jnp.left_shift(jnp.int32(1), 31 - bi)
            cnt = count(lambda k, kidx: k >= cand)
            return jnp.where(cnt >= k_sel, cand, thr)

        thr = lax.fori_loop(0, 32, bit_step, jnp.full((qb, 1), INT_MIN, I32))
        n_gt = count(lambda k, kidx: k > thr)
        n_ge = count(lambda k, kidx: k >= thr)
        need = k_sel - n_gt

        def tie_limit():
            n_bits = max(1, (lp - 1).bit_length())

            def idx_step(bi, lo):
                cand = lo + jnp.left_shift(jnp.int32(1), n_bits - 1 - bi)
                cnt = count(lambda k, kidx: (k == thr) & (kidx <= cand))
                return jnp.where(cnt < need, cand, lo)

            lo = lax.fori_loop(0, n_bits, idx_step, jnp.full((qb, 1), -1, I32))
            return lo + 1

        tie_idx = lax.cond(jnp.max(n_ge - n_gt - need) > 0, tie_limit,
                           lambda: jnp.full((qb, 1), lp, I32))
    else:
        blk = lax.broadcasted_iota(I32, (qb, kmt_ref.shape[2]), 1)
        sel_blocks = []
        for g in range(N_KV):
            qg = q_ref[g]
            qsum = qg[0:qb]
            for j in range(1, GROUP):
                qsum = qsum + qg[j * qb:(j + 1) * qb]
            gate = jnp.dot(qsum, kmt_ref[g], preferred_element_type=F32,
                           precision=lax.Precision.HIGHEST)
            gate = jnp.where(blk < c_max, gate, -jnp.inf)
            sel = jnp.zeros(blk.shape, jnp.bool_)
            for _ in range(MOBA_TOPK):
                mx = jnp.max(gate, axis=1, keepdims=True)
                first = jnp.min(jnp.where(gate == mx, blk, blk.shape[1]), axis=1, keepdims=True)
                hit = blk == first
                sel = sel | (hit & (blk < c_max))
                gate = jnp.where(hit, -jnp.inf, gate)
            sel_blocks.append(sel.astype(I32))

    def attend(c, carry):
        k0 = pl.multiple_of(c * KEY_CHUNK, KEY_CHUNK)
        kidx = k0 + lane
        causal = kidx <= qpos
        off = pl.multiple_of(jnp.clip(STRIP_ORIGIN - (q0 - k0), 0, STRIP_ORIGIN), 128)
        if mode == "dsa":
            k = key_s[:, pl.ds(k0, KEY_CHUNK)]
            mask1 = ((k > thr) | ((k == thr) & (kidx <= tie_idx))) & causal
            mask = jnp.concatenate([mask1] * GROUP, axis=0)

        logits = [jnp.dot(qs_s[g], kt_ref[g, :, pl.ds(k0, KEY_CHUNK)], preferred_element_type=F32)
                  for g in range(N_KV)]
        probs, alphas = [], []
        for g in range(N_KV):
            if mode != "dsa":
                picked = jnp.sum(jnp.where(blk == c, sel_blocks[g], 0), axis=1, keepdims=True)
                mask1 = jnp.where(c == c_max, causal.astype(I32), picked) > 0
                mask = jnp.concatenate([mask1] * GROUP, axis=0)
            s = jnp.where(mask, logits[g] + strip_ref[g, :, pl.ds(off, KEY_CHUNK)], NEG)
            m_old = m_s[g]
            m_new = jnp.maximum(m_old, jnp.max(s, axis=1, keepdims=True))
            alpha = jnp.exp(m_old - m_new)
            p = jnp.exp(s - m_new)
            l_s[g] = alpha * l_s[g] + jnp.sum(p, axis=1, keepdims=True)
            m_s[g] = m_new
            probs.append(p.astype(BF16))
            alphas.append(alpha)
        vblk = v_ref[pl.ds(k0, KEY_CHUNK), :]
        for g in range(N_KV):
            acc_s[g] = alphas[g] * acc_s[g] + jnp.dot(probs[g], vblk, preferred_element_type=F32)
        return carry

    lax.fori_loop(0, n_chunks, attend, 0)
    for g in range(N_KV):
        o_ref[g] = acc_s[g][:, g * HEAD_DIM:(g + 1) * HEAD_DIM] / l_s[g]


def _sparse_attention(mode, q_st, kt, v, strip, extra, qb, q_pos0, k_sel=0):
    bsz, nq = q_st.shape[:2]
    lp = v.shape[1]
    rows = GROUP * qb
    per_q = lambda *tail: pl.BlockSpec((None, None) + tail, lambda b, i: (b, i) + (0,) * len(tail))
    per_b = lambda *tail: pl.BlockSpec((None,) + tail, lambda b, i: (b,) + (0,) * len(tail))
    in_specs = [per_q(N_KV, rows, HEAD_DIM), per_b(N_KV, HEAD_DIM, lp), per_b(lp, N_KV * HEAD_DIM),
                pl.BlockSpec((N_KV, rows, STRIP_W), lambda b, i: (0, 0, 0))]
    scratch = [pltpu.VMEM((N_KV, rows, HEAD_DIM), BF16),
               pltpu.VMEM((N_KV, rows, 1), F32),
               pltpu.VMEM((N_KV, rows, 1), F32),
               pltpu.VMEM((N_KV, rows, N_KV * HEAD_DIM), F32)]
    if mode == "dsa":
        qi_st, wi_st, kit = extra
        in_specs += [per_q(IDX_HEADS * qb, IDX_DIM), per_q(IDX_HEADS * qb, 1), per_b(IDX_DIM, lp)]
        scratch.append(pltpu.VMEM((qb, lp), I32))
    else:
        (kmt,) = extra
        in_specs.append(per_b(N_KV, HEAD_DIM, kmt.shape[3]))
    return pl.pallas_call(
        functools.partial(_attn_kernel, mode, qb, lp, q_pos0, k_sel),
        grid=(bsz, nq),
        in_specs=in_specs,
        out_specs=per_q(N_KV, rows, HEAD_DIM),
        out_shape=jax.ShapeDtypeStruct((bsz, nq, N_KV, rows, HEAD_DIM), F32),
        scratch_shapes=scratch,
        compiler_params=_cparams(("parallel", "arbitrary")),
        name=mode + "_attention",
    )(q_st, kt, v, strip, *extra)


QT = 128


def _attn_t_kernel(mode, lp, k_sel, *refs):
    if mode == "dsa":
        (qt_ref, k_ref, vt_ref, strip_ref, qit_ref, wit_ref, ki_ref, o_ref,
         qbd_s, m_s, l_s, acc_s, key_s) = refs
    else:
        (qt_ref, k_ref, vt_ref, strip_ref, km_ref, o_ref,
         qbd_s, m_s, l_s, acc_s) = refs
    kv_dim = N_KV * HEAD_DIM
    cols = GROUP * QT
    i = pl.program_id(1)
    q0 = i * QT
    c_max = (q0 + QT - 1) // KEY_CHUNK
    n_chunks = c_max + 1
    qpos = q0 + lax.broadcasted_iota(I32, (1, QT), 1)
    krow = lax.broadcasted_iota(I32, (KEY_CHUNK, 1), 0)
    scale = HEAD_DIM ** -0.5

    qbd_s[...] = jnp.zeros(qbd_s.shape, BF16)
    for g in range(N_KV):
        qbd_s[g * HEAD_DIM:(g + 1) * HEAD_DIM, g * cols:(g + 1) * cols] = (
            qt_ref[g * HEAD_DIM:(g + 1) * HEAD_DIM, :] * scale).astype(BF16)
    m_s[...] = jnp.full(m_s.shape, NEG, F32)
    l_s[...] = jnp.zeros(l_s.shape, F32)
    acc_s[...] = jnp.zeros(acc_s.shape, F32)

    if mode == "dsa":
        qit = qit_ref[...].astype(BF16)
        wit = wit_ref[...]

        def score_chunk(c, carry):
            k0 = pl.multiple_of(c * KEY_CHUNK, KEY_CHUNK)
            kib = ki_ref[pl.ds(k0, KEY_CHUNK), :]
            sc = jnp.zeros((KEY_CHUNK, QT), F32)
            for hp in range(IDX_HEADS // 2):
                lanes = slice(2 * hp * QT, 2 * (hp + 1) * QT)
                s = jnp.maximum(jnp.dot(kib, qit[:, lanes], preferred_element_type=F32), 0.0) * wit[:, lanes]
                sc = sc + s[:, :QT] + s[:, QT:]
            sc = jnp.where(k0 + krow <= qpos, sc, -jnp.inf)
            bits = pltpu.bitcast(sc, I32)
            bits = jnp.where(bits == INT_MIN, 0, bits)
            key_s[pl.ds(k0, KEY_CHUNK), :] = jnp.where(bits < 0, bits ^ 0x7FFFFFFF, bits)
            return carry

        lax.fori_loop(0, n_chunks, score_chunk, 0)

        def count(pred):
            def body(c, acc):
                k0 = pl.multiple_of(c * KEY_CHUNK, KEY_CHUNK)
                hit = pred(key_s[pl.ds(k0, KEY_CHUNK), :], k0).astype(I32)
                return acc + jnp.sum(hit.reshape(KEY_CHUNK // 8, 8, QT), axis=0)
            acc = lax.fori_loop(0, n_chunks, body, jnp.zeros((8, QT), I32))
            return jnp.sum(acc, axis=0, keepdims=True)

        def bit_step(bi, thr):
            cand = thr + jnp.left_shift(jnp.int32(1), 31 - bi)
            cnt = count(lambda k, k0: k >= cand)
            return jnp.where(cnt >= k_sel, cand, thr)

        thr = lax.fori_loop(0, 32, bit_step, jnp.full((1, QT), INT_MIN, I32))
        n_gt = count(lambda k, k0: k > thr)
        n_ge = count(lambda k, k0: k >= thr)
        need = k_sel - n_gt

        def tie_limit():
            n_bits = max(1, (lp - 1).bit_length())

            def idx_step(bi, lo):
                cand = lo + jnp.left_shift(jnp.int32(1), n_bits - 1 - bi)
                cnt = count(lambda k, k0: (k == thr) & (k0 + krow <= cand))
                return jnp.where(cnt < need, cand, lo)

            lo = lax.fori_loop(0, n_bits, idx_step, jnp.full((1, QT), -1, I32))
            return lo + 1

        tie_idx = lax.cond(jnp.max(n_ge - n_gt - need) > 0, tie_limit,
                           lambda: jnp.full((1, QT), lp, I32))
    else:
        nbp = km_ref.shape[1]
        blk = lax.broadcasted_iota(I32, (nbp, QT), 0)
        sel_blocks = []
        for g in range(N_KV):
            qg = qt_ref[g * HEAD_DIM:(g + 1) * HEAD_DIM, :]
            qsum = qg[:, 0:QT]
            for j in range(1, GROUP):
                qsum = qsum + qg[:, j * QT:(j + 1) * QT]
            gate = jnp.dot(km_ref[g], qsum, preferred_element_type=F32, precision=lax.Precision.HIGHEST)
            gate = jnp.where(blk < c_max, gate, -jnp.inf)
            sel = jnp.zeros(blk.shape, I32)
            for _ in range(MOBA_TOPK):
                mx = jnp.max(gate, axis=0, keepdims=True)
                first = jnp.min(jnp.where(gate == mx, blk, nbp), axis=0, keepdims=True)
                hit = blk == first
                sel = jnp.where(hit & (blk < c_max), 1, sel)
                gate = jnp.where(hit, -jnp.inf, gate)
            sel_blocks.append(sel)

    def attend(c, carry):
        k0 = pl.multiple_of(c * KEY_CHUNK, KEY_CHUNK)
        kidx = k0 + krow
        causal = kidx <= qpos
        off = pl.multiple_of(jnp.clip(STRIP_ORIGIN - (q0 - k0), 0, STRIP_ORIGIN), 128)
        if mode == "dsa":
            k = key_s[pl.ds(k0, KEY_CHUNK), :]
            mask = ((k > thr) | ((k == thr) & (kidx <= tie_idx))) & causal
        kblk = k_ref[pl.ds(k0, KEY_CHUNK), :]

        def logits(g):
            return jnp.dot(kblk, qbd_s[:, g * cols:(g + 1) * cols], preferred_element_type=F32)

        s_next = logits(0)
        for g in range(N_KV):
            if mode != "dsa":
                picked = jnp.sum(jnp.where(blk == c, sel_blocks[g], 0), axis=0, keepdims=True)
                mask = jnp.where(c == c_max, causal.astype(I32), picked) > 0
            s_all = s_next
            if g + 1 < N_KV:
                s_next = logits(g + 1)
            probs, alphas = [], []
            for j in range(GROUP):
                h = g * GROUP + j
                lanes = slice(j * QT, (j + 1) * QT)
                s = s_all[:, lanes] + strip_ref[g, pl.ds(off, KEY_CHUNK), lanes]
                s = jnp.where(mask, s, NEG)
                m_old = m_s[pl.ds(h, 1), :]
                m_new = jnp.maximum(m_old, jnp.max(s, axis=0, keepdims=True))
                alpha = jnp.exp(m_old - m_new)
                p = jnp.exp(s - m_new)
                l_s[pl.ds(h, 1), :] = alpha * l_s[pl.ds(h, 1), :] + jnp.sum(p, axis=0, keepdims=True)
                m_s[pl.ds(h, 1), :] = m_new
                probs.append(p.astype(BF16))
                alphas.append(alpha)
            pv = jnp.dot(vt_ref[g * HEAD_DIM:(g + 1) * HEAD_DIM, pl.ds(k0, KEY_CHUNK)],
                         jnp.concatenate(probs, axis=1), preferred_element_type=F32)
            acc_s[g] = jnp.concatenate(alphas, axis=1) * acc_s[g] + pv
        return carry

    lax.fori_loop(0, n_chunks, attend, 0)
    for g in range(N_KV):
        inv = jnp.concatenate([1.0 / l_s[pl.ds(g * GROUP + j, 1), :] for j in range(GROUP)], axis=1)
        o_ref[g * HEAD_DIM:(g + 1) * HEAD_DIM, :] = acc_s[g] * inv


def _sparse_attention_t(mode, qt, k, vt, strip_t, extra, k_sel=0):
    bsz, nq = qt.shape[:2]
    lp = k.shape[1]
    kv_dim = N_KV * HEAD_DIM
    cols = GROUP * QT
    per_q = lambda *tail: pl.BlockSpec((None, None) + tail, lambda b, i: (b, i) + (0,) * len(tail))
    per_b = lambda *tail: pl.BlockSpec((None,) + tail, lambda b, i: (b,) + (0,) * len(tail))
    in_specs = [per_q(kv_dim, cols), per_b(lp, kv_dim), per_b(kv_dim, lp),
                pl.BlockSpec((N_KV, STRIP_W, cols), lambda b, i: (0, 0, 0))]
    scratch = [pltpu.VMEM((kv_dim, N_KV * cols), BF16),
               pltpu.VMEM((N_HEADS, QT), F32),
               pltpu.VMEM((N_HEADS, QT), F32),
               pltpu.VMEM((N_KV, HEAD_DIM, cols), F32)]
    if mode == "dsa":
        qit, wit, ki = extra
        in_specs += [per_q(IDX_DIM, IDX_HEADS * QT), per_q(1, IDX_HEADS * QT), per_b(lp, IDX_DIM)]
        scratch.append(pltpu.VMEM((lp, QT), I32))
    else:
        (km,) = extra
        in_specs.append(per_b(N_KV, km.shape[2], HEAD_DIM))
    return pl.pallas_call(
        functools.partial(_attn_t_kernel, mode, lp, k_sel),
        grid=(bsz, nq),
        in_specs=in_specs,
        out_specs=per_q(kv_dim, cols),
        out_shape=jax.ShapeDtypeStruct((bsz, nq, kv_dim, cols), F32),
        scratch_shapes=scratch,
        compiler_params=_cparams(("parallel", "arbitrary")),
        name=mode + "_attention_t",
    )(qt, k, vt, strip_t, *extra)


def _top_rows(s, k, payload=None, rank=None):
    rid = lax.broadcasted_iota(I32, s.shape, 0) if rank is None else rank
    vals, ids = [], []
    for _ in range(k):
        mx = jnp.max(s, axis=0, keepdims=True)
        first = jnp.min(jnp.where(s == mx, rid, jnp.iinfo(jnp.int32).max), axis=0, keepdims=True)
        hit = rid == first
        vals.append(mx)
        if payload is None:
            ids.append(first)
        else:
            ids.append(jnp.max(jnp.where(hit, payload, -1), axis=0, keepdims=True))
        s = jnp.where(hit, -jnp.inf, s)
    return jnp.concatenate(vals, axis=0), jnp.concatenate(ids, axis=0)


def _top_rows_paired(s, k):
    half = s.shape[0] // 2
    lo, hi = s[:half], s[half:]
    rid = lax.broadcasted_iota(I32, lo.shape, 0)
    first_lo = lo >= hi
    top = jnp.where(first_lo, lo, hi)
    rest = jnp.where(first_lo, hi, lo)
    top_id = jnp.where(first_lo, rid, rid + half)
    rest_id = jnp.where(first_lo, rid + half, rid)
    vals, ids = [], []
    for _ in range(k):
        mx = jnp.max(top, axis=0, keepdims=True)
        first = jnp.min(jnp.where(top == mx, top_id, jnp.iinfo(jnp.int32).max), axis=0, keepdims=True)
        hit = top_id == first
        vals.append(mx)
        ids.append(first)
        top = jnp.where(hit, rest, top)
        top_id = jnp.where(hit, rest_id, top_id)
        rest = jnp.where(hit, -jnp.inf, rest)
    return jnp.concatenate(vals, axis=0), jnp.concatenate(ids, axis=0)


def _peer_route_kernel(x_ref, wqt_ref, keys_ref, row_ref, par_ref, g_ref):
    xb = x_ref[...].astype(BF16)
    qt = lax.dot_general(wqt_ref[...], xb, (((1,), (1,)), ((), ())),
                         preferred_element_type=F32)
    n_t = x_ref.shape[0]
    piece = lax.broadcasted_iota(I32, (11 * 8, n_t), 0) // 8
    within = lax.broadcasted_iota(I32, (11 * 8, n_t), 0) % 8
    by_b = (piece >= 2) & (piece < 10)
    ca = jnp.where(piece < 2, 0, jnp.where(by_b, within, 8 + within))
    cb = jnp.where(piece == 0, within, jnp.where(piece == 1, 8 + within, jnp.where(by_b, piece - 2, 0)))
    cand_ok = ((ca + 1) * (cb + 1) <= PEER_TOPK) & jnp.logical_not(by_b & (within == 0))
    cand_rank = ca * PEER_TOPK + cb

    def pieces(first, second, combine):
        out = [combine(first[0:1], second[0:8]), combine(first[0:1], second[8:16])]
        out += [combine(first[0:8], second[b:b + 1]) for b in range(8)]
        out.append(combine(first[8:16], second[0:1]))
        return jnp.concatenate(out, axis=0)

    e_rows, g_rows = [], []
    for h in range(PEER_HEADS):
        sv, si = [], []
        for c in range(2):
            hc = 2 * h + c
            qhc = qt[hc * PEER_HALF:(hc + 1) * PEER_HALF].astype(BF16)
            s = jnp.dot(keys_ref[hc], qhc, preferred_element_type=F32)
            v_, i_ = _top_rows_paired(s, PEER_TOPK)
            sv.append(v_)
            si.append(i_)
        cand = jnp.where(cand_ok, pieces(sv[0], sv[1], lambda u, w: u + w), -jnp.inf)
        cidx = pieces(si[0], si[1], lambda u, w: u * PEER_NKEYS + w)
        gv, ge = _top_rows(cand, PEER_TOPK, payload=cidx, rank=cand_rank)
        ex = jnp.exp(gv - gv[0:1])
        g_rows.append(ex / jnp.sum(ex, axis=0, keepdims=True))
        e_rows.append(ge)
    e_t = jnp.concatenate(e_rows, axis=0).T
    row_ref[...] = jnp.right_shift(e_t, 1) * 8
    par_ref[...] = e_t & 1
    g_ref[...] = jnp.concatenate(g_rows, axis=0).T


def _peer_route(x, wqt_bf16, keys_bf16, tb):
    m, d = x.shape
    out = pl.BlockSpec((tb, PEER_PAIRS), lambda i: (i, 0))
    return pl.pallas_call(
        _peer_route_kernel,
        grid=(m // tb,),
        in_specs=[pl.BlockSpec((tb, d), lambda i: (i, 0)),
                  pl.BlockSpec(wqt_bf16.shape, lambda i: (0, 0)),
                  pl.BlockSpec(keys_bf16.shape, lambda i: (0, 0, 0))],
        out_specs=[out, out, out],
        out_shape=[jax.ShapeDtypeStruct((m, PEER_PAIRS), I32),
                   jax.ShapeDtypeStruct((m, PEER_PAIRS), I32),
                   jax.ShapeDtypeStruct((m, PEER_PAIRS), F32)],
        compiler_params=_cparams(("parallel",)),
        name="peer_route",
    )(x, wqt_bf16, keys_bf16)


FOLD_PAIRS = 32
UP_SLOTS = 4
TILE_ROWS = 8
HI_MASK = 0xFFFF0000


def _pack_table(w):
    n, d = w.shape
    bits = lax.bitcast_convert_type(w.astype(BF16), jnp.uint16).astype(jnp.uint32)
    bits = bits.reshape(n // 2, 2, d // 128, 128)
    return (bits[:, 0] | (bits[:, 1] << 16)).astype(jnp.uint32).reshape(n // 2 * (d // 128), 128)


def _load_tile(tbl_ref, row_ref, idx):
    return tbl_ref[pl.ds(pl.multiple_of(row_ref[idx], TILE_ROWS), TILE_ROWS), :]


def _peer_up_kernel(tb, row_ref, par_ref, x_ref, g_ref, fold_ref, tbl_ref, c_ref, prod_s):
    diag = (lax.broadcasted_iota(I32, (PEER_PAIRS, PEER_PAIRS), 0)
            == lax.broadcasted_iota(I32, (PEER_PAIRS, PEER_PAIRS), 1))

    def gather(t, slot):
        xt = x_ref[t]
        rows_t = row_ref.at[pl.ds(t * PEER_PAIRS, PEER_PAIRS)]
        for p in range(0, PEER_PAIRS, 2):
            w0 = _load_tile(tbl_ref, rows_t, p)
            w1 = _load_tile(tbl_ref, rows_t, p + 1)
            even = [pltpu.bitcast(jnp.left_shift(w, jnp.uint32(16)), F32) * xt for w in (w0, w1)]
            odd = [pltpu.bitcast(w & jnp.uint32(HI_MASK), F32) * xt for w in (w0, w1)]
            rows = pl.ds(p * TILE_ROWS, 2 * TILE_ROWS)
            prod_s[slot, rows, 0:128] = jnp.concatenate(even, axis=0).astype(BF16)
            prod_s[slot, rows, 128:256] = jnp.concatenate(odd, axis=0).astype(BF16)

    def reduce(t, slot):
        depth = FOLD_PAIRS * TILE_ROWS
        parts = [jnp.dot(fold_ref[...], prod_s[slot, pl.ds(i * depth, depth), :], preferred_element_type=F32)
                 for i in range(PEER_PAIRS // FOLD_PAIRS)]
        part = jnp.concatenate(parts, axis=0)
        rows = []
        for half in range(2):
            col = jnp.sum(part[:, half * 128:(half + 1) * 128], axis=1, keepdims=True)
            rows.append(jnp.sum(jnp.where(diag, col, 0.0), axis=0, keepdims=True))
        a = jnp.where(par_ref[pl.ds(t, 1), :] == 1, rows[1], rows[0])
        act = 0.5 * a * (1.0 + lax.erf(a * (2.0 ** -0.5)))
        c_ref[pl.ds(t, 1), :] = g_ref[pl.ds(t, 1), :] * act

    prod_s[UP_SLOTS:2 * UP_SLOTS] = jnp.zeros((UP_SLOTS,) + prod_s.shape[1:], BF16)

    def step(k, carry):
        t0 = 2 * UP_SLOTS * k
        for j in range(UP_SLOTS):
            gather(t0 + j, j)
            reduce(jnp.maximum(t0 - UP_SLOTS + j, 0), UP_SLOTS + j)
        for j in range(UP_SLOTS):
            gather(t0 + UP_SLOTS + j, UP_SLOTS + j)
            reduce(t0 + j, j)
        return carry

    lax.fori_loop(0, tb // (2 * UP_SLOTS), step, 0)
    for j in range(UP_SLOTS):
        reduce(tb - UP_SLOTS + j, UP_SLOTS + j)


DOWN_GROUP = 8


def _peer_down_kernel(tb, row_ref, par_ref, c_ref, spread_ref, tbl_ref, f_ref, tile_s, coef_s):
    width = PEER_PAIRS * 2 * TILE_ROWS
    lane = lax.broadcasted_iota(I32, (TILE_ROWS, width), 1)
    own_row = ((lane & (2 * TILE_ROWS - 1)) >> 1) == lax.broadcasted_iota(I32, (TILE_ROWS, width), 0)
    lane_par = (lax.broadcasted_iota(I32, (DOWN_GROUP, width), 1) & 1).astype(F32)

    def gather(t, slot):
        rows_t = row_ref.at[pl.ds(t * PEER_PAIRS, PEER_PAIRS)]
        for p in range(PEER_PAIRS):
            tile_s[slot, pl.ds(p * TILE_ROWS, TILE_ROWS), :] = _load_tile(tbl_ref, rows_t, p)

    def apply(t, j, slot):
        tiles = pltpu.bitcast(tile_s[slot], BF16)
        lhs = [jnp.where(own_row, coef_s[pl.ds(part * DOWN_GROUP + j, 1), :], 0.0) for part in range(2)]
        out = jnp.dot(jnp.concatenate(lhs, axis=0).astype(BF16), tiles, preferred_element_type=F32)
        f_ref[t] = out[0:TILE_ROWS] + out[TILE_ROWS:2 * TILE_ROWS]

    def group(gi, carry):
        t0 = pl.multiple_of(gi * DOWN_GROUP, DOWN_GROUP)
        gather(t0, 0)
        c = c_ref[pl.ds(t0, DOWN_GROUP), :]
        c_hi = c.astype(BF16)
        c_lo = (c - c_hi.astype(F32)).astype(BF16)
        par = par_ref[pl.ds(t0, DOWN_GROUP), :].astype(BF16)
        rep = jnp.dot(jnp.concatenate([c_hi, c_lo, par], axis=0), spread_ref[...],
                      preferred_element_type=F32)
        keep = rep[2 * DOWN_GROUP:] == lane_par
        coef_s[0:DOWN_GROUP, :] = jnp.where(keep, rep[0:DOWN_GROUP], 0.0)
        coef_s[DOWN_GROUP:2 * DOWN_GROUP, :] = jnp.where(keep, rep[DOWN_GROUP:2 * DOWN_GROUP], 0.0)
        for j in range(DOWN_GROUP):
            if j + 1 < DOWN_GROUP:
                gather(t0 + j + 1, (j + 1) % 2)
            apply(t0 + j, j, j % 2)
        return carry

    lax.fori_loop(0, tb // DOWN_GROUP, group, 0)


def _peer_experts(x, row, par, g, u_tbl, v_tbl, tb):
    m, d = x.shape
    x3 = x.reshape(m, d // 128, 128)
    flat = row.reshape(m * PEER_PAIRS)
    smem = pl.BlockSpec((tb * PEER_PAIRS,), lambda i: (i,), memory_space=pltpu.SMEM)
    vrow = pl.BlockSpec((tb, PEER_PAIRS), lambda i: (i, 0))
    tile = pl.BlockSpec((tb, d // 128, 128), lambda i: (i, 0, 0))
    resident = pl.BlockSpec(memory_space=pltpu.VMEM)
    const = lambda a: pl.BlockSpec(a.shape, lambda i: (0, 0))
    fold = jnp.asarray(np.arange(FOLD_PAIRS * TILE_ROWS)[None, :] // TILE_ROWS == np.arange(FOLD_PAIRS)[:, None], BF16)
    width = PEER_PAIRS * 2 * TILE_ROWS
    spread = jnp.asarray(np.arange(width)[None, :] // (2 * TILE_ROWS) == np.arange(PEER_PAIRS)[:, None], BF16)
    c = pl.pallas_call(
        functools.partial(_peer_up_kernel, tb),
        grid=(m // tb,),
        in_specs=[smem, vrow, tile, vrow, const(fold), resident],
        out_specs=vrow,
        out_shape=jax.ShapeDtypeStruct((m, PEER_PAIRS), F32),
        scratch_shapes=[pltpu.VMEM((2 * UP_SLOTS, PEER_PAIRS * TILE_ROWS, 256), BF16)],
        compiler_params=_cparams(("arbitrary",)),
        name="peer_up",
    )(flat, par, x3, g, fold, u_tbl)
    f = pl.pallas_call(
        functools.partial(_peer_down_kernel, tb),
        grid=(m // tb,),
        in_specs=[smem, vrow, vrow, const(spread), resident],
        out_specs=tile,
        out_shape=jax.ShapeDtypeStruct((m, d // 128, 128), F32),
        scratch_shapes=[pltpu.VMEM((2, PEER_PAIRS * TILE_ROWS, 128), jnp.uint32),
                        pltpu.VMEM((2 * DOWN_GROUP, width), F32)],
        compiler_params=_cparams(("arbitrary",)),
        name="peer_down",
    )(flat, par, c, spread, v_tbl)
    return f.reshape(m, d)


def _stack_heads(a, qb, n_lead):
    bsz, q, n, hd = a.shape
    n_in = n // n_lead
    a = a.reshape(bsz, q // qb, qb, n_lead, n_in, hd)
    return a.transpose(0, 1, 3, 4, 2, 5).reshape(bsz, q // qb, n_lead, n_in * qb, hd)


def _unstack_heads(o, qb):
    bsz, nq = o.shape[:2]
    o = o.reshape(bsz, nq, N_KV, GROUP, qb, HEAD_DIM).transpose(0, 1, 4, 2, 3, 5)
    return o.reshape(bsz * nq * qb, N_HEADS * HEAD_DIM)


def _run_trunk(x, p, past, prm, qb, tm, tb_route, tb_exp):
    bsz, q_len, d = x.shape
    depth = prm["w_o"].shape[0]
    m = bsz * q_len
    tm, tb_route, tb_exp = min(tm, m), min(tb_route, m), min(tb_exp, m)
    alpha = (2 * depth) ** 0.25
    attn_dim = N_HEADS * HEAD_DIM
    kv_dim = N_KV * HEAD_DIM
    if past is None:
        q_pos0 = 0
        n_pages = q_len // PAGE
        table = jnp.arange(bsz * n_pages, dtype=I32).reshape(bsz, n_pages)
    else:
        cache_k, cache_v, cache_ki, table = past
        q_pos0 = table.shape[1] * PAGE
    on_lanes = past is None and q_len % QT == 0
    if on_lanes:
        qb = QT
    nq = q_len // qb
    strip = _bias_strip(prm["rel_bias"], qb).reshape(N_KV, GROUP, qb, STRIP_W)
    if on_lanes:
        strip = strip.transpose(0, 3, 1, 2).reshape(N_KV, STRIP_W, GROUP * qb)
    else:
        strip = strip.reshape(N_KV, GROUP * qb, STRIP_W)
    xf = x.reshape(m, d)
    ks, vs, kis = [], [], []

    def relayout(new_rows, cache, layer, want):
        w = new_rows.shape[2]
        if past is None:
            return _relayout(new_rows.reshape(bsz * n_pages, PAGE, w), table, None, False, want)
        n_pool = cache.shape[1]
        pool = cache.reshape((cache.shape[0] * n_pool,) + cache.shape[2:])
        pool_t = jnp.moveaxis(pool, 1, -1).reshape(pool.shape[0], w, PAGE)
        pad = jnp.zeros((bsz, STEP_KEYS - q_len, w), F32)
        return _relayout(pool_t, table + layer * n_pool, jnp.concatenate([new_rows, pad], axis=1), True, want)

    for l in range(depth):
        dsa = l % 2 == 0
        w_in = prm["w_in_dsa"][l // 2] if dsa else prm["w_in_moba"][l // 2]
        n_in = w_in.shape[1]
        n_pad = -(-n_in // 128) * 128
        w_in = jnp.pad(w_in, ((0, 0), (0, n_pad - n_in))).astype(BF16)
        proj = _matmul(xf, w_in, tm)
        qh = proj[:, :attn_dim].reshape(bsz, q_len, N_HEADS, HEAD_DIM)
        k = proj[:, attn_dim:attn_dim + kv_dim].reshape(bsz, q_len, kv_dim)
        v = proj[:, attn_dim + kv_dim:attn_dim + 2 * kv_dim].reshape(bsz, q_len, kv_dim)
        ks.append(k.reshape(bsz, q_len, N_KV, HEAD_DIM))
        vs.append(v.reshape(bsz, q_len, N_KV, HEAD_DIM))
        k_lay = relayout(k, None if past is None else cache_k, l,
                         (("x",) if on_lanes else ("xt",)) + (() if dsa else ("sums",)))
        v_lay = relayout(v, None if past is None else cache_v, l, ("xt",) if on_lanes else ("x",))
        if on_lanes:
            q_in = qh.reshape(bsz, nq, qb, N_KV, GROUP, HEAD_DIM).transpose(0, 1, 3, 5, 4, 2)
            q_in = q_in.reshape(bsz, nq, kv_dim, GROUP * qb)
            kr, vt = k_lay["x"], v_lay["xt"]
        else:
            kt = k_lay["xt"].reshape(bsz, N_KV, HEAD_DIM, k_lay["xt"].shape[2])
            vr = v_lay["x"]
            q_in = _stack_heads(qh, qb, N_KV)
        if dsa:
            off = attn_dim + 2 * kv_dim
            qi = proj[:, off:off + IDX_HEADS * IDX_DIM].reshape(bsz, q_len, IDX_HEADS, IDX_DIM)
            off += IDX_HEADS * IDX_DIM
            ki = proj[:, off:off + IDX_DIM].reshape(bsz, q_len, IDX_DIM)
            wi = proj[:, off + IDX_DIM:off + IDX_DIM + IDX_HEADS].reshape(bsz, q_len, IDX_HEADS, 1)
            kis.append(ki)
            ki_lay = relayout(ki, None if past is None else cache_ki, l // 2, ("x",) if on_lanes else ("xt",))
            k_sel = min(IDX_TOPK, (q_pos0 + q_len) // 4)
            if on_lanes:
                qi_t = qi.reshape(bsz, nq, qb, IDX_HEADS, IDX_DIM).transpose(0, 1, 4, 3, 2)
                qi_t = qi_t.reshape(bsz, nq, IDX_DIM, IDX_HEADS * qb)
                wi_t = wi.reshape(bsz, nq, qb, IDX_HEADS).transpose(0, 1, 3, 2).reshape(bsz, nq, 1, IDX_HEADS * qb)
                o = _sparse_attention_t("dsa", q_in, kr, vt, strip, (qi_t, wi_t, ki_lay["x"]), k_sel)
            else:
                qi_st = _stack_heads(qi, qb, 1)[:, :, 0]
                wi_st = _stack_heads(wi, qb, 1)[:, :, 0]
                o = _sparse_attention("dsa", q_in, kt, vr, strip, (qi_st, wi_st, ki_lay["xt"]), qb, q_pos0, k_sel)
        else:
            ksum = k_lay["sums"]
            nb = ksum.shape[1]
            kmean = (ksum * (1.0 / MOBA_BLOCK)).reshape(bsz, nb, N_KV, HEAD_DIM)
            if on_lanes:
                km = jnp.pad(kmean.transpose(0, 2, 1, 3), ((0, 0), (0, 0), (0, -nb % 8), (0, 0)))
                o = _sparse_attention_t("moba", q_in, kr, vt, strip, (km,))
            else:
                kmt = jnp.pad(kmean.transpose(0, 2, 3, 1), ((0, 0), (0, 0), (0, 0), (0, -nb % 128)))
                o = _sparse_attention("moba", q_in, kt, vr, strip, (kmt,), qb, q_pos0)
        if on_lanes:
            o = o.reshape(bsz, nq, N_KV, HEAD_DIM, GROUP, qb).transpose(0, 1, 5, 2, 4, 3)
            o = o.reshape(m, attn_dim)
        else:
            o = _unstack_heads(o, qb)
        xf = _attn_out(o, prm["w_o"][l].astype(BF16), xf, prm["ln_g"][l, 0][None], prm["ln_b"][l, 0][None],
                       alpha, tm)
        row, par, g = _peer_route(xf, prm["peer_wq"][l].T.astype(BF16),
                                 prm["peer_keys"][l].reshape(2 * PEER_HEADS, PEER_NKEYS, PEER_HALF).astype(BF16),
                                 tb_route)
        f = _peer_experts(xf, row, par, g, _pack_table(prm["peer_u"][l]), _pack_table(prm["peer_v"][l]), tb_exp)
        xf = _layer_tail(xf, f, prm["ln_g"][l, 1][None], prm["ln_b"][l, 1][None],
                         prm["gate_w"][l].astype(BF16), prm["gate_b"][l][None],
                         p[l].reshape(m, -1), prm["ple_w"][l].astype(BF16), alpha, tm)
    return xf.reshape(bsz, q_len, d), jnp.stack(ks), jnp.stack(vs), jnp.stack(kis)


def kernel(x_prompt, x_sample, cache_k, cache_v, cache_ki, page_table, p_prompt, p_sample, w_in_dsa, w_in_moba, w_o, rel_bias, ln_g, ln_b, peer_wq, peer_keys, peer_u, peer_v, gate_w, gate_b, ple_w):
    prm = {
        "w_in_dsa": w_in_dsa, "w_in_moba": w_in_moba, "w_o": w_o, "rel_bias": rel_bias,
        "ln_g": ln_g, "ln_b": ln_b, "peer_wq": peer_wq, "peer_keys": peer_keys,
        "peer_u": peer_u, "peer_v": peer_v, "gate_w": gate_w, "gate_b": gate_b, "ple_w": ple_w,
    }
    q_sample = x_sample.shape[1]
    y_prompt, k_prompt, v_prompt, ki_prompt = _run_trunk(
        x_prompt, p_prompt, None, prm, qb=128, tm=512, tb_route=256, tb_exp=128)
    y_sample, k_sample, v_sample, ki_sample = _run_trunk(
        x_sample, p_sample, (cache_k, cache_v, cache_ki, page_table), prm,
        qb=q_sample, tm=256, tb_route=256, tb_exp=64)
    return (y_prompt, y_sample, k_prompt, v_prompt, ki_prompt, k_sample, v_sample, ki_sample)
```

```python
import functools
import math

import numpy as np
import jax
import jax.numpy as jnp
from jax import lax
from jax.experimental import pallas as pl
from jax.experimental.pallas import tpu as pltpu

F32 = jnp.float32
BF16 = jnp.bfloat16
I32 = jnp.int32

N_HEADS = 16
HEAD_DIM = 64
N_KV = 4
GROUP = 4
IDX_HEADS = 8
IDX_DIM = 64
IDX_TOPK = 256
MOBA_BLOCK = 256
MOBA_TOPK = 3
REL_BUCKETS = 32
REL_MAX_DIST = 128
PEER_HEADS = 8
PEER_NKEYS = 128
PEER_HALF = 128
PEER_TOPK = 16
PEER_PAIRS = PEER_HEADS * PEER_TOPK
LN_EPS = 1e-5
PAGE = 128

KEY_CHUNK = 256
STRIP_W = 768
STRIP_ORIGIN = 512
NEG = -1e30
INT_MIN = -2147483648
VMEM_LIMIT = 56 * 1024 * 1024


def _cparams(sem):
    return pltpu.CompilerParams(dimension_semantics=sem, vmem_limit_bytes=VMEM_LIMIT)


def _mm_kernel(a_ref, b_ref, o_ref):
    o_ref[...] = jnp.dot(a_ref[...].astype(BF16), b_ref[...], preferred_element_type=F32)


def _matmul(a, b_bf16, tm):
    m, k = a.shape
    n = b_bf16.shape[1]
    return pl.pallas_call(
        _mm_kernel,
        grid=(m // tm,),
        in_specs=[pl.BlockSpec((tm, k), lambda i: (i, 0)),
                  pl.BlockSpec((k, n), lambda i: (0, 0))],
        out_specs=pl.BlockSpec((tm, n), lambda i: (i, 0)),
        out_shape=jax.ShapeDtypeStruct((m, n), F32),
        compiler_params=_cparams(("parallel",)),
        name="in_proj",
    )(a, b_bf16)


def _layer_norm(y, g, b):
    mu = jnp.mean(y, axis=-1, keepdims=True)
    yc = y - mu
    var = jnp.mean(yc * yc, axis=-1, keepdims=True)
    return yc * lax.rsqrt(var + LN_EPS) * g + b


def _attn_out_kernel(alpha, o_ref, w_ref, x_ref, g_ref, b_ref, y_ref):
    y = alpha * x_ref[...] + jnp.dot(o_ref[...].astype(BF16), w_ref[...],
                                     preferred_element_type=F32)
    y_ref[...] = _layer_norm(y, g_ref[...], b_ref[...])


def _attn_out(o, w_bf16, x, g, b, alpha, tm):
    m, d = x.shape
    row = pl.BlockSpec((tm, d), lambda i: (i, 0))
    vec = pl.BlockSpec((1, d), lambda i: (0, 0))
    return pl.pallas_call(
        functools.partial(_attn_out_kernel, alpha),
        grid=(m // tm,),
        in_specs=[row, pl.BlockSpec((d, d), lambda i: (0, 0)), row, vec, vec],
        out_specs=row,
        out_shape=jax.ShapeDtypeStruct((m, d), F32),
        compiler_params=_cparams(("parallel",)),
        name="attn_out_ln",
    )(o, w_bf16, x, g, b)


def _layer_tail_kernel(alpha, x_ref, f_ref, g_ref, b_ref, gw_ref, gb_ref, p_ref, pw_ref, y_ref):
    x2 = _layer_norm(alpha * x_ref[...] + f_ref[...], g_ref[...], b_ref[...])
    z = jnp.dot(x2.astype(BF16), gw_ref[...], preferred_element_type=F32) + gb_ref[...]
    gate = 1.0 / (1.0 + jnp.exp(-z))
    e = jnp.dot(p_ref[...].astype(BF16), pw_ref[...], preferred_element_type=F32)
    y_ref[...] = x2 + gate * e


def _layer_tail(x, f, g, b, gw_bf16, gb, p, pw_bf16, alpha, tm):
    m, d = x.shape
    pd = p.shape[1]
    row = pl.BlockSpec((tm, d), lambda i: (i, 0))
    vec = pl.BlockSpec((1, d), lambda i: (0, 0))
    return pl.pallas_call(
        functools.partial(_layer_tail_kernel, alpha),
        grid=(m // tm,),
        in_specs=[row, row, vec, vec, pl.BlockSpec((d, d), lambda i: (0, 0)), vec,
                  pl.BlockSpec((tm, pd), lambda i: (i, 0)),
                  pl.BlockSpec((pd, d), lambda i: (0, 0))],
        out_specs=row,
        out_shape=jax.ShapeDtypeStruct((m, d), F32),
        compiler_params=_cparams(("parallel",)),
        name="layer_tail",
    )(x, f, g, b, gw_bf16, gb, p, pw_bf16)


STEP_PAGES = 8
STEP_KEYS = STEP_PAGES * PAGE
STEP_CHUNKS = STEP_KEYS // KEY_CHUNK


def _relayout_kernel(n_pool_steps, has_new, pages_t, want, *refs):
    page_refs = refs[1:1 + STEP_PAGES]
    new_ref = refs[1 + STEP_PAGES] if has_new else None
    outs = dict(zip(want, refs[len(refs) - len(want):]))
    j = pl.program_id(1)
    need_rows = "x" in want or "sums" in want

    def emit(rows, cols):
        if "xt" in want:
            outs["xt"][...] = (rows.T if cols is None else cols).astype(BF16)
        if need_rows and rows is None:
            rows = cols.T
        if "x" in want:
            outs["x"][...] = rows.astype(BF16)
        if "sums" in want:
            for i in range(STEP_CHUNKS):
                outs["sums"][pl.ds(i, 1), :] = jnp.sum(rows[i * KEY_CHUNK:(i + 1) * KEY_CHUNK], axis=0, keepdims=True)

    def from_pages():
        if pages_t:
            emit(None, jnp.concatenate([r[...] for r in page_refs], axis=1))
        else:
            emit(jnp.concatenate([r[...] for r in page_refs], axis=0), None)

    if has_new:
        pl.when(j < n_pool_steps)(from_pages)

        @pl.when(j >= n_pool_steps)
        def _():
            emit(new_ref[...], None)
    else:
        from_pages()


def _relayout(pool, table, new, pages_t, want):
    bsz, n_pages = table.shape
    w = pool.shape[1] if pages_t else pool.shape[2]
    n_pool_steps = n_pages // STEP_PAGES
    n_steps = n_pool_steps + (1 if new is not None else 0)
    lp = n_steps * STEP_KEYS
    last = n_pool_steps - 1

    def page_map(i):
        return lambda b, j, tbl: (tbl[b, STEP_PAGES * jnp.minimum(j, last) + i], 0, 0)

    page_block = (None, w, PAGE) if pages_t else (None, PAGE, w)
    in_specs = [pl.BlockSpec(page_block, page_map(i)) for i in range(STEP_PAGES)]
    args = [pool] * STEP_PAGES
    if new is not None:
        in_specs.append(pl.BlockSpec((None, STEP_KEYS, w), lambda b, j, tbl: (b, 0, 0)))
        args.append(new)
    specs = {"xt": (pl.BlockSpec((None, w, STEP_KEYS), lambda b, j, tbl: (b, 0, j)),
                    jax.ShapeDtypeStruct((bsz, w, lp), BF16)),
             "x": (pl.BlockSpec((None, STEP_KEYS, w), lambda b, j, tbl: (b, j, 0)),
                   jax.ShapeDtypeStruct((bsz, lp, w), BF16)),
             "sums": (pl.BlockSpec((None, None, STEP_CHUNKS, w), lambda b, j, tbl: (b, j, 0, 0)),
                      jax.ShapeDtypeStruct((bsz, n_steps, STEP_CHUNKS, w), F32))}
    outs = pl.pallas_call(
        functools.partial(_relayout_kernel, n_pool_steps, new is not None, pages_t, want),
        grid_spec=pltpu.PrefetchScalarGridSpec(
            num_scalar_prefetch=1,
            grid=(bsz, n_steps),
            in_specs=in_specs,
            out_specs=[specs[name][0] for name in want]),
        out_shape=[specs[name][1] for name in want],
        compiler_params=_cparams(("parallel", "arbitrary")),
        name="kv_relayout",
    )(table, *args)
    outs = dict(zip(want, outs))
    if "sums" in outs:
        outs["sums"] = outs["sums"].reshape(bsz, n_steps * STEP_CHUNKS, w)
    return outs


def _bucket_table():
    n = np.arange(REL_MAX_DIST + 1)
    exact = REL_BUCKETS // 2
    nf = np.maximum(n, 1).astype(np.float32)
    large = exact + (np.log(nf / np.float32(exact)) / np.float32(math.log(REL_MAX_DIST / exact))
                     * np.float32(REL_BUCKETS - exact)).astype(np.int32)
    return np.where(n < exact, n, np.minimum(large, REL_BUCKETS - 1)).astype(np.int32)


def _bias_strip_kernel(bk_ref, rb_ref, o_ref):
    bk = bk_ref[...]
    for h in range(N_HEADS):
        acc = jnp.zeros(bk.shape, F32)
        for bkt in range(REL_BUCKETS):
            acc = jnp.where(bk == bkt, rb_ref[bkt, h], acc)
        o_ref[h] = acc


def _bias_strip(rel_bias, qb):
    qi = np.arange(qb)[:, None]
    z = np.arange(STRIP_W)[None, :]
    dist = np.clip(qi - z + STRIP_ORIGIN, 0, REL_MAX_DIST)
    bk = jnp.asarray(_bucket_table()[dist])
    out = pl.pallas_call(
        _bias_strip_kernel,
        in_specs=[pl.BlockSpec(memory_space=pltpu.VMEM), pl.BlockSpec(memory_space=pltpu.SMEM)],
        out_specs=pl.BlockSpec(memory_space=pltpu.VMEM),
        out_shape=jax.ShapeDtypeStruct((N_HEADS, qb, STRIP_W), F32),
        name="bias_strip",
    )(bk, rel_bias)
    return out


def _attn_kernel(mode, qb, lp, q_pos0, k_sel, *refs):
    if mode == "dsa":
        (q_ref, kt_ref, v_ref, strip_ref, qi_ref, wi_ref, kit_ref, o_ref,
         qs_s, m_s, l_s, acc_s, key_s) = refs
    else:
        (q_ref, kt_ref, v_ref, strip_ref, kmt_ref, o_ref,
         qs_s, m_s, l_s, acc_s) = refs
    rows = GROUP * qb
    i = pl.program_id(1)
    q0 = q_pos0 + i * qb
    c_max = (q0 + qb - 1) // KEY_CHUNK
    n_chunks = c_max + 1
    qpos = q0 + lax.broadcasted_iota(I32, (qb, 1), 0)
    lane = lax.broadcasted_iota(I32, (1, KEY_CHUNK), 1)
    scale = HEAD_DIM ** -0.5

    for g in range(N_KV):
        qs_s[g] = (q_ref[g] * scale).astype(BF16)
    m_s[...] = jnp.full(m_s.shape, NEG, F32)
    l_s[...] = jnp.zeros(l_s.shape, F32)
    acc_s[...] = jnp.zeros(acc_s.shape, F32)

    if mode == "dsa":
        qi = qi_ref[...].astype(BF16)
        wi = wi_ref[...]

        def score_chunk(c, carry):
            k0 = pl.multiple_of(c * KEY_CHUNK, KEY_CHUNK)
            s = jnp.dot(qi, kit_ref[:, pl.ds(k0, KEY_CHUNK)], preferred_element_type=F32)
            s = jnp.maximum(s, 0.0) * wi
            sc = s[0:qb]
            for h in range(1, IDX_HEADS):
                sc = sc + s[h * qb:(h + 1) * qb]
            sc = jnp.where(k0 + lane <= qpos, sc, -jnp.inf)
            bits = pltpu.bitcast(sc, I32)
            bits = jnp.where(bits == INT_MIN, 0, bits)
            key_s[:, pl.ds(k0, KEY_CHUNK)] = jnp.where(bits < 0, bits ^ 0x7FFFFFFF, bits)
            return carry

        scan = STEP_KEYS if qb <= 32 else KEY_CHUNK
        n_scan = (n_chunks * KEY_CHUNK + scan - 1) // scan
        lax.fori_loop(0, n_scan * (scan // KEY_CHUNK), score_chunk, 0)
        scan_lane = lax.broadcasted_iota(I32, (1, scan), 1)

        def count(pred):
            def body(c, acc):
                k0 = pl.multiple_of(c * scan, scan)
                return acc + pred(key_s[:, pl.ds(k0, scan)], k0 + scan_lane).astype(I32)
            acc = lax.fori_loop(0, n_scan, body, jnp.zeros((qb, scan), I32))
            return jnp.sum(acc, axis=1, keepdims=True)

        def bit_step(bi, thr):
            cand = thr + jnp.left_shift(jnp.int32(1), 31 - bi)
            cnt = count(lambda k, kidx: k >= cand)
            return jnp.where(cnt >= k_sel, cand, thr)

        thr = lax.fori_loop(0, 32, bit_step, jnp.full((qb, 1), INT_MIN, I32))
        n_gt = count(lambda k, kidx: k > thr)
        n_ge = count(lambda k, kidx: k >= thr)
        need = k_sel - n_gt

        def tie_limit():
            n_bits = max(1, (lp - 1).bit_length())

            def idx_step(bi, lo):
                cand = lo + jnp.left_shift(jnp.int32(1), n_bits - 1 - bi)
                cnt = count(lambda k, kidx: (k == thr) & (kidx <= cand))
                return jnp.where(cnt < need, cand, lo)

            lo = lax.fori_loop(0, n_bits, idx_step, jnp.full((qb, 1), -1, I32))
            return lo + 1

        tie_idx = lax.cond(jnp.max(n_ge - n_gt - need) > 0, tie_limit,
                           lambda: jnp.full((qb, 1), lp, I32))
    else:
        blk = lax.broadcasted_iota(I32, (qb, kmt_ref.shape[2]), 1)
        sel_blocks = []
        for g in range(N_KV):
            qg = q_ref[g]
            qsum = qg[0:qb]
            for j in range(1, GROUP):
                qsum = qsum + qg[j * qb:(j + 1) * qb]
            gate = jnp.dot(qsum, kmt_ref[g], preferred_element_type=F32,
                           precision=lax.Precision.HIGHEST)
            gate = jnp.where(blk < c_max, gate, -jnp.inf)
            sel = jnp.zeros(blk.shape, jnp.bool_)
            for _ in range(MOBA_TOPK):
                mx = jnp.max(gate, axis=1, keepdims=True)
                first = jnp.min(jnp.where(gate == mx, blk, blk.shape[1]), axis=1, keepdims=True)
                hit = blk == first
                sel = sel | (hit & (blk < c_max))
                gate = jnp.where(hit, -jnp.inf, gate)
            sel_blocks.append(sel.astype(I32))

    def attend(c, carry):
        k0 = pl.multiple_of(c * KEY_CHUNK, KEY_CHUNK)
        kidx = k0 + lane
        causal = kidx <= qpos
        off = pl.multiple_of(jnp.clip(STRIP_ORIGIN - (q0 - k0), 0, STRIP_ORIGIN), 128)
        if mode == "dsa":
            k = key_s[:, pl.ds(k0, KEY_CHUNK)]
            mask1 = ((k > thr) | ((k == thr) & (kidx <= tie_idx))) & causal
            mask = jnp.concatenate([mask1] * GROUP, axis=0)

        logits = [jnp.dot(qs_s[g], kt_ref[g, :, pl.ds(k0, KEY_CHUNK)], preferred_element_type=F32)
                  for g in range(N_KV)]
        probs, alphas = [], []
        for g in range(N_KV):
            if mode != "dsa":
                picked = jnp.sum(jnp.where(blk == c, sel_blocks[g], 0), axis=1, keepdims=True)
                mask1 = jnp.where(c == c_max, causal.astype(I32), picked) > 0
                mask = jnp.concatenate([mask1] * GROUP, axis=0)
            s = jnp.where(mask, logits[g] + strip_ref[g, :, pl.ds(off, KEY_CHUNK)], NEG)
            m_old = m_s[g]
            m_new = jnp.maximum(m_old, jnp.max(s, axis=1, keepdims=True))
            alpha = jnp.exp(m_old - m_new)
            p = jnp.exp(s - m_new)
            l_s[g] = alpha * l_s[g] + jnp.sum(p, axis=1, keepdims=True)
            m_s[g] = m_new
            probs.append(p.astype(BF16))
            alphas.append(alpha)
        vblk = v_ref[pl.ds(k0, KEY_CHUNK), :]
        for g in range(N_KV):
            acc_s[g] = alphas[g] * acc_s[g] + jnp.dot(probs[g], vblk, preferred_element_type=F32)
        return carry

    lax.fori_loop(0, n_chunks, attend, 0)
    for g in range(N_KV):
        o_ref[g] = acc_s[g][:, g * HEAD_DIM:(g + 1) * HEAD_DIM] / l_s[g]


def _sparse_attention(mode, q_st, kt, v, strip, extra, qb, q_pos0, k_sel=0):
    bsz, nq = q_st.shape[:2]
    lp = v.shape[1]
    rows = GROUP * qb
    per_q = lambda *tail: pl.BlockSpec((None, None) + tail, lambda b, i: (b, i) + (0,) * len(tail))
    per_b = lambda *tail: pl.BlockSpec((None,) + tail, lambda b, i: (b,) + (0,) * len(tail))
    in_specs = [per_q(N_KV, rows, HEAD_DIM), per_b(N_KV, HEAD_DIM, lp), per_b(lp, N_KV * HEAD_DIM),
                pl.BlockSpec((N_KV, rows, STRIP_W), lambda b, i: (0, 0, 0))]
    scratch = [pltpu.VMEM((N_KV, rows, HEAD_DIM), BF16),
               pltpu.VMEM((N_KV, rows, 1), F32),
               pltpu.VMEM((N_KV, rows, 1), F32),
               pltpu.VMEM((N_KV, rows, N_KV * HEAD_DIM), F32)]
    if mode == "dsa":
        qi_st, wi_st, kit = extra
        in_specs += [per_q(IDX_HEADS * qb, IDX_DIM), per_q(IDX_HEADS * qb, 1), per_b(IDX_DIM, lp)]
        scratch.append(pltpu.VMEM((qb, lp), I32))
    else:
        (kmt,) = extra
        in_specs.append(per_b(N_KV, HEAD_DIM, kmt.shape[3]))
    return pl.pallas_call(
        functools.partial(_attn_kernel, mode, qb, lp, q_pos0, k_sel),
        grid=(bsz, nq),
        in_specs=in_specs,
        out_specs=per_q(N_KV, rows, HEAD_DIM),
        out_shape=jax.ShapeDtypeStruct((bsz, nq, N_KV, rows, HEAD_DIM), F32),
        scratch_shapes=scratch,
        compiler_params=_cparams(("parallel", "arbitrary")),
        name=mode + "_attention",
    )(q_st, kt, v, strip, *extra)


QT = 128


def _attn_t_kernel(mode, lp, k_sel, *refs):
    if mode == "dsa":
        (qt_ref, k_ref, vt_ref, strip_ref, qit_ref, wit_ref, ki_ref, o_ref,
         qbd_s, m_s, l_s, acc_s, key_s) = refs
    else:
        (qt_ref, k_ref, vt_ref, strip_ref, km_ref, o_ref,
         qbd_s, m_s, l_s, acc_s) = refs
    kv_dim = N_KV * HEAD_DIM
    cols = GROUP * QT
    i = pl.program_id(1)
    q0 = i * QT
    c_max = (q0 + QT - 1) // KEY_CHUNK
    n_chunks = c_max + 1
    qpos = q0 + lax.broadcasted_iota(I32, (1, QT), 1)
    krow = lax.broadcasted_iota(I32, (KEY_CHUNK, 1), 0)
    scale = HEAD_DIM ** -0.5

    qbd_s[...] = jnp.zeros(qbd_s.shape, BF16)
    for g in range(N_KV):
        qbd_s[g * HEAD_DIM:(g + 1) * HEAD_DIM, g * cols:(g + 1) * cols] = (
            qt_ref[g * HEAD_DIM:(g + 1) * HEAD_DIM, :] * scale).astype(BF16)
    m_s[...] = jnp.full(m_s.shape, NEG, F32)
    l_s[...] = jnp.zeros(l_s.shape, F32)
    acc_s[...] = jnp.zeros(acc_s.shape, F32)

    if mode == "dsa":
        qit = qit_ref[...].astype(BF16)
        wit = wit_ref[...]

        def score_chunk(c, carry):
            k0 = pl.multiple_of(c * KEY_CHUNK, KEY_CHUNK)
            kib = ki_ref[pl.ds(k0, KEY_CHUNK), :]
            sc = jnp.zeros((KEY_CHUNK, QT), F32)
            for hp in range(IDX_HEADS // 2):
                lanes = slice(2 * hp * QT, 2 * (hp + 1) * QT)
                s = jnp.maximum(jnp.dot(kib, qit[:, lanes], preferred_element_type=F32), 0.0) * wit[:, lanes]
                sc = sc + s[:, :QT] + s[:, QT:]
            sc = jnp.where(k0 + krow <= qpos, sc, -jnp.inf)
            bits = pltpu.bitcast(sc, I32)
            bits = jnp.where(bits == INT_MIN, 0, bits)
            key_s[pl.ds(k0, KEY_CHUNK), :] = jnp.where(bits < 0, bits ^ 0x7FFFFFFF, bits)
            return carry

        lax.fori_loop(0, n_chunks, score_chunk, 0)

        def count(pred):
            def body(c, acc):
                k0 = pl.multiple_of(c * KEY_CHUNK, KEY_CHUNK)
                hit = pred(key_s[pl.ds(k0, KEY_CHUNK), :], k0).astype(I32)
                return acc + jnp.sum(hit.reshape(KEY_CHUNK // 8, 8, QT), axis=0)
            acc = lax.fori_loop(0, n_chunks, body, jnp.zeros((8, QT), I32))
            return jnp.sum(acc, axis=0, keepdims=True)

        def bit_step(bi, thr):
            cand = thr + jnp.left_shift(jnp.int32(1), 31 - bi)
            cnt = count(lambda k, k0: k >= cand)
            return jnp.where(cnt >= k_sel, cand, thr)

        thr = lax.fori_loop(0, 32, bit_step, jnp.full((1, QT), INT_MIN, I32))
        n_gt = count(lambda k, k0: k > thr)
        n_ge = count(lambda k, k0: k >= thr)
        need = k_sel - n_gt

        def tie_limit():
            n_bits = max(1, (lp - 1).bit_length())

            def idx_step(bi, lo):
                cand = lo + jnp.left_shift(jnp.int32(1), n_bits - 1 - bi)
                cnt = count(lambda k, k0: (k == thr) & (k0 + krow <= cand))
                return jnp.where(cnt < need, cand, lo)

            lo = lax.fori_loop(0, n_bits, idx_step, jnp.full((1, QT), -1, I32))
            return lo + 1

        tie_idx = lax.cond(jnp.max(n_ge - n_gt - need) > 0, tie_limit,
                           lambda: jnp.full((1, QT), lp, I32))
    else:
        nbp = km_ref.shape[1]
        blk = lax.broadcasted_iota(I32, (nbp, QT), 0)
        sel_blocks = []
        for g in range(N_KV):
            qg = qt_ref[g * HEAD_DIM:(g + 1) * HEAD_DIM, :]
            qsum = qg[:, 0:QT]
            for j in range(1, GROUP):
                qsum = qsum + qg[:, j * QT:(j + 1) * QT]
            gate = jnp.dot(km_ref[g], qsum, preferred_element_type=F32, precision=lax.Precision.HIGHEST)
            gate = jnp.where(blk < c_max, gate, -jnp.inf)
            sel = jnp.zeros(blk.shape, I32)
            for _ in range(MOBA_TOPK):
                mx = jnp.max(gate, axis=0, keepdims=True)
                first = jnp.min(jnp.where(gate == mx, blk, nbp), axis=0, keepdims=True)
                hit = blk == first
                sel = jnp.where(hit & (blk < c_max), 1, sel)
                gate = jnp.where(hit, -jnp.inf, gate)
            sel_blocks.append(sel)

    def attend(c, carry):
        k0 = pl.multiple_of(c * KEY_CHUNK, KEY_CHUNK)
        kidx = k0 + krow
        causal = kidx <= qpos
        off = pl.multiple_of(jnp.clip(STRIP_ORIGIN - (q0 - k0), 0, STRIP_ORIGIN), 128)
        if mode == "dsa":
            k = key_s[pl.ds(k0, KEY_CHUNK), :]
            mask = ((k > thr) | ((k == thr) & (kidx <= tie_idx))) & causal
        kblk = k_ref[pl.ds(k0, KEY_CHUNK), :]

        def logits(g):
            return jnp.dot(kblk, qbd_s[:, g * cols:(g + 1) * cols], preferred_element_type=F32)

        s_next = logits(0)
        for g in range(N_KV):
            if mode != "dsa":
                picked = jnp.sum(jnp.where(blk == c, sel_blocks[g], 0), axis=0, keepdims=True)
                mask = jnp.where(c == c_max, causal.astype(I32), picked) > 0
            s_all = s_next
            if g + 1 < N_KV:
                s_next = logits(g + 1)
            probs, alphas = [], []
            for j in range(GROUP):
                h = g * GROUP + j
                lanes = slice(j * QT, (j + 1) * QT)
                s = s_all[:, lanes] + strip_ref[g, pl.ds(off, KEY_CHUNK), lanes]
                s = jnp.where(mask, s, NEG)
                m_old = m_s[pl.ds(h, 1), :]
                m_new = jnp.maximum(m_old, jnp.max(s, axis=0, keepdims=True))
                alpha = jnp.exp(m_old - m_new)
                p = jnp.exp(s - m_new)
                l_s[pl.ds(h, 1), :] = alpha * l_s[pl.ds(h, 1), :] + jnp.sum(p, axis=0, keepdims=True)
                m_s[pl.ds(h, 1), :] = m_new
                probs.append(p.astype(BF16))
                alphas.append(alpha)
            pv = jnp.dot(vt_ref[g * HEAD_DIM:(g + 1) * HEAD_DIM, pl.ds(k0, KEY_CHUNK)],
                         jnp.concatenate(probs, axis=1), preferred_element_type=F32)
            acc_s[g] = jnp.concatenate(alphas, axis=1) * acc_s[g] + pv
        return carry

    lax.fori_loop(0, n_chunks, attend, 0)
    for g in range(N_KV):
        for j in range(0, GROUP, 2):
            h = g * GROUP + j
            pair = [acc_s[g][:, (j + i) * QT:(j + i + 1) * QT] * (1.0 / l_s[pl.ds(h + i, 1), :]) for i in range(2)]
            o_ref[:, h * HEAD_DIM:(h + 2) * HEAD_DIM] = jnp.concatenate(pair, axis=0).T


def _sparse_attention_t(mode, qt, k, vt, strip_t, extra, k_sel=0):
    bsz, nq = qt.shape[:2]
    lp = k.shape[1]
    kv_dim = N_KV * HEAD_DIM
    cols = GROUP * QT
    per_q = lambda *tail: pl.BlockSpec((None, None) + tail, lambda b, i: (b, i) + (0,) * len(tail))
    per_b = lambda *tail: pl.BlockSpec((None,) + tail, lambda b, i: (b,) + (0,) * len(tail))
    in_specs = [per_q(kv_dim, cols), per_b(lp, kv_dim), per_b(kv_dim, lp),
                pl.BlockSpec((N_KV, STRIP_W, cols), lambda b, i: (0, 0, 0))]
    scratch = [pltpu.VMEM((kv_dim, N_KV * cols), BF16),
               pltpu.VMEM((N_HEADS, QT), F32),
               pltpu.VMEM((N_HEADS, QT), F32),
               pltpu.VMEM((N_KV, HEAD_DIM, cols), F32)]
    if mode == "dsa":
        qit, wit, ki = extra
        in_specs += [per_q(IDX_DIM, IDX_HEADS * QT), per_q(1, IDX_HEADS * QT), per_b(lp, IDX_DIM)]
        scratch.append(pltpu.VMEM((lp, QT), I32))
    else:
        (km,) = extra
        in_specs.append(per_b(N_KV, km.shape[2], HEAD_DIM))
    return pl.pallas_call(
        functools.partial(_attn_t_kernel, mode, lp, k_sel),
        grid=(bsz, nq),
        in_specs=in_specs,
        out_specs=per_q(QT, N_HEADS * HEAD_DIM),
        out_shape=jax.ShapeDtypeStruct((bsz, nq, QT, N_HEADS * HEAD_DIM), F32),
        scratch_shapes=scratch,
        compiler_params=_cparams(("parallel", "arbitrary")),
        name=mode + "_attention_t",
    )(qt, k, vt, strip_t, *extra)


def _top_rows(s, k, payload=None, rank=None):
    rid = lax.broadcasted_iota(I32, s.shape, 0) if rank is None else rank
    vals, ids = [], []
    for _ in range(k):
        mx = jnp.max(s, axis=0, keepdims=True)
        first = jnp.min(jnp.where(s == mx, rid, jnp.iinfo(jnp.int32).max), axis=0, keepdims=True)
        hit = rid == first
        vals.append(mx)
        if payload is None:
            ids.append(first)
        else:
            ids.append(jnp.max(jnp.where(hit, payload, -1), axis=0, keepdims=True))
        s = jnp.where(hit, -jnp.inf, s)
    return jnp.concatenate(vals, axis=0), jnp.concatenate(ids, axis=0)


def _top_rows_paired(s, k):
    half = s.shape[0] // 2
    lo, hi = s[:half], s[half:]
    rid = lax.broadcasted_iota(I32, lo.shape, 0)
    first_lo = lo >= hi
    top = jnp.where(first_lo, lo, hi)
    rest = jnp.where(first_lo, hi, lo)
    top_id = jnp.where(first_lo, rid, rid + half)
    rest_id = jnp.where(first_lo, rid + half, rid)
    vals, ids = [], []
    for _ in range(k):
        mx = jnp.max(top, axis=0, keepdims=True)
        first = jnp.min(jnp.where(top == mx, top_id, jnp.iinfo(jnp.int32).max), axis=0, keepdims=True)
        hit = top_id == first
        vals.append(mx)
        ids.append(first)
        top = jnp.where(hit, rest, top)
        top_id = jnp.where(hit, rest_id, top_id)
        rest = jnp.where(hit, -jnp.inf, rest)
    return jnp.concatenate(vals, axis=0), jnp.concatenate(ids, axis=0)


def _peer_route_kernel(x_ref, wqt_ref, keys_ref, row_ref, par_ref, g_ref):
    xb = x_ref[...].astype(BF16)
    qt = lax.dot_general(wqt_ref[...], xb, (((1,), (1,)), ((), ())),
                         preferred_element_type=F32)
    n_t = x_ref.shape[0]
    piece = lax.broadcasted_iota(I32, (11 * 8, n_t), 0) // 8
    within = lax.broadcasted_iota(I32, (11 * 8, n_t), 0) % 8
    by_b = (piece >= 2) & (piece < 10)
    ca = jnp.where(piece < 2, 0, jnp.where(by_b, within, 8 + within))
    cb = jnp.where(piece == 0, within, jnp.where(piece == 1, 8 + within, jnp.where(by_b, piece - 2, 0)))
    cand_ok = ((ca + 1) * (cb + 1) <= PEER_TOPK) & jnp.logical_not(by_b & (within == 0))
    cand_rank = ca * PEER_TOPK + cb

    def pieces(first, second, combine):
        out = [combine(first[0:1], second[0:8]), combine(first[0:1], second[8:16])]
        out += [combine(first[0:8], second[b:b + 1]) for b in range(8)]
        out.append(combine(first[8:16], second[0:1]))
        return jnp.concatenate(out, axis=0)

    e_rows, g_rows = [], []
    for h in range(PEER_HEADS):
        sv, si = [], []
        for c in range(2):
            hc = 2 * h + c
            qhc = qt[hc * PEER_HALF:(hc + 1) * PEER_HALF].astype(BF16)
            s = jnp.dot(keys_ref[hc], qhc, preferred_element_type=F32)
            v_, i_ = _top_rows_paired(s, PEER_TOPK)
            sv.append(v_)
            si.append(i_)
        cand = jnp.where(cand_ok, pieces(sv[0], sv[1], lambda u, w: u + w), -jnp.inf)
        cidx = pieces(si[0], si[1], lambda u, w: u * PEER_NKEYS + w)
        gv, ge = _top_rows(cand, PEER_TOPK, payload=cidx, rank=cand_rank)
        ex = jnp.exp(gv - gv[0:1])
        g_rows.append(ex / jnp.sum(ex, axis=0, keepdims=True))
        e_rows.append(ge)
    e_t = jnp.concatenate(e_rows, axis=0).T
    row_ref[...] = jnp.right_shift(e_t, 1) * 8
    par_ref[...] = e_t & 1
    g_ref[...] = jnp.concatenate(g_rows, axis=0).T


def _peer_route(x, wqt_bf16, keys_bf16, tb):
    m, d = x.shape
    out = pl.BlockSpec((tb, PEER_PAIRS), lambda i: (i, 0))
    return pl.pallas_call(
        _peer_route_kernel,
        grid=(m // tb,),
        in_specs=[pl.BlockSpec((tb, d), lambda i: (i, 0)),
                  pl.BlockSpec(wqt_bf16.shape, lambda i: (0, 0)),
                  pl.BlockSpec(keys_bf16.shape, lambda i: (0, 0, 0))],
        out_specs=[out, out, out],
        out_shape=[jax.ShapeDtypeStruct((m, PEER_PAIRS), I32),
                   jax.ShapeDtypeStruct((m, PEER_PAIRS), I32),
                   jax.ShapeDtypeStruct((m, PEER_PAIRS), F32)],
        compiler_params=_cparams(("parallel",)),
        name="peer_route",
    )(x, wqt_bf16, keys_bf16)


FOLD_PAIRS = 16
UP_SLOTS = 4
TILE_ROWS = 8


def _pack_table(w):
    n, d = w.shape
    bits = lax.bitcast_convert_type(w.astype(BF16), jnp.uint16).astype(jnp.uint32)
    bits = bits.reshape(n // 2, 2, d // 128, 128)
    return (bits[:, 0] | (bits[:, 1] << 16)).astype(jnp.uint32).reshape(n // 2 * (d // 128), 128)


def _load_tile(tbl_ref, row_ref, idx):
    return tbl_ref[pl.ds(pl.multiple_of(row_ref[idx], TILE_ROWS), TILE_ROWS), :]


def _peer_up_kernel(tb, row_ref, par_ref, x_ref, g_ref, fold_ref, tbl_ref, c_ref, prod_s):
    diag = (lax.broadcasted_iota(I32, (PEER_PAIRS, PEER_PAIRS), 0)
            == lax.broadcasted_iota(I32, (PEER_PAIRS, PEER_PAIRS), 1))
    tile_rows = 2 * TILE_ROWS

    def gather(t, slot):
        xw = pltpu.bitcast(x_ref[t].astype(BF16).astype(F32), jnp.uint32)
        x2 = pltpu.bitcast(xw | jnp.right_shift(xw, jnp.uint32(16)), BF16)
        rows_t = row_ref.at[pl.ds(t * PEER_PAIRS, PEER_PAIRS)]
        for p in range(PEER_PAIRS):
            w = pltpu.bitcast(_load_tile(tbl_ref, rows_t, p), BF16)
            prod_s[slot, pl.ds(p * tile_rows, tile_rows), :] = w * x2

    def reduce(t, slot):
        depth = FOLD_PAIRS * tile_rows
        parts = [jnp.dot(fold_ref[...], prod_s[slot, pl.ds(i * depth, depth), :], preferred_element_type=F32)
                 for i in range(PEER_PAIRS // FOLD_PAIRS)]
        rows = []
        for half in range(2):
            part = jnp.concatenate([pt[half * FOLD_PAIRS:(half + 1) * FOLD_PAIRS] for pt in parts], axis=0)
            col = jnp.sum(part, axis=1, keepdims=True)
            rows.append(jnp.sum(jnp.where(diag, col, 0.0), axis=0, keepdims=True))
        a = jnp.where(par_ref[pl.ds(t, 1), :] == 1, rows[1], rows[0])
        act = 0.5 * a * (1.0 + lax.erf(a * (2.0 ** -0.5)))
        c_ref[pl.ds(t, 1), :] = g_ref[pl.ds(t, 1), :] * act

    prod_s[UP_SLOTS:2 * UP_SLOTS] = jnp.zeros((UP_SLOTS,) + prod_s.shape[1:], BF16)

    def step(k, carry):
        t0 = 2 * UP_SLOTS * k
        for j in range(UP_SLOTS):
            gather(t0 + j, j)
            reduce(jnp.maximum(t0 - UP_SLOTS + j, 0), UP_SLOTS + j)
        for j in range(UP_SLOTS):
            gather(t0 + UP_SLOTS + j, UP_SLOTS + j)
            reduce(t0 + j, j)
        return carry

    lax.fori_loop(0, tb // (2 * UP_SLOTS), step, 0)
    for j in range(UP_SLOTS):
        reduce(tb - UP_SLOTS + j, UP_SLOTS + j)


DOWN_GROUP = 8


def _peer_down_kernel(tb, row_ref, par_ref, c_ref, spread_ref, tbl_ref, f_ref, tile_s, coef_s):
    width = PEER_PAIRS * 2 * TILE_ROWS
    lane = lax.broadcasted_iota(I32, (TILE_ROWS, width), 1)
    own_row = ((lane & (2 * TILE_ROWS - 1)) >> 1) == lax.broadcasted_iota(I32, (TILE_ROWS, width), 0)
    lane_par = (lax.broadcasted_iota(I32, (DOWN_GROUP, width), 1) & 1).astype(F32)

    def gather(t, slot):
        rows_t = row_ref.at[pl.ds(t * PEER_PAIRS, PEER_PAIRS)]
        for p in range(PEER_PAIRS):
            tile_s[slot, pl.ds(p * TILE_ROWS, TILE_ROWS), :] = _load_tile(tbl_ref, rows_t, p)

    def apply(t, j, slot):
        tiles = pltpu.bitcast(tile_s[slot], BF16)
        lhs = [jnp.where(own_row, coef_s[pl.ds(part * DOWN_GROUP + j, 1), :], 0.0) for part in range(2)]
        out = jnp.dot(jnp.concatenate(lhs, axis=0).astype(BF16), tiles, preferred_element_type=F32)
        f_ref[t] = out[0:TILE_ROWS] + out[TILE_ROWS:2 * TILE_ROWS]

    def group(gi, carry):
        t0 = pl.multiple_of(gi * DOWN_GROUP, DOWN_GROUP)
        gather(t0, 0)
        c = c_ref[pl.ds(t0, DOWN_GROUP), :]
        c_hi = c.astype(BF16)
        c_lo = (c - c_hi.astype(F32)).astype(BF16)
        par = par_ref[pl.ds(t0, DOWN_GROUP), :].astype(BF16)
        rep = jnp.dot(jnp.concatenate([c_hi, c_lo, par], axis=0), spread_ref[...],
                      preferred_element_type=F32)
        keep = rep[2 * DOWN_GROUP:] == lane_par
        coef_s[0:DOWN_GROUP, :] = jnp.where(keep, rep[0:DOWN_GROUP], 0.0)
        coef_s[DOWN_GROUP:2 * DOWN_GROUP, :] = jnp.where(keep, rep[DOWN_GROUP:2 * DOWN_GROUP], 0.0)
        for j in range(DOWN_GROUP):
            if j + 1 < DOWN_GROUP:
                gather(t0 + j + 1, (j + 1) % 2)
            apply(t0 + j, j, j % 2)
        return carry

    lax.fori_loop(0, tb // DOWN_GROUP, group, 0)


def _peer_experts(x, row, par, g, u_tbl, v_tbl, tb):
    m, d = x.shape
    x3 = x.reshape(m, d // 128, 128)
    flat = row.reshape(m * PEER_PAIRS)
    smem = pl.BlockSpec((tb * PEER_PAIRS,), lambda i: (i,), memory_space=pltpu.SMEM)
    vrow = pl.BlockSpec((tb, PEER_PAIRS), lambda i: (i, 0))
    tile = pl.BlockSpec((tb, d // 128, 128), lambda i: (i, 0, 0))
    resident = pl.BlockSpec(memory_space=pltpu.VMEM)
    const = lambda a: pl.BlockSpec(a.shape, lambda i: (0, 0))
    fold_col = np.arange(FOLD_PAIRS * 2 * TILE_ROWS)[None, :]
    fold_row = np.arange(2 * FOLD_PAIRS)[:, None]
    fold = jnp.asarray((fold_col // (2 * TILE_ROWS) == fold_row % FOLD_PAIRS)
                       & (fold_col % 2 == fold_row // FOLD_PAIRS), BF16)
    width = PEER_PAIRS * 2 * TILE_ROWS
    spread = jnp.asarray(np.arange(width)[None, :] // (2 * TILE_ROWS) == np.arange(PEER_PAIRS)[:, None], BF16)
    c = pl.pallas_call(
        functools.partial(_peer_up_kernel, tb),
        grid=(m // tb,),
        in_specs=[smem, vrow, tile, vrow, const(fold), resident],
        out_specs=vrow,
        out_shape=jax.ShapeDtypeStruct((m, PEER_PAIRS), F32),
        scratch_shapes=[pltpu.VMEM((2 * UP_SLOTS, PEER_PAIRS * 2 * TILE_ROWS, 128), BF16)],
        compiler_params=_cparams(("arbitrary",)),
        name="peer_up",
    )(flat, par, x3, g, fold, u_tbl)
    f = pl.pallas_call(
        functools.partial(_peer_down_kernel, tb),
        grid=(m // tb,),
        in_specs=[smem, vrow, vrow, const(spread), resident],
        out_specs=tile,
        out_shape=jax.ShapeDtypeStruct((m, d // 128, 128), F32),
        scratch_shapes=[pltpu.VMEM((2, PEER_PAIRS * TILE_ROWS, 128), jnp.uint32),
                        pltpu.VMEM((2 * DOWN_GROUP, width), F32)],
        compiler_params=_cparams(("arbitrary",)),
        name="peer_down",
    )(flat, par, c, spread, v_tbl)
    return f.reshape(m, d)


def _stack_heads(a, qb, n_lead):
    bsz, q, n, hd = a.shape
    n_in = n // n_lead
    a = a.reshape(bsz, q // qb, qb, n_lead, n_in, hd)
    return a.transpose(0, 1, 3, 4, 2, 5).reshape(bsz, q // qb, n_lead, n_in * qb, hd)


def _unstack_heads(o, qb):
    bsz, nq = o.shape[:2]
    o = o.reshape(bsz, nq, N_KV, GROUP, qb, HEAD_DIM).transpose(0, 1, 4, 2, 3, 5)
    return o.reshape(bsz * nq * qb, N_HEADS * HEAD_DIM)


def _run_trunk(x, p, past, prm, qb, tm, tb_route, tb_exp):
    bsz, q_len, d = x.shape
    depth = prm["w_o"].shape[0]
    m = bsz * q_len
    tm, tb_route, tb_exp = min(tm, m), min(tb_route, m), min(tb_exp, m)
    alpha = (2 * depth) ** 0.25
    attn_dim = N_HEADS * HEAD_DIM
    kv_dim = N_KV * HEAD_DIM
    if past is None:
        q_pos0 = 0
        n_pages = q_len // PAGE
        table = jnp.arange(bsz * n_pages, dtype=I32).reshape(bsz, n_pages)
    else:
        cache_k, cache_v, cache_ki, table = past
        q_pos0 = table.shape[1] * PAGE
    on_lanes = past is None and q_len % QT == 0
    if on_lanes:
        qb = QT
    nq = q_len // qb
    strip = _bias_strip(prm["rel_bias"], qb).reshape(N_KV, GROUP, qb, STRIP_W)
    if on_lanes:
        strip = strip.transpose(0, 3, 1, 2).reshape(N_KV, STRIP_W, GROUP * qb)
    else:
        strip = strip.reshape(N_KV, GROUP * qb, STRIP_W)
    xf = x.reshape(m, d)
    ks, vs, kis = [], [], []

    def relayout(new_rows, cache, layer, want):
        w = new_rows.shape[2]
        if past is None:
            return _relayout(new_rows.reshape(bsz * n_pages, PAGE, w), table, None, False, want)
        n_pool = cache.shape[1]
        pool = cache.reshape((cache.shape[0] * n_pool,) + cache.shape[2:])
        pool_t = jnp.moveaxis(pool, 1, -1).reshape(pool.shape[0], w, PAGE)
        pad = jnp.zeros((bsz, STEP_KEYS - q_len, w), F32)
        return _relayout(pool_t, table + layer * n_pool, jnp.concatenate([new_rows, pad], axis=1), True, want)

    for l in range(depth):
        dsa = l % 2 == 0
        w_in = prm["w_in_dsa"][l // 2] if dsa else prm["w_in_moba"][l // 2]
        n_in = w_in.shape[1]
        n_pad = -(-n_in // 128) * 128
        w_in = jnp.pad(w_in, ((0, 0), (0, n_pad - n_in))).astype(BF16)
        proj = _matmul(xf, w_in, tm)
        qh = proj[:, :attn_dim].reshape(bsz, q_len, N_HEADS, HEAD_DIM)
        k = proj[:, attn_dim:attn_dim + kv_dim].reshape(bsz, q_len, kv_dim)
        v = proj[:, attn_dim + kv_dim:attn_dim + 2 * kv_dim].reshape(bsz, q_len, kv_dim)
        ks.append(k.reshape(bsz, q_len, N_KV, HEAD_DIM))
        vs.append(v.reshape(bsz, q_len, N_KV, HEAD_DIM))
        k_lay = relayout(k, None if past is None else cache_k, l,
                         (("x",) if on_lanes else ("xt",)) + (() if dsa else ("sums",)))
        v_lay = relayout(v, None if past is None else cache_v, l, ("xt",) if on_lanes else ("x",))
        if on_lanes:
            q_in = qh.reshape(bsz, nq, qb, N_KV, GROUP, HEAD_DIM).transpose(0, 1, 3, 5, 4, 2)
            q_in = q_in.reshape(bsz, nq, kv_dim, GROUP * qb)
            kr, vt = k_lay["x"], v_lay["xt"]
        else:
            kt = k_lay["xt"].reshape(bsz, N_KV, HEAD_DIM, k_lay["xt"].shape[2])
            vr = v_lay["x"]
            q_in = _stack_heads(qh, qb, N_KV)
        if dsa:
            off = attn_dim + 2 * kv_dim
            qi = proj[:, off:off + IDX_HEADS * IDX_DIM].reshape(bsz, q_len, IDX_HEADS, IDX_DIM)
            off += IDX_HEADS * IDX_DIM
            ki = proj[:, off:off + IDX_DIM].reshape(bsz, q_len, IDX_DIM)
            wi = proj[:, off + IDX_DIM:off + IDX_DIM + IDX_HEADS].reshape(bsz, q_len, IDX_HEADS, 1)
            kis.append(ki)
            ki_lay = relayout(ki, None if past is None else cache_ki, l // 2, ("x",) if on_lanes else ("xt",))
            k_sel = min(IDX_TOPK, (q_pos0 + q_len) // 4)
            if on_lanes:
                qi_t = qi.reshape(bsz, nq, qb, IDX_HEADS, IDX_DIM).transpose(0, 1, 4, 3, 2)
                qi_t = qi_t.reshape(bsz, nq, IDX_DIM, IDX_HEADS * qb)
                wi_t = wi.reshape(bsz, nq, qb, IDX_HEADS).transpose(0, 1, 3, 2).reshape(bsz, nq, 1, IDX_HEADS * qb)
                o = _sparse_attention_t("dsa", q_in, kr, vt, strip, (qi_t, wi_t, ki_lay["x"]), k_sel)
            else:
                qi_st = _stack_heads(qi, qb, 1)[:, :, 0]
                wi_st = _stack_heads(wi, qb, 1)[:, :, 0]
                o = _sparse_attention("dsa", q_in, kt, vr, strip, (qi_st, wi_st, ki_lay["xt"]), qb, q_pos0, k_sel)
        else:
            ksum = k_lay["sums"]
            nb = ksum.shape[1]
            kmean = (ksum * (1.0 / MOBA_BLOCK)).reshape(bsz, nb, N_KV, HEAD_DIM)
            if on_lanes:
                km = jnp.pad(kmean.transpose(0, 2, 1, 3), ((0, 0), (0, 0), (0, -nb % 8), (0, 0)))
                o = _sparse_attention_t("moba", q_in, kr, vt, strip, (km,))
            else:
                kmt = jnp.pad(kmean.transpose(0, 2, 3, 1), ((0, 0), (0, 0), (0, 0), (0, -nb % 128)))
                o = _sparse_attention("moba", q_in, kt, vr, strip, (kmt,), qb, q_pos0)
        if on_lanes:
            o = o.reshape(m, attn_dim)
        else:
            o = _unstack_heads(o, qb)
        xf = _attn_out(o, prm["w_o"][l].astype(BF16), xf, prm["ln_g"][l, 0][None], prm["ln_b"][l, 0][None],
                       alpha, tm)
        row, par, g = _peer_route(xf, prm["peer_wq"][l].T.astype(BF16),
                                 prm["peer_keys"][l].reshape(2 * PEER_HEADS, PEER_NKEYS, PEER_HALF).astype(BF16),
                                 tb_route)
        f = _peer_experts(xf, row, par, g, _pack_table(prm["peer_u"][l]), _pack_table(prm["peer_v"][l]), tb_exp)
        xf = _layer_tail(xf, f, prm["ln_g"][l, 1][None], prm["ln_b"][l, 1][None],
                         prm["gate_w"][l].astype(BF16), prm["gate_b"][l][None],
                         p[l].reshape(m, -1), prm["ple_w"][l].astype(BF16), alpha, tm)
    return xf.reshape(bsz, q_len, d), jnp.stack(ks), jnp.stack(vs), jnp.stack(kis)


def kernel(x_prompt, x_sample, cache_k, cache_v, cache_ki, page_table, p_prompt, p_sample, w_in_dsa, w_in_moba, w_o, rel_bias, ln_g, ln_b, peer_wq, peer_keys, peer_u, peer_v, gate_w, gate_b, ple_w):
    prm = {
        "w_in_dsa": w_in_dsa, "w_in_moba": w_in_moba, "w_o": w_o, "rel_bias": rel_bias,
        "ln_g": ln_g, "ln_b": ln_b, "peer_wq": peer_wq, "peer_keys": peer_keys,
        "peer_u": peer_u, "peer_v": peer_v, "gate_w": gate_w, "gate_b": gate_b, "ple_w": ple_w,
    }
    q_sample = x_sample.shape[1]
    y_prompt, k_prompt, v_prompt, ki_prompt = _run_trunk(
        x_prompt, p_prompt, None, prm, qb=128, tm=512, tb_route=256, tb_exp=128)
    y_sample, k_sample, v_sample, ki_sample = _run_trunk(
        x_sample, p_sample, (cache_k, cache_v, cache_ki, page_table), prm,
        qb=q_sample, tm=256, tb_route=256, tb_exp=64)
    return (y_prompt, y_sample, k_prompt, v_prompt, ki_prompt, k_sample, v_sample, ki_sample)
```

```python
import functools
import math

import numpy as np
import jax
import jax.numpy as jnp
from jax import lax
from jax.experimental import pallas as pl
from jax.experimental.pallas import tpu as pltpu

F32 = jnp.float32
BF16 = jnp.bfloat16
I32 = jnp.int32

N_HEADS = 16
HEAD_DIM = 64
N_KV = 4
GROUP = 4
IDX_HEADS = 8
IDX_DIM = 64
IDX_TOPK = 256
MOBA_BLOCK = 256
MOBA_TOPK = 3
REL_BUCKETS = 32
REL_MAX_DIST = 128
PEER_HEADS = 8
PEER_NKEYS = 128
PEER_HALF = 128
PEER_TOPK = 16
PEER_PAIRS = PEER_HEADS * PEER_TOPK
LN_EPS = 1e-5
PAGE = 128

KEY_CHUNK = 256
STRIP_W = 768
STRIP_ORIGIN = 512
NEG = -1e30
INT_MIN = -2147483648
VMEM_LIMIT = 56 * 1024 * 1024


def _cparams(sem):
    return pltpu.CompilerParams(dimension_semantics=sem, vmem_limit_bytes=VMEM_LIMIT)


def _mm_kernel(a_ref, b_ref, o_ref):
    o_ref[...] = jnp.dot(a_ref[...].astype(BF16), b_ref[...], preferred_element_type=F32)


def _matmul(a, b_bf16, tm):
    m, k = a.shape
    n = b_bf16.shape[1]
    return pl.pallas_call(
        _mm_kernel,
        grid=(m // tm,),
        in_specs=[pl.BlockSpec((tm, k), lambda i: (i, 0)),
                  pl.BlockSpec((k, n), lambda i: (0, 0))],
        out_specs=pl.BlockSpec((tm, n), lambda i: (i, 0)),
        out_shape=jax.ShapeDtypeStruct((m, n), F32),
        compiler_params=_cparams(("parallel",)),
        name="in_proj",
    )(a, b_bf16)


def _layer_norm(y, g, b):
    mu = jnp.mean(y, axis=-1, keepdims=True)
    yc = y - mu
    var = jnp.mean(yc * yc, axis=-1, keepdims=True)
    return yc * lax.rsqrt(var + LN_EPS) * g + b


def _attn_out_kernel(alpha, o_ref, w_ref, x_ref, g_ref, b_ref, y_ref):
    y = alpha * x_ref[...] + jnp.dot(o_ref[...].astype(BF16), w_ref[...],
                                     preferred_element_type=F32)
    y_ref[...] = _layer_norm(y, g_ref[...], b_ref[...])


def _attn_out(o, w_bf16, x, g, b, alpha, tm):
    m, d = x.shape
    row = pl.BlockSpec((tm, d), lambda i: (i, 0))
    vec = pl.BlockSpec((1, d), lambda i: (0, 0))
    return pl.pallas_call(
        functools.partial(_attn_out_kernel, alpha),
        grid=(m // tm,),
        in_specs=[row, pl.BlockSpec((d, d), lambda i: (0, 0)), row, vec, vec],
        out_specs=row,
        out_shape=jax.ShapeDtypeStruct((m, d), F32),
        compiler_params=_cparams(("parallel",)),
        name="attn_out_ln",
    )(o, w_bf16, x, g, b)


def _layer_tail_kernel(alpha, x_ref, f_ref, g_ref, b_ref, gw_ref, gb_ref, p_ref, pw_ref, y_ref):
    x2 = _layer_norm(alpha * x_ref[...] + f_ref[...], g_ref[...], b_ref[...])
    z = jnp.dot(x2.astype(BF16), gw_ref[...], preferred_element_type=F32) + gb_ref[...]
    gate = 1.0 / (1.0 + jnp.exp(-z))
    e = jnp.dot(p_ref[...].astype(BF16), pw_ref[...], preferred_element_type=F32)
    y_ref[...] = x2 + gate * e


def _layer_tail(x, f, g, b, gw_bf16, gb, p, pw_bf16, alpha, tm):
    m, d = x.shape
    pd = p.shape[1]
    row = pl.BlockSpec((tm, d), lambda i: (i, 0))
    vec = pl.BlockSpec((1, d), lambda i: (0, 0))
    return pl.pallas_call(
        functools.partial(_layer_tail_kernel, alpha),
        grid=(m // tm,),
        in_specs=[row, row, vec, vec, pl.BlockSpec((d, d), lambda i: (0, 0)), vec,
                  pl.BlockSpec((tm, pd), lambda i: (i, 0)),
                  pl.BlockSpec((pd, d), lambda i: (0, 0))],
        out_specs=row,
        out_shape=jax.ShapeDtypeStruct((m, d), F32),
        compiler_params=_cparams(("parallel",)),
        name="layer_tail",
    )(x, f, g, b, gw_bf16, gb, p, pw_bf16)


STEP_PAGES = 8
STEP_KEYS = STEP_PAGES * PAGE
STEP_CHUNKS = STEP_KEYS // KEY_CHUNK


def _relayout_kernel(n_pool_steps, has_new, pages_t, want, *refs):
    page_refs = refs[1:1 + STEP_PAGES]
    new_ref = refs[1 + STEP_PAGES] if has_new else None
    outs = dict(zip(want, refs[len(refs) - len(want):]))
    j = pl.program_id(1)
    need_rows = "x" in want or "sums" in want

    def emit(rows, cols):
        if "xt" in want:
            outs["xt"][...] = (rows.T if cols is None else cols).astype(BF16)
        if need_rows and rows is None:
            rows = cols.T
        if "x" in want:
            outs["x"][...] = rows.astype(BF16)
        if "sums" in want:
            for i in range(STEP_CHUNKS):
                outs["sums"][pl.ds(i, 1), :] = jnp.sum(rows[i * KEY_CHUNK:(i + 1) * KEY_CHUNK], axis=0, keepdims=True)

    def from_pages():
        if pages_t:
            emit(None, jnp.concatenate([r[...] for r in page_refs], axis=1))
        else:
            emit(jnp.concatenate([r[...] for r in page_refs], axis=0), None)

    if has_new:
        pl.when(j < n_pool_steps)(from_pages)

        @pl.when(j >= n_pool_steps)
        def _():
            emit(new_ref[...], None)
    else:
        from_pages()


def _relayout(pool, table, new, pages_t, want):
    bsz, n_pages = table.shape
    w = pool.shape[1] if pages_t else pool.shape[2]
    n_pool_steps = n_pages // STEP_PAGES
    n_steps = n_pool_steps + (1 if new is not None else 0)
    lp = n_steps * STEP_KEYS
    last = n_pool_steps - 1

    def page_map(i):
        return lambda b, j, tbl: (tbl[b, STEP_PAGES * jnp.minimum(j, last) + i], 0, 0)

    page_block = (None, w, PAGE) if pages_t else (None, PAGE, w)
    in_specs = [pl.BlockSpec(page_block, page_map(i)) for i in range(STEP_PAGES)]
    args = [pool] * STEP_PAGES
    if new is not None:
        in_specs.append(pl.BlockSpec((None, STEP_KEYS, w), lambda b, j, tbl: (b, 0, 0)))
        args.append(new)
    specs = {"xt": (pl.BlockSpec((None, w, STEP_KEYS), lambda b, j, tbl: (b, 0, j)),
                    jax.ShapeDtypeStruct((bsz, w, lp), BF16)),
             "x": (pl.BlockSpec((None, STEP_KEYS, w), lambda b, j, tbl: (b, j, 0)),
                   jax.ShapeDtypeStruct((bsz, lp, w), BF16)),
             "sums": (pl.BlockSpec((None, None, STEP_CHUNKS, w), lambda b, j, tbl: (b, j, 0, 0)),
                      jax.ShapeDtypeStruct((bsz, n_steps, STEP_CHUNKS, w), F32))}
    outs = pl.pallas_call(
        functools.partial(_relayout_kernel, n_pool_steps, new is not None, pages_t, want),
        grid_spec=pltpu.PrefetchScalarGridSpec(
            num_scalar_prefetch=1,
            grid=(bsz, n_steps),
            in_specs=in_specs,
            out_specs=[specs[name][0] for name in want]),
        out_shape=[specs[name][1] for name in want],
        compiler_params=_cparams(("parallel", "arbitrary")),
        name="kv_relayout",
    )(table, *args)
    outs = dict(zip(want, outs))
    if "sums" in outs:
        outs["sums"] = outs["sums"].reshape(bsz, n_steps * STEP_CHUNKS, w)
    return outs


def _bucket_table():
    n = np.arange(REL_MAX_DIST + 1)
    exact = REL_BUCKETS // 2
    nf = np.maximum(n, 1).astype(np.float32)
    large = exact + (np.log(nf / np.float32(exact)) / np.float32(math.log(REL_MAX_DIST / exact))
                     * np.float32(REL_BUCKETS - exact)).astype(np.int32)
    return np.where(n < exact, n, np.minimum(large, REL_BUCKETS - 1)).astype(np.int32)


def _bias_strip_kernel(bk_ref, rb_ref, o_ref):
    bk = bk_ref[...]
    for h in range(N_HEADS):
        acc = jnp.zeros(bk.shape, F32)
        for bkt in range(REL_BUCKETS):
            acc = jnp.where(bk == bkt, rb_ref[bkt, h], acc)
        o_ref[h] = acc


def _bias_strip(rel_bias, qb):
    qi = np.arange(qb)[:, None]
    z = np.arange(STRIP_W)[None, :]
    dist = np.clip(qi - z + STRIP_ORIGIN, 0, REL_MAX_DIST)
    bk = jnp.asarray(_bucket_table()[dist])
    out = pl.pallas_call(
        _bias_strip_kernel,
        in_specs=[pl.BlockSpec(memory_space=pltpu.VMEM), pl.BlockSpec(memory_space=pltpu.SMEM)],
        out_specs=pl.BlockSpec(memory_space=pltpu.VMEM),
        out_shape=jax.ShapeDtypeStruct((N_HEADS, qb, STRIP_W), F32),
        name="bias_strip",
    )(bk, rel_bias)
    return out


def _attn_kernel(mode, qb, lp, q_pos0, k_sel, *refs):
    if mode == "dsa":
        (q_ref, kt_ref, v_ref, strip_ref, qi_ref, wi_ref, kit_ref, o_ref,
         qs_s, m_s, l_s, acc_s, key_s) = refs
    else:
        (q_ref, kt_ref, v_ref, strip_ref, kmt_ref, o_ref,
         qs_s, m_s, l_s, acc_s) = refs
    rows = GROUP * qb
    i = pl.program_id(1)
    q0 = q_pos0 + i * qb
    c_max = (q0 + qb - 1) // KEY_CHUNK
    n_chunks = c_max + 1
    qpos = q0 + lax.broadcasted_iota(I32, (qb, 1), 0)
    lane = lax.broadcasted_iota(I32, (1, KEY_CHUNK), 1)
    scale = HEAD_DIM ** -0.5

    for g in range(N_KV):
        qs_s[g] = (q_ref[g] * scale).astype(BF16)
    m_s[...] = jnp.full(m_s.shape, NEG, F32)
    l_s[...] = jnp.zeros(l_s.shape, F32)
    acc_s[...] = jnp.zeros(acc_s.shape, F32)

    if mode == "dsa":
        qi = qi_ref[...].astype(BF16)
        wi = wi_ref[...]

        def score_chunk(c, carry):
            k0 = pl.multiple_of(c * KEY_CHUNK, KEY_CHUNK)
            s = jnp.dot(qi, kit_ref[:, pl.ds(k0, KEY_CHUNK)], preferred_element_type=F32)
            s = jnp.maximum(s, 0.0) * wi
            sc = s[0:qb]
            for h in range(1, IDX_HEADS):
                sc = sc + s[h * qb:(h + 1) * qb]
            sc = jnp.where(k0 + lane <= qpos, sc, -jnp.inf)
            bits = pltpu.bitcast(sc, I32)
            bits = jnp.where(bits == INT_MIN, 0, bits)
            key_s[:, pl.ds(k0, KEY_CHUNK)] = jnp.where(bits < 0, bits ^ 0x7FFFFFFF, bits)
            return carry

        scan = STEP_KEYS if qb <= 32 else KEY_CHUNK
        n_scan = (n_chunks * KEY_CHUNK + scan - 1) // scan
        lax.fori_loop(0, n_scan * (scan // KEY_CHUNK), score_chunk, 0)
        scan_lane = lax.broadcasted_iota(I32, (1, scan), 1)

        def count(pred):
            def body(c, acc):
                k0 = pl.multiple_of(c * scan, scan)
                return acc + pred(key_s[:, pl.ds(k0, scan)], k0 + scan_lane).astype(I32)
            acc = lax.fori_loop(0, n_scan, body, jnp.zeros((qb, scan), I32))
            return jnp.sum(acc, axis=1, keepdims=True)

        def bit_step(bi, thr):
            cand = thr + jnp.left_shift(jnp.int32(1), 31 - bi)
            cnt = count(lambda k, kidx: k >= cand)
            return jnp.where(cnt >= k_sel, cand, thr)

        thr = lax.fori_loop(0, 32, bit_step, jnp.full((qb, 1), INT_MIN, I32))
        n_gt = count(lambda k, kidx: k > thr)
        n_ge = count(lambda k, kidx: k >= thr)
        need = k_sel - n_gt

        def tie_limit():
            n_bits = max(1, (lp - 1).bit_length())

            def idx_step(bi, lo):
                cand = lo + jnp.left_shift(jnp.int32(1), n_bits - 1 - bi)
                cnt = count(lambda k, kidx: (k == thr) & (kidx <= cand))
                return jnp.where(cnt < need, cand, lo)

            lo = lax.fori_loop(0, n_bits, idx_step, jnp.full((qb, 1), -1, I32))
            return lo + 1

        tie_idx = lax.cond(jnp.max(n_ge - n_gt - need) > 0, tie_limit,
                           lambda: jnp.full((qb, 1), lp, I32))
    else:
        blk = lax.broadcasted_iota(I32, (qb, kmt_ref.shape[2]), 1)
        sel_blocks = []
        for g in range(N_KV):
            qg = q_ref[g]
            qsum = qg[0:qb]
            for j in range(1, GROUP):
                qsum = qsum + qg[j * qb:(j + 1) * qb]
            gate = jnp.dot(qsum, kmt_ref[g], preferred_element_type=F32,
                           precision=lax.Precision.HIGHEST)
            gate = jnp.where(blk < c_max, gate, -jnp.inf)
            sel = jnp.zeros(blk.shape, jnp.bool_)
            for _ in range(MOBA_TOPK):
                mx = jnp.max(gate, axis=1, keepdims=True)
                first = jnp.min(jnp.where(gate == mx, blk, blk.shape[1]), axis=1, keepdims=True)
                hit = blk == first
                sel = sel | (hit & (blk < c_max))
                gate = jnp.where(hit, -jnp.inf, gate)
            sel_blocks.append(sel.astype(I32))

    def attend(c, carry):
        k0 = pl.multiple_of(c * KEY_CHUNK, KEY_CHUNK)
        kidx = k0 + lane
        causal = kidx <= qpos
        off = pl.multiple_of(jnp.clip(STRIP_ORIGIN - (q0 - k0), 0, STRIP_ORIGIN), 128)
        if mode == "dsa":
            k = key_s[:, pl.ds(k0, KEY_CHUNK)]
            mask1 = ((k > thr) | ((k == thr) & (kidx <= tie_idx))) & causal
            mask = jnp.concatenate([mask1] * GROUP, axis=0)

        logits = [jnp.dot(qs_s[g], kt_ref[g, :, pl.ds(k0, KEY_CHUNK)], preferred_element_type=F32)
                  for g in range(N_KV)]
        probs, alphas = [], []
        for g in range(N_KV):
            if mode != "dsa":
                picked = jnp.sum(jnp.where(blk == c, sel_blocks[g], 0), axis=1, keepdims=True)
                mask1 = jnp.where(c == c_max, causal.astype(I32), picked) > 0
                mask = jnp.concatenate([mask1] * GROUP, axis=0)
            s = jnp.where(mask, logits[g] + strip_ref[g, :, pl.ds(off, KEY_CHUNK)], NEG)
            m_old = m_s[g]
            m_new = jnp.maximum(m_old, jnp.max(s, axis=1, keepdims=True))
            alpha = jnp.exp(m_old - m_new)
            p = jnp.exp(s - m_new)
            l_s[g] = alpha * l_s[g] + jnp.sum(p, axis=1, keepdims=True)
            m_s[g] = m_new
            probs.append(p.astype(BF16))
            alphas.append(alpha)
        vblk = v_ref[pl.ds(k0, KEY_CHUNK), :]
        for g in range(N_KV):
            acc_s[g] = alphas[g] * acc_s[g] + jnp.dot(probs[g], vblk, preferred_element_type=F32)
        return carry

    lax.fori_loop(0, n_chunks, attend, 0)
    for g in range(N_KV):
        o_ref[g] = acc_s[g][:, g * HEAD_DIM:(g + 1) * HEAD_DIM] / l_s[g]


def _sparse_attention(mode, q_st, kt, v, strip, extra, qb, q_pos0, k_sel=0):
    bsz, nq = q_st.shape[:2]
    lp = v.shape[1]
    rows = GROUP * qb
    per_q = lambda *tail: pl.BlockSpec((None, None) + tail, lambda b, i: (b, i) + (0,) * len(tail))
    per_b = lambda *tail: pl.BlockSpec((None,) + tail, lambda b, i: (b,) + (0,) * len(tail))
    in_specs = [per_q(N_KV, rows, HEAD_DIM), per_b(N_KV, HEAD_DIM, lp), per_b(lp, N_KV * HEAD_DIM),
                pl.BlockSpec((N_KV, rows, STRIP_W), lambda b, i: (0, 0, 0))]
    scratch = [pltpu.VMEM((N_KV, rows, HEAD_DIM), BF16),
               pltpu.VMEM((N_KV, rows, 1), F32),
               pltpu.VMEM((N_KV, rows, 1), F32),
               pltpu.VMEM((N_KV, rows, N_KV * HEAD_DIM), F32)]
    if mode == "dsa":
        qi_st, wi_st, kit = extra
        in_specs += [per_q(IDX_HEADS * qb, IDX_DIM), per_q(IDX_HEADS * qb, 1), per_b(IDX_DIM, lp)]
        scratch.append(pltpu.VMEM((qb, lp), I32))
    else:
        (kmt,) = extra
        in_specs.append(per_b(N_KV, HEAD_DIM, kmt.shape[3]))
    return pl.pallas_call(
        functools.partial(_attn_kernel, mode, qb, lp, q_pos0, k_sel),
        grid=(bsz, nq),
        in_specs=in_specs,
        out_specs=per_q(N_KV, rows, HEAD_DIM),
        out_shape=jax.ShapeDtypeStruct((bsz, nq, N_KV, rows, HEAD_DIM), F32),
        scratch_shapes=scratch,
        compiler_params=_cparams(("parallel", "arbitrary")),
        name=mode + "_attention",
    )(q_st, kt, v, strip, *extra)


QT = 128


def _attn_t_kernel(mode, lp, k_sel, *refs):
    if mode == "dsa":
        (qt_ref, k_ref, vt_ref, strip_ref, qit_ref, wit_ref, ki_ref, o_ref,
         qbd_s, m_s, l_s, acc_s, key_s) = refs
    else:
        (qt_ref, k_ref, vt_ref, strip_ref, km_ref, o_ref,
         qbd_s, m_s, l_s, acc_s) = refs
    kv_dim = N_KV * HEAD_DIM
    cols = GROUP * QT
    i = pl.program_id(1)
    q0 = i * QT
    c_max = (q0 + QT - 1) // KEY_CHUNK
    n_chunks = c_max + 1
    qpos = q0 + lax.broadcasted_iota(I32, (1, QT), 1)
    krow = lax.broadcasted_iota(I32, (KEY_CHUNK, 1), 0)
    scale = HEAD_DIM ** -0.5

    qbd_s[...] = jnp.zeros(qbd_s.shape, BF16)
    for g in range(N_KV):
        qbd_s[g * HEAD_DIM:(g + 1) * HEAD_DIM, g * cols:(g + 1) * cols] = (
            qt_ref[g * HEAD_DIM:(g + 1) * HEAD_DIM, :] * scale).astype(BF16)
    m_s[...] = jnp.full(m_s.shape, NEG, F32)
    l_s[...] = jnp.zeros(l_s.shape, F32)
    acc_s[...] = jnp.zeros(acc_s.shape, F32)

    if mode == "dsa":
        qit = qit_ref[...].astype(BF16)
        wit = wit_ref[...]

        def score_chunk(c, carry):
            k0 = pl.multiple_of(c * KEY_CHUNK, KEY_CHUNK)
            kib = ki_ref[pl.ds(k0, KEY_CHUNK), :]
            sc = jnp.zeros((KEY_CHUNK, QT), F32)
            for hp in range(IDX_HEADS // 2):
                lanes = slice(2 * hp * QT, 2 * (hp + 1) * QT)
                s = jnp.maximum(jnp.dot(kib, qit[:, lanes], preferred_element_type=F32), 0.0) * wit[:, lanes]
                sc = sc + s[:, :QT] + s[:, QT:]
            sc = jnp.where(k0 + krow <= qpos, sc, -jnp.inf)
            bits = pltpu.bitcast(sc, I32)
            bits = jnp.where(bits == INT_MIN, 0, bits)
            key_s[pl.ds(k0, KEY_CHUNK), :] = jnp.where(bits < 0, bits ^ 0x7FFFFFFF, bits)
            return carry

        lax.fori_loop(0, n_chunks, score_chunk, 0)

        def count(pred):
            def body(c, acc):
                k0 = pl.multiple_of(c * KEY_CHUNK, KEY_CHUNK)
                hit = pred(key_s[pl.ds(k0, KEY_CHUNK), :], k0).astype(I32)
                return acc + jnp.sum(hit.reshape(KEY_CHUNK // 8, 8, QT), axis=0)
            acc = lax.fori_loop(0, n_chunks, body, jnp.zeros((8, QT), I32))
            return jnp.sum(acc, axis=0, keepdims=True)

        def bit_step(bi, thr):
            cand = thr + jnp.left_shift(jnp.int32(1), 31 - bi)
            cnt = count(lambda k, k0: k >= cand)
            return jnp.where(cnt >= k_sel, cand, thr)

        thr = lax.fori_loop(0, 32, bit_step, jnp.full((1, QT), INT_MIN, I32))
        n_gt = count(lambda k, k0: k > thr)
        n_ge = count(lambda k, k0: k >= thr)
        need = k_sel - n_gt

        def tie_limit():
            n_bits = max(1, (lp - 1).bit_length())

            def idx_step(bi, lo):
                cand = lo + jnp.left_shift(jnp.int32(1), n_bits - 1 - bi)
                cnt = count(lambda k, k0: (k == thr) & (k0 + krow <= cand))
                return jnp.where(cnt < need, cand, lo)

            lo = lax.fori_loop(0, n_bits, idx_step, jnp.full((1, QT), -1, I32))
            return lo + 1

        tie_idx = lax.cond(jnp.max(n_ge - n_gt - need) > 0, tie_limit,
                           lambda: jnp.full((1, QT), lp, I32))
    else:
        nbp = km_ref.shape[1]
        blk = lax.broadcasted_iota(I32, (nbp, QT), 0)
        sel_blocks = []
        for g in range(N_KV):
            qg = qt_ref[g * HEAD_DIM:(g + 1) * HEAD_DIM, :]
            qsum = qg[:, 0:QT]
            for j in range(1, GROUP):
                qsum = qsum + qg[:, j * QT:(j + 1) * QT]
            gate = jnp.dot(km_ref[g], qsum, preferred_element_type=F32, precision=lax.Precision.HIGHEST)
            gate = jnp.where(blk < c_max, gate, -jnp.inf)
            sel = jnp.zeros(blk.shape, I32)
            for _ in range(MOBA_TOPK):
                mx = jnp.max(gate, axis=0, keepdims=True)
                first = jnp.min(jnp.where(gate == mx, blk, nbp), axis=0, keepdims=True)
                hit = blk == first
                sel = jnp.where(hit & (blk < c_max), 1, sel)
                gate = jnp.where(hit, -jnp.inf, gate)
            sel_blocks.append(sel)

    def attend(c, carry):
        k0 = pl.multiple_of(c * KEY_CHUNK, KEY_CHUNK)
        kidx = k0 + krow
        causal = kidx <= qpos
        off = pl.multiple_of(jnp.clip(STRIP_ORIGIN - (q0 - k0), 0, STRIP_ORIGIN), 128)
        if mode == "dsa":
            k = key_s[pl.ds(k0, KEY_CHUNK), :]
            mask = ((k > thr) | ((k == thr) & (kidx <= tie_idx))) & causal
        kblk = k_ref[pl.ds(k0, KEY_CHUNK), :]

        def logits(g):
            return jnp.dot(kblk, qbd_s[:, g * cols:(g + 1) * cols], preferred_element_type=F32)

        s_next = logits(0)
        for g in range(N_KV):
            if mode != "dsa":
                picked = jnp.sum(jnp.where(blk == c, sel_blocks[g], 0), axis=0, keepdims=True)
                mask = jnp.where(c == c_max, causal.astype(I32), picked) > 0
            s_all = s_next
            if g + 1 < N_KV:
                s_next = logits(g + 1)
            probs, alphas = [], []
            for j in range(GROUP):
                h = g * GROUP + j
                lanes = slice(j * QT, (j + 1) * QT)
                s = s_all[:, lanes] + strip_ref[g, pl.ds(off, KEY_CHUNK), lanes]
                s = jnp.where(mask, s, NEG)
                m_old = m_s[pl.ds(h, 1), :]
                m_new = jnp.maximum(m_old, jnp.max(s, axis=0, keepdims=True))
                alpha = jnp.exp(m_old - m_new)
                p = jnp.exp(s - m_new)
                l_s[pl.ds(h, 1), :] = alpha * l_s[pl.ds(h, 1), :] + jnp.sum(p, axis=0, keepdims=True)
                m_s[pl.ds(h, 1), :] = m_new
                probs.append(p.astype(BF16))
                alphas.append(alpha)
            pv = jnp.dot(vt_ref[g * HEAD_DIM:(g + 1) * HEAD_DIM, pl.ds(k0, KEY_CHUNK)],
                         jnp.concatenate(probs, axis=1), preferred_element_type=F32)
            acc_s[g] = jnp.concatenate(alphas, axis=1) * acc_s[g] + pv
        return carry

    lax.fori_loop(0, n_chunks, attend, 0)
    for g in range(N_KV):
        for j in range(0, GROUP, 2):
            h = g * GROUP + j
            pair = [acc_s[g][:, (j + i) * QT:(j + i + 1) * QT] * (1.0 / l_s[pl.ds(h + i, 1), :]) for i in range(2)]
            o_ref[:, h * HEAD_DIM:(h + 2) * HEAD_DIM] = jnp.concatenate(pair, axis=0).T


def _sparse_attention_t(mode, qt, k, vt, strip_t, extra, k_sel=0):
    bsz, nq = qt.shape[:2]
    lp = k.shape[1]
    kv_dim = N_KV * HEAD_DIM
    cols = GROUP * QT
    per_q = lambda *tail: pl.BlockSpec((None, None) + tail, lambda b, i: (b, i) + (0,) * len(tail))
    per_b = lambda *tail: pl.BlockSpec((None,) + tail, lambda b, i: (b,) + (0,) * len(tail))
    in_specs = [per_q(kv_dim, cols), per_b(lp, kv_dim), per_b(kv_dim, lp),
                pl.BlockSpec((N_KV, STRIP_W, cols), lambda b, i: (0, 0, 0))]
    scratch = [pltpu.VMEM((kv_dim, N_KV * cols), BF16),
               pltpu.VMEM((N_HEADS, QT), F32),
               pltpu.VMEM((N_HEADS, QT), F32),
               pltpu.VMEM((N_KV, HEAD_DIM, cols), F32)]
    if mode == "dsa":
        qit, wit, ki = extra
        in_specs += [per_q(IDX_DIM, IDX_HEADS * QT), per_q(1, IDX_HEADS * QT), per_b(lp, IDX_DIM)]
        scratch.append(pltpu.VMEM((lp, QT), I32))
    else:
        (km,) = extra
        in_specs.append(per_b(N_KV, km.shape[2], HEAD_DIM))
    return pl.pallas_call(
        functools.partial(_attn_t_kernel, mode, lp, k_sel),
        grid=(bsz, nq),
        in_specs=in_specs,
        out_specs=per_q(QT, N_HEADS * HEAD_DIM),
        out_shape=jax.ShapeDtypeStruct((bsz, nq, QT, N_HEADS * HEAD_DIM), F32),
        scratch_shapes=scratch,
        compiler_params=_cparams(("parallel", "arbitrary")),
        name=mode + "_attention_t",
    )(qt, k, vt, strip_t, *extra)


def _top_rows(s, k, payload=None, rank=None):
    rid = lax.broadcasted_iota(I32, s.shape, 0) if rank is None else rank
    vals, ids = [], []
    for _ in range(k):
        mx = jnp.max(s, axis=0, keepdims=True)
        first = jnp.min(jnp.where(s == mx, rid, jnp.iinfo(jnp.int32).max), axis=0, keepdims=True)
        hit = rid == first
        vals.append(mx)
        if payload is None:
            ids.append(first)
        else:
            ids.append(jnp.max(jnp.where(hit, payload, -1), axis=0, keepdims=True))
        s = jnp.where(hit, -jnp.inf, s)
    return jnp.concatenate(vals, axis=0), jnp.concatenate(ids, axis=0)


def _top_rows_paired(s, k):
    half = s.shape[0] // 2
    lo, hi = s[:half], s[half:]
    rid = lax.broadcasted_iota(I32, lo.shape, 0)
    first_lo = lo >= hi
    top = jnp.where(first_lo, lo, hi)
    rest = jnp.where(first_lo, hi, lo)
    top_id = jnp.where(first_lo, rid, rid + half)
    rest_id = jnp.where(first_lo, rid + half, rid)
    vals, ids = [], []
    for _ in range(k):
        mx = jnp.max(top, axis=0, keepdims=True)
        first = jnp.min(jnp.where(top == mx, top_id, jnp.iinfo(jnp.int32).max), axis=0, keepdims=True)
        hit = top_id == first
        vals.append(mx)
        ids.append(first)
        top = jnp.where(hit, rest, top)
        top_id = jnp.where(hit, rest_id, top_id)
        rest = jnp.where(hit, -jnp.inf, rest)
    return jnp.concatenate(vals, axis=0), jnp.concatenate(ids, axis=0)


def _peer_route_kernel(x_ref, wqt_ref, keys_ref, row_ref, par_ref, g_ref):
    xb = x_ref[...].astype(BF16)
    qt = lax.dot_general(wqt_ref[...], xb, (((1,), (1,)), ((), ())),
                         preferred_element_type=F32)
    n_t = x_ref.shape[0]
    piece = lax.broadcasted_iota(I32, (11 * 8, n_t), 0) // 8
    within = lax.broadcasted_iota(I32, (11 * 8, n_t), 0) % 8
    by_b = (piece >= 2) & (piece < 10)
    ca = jnp.where(piece < 2, 0, jnp.where(by_b, within, 8 + within))
    cb = jnp.where(piece == 0, within, jnp.where(piece == 1, 8 + within, jnp.where(by_b, piece - 2, 0)))
    cand_ok = ((ca + 1) * (cb + 1) <= PEER_TOPK) & jnp.logical_not(by_b & (within == 0))
    cand_rank = ca * PEER_TOPK + cb

    def pieces(first, second, combine):
        out = [combine(first[0:1], second[0:8]), combine(first[0:1], second[8:16])]
        out += [combine(first[0:8], second[b:b + 1]) for b in range(8)]
        out.append(combine(first[8:16], second[0:1]))
        return jnp.concatenate(out, axis=0)

    e_rows, g_rows = [], []
    for h in range(PEER_HEADS):
        sv, si = [], []
        for c in range(2):
            hc = 2 * h + c
            qhc = qt[hc * PEER_HALF:(hc + 1) * PEER_HALF].astype(BF16)
            s = jnp.dot(keys_ref[hc], qhc, preferred_element_type=F32)
            v_, i_ = _top_rows_paired(s, PEER_TOPK)
            sv.append(v_)
            si.append(i_)
        cand = jnp.where(cand_ok, pieces(sv[0], sv[1], lambda u, w: u + w), -jnp.inf)
        cidx = pieces(si[0], si[1], lambda u, w: u * PEER_NKEYS + w)
        gv, ge = _top_rows(cand, PEER_TOPK, payload=cidx, rank=cand_rank)
        ex = jnp.exp(gv - gv[0:1])
        g_rows.append(ex / jnp.sum(ex, axis=0, keepdims=True))
        e_rows.append(ge)
    e_t = jnp.concatenate(e_rows, axis=0).T
    row_ref[...] = jnp.right_shift(e_t, 1) * 8
    par_ref[...] = e_t & 1
    g_ref[...] = jnp.concatenate(g_rows, axis=0).T


def _peer_route(x, wqt_bf16, keys_bf16, tb):
    m, d = x.shape
    out = pl.BlockSpec((tb, PEER_PAIRS), lambda i: (i, 0))
    return pl.pallas_call(
        _peer_route_kernel,
        grid=(m // tb,),
        in_specs=[pl.BlockSpec((tb, d), lambda i: (i, 0)),
                  pl.BlockSpec(wqt_bf16.shape, lambda i: (0, 0)),
                  pl.BlockSpec(keys_bf16.shape, lambda i: (0, 0, 0))],
        out_specs=[out, out, out],
        out_shape=[jax.ShapeDtypeStruct((m, PEER_PAIRS), I32),
                   jax.ShapeDtypeStruct((m, PEER_PAIRS), I32),
                   jax.ShapeDtypeStruct((m, PEER_PAIRS), F32)],
        compiler_params=_cparams(("parallel",)),
        name="peer_route",
    )(x, wqt_bf16, keys_bf16)


FOLD_PAIRS = 32
HI_MASK = 0xFFFF0000
UP_SLOTS = 4
TILE_ROWS = 8


def _pack_table(w):
    n, d = w.shape
    bits = lax.bitcast_convert_type(w.astype(BF16), jnp.uint16).astype(jnp.uint32)
    bits = bits.reshape(n // 2, 2, d // 128, 128)
    return (bits[:, 0] | (bits[:, 1] << 16)).astype(jnp.uint32).reshape(n // 2 * (d // 128), 128)


def _load_tile(tbl_ref, row_ref, idx):
    return tbl_ref[pl.ds(pl.multiple_of(row_ref[idx], TILE_ROWS), TILE_ROWS), :]


def _peer_up_kernel(tb, row_ref, par_ref, x_ref, g_ref, fold_ref, tbl_ref, c_ref, prod_s):
    diag = (lax.broadcasted_iota(I32, (PEER_PAIRS, PEER_PAIRS), 0)
            == lax.broadcasted_iota(I32, (PEER_PAIRS, PEER_PAIRS), 1))

    def gather(t, slot):
        xt = x_ref[t]
        rows_t = row_ref.at[pl.ds(t * PEER_PAIRS, PEER_PAIRS)]
        for p in range(0, PEER_PAIRS, 2):
            w0 = _load_tile(tbl_ref, rows_t, p)
            w1 = _load_tile(tbl_ref, rows_t, p + 1)
            even = [pltpu.bitcast(jnp.left_shift(w, jnp.uint32(16)), F32) * xt for w in (w0, w1)]
            odd = [pltpu.bitcast(w & jnp.uint32(HI_MASK), F32) * xt for w in (w0, w1)]
            rows = pl.ds(p * TILE_ROWS, 2 * TILE_ROWS)
            prod_s[slot, rows, 0:128] = jnp.concatenate(even, axis=0).astype(BF16)
            prod_s[slot, rows, 128:256] = jnp.concatenate(odd, axis=0).astype(BF16)

    def reduce(t, slot):
        depth = FOLD_PAIRS * TILE_ROWS
        parts = [jnp.dot(fold_ref[...], prod_s[slot, pl.ds(i * depth, depth), :], preferred_element_type=F32)
                 for i in range(PEER_PAIRS // FOLD_PAIRS)]
        part = jnp.concatenate(parts, axis=0)
        rows = []
        for half in range(2):
            col = jnp.sum(part[:, half * 128:(half + 1) * 128], axis=1, keepdims=True)
            rows.append(jnp.sum(jnp.where(diag, col, 0.0), axis=0, keepdims=True))
        a = jnp.where(par_ref[pl.ds(t, 1), :] == 1, rows[1], rows[0])
        act = 0.5 * a * (1.0 + lax.erf(a * (2.0 ** -0.5)))
        c_ref[pl.ds(t, 1), :] = g_ref[pl.ds(t, 1), :] * act

    prod_s[UP_SLOTS:2 * UP_SLOTS] = jnp.zeros((UP_SLOTS,) + prod_s.shape[1:], BF16)

    def step(k, carry):
        t0 = 2 * UP_SLOTS * k
        for j in range(UP_SLOTS):
            gather(t0 + j, j)
            reduce(jnp.maximum(t0 - UP_SLOTS + j, 0), UP_SLOTS + j)
        for j in range(UP_SLOTS):
            gather(t0 + UP_SLOTS + j, UP_SLOTS + j)
            reduce(t0 + j, j)
        return carry

    lax.fori_loop(0, tb // (2 * UP_SLOTS), step, 0)
    for j in range(UP_SLOTS):
        reduce(tb - UP_SLOTS + j, UP_SLOTS + j)


DOWN_GROUP = 8


def _peer_down_kernel(tb, row_ref, par_ref, c_ref, spread_ref, tbl_ref, f_ref, tile_s, coef_s):
    width = PEER_PAIRS * 2 * TILE_ROWS
    lane = lax.broadcasted_iota(I32, (TILE_ROWS, width), 1)
    own_row = ((lane & (2 * TILE_ROWS - 1)) >> 1) == lax.broadcasted_iota(I32, (TILE_ROWS, width), 0)
    lane_par = (lax.broadcasted_iota(I32, (DOWN_GROUP, width), 1) & 1).astype(F32)

    def gather(t, slot):
        rows_t = row_ref.at[pl.ds(t * PEER_PAIRS, PEER_PAIRS)]
        for p in range(PEER_PAIRS):
            tile_s[slot, pl.ds(p * TILE_ROWS, TILE_ROWS), :] = _load_tile(tbl_ref, rows_t, p)

    def apply(t, j, slot):
        tiles = pltpu.bitcast(tile_s[slot], BF16)
        lhs = [jnp.where(own_row, coef_s[pl.ds(part * DOWN_GROUP + j, 1), :], 0.0) for part in range(2)]
        out = jnp.dot(jnp.concatenate(lhs, axis=0).astype(BF16), tiles, preferred_element_type=F32)
        f_ref[t] = out[0:TILE_ROWS] + out[TILE_ROWS:2 * TILE_ROWS]

    def group(gi, carry):
        t0 = pl.multiple_of(gi * DOWN_GROUP, DOWN_GROUP)
        gather(t0, 0)
        c = c_ref[pl.ds(t0, DOWN_GROUP), :]
        c_hi = c.astype(BF16)
        c_lo = (c - c_hi.astype(F32)).astype(BF16)
        par = par_ref[pl.ds(t0, DOWN_GROUP), :].astype(BF16)
        rep = jnp.dot(jnp.concatenate([c_hi, c_lo, par], axis=0), spread_ref[...],
                      preferred_element_type=F32)
        keep = rep[2 * DOWN_GROUP:] == lane_par
        coef_s[0:DOWN_GROUP, :] = jnp.where(keep, rep[0:DOWN_GROUP], 0.0)
        coef_s[DOWN_GROUP:2 * DOWN_GROUP, :] = jnp.where(keep, rep[DOWN_GROUP:2 * DOWN_GROUP], 0.0)
        for j in range(DOWN_GROUP):
            if j + 1 < DOWN_GROUP:
                gather(t0 + j + 1, (j + 1) % 2)
            apply(t0 + j, j, j % 2)
        return carry

    lax.fori_loop(0, tb // DOWN_GROUP, group, 0)


def _peer_experts(x, row, par, g, u_tbl, v_tbl, tb):
    m, d = x.shape
    x3 = x.reshape(m, d // 128, 128)
    flat = row.reshape(m * PEER_PAIRS)
    smem = pl.BlockSpec((tb * PEER_PAIRS,), lambda i: (i,), memory_space=pltpu.SMEM)
    vrow = pl.BlockSpec((tb, PEER_PAIRS), lambda i: (i, 0))
    tile = pl.BlockSpec((tb, d // 128, 128), lambda i: (i, 0, 0))
    resident = pl.BlockSpec(memory_space=pltpu.VMEM)
    const = lambda a: pl.BlockSpec(a.shape, lambda i: (0, 0))
    fold = jnp.asarray(np.arange(FOLD_PAIRS * TILE_ROWS)[None, :] // TILE_ROWS == np.arange(FOLD_PAIRS)[:, None], BF16)
    width = PEER_PAIRS * 2 * TILE_ROWS
    spread = jnp.asarray(np.arange(width)[None, :] // (2 * TILE_ROWS) == np.arange(PEER_PAIRS)[:, None], BF16)
    c = pl.pallas_call(
        functools.partial(_peer_up_kernel, tb),
        grid=(m // tb,),
        in_specs=[smem, vrow, tile, vrow, const(fold), resident],
        out_specs=vrow,
        out_shape=jax.ShapeDtypeStruct((m, PEER_PAIRS), F32),
        scratch_shapes=[pltpu.VMEM((2 * UP_SLOTS, PEER_PAIRS * TILE_ROWS, 256), BF16)],
        compiler_params=_cparams(("arbitrary",)),
        name="peer_up",
    )(flat, par, x3, g, fold, u_tbl)
    f = pl.pallas_call(
        functools.partial(_peer_down_kernel, tb),
        grid=(m // tb,),
        in_specs=[smem, vrow, vrow, const(spread), resident],
        out_specs=tile,
        out_shape=jax.ShapeDtypeStruct((m, d // 128, 128), F32),
        scratch_shapes=[pltpu.VMEM((2, PEER_PAIRS * TILE_ROWS, 128), jnp.uint32),
                        pltpu.VMEM((2 * DOWN_GROUP, width), F32)],
        compiler_params=_cparams(("arbitrary",)),
        name="peer_down",
    )(flat, par, c, spread, v_tbl)
    return f.reshape(m, d)


def _stack_heads(a, qb, n_lead):
    bsz, q, n, hd = a.shape
    n_in = n // n_lead
    a = a.reshape(bsz, q // qb, qb, n_lead, n_in, hd)
    return a.transpose(0, 1, 3, 4, 2, 5).reshape(bsz, q // qb, n_lead, n_in * qb, hd)


def _unstack_heads(o, qb):
    bsz, nq = o.shape[:2]
    o = o.reshape(bsz, nq, N_KV, GROUP, qb, HEAD_DIM).transpose(0, 1, 4, 2, 3, 5)
    return o.reshape(bsz * nq * qb, N_HEADS * HEAD_DIM)


def _run_trunk(x, p, past, prm, qb, tm, tb_route, tb_exp):
    bsz, q_len, d = x.shape
    depth = prm["w_o"].shape[0]
    m = bsz * q_len
    tm, tb_route, tb_exp = min(tm, m), min(tb_route, m), min(tb_exp, m)
    alpha = (2 * depth) ** 0.25
    attn_dim = N_HEADS * HEAD_DIM
    kv_dim = N_KV * HEAD_DIM
    if past is None:
        q_pos0 = 0
        n_pages = q_len // PAGE
        table = jnp.arange(bsz * n_pages, dtype=I32).reshape(bsz, n_pages)
    else:
        cache_k, cache_v, cache_ki, table = past
        q_pos0 = table.shape[1] * PAGE
    on_lanes = past is None and q_len % QT == 0
    if on_lanes:
        qb = QT
    nq = q_len // qb
    strip = _bias_strip(prm["rel_bias"], qb).reshape(N_KV, GROUP, qb, STRIP_W)
    if on_lanes:
        strip = strip.transpose(0, 3, 1, 2).reshape(N_KV, STRIP_W, GROUP * qb)
    else:
        strip = strip.reshape(N_KV, GROUP * qb, STRIP_W)
    xf = x.reshape(m, d)
    ks, vs, kis = [], [], []

    def relayout(new_rows, cache, layer, want):
        w = new_rows.shape[2]
        if past is None:
            return _relayout(new_rows.reshape(bsz * n_pages, PAGE, w), table, None, False, want)
        n_pool = cache.shape[1]
        pool = cache.reshape((cache.shape[0] * n_pool,) + cache.shape[2:])
        pool_t = jnp.moveaxis(pool, 1, -1).reshape(pool.shape[0], w, PAGE)
        pad = jnp.zeros((bsz, STEP_KEYS - q_len, w), F32)
        return _relayout(pool_t, table + layer * n_pool, jnp.concatenate([new_rows, pad], axis=1), True, want)

    for l in range(depth):
        dsa = l % 2 == 0
        w_in = prm["w_in_dsa"][l // 2] if dsa else prm["w_in_moba"][l // 2]
        n_in = w_in.shape[1]
        n_pad = -(-n_in // 128) * 128
        w_in = jnp.pad(w_in, ((0, 0), (0, n_pad - n_in))).astype(BF16)
        proj = _matmul(xf, w_in, tm)
        qh = proj[:, :attn_dim].reshape(bsz, q_len, N_HEADS, HEAD_DIM)
        k = proj[:, attn_dim:attn_dim + kv_dim].reshape(bsz, q_len, kv_dim)
        v = proj[:, attn_dim + kv_dim:attn_dim + 2 * kv_dim].reshape(bsz, q_len, kv_dim)
        ks.append(k.reshape(bsz, q_len, N_KV, HEAD_DIM))
        vs.append(v.reshape(bsz, q_len, N_KV, HEAD_DIM))
        k_lay = relayout(k, None if past is None else cache_k, l,
                         (("x",) if on_lanes else ("xt",)) + (() if dsa else ("sums",)))
        v_lay = relayout(v, None if past is None else cache_v, l, ("xt",) if on_lanes else ("x",))
        if on_lanes:
            q_in = qh.reshape(bsz, nq, qb, N_KV, GROUP, HEAD_DIM).transpose(0, 1, 3, 5, 4, 2)
            q_in = q_in.reshape(bsz, nq, kv_dim, GROUP * qb)
            kr, vt = k_lay["x"], v_lay["xt"]
        else:
            kt = k_lay["xt"].reshape(bsz, N_KV, HEAD_DIM, k_lay["xt"].shape[2])
            vr = v_lay["x"]
            q_in = _stack_heads(qh, qb, N_KV)
        if dsa:
            off = attn_dim + 2 * kv_dim
            qi = proj[:, off:off + IDX_HEADS * IDX_DIM].reshape(bsz, q_len, IDX_HEADS, IDX_DIM)
            off += IDX_HEADS * IDX_DIM
            ki = proj[:, off:off + IDX_DIM].reshape(bsz, q_len, IDX_DIM)
            wi = proj[:, off + IDX_DIM:off + IDX_DIM + IDX_HEADS].reshape(bsz, q_len, IDX_HEADS, 1)
            kis.append(ki)
            ki_lay = relayout(ki, None if past is None else cache_ki, l // 2, ("x",) if on_lanes else ("xt",))
            k_sel = min(IDX_TOPK, (q_pos0 + q_len) // 4)
            if on_lanes:
                qi_t = qi.reshape(bsz, nq, qb, IDX_HEADS, IDX_DIM).transpose(0, 1, 4, 3, 2)
                qi_t = qi_t.reshape(bsz, nq, IDX_DIM, IDX_HEADS * qb)
                wi_t = wi.reshape(bsz, nq, qb, IDX_HEADS).transpose(0, 1, 3, 2).reshape(bsz, nq, 1, IDX_HEADS * qb)
                o = _sparse_attention_t("dsa", q_in, kr, vt, strip, (qi_t, wi_t, ki_lay["x"]), k_sel)
            else:
                qi_st = _stack_heads(qi, qb, 1)[:, :, 0]
                wi_st = _stack_heads(wi, qb, 1)[:, :, 0]
                o = _sparse_attention("dsa", q_in, kt, vr, strip, (qi_st, wi_st, ki_lay["xt"]), qb, q_pos0, k_sel)
        else:
            ksum = k_lay["sums"]
            nb = ksum.shape[1]
            kmean = (ksum * (1.0 / MOBA_BLOCK)).reshape(bsz, nb, N_KV, HEAD_DIM)
            if on_lanes:
                km = jnp.pad(kmean.transpose(0, 2, 1, 3), ((0, 0), (0, 0), (0, -nb % 8), (0, 0)))
                o = _sparse_attention_t("moba", q_in, kr, vt, strip, (km,))
            else:
                kmt = jnp.pad(kmean.transpose(0, 2, 3, 1), ((0, 0), (0, 0), (0, 0), (0, -nb % 128)))
                o = _sparse_attention("moba", q_in, kt, vr, strip, (kmt,), qb, q_pos0)
        if on_lanes:
            o = o.reshape(m, attn_dim)
        else:
            o = _unstack_heads(o, qb)
        xf = _attn_out(o, prm["w_o"][l].astype(BF16), xf, prm["ln_g"][l, 0][None], prm["ln_b"][l, 0][None],
                       alpha, tm)
        row, par, g = _peer_route(xf, prm["peer_wq"][l].T.astype(BF16),
                                 prm["peer_keys"][l].reshape(2 * PEER_HEADS, PEER_NKEYS, PEER_HALF).astype(BF16),
                                 tb_route)
        f = _peer_experts(xf, row, par, g, _pack_table(prm["peer_u"][l]), _pack_table(prm["peer_v"][l]), tb_exp)
        xf = _layer_tail(xf, f, prm["ln_g"][l, 1][None], prm["ln_b"][l, 1][None],
                         prm["gate_w"][l].astype(BF16), prm["gate_b"][l][None],
                         p[l].reshape(m, -1), prm["ple_w"][l].astype(BF16), alpha, tm)
    return xf.reshape(bsz, q_len, d), jnp.stack(ks), jnp.stack(vs), jnp.stack(kis)


def kernel(x_prompt, x_sample, cache_k, cache_v, cache_ki, page_table, p_prompt, p_sample, w_in_dsa, w_in_moba, w_o, rel_bias, ln_g, ln_b, peer_wq, peer_keys, peer_u, peer_v, gate_w, gate_b, ple_w):
    prm = {
        "w_in_dsa": w_in_dsa, "w_in_moba": w_in_moba, "w_o": w_o, "rel_bias": rel_bias,
        "ln_g": ln_g, "ln_b": ln_b, "peer_wq": peer_wq, "peer_keys": peer_keys,
        "peer_u": peer_u, "peer_v": peer_v, "gate_w": gate_w, "gate_b": gate_b, "ple_w": ple_w,
    }
    q_sample = x_sample.shape[1]
    y_prompt, k_prompt, v_prompt, ki_prompt = _run_trunk(
        x_prompt, p_prompt, None, prm, qb=128, tm=512, tb_route=256, tb_exp=128)
    y_sample, k_sample, v_sample, ki_sample = _run_trunk(
        x_sample, p_sample, (cache_k, cache_v, cache_ki, page_table), prm,
        qb=q_sample, tm=256, tb_route=256, tb_exp=64)
    return (y_prompt, y_sample, k_prompt, v_prompt, ki_prompt, k_sample, v_sample, ki_sample)
```

```python
import functools
import math

import numpy as np
import jax
import jax.numpy as jnp
from jax import lax
from jax.experimental import pallas as pl
from jax.experimental.pallas import tpu as pltpu

F32 = jnp.float32
BF16 = jnp.bfloat16
I32 = jnp.int32

N_HEADS = 16
HEAD_DIM = 64
N_KV = 4
GROUP = 4
IDX_HEADS = 8
IDX_DIM = 64
IDX_TOPK = 256
MOBA_BLOCK = 256
MOBA_TOPK = 3
REL_BUCKETS = 32
REL_MAX_DIST = 128
PEER_HEADS = 8
PEER_NKEYS = 128
PEER_HALF = 128
PEER_TOPK = 16
PEER_PAIRS = PEER_HEADS * PEER_TOPK
LN_EPS = 1e-5
PAGE = 128

KEY_CHUNK = 256
STRIP_W = 768
STRIP_ORIGIN = 512
NEG = -1e30
INT_MIN = -2147483648
VMEM_LIMIT = 56 * 1024 * 1024


def _cparams(sem):
    return pltpu.CompilerParams(dimension_semantics=sem, vmem_limit_bytes=VMEM_LIMIT)


def _mm_kernel(a_ref, b_ref, o_ref):
    o_ref[...] = jnp.dot(a_ref[...].astype(BF16), b_ref[...], preferred_element_type=F32)


def _matmul(a, b_bf16, tm):
    m, k = a.shape
    n = b_bf16.shape[1]
    return pl.pallas_call(
        _mm_kernel,
        grid=(m // tm,),
        in_specs=[pl.BlockSpec((tm, k), lambda i: (i, 0)),
                  pl.BlockSpec((k, n), lambda i: (0, 0))],
        out_specs=pl.BlockSpec((tm, n), lambda i: (i, 0)),
        out_shape=jax.ShapeDtypeStruct((m, n), F32),
        compiler_params=_cparams(("parallel",)),
        name="in_proj",
    )(a, b_bf16)


def _in_proj_lanes_kernel(has_idx, *refs):
    if has_idx:
        x_ref, wqt_ref, wrow_ref, wqit_ref, wit_ref, qt_ref, k_ref, v_ref, ki_ref, qit_ref, wi_ref = refs
    else:
        x_ref, wqt_ref, wrow_ref, qt_ref, k_ref, v_ref = refs
    kv_dim = N_KV * HEAD_DIM
    xb = x_ref[...].astype(BF16)
    n_blk = xb.shape[0] // QT
    nt = (((1,), (1,)), ((), ()))
    rows = jnp.dot(xb, wrow_ref[...], preferred_element_type=F32)
    k_ref[...] = rows[:, 0:kv_dim]
    v_ref[...] = rows[:, kv_dim:2 * kv_dim]
    q_t = lax.dot_general(wqt_ref[...], xb, nt, preferred_element_type=F32)
    for i in range(n_blk):
        for g in range(N_KV):
            for j in range(GROUP):
                h = g * GROUP + j
                qt_ref[i, g * HEAD_DIM:(g + 1) * HEAD_DIM, j * QT:(j + 1) * QT] = (
                    q_t[h * HEAD_DIM:(h + 1) * HEAD_DIM, i * QT:(i + 1) * QT])
    if has_idx:
        ki_ref[...] = rows[:, 2 * kv_dim:2 * kv_dim + IDX_DIM]
        qi_t = lax.dot_general(wqit_ref[...], xb, nt, preferred_element_type=F32)
        wi_t = lax.dot_general(wit_ref[...], xb, nt, preferred_element_type=F32)
        for i in range(n_blk):
            for h in range(IDX_HEADS):
                qit_ref[i, :, h * QT:(h + 1) * QT] = qi_t[h * IDX_DIM:(h + 1) * IDX_DIM, i * QT:(i + 1) * QT]
                wi_ref[i, :, h * QT:(h + 1) * QT] = wi_t[h:h + 1, i * QT:(i + 1) * QT]


def _in_proj_lanes(x, w_in, has_idx, tm):
    m, d = x.shape
    attn_dim = N_HEADS * HEAD_DIM
    kv_dim = N_KV * HEAD_DIM
    n_blk = tm // QT
    w_rows = w_in[:, attn_dim:attn_dim + 2 * kv_dim]
    off = attn_dim + 2 * kv_dim
    if has_idx:
        w_ki = w_in[:, off + IDX_HEADS * IDX_DIM:off + IDX_HEADS * IDX_DIM + IDX_DIM]
        w_rows = jnp.concatenate([w_rows, w_ki, jnp.zeros((d, 128 - IDX_DIM), F32)], axis=1)
    args = [x, w_in[:, :attn_dim].T.astype(BF16), w_rows.astype(BF16)]
    whole = lambda a: pl.BlockSpec(a.shape, lambda i: (0, 0))
    in_specs = [pl.BlockSpec((tm, d), lambda i: (i, 0)), whole(args[1]), whole(args[2])]
    blk3 = lambda r, c: pl.BlockSpec((n_blk, r, c), lambda i: (i, 0, 0))
    rowb = lambda c: pl.BlockSpec((tm, c), lambda i: (i, 0))
    out_specs = [blk3(kv_dim, GROUP * QT), rowb(kv_dim), rowb(kv_dim)]
    out_shape = [jax.ShapeDtypeStruct((m // QT, kv_dim, GROUP * QT), F32),
                 jax.ShapeDtypeStruct((m, kv_dim), F32), jax.ShapeDtypeStruct((m, kv_dim), F32)]
    if has_idx:
        w_wi = w_in[:, off + IDX_HEADS * IDX_DIM + IDX_DIM:off + IDX_HEADS * IDX_DIM + IDX_DIM + IDX_HEADS]
        args += [w_in[:, off:off + IDX_HEADS * IDX_DIM].T.astype(BF16), w_wi.T.astype(BF16)]
        in_specs += [whole(args[3]), whole(args[4])]
        out_specs += [rowb(IDX_DIM), blk3(IDX_DIM, IDX_HEADS * QT), blk3(1, IDX_HEADS * QT)]
        out_shape += [jax.ShapeDtypeStruct((m, IDX_DIM), F32),
                      jax.ShapeDtypeStruct((m // QT, IDX_DIM, IDX_HEADS * QT), F32),
                      jax.ShapeDtypeStruct((m // QT, 1, IDX_HEADS * QT), F32)]
    return pl.pallas_call(
        functools.partial(_in_proj_lanes_kernel, has_idx),
        grid=(m // tm,),
        in_specs=in_specs,
        out_specs=out_specs,
        out_shape=out_shape,
        compiler_params=_cparams(("parallel",)),
        name="in_proj_lanes",
    )(*args)


def _layer_norm(y, g, b):
    mu = jnp.mean(y, axis=-1, keepdims=True)
    yc = y - mu
    var = jnp.mean(yc * yc, axis=-1, keepdims=True)
    return yc * lax.rsqrt(var + LN_EPS) * g + b


def _attn_out_kernel(alpha, o_ref, w_ref, x_ref, g_ref, b_ref, y_ref):
    y = alpha * x_ref[...] + jnp.dot(o_ref[...].astype(BF16), w_ref[...],
                                     preferred_element_type=F32)
    y_ref[...] = _layer_norm(y, g_ref[...], b_ref[...])


def _attn_out(o, w_bf16, x, g, b, alpha, tm):
    m, d = x.shape
    row = pl.BlockSpec((tm, d), lambda i: (i, 0))
    vec = pl.BlockSpec((1, d), lambda i: (0, 0))
    return pl.pallas_call(
        functools.partial(_attn_out_kernel, alpha),
        grid=(m // tm,),
        in_specs=[row, pl.BlockSpec((d, d), lambda i: (0, 0)), row, vec, vec],
        out_specs=row,
        out_shape=jax.ShapeDtypeStruct((m, d), F32),
        compiler_params=_cparams(("parallel",)),
        name="attn_out_ln",
    )(o, w_bf16, x, g, b)


def _layer_tail_kernel(alpha, x_ref, f_ref, g_ref, b_ref, gw_ref, gb_ref, p_ref, pw_ref, y_ref):
    x2 = _layer_norm(alpha * x_ref[...] + f_ref[...], g_ref[...], b_ref[...])
    z = jnp.dot(x2.astype(BF16), gw_ref[...], preferred_element_type=F32) + gb_ref[...]
    gate = 1.0 / (1.0 + jnp.exp(-z))
    e = jnp.dot(p_ref[...].astype(BF16), pw_ref[...], preferred_element_type=F32)
    y_ref[...] = x2 + gate * e


def _layer_tail(x, f, g, b, gw_bf16, gb, p, pw_bf16, alpha, tm):
    m, d = x.shape
    pd = p.shape[1]
    row = pl.BlockSpec((tm, d), lambda i: (i, 0))
    vec = pl.BlockSpec((1, d), lambda i: (0, 0))
    return pl.pallas_call(
        functools.partial(_layer_tail_kernel, alpha),
        grid=(m // tm,),
        in_specs=[row, row, vec, vec, pl.BlockSpec((d, d), lambda i: (0, 0)), vec,
                  pl.BlockSpec((tm, pd), lambda i: (i, 0)),
                  pl.BlockSpec((pd, d), lambda i: (0, 0))],
        out_specs=row,
        out_shape=jax.ShapeDtypeStruct((m, d), F32),
        compiler_params=_cparams(("parallel",)),
        name="layer_tail",
    )(x, f, g, b, gw_bf16, gb, p, pw_bf16)


STEP_PAGES = 8
STEP_KEYS = STEP_PAGES * PAGE
STEP_CHUNKS = STEP_KEYS // KEY_CHUNK


def _relayout_kernel(n_pool_steps, has_new, pages_t, want, *refs):
    page_refs = refs[1:1 + STEP_PAGES]
    new_ref = refs[1 + STEP_PAGES] if has_new else None
    outs = dict(zip(want, refs[len(refs) - len(want):]))
    j = pl.program_id(1)
    need_rows = "x" in want or "sums" in want

    def emit(rows, cols):
        if "xt" in want:
            outs["xt"][...] = (rows.T if cols is None else cols).astype(BF16)
        if need_rows and rows is None:
            rows = cols.T
        if "x" in want:
            outs["x"][...] = rows.astype(BF16)
        if "sums" in want:
            for i in range(STEP_CHUNKS):
                outs["sums"][pl.ds(i, 1), :] = jnp.sum(rows[i * KEY_CHUNK:(i + 1) * KEY_CHUNK], axis=0, keepdims=True)

    def from_pages():
        if pages_t:
            emit(None, jnp.concatenate([r[...] for r in page_refs], axis=1))
        else:
            emit(jnp.concatenate([r[...] for r in page_refs], axis=0), None)

    if has_new:
        pl.when(j < n_pool_steps)(from_pages)

        @pl.when(j >= n_pool_steps)
        def _():
            emit(new_ref[...], None)
    else:
        from_pages()


def _relayout(pool, table, new, pages_t, want):
    bsz, n_pages = table.shape
    w = pool.shape[1] if pages_t else pool.shape[2]
    n_pool_steps = n_pages // STEP_PAGES
    n_steps = n_pool_steps + (1 if new is not None else 0)
    lp = n_steps * STEP_KEYS
    last = n_pool_steps - 1

    def page_map(i):
        return lambda b, j, tbl: (tbl[b, STEP_PAGES * jnp.minimum(j, last) + i], 0, 0)

    page_block = (None, w, PAGE) if pages_t else (None, PAGE, w)
    in_specs = [pl.BlockSpec(page_block, page_map(i)) for i in range(STEP_PAGES)]
    args = [pool] * STEP_PAGES
    if new is not None:
        in_specs.append(pl.BlockSpec((None, STEP_KEYS, w), lambda b, j, tbl: (b, 0, 0)))
        args.append(new)
    specs = {"xt": (pl.BlockSpec((None, w, STEP_KEYS), lambda b, j, tbl: (b, 0, j)),
                    jax.ShapeDtypeStruct((bsz, w, lp), BF16)),
             "x": (pl.BlockSpec((None, STEP_KEYS, w), lambda b, j, tbl: (b, j, 0)),
                   jax.ShapeDtypeStruct((bsz, lp, w), BF16)),
             "sums": (pl.BlockSpec((None, None, STEP_CHUNKS, w), lambda b, j, tbl: (b, j, 0, 0)),
                      jax.ShapeDtypeStruct((bsz, n_steps, STEP_CHUNKS, w), F32))}
    outs = pl.pallas_call(
        functools.partial(_relayout_kernel, n_pool_steps, new is not None, pages_t, want),
        grid_spec=pltpu.PrefetchScalarGridSpec(
            num_scalar_prefetch=1,
            grid=(bsz, n_steps),
            in_specs=in_specs,
            out_specs=[specs[name][0] for name in want]),
        out_shape=[specs[name][1] for name in want],
        compiler_params=_cparams(("parallel", "arbitrary")),
        name="kv_relayout",
    )(table, *args)
    outs = dict(zip(want, outs))
    if "sums" in outs:
        outs["sums"] = outs["sums"].reshape(bsz, n_steps * STEP_CHUNKS, w)
    return outs


def _bucket_table():
    n = np.arange(REL_MAX_DIST + 1)
    exact = REL_BUCKETS // 2
    nf = np.maximum(n, 1).astype(np.float32)
    large = exact + (np.log(nf / np.float32(exact)) / np.float32(math.log(REL_MAX_DIST / exact))
                     * np.float32(REL_BUCKETS - exact)).astype(np.int32)
    return np.where(n < exact, n, np.minimum(large, REL_BUCKETS - 1)).astype(np.int32)


def _bias_strip_kernel(bk_ref, rb_ref, o_ref):
    bk = bk_ref[...]
    for h in range(N_HEADS):
        acc = jnp.zeros(bk.shape, F32)
        for bkt in range(REL_BUCKETS):
            acc = jnp.where(bk == bkt, rb_ref[bkt, h], acc)
        o_ref[h] = acc


def _bias_strip(rel_bias, qb):
    qi = np.arange(qb)[:, None]
    z = np.arange(STRIP_W)[None, :]
    dist = np.clip(qi - z + STRIP_ORIGIN, 0, REL_MAX_DIST)
    bk = jnp.asarray(_bucket_table()[dist])
    out = pl.pallas_call(
        _bias_strip_kernel,
        in_specs=[pl.BlockSpec(memory_space=pltpu.VMEM), pl.BlockSpec(memory_space=pltpu.SMEM)],
        out_specs=pl.BlockSpec(memory_space=pltpu.VMEM),
        out_shape=jax.ShapeDtypeStruct((N_HEADS, qb, STRIP_W), F32),
        name="bias_strip",
    )(bk, rel_bias)
    return out


def _attn_kernel(mode, qb, lp, q_pos0, k_sel, *refs):
    if mode == "dsa":
        (q_ref, kt_ref, v_ref, strip_ref, qi_ref, wi_ref, kit_ref, o_ref,
         qs_s, m_s, l_s, acc_s, key_s) = refs
    else:
        (q_ref, kt_ref, v_ref, strip_ref, kmt_ref, o_ref,
         qs_s, m_s, l_s, acc_s) = refs
    rows = GROUP * qb
    i = pl.program_id(1)
    q0 = q_pos0 + i * qb
    c_max = (q0 + qb - 1) // KEY_CHUNK
    n_chunks = c_max + 1
    qpos = q0 + lax.broadcasted_iota(I32, (qb, 1), 0)
    lane = lax.broadcasted_iota(I32, (1, KEY_CHUNK), 1)
    scale = HEAD_DIM ** -0.5

    for g in range(N_KV):
        qs_s[g] = (q_ref[g] * scale).astype(BF16)
    m_s[...] = jnp.full(m_s.shape, NEG, F32)
    l_s[...] = jnp.zeros(l_s.shape, F32)
    acc_s[...] = jnp.zeros(acc_s.shape, F32)

    if mode == "dsa":
        qi = qi_ref[...].astype(BF16)
        wi = wi_ref[...]

        def score_chunk(c, carry):
            k0 = pl.multiple_of(c * KEY_CHUNK, KEY_CHUNK)
            s = jnp.dot(qi, kit_ref[:, pl.ds(k0, KEY_CHUNK)], preferred_element_type=F32)
            s = jnp.maximum(s, 0.0) * wi
            sc = s[0:qb]
            for h in range(1, IDX_HEADS):
                sc = sc + s[h * qb:(h + 1) * qb]
            sc = jnp.where(k0 + lane <= qpos, sc, -jnp.inf)
            bits = pltpu.bitcast(sc, I32)
            bits = jnp.where(bits == INT_MIN, 0, bits)
            key_s[:, pl.ds(k0, KEY_CHUNK)] = jnp.where(bits < 0, bits ^ 0x7FFFFFFF, bits)
            return carry

        scan = STEP_KEYS if qb <= 32 else KEY_CHUNK
        n_scan = (n_chunks * KEY_CHUNK + scan - 1) // scan
        lax.fori_loop(0, n_scan * (scan // KEY_CHUNK), score_chunk, 0)
        scan_lane = lax.broadcasted_iota(I32, (1, scan), 1)

        def count(pred):
            def body(c, acc):
                k0 = pl.multiple_of(c * scan, scan)
                return acc + pred(key_s[:, pl.ds(k0, scan)], k0 + scan_lane).astype(I32)
            acc = lax.fori_loop(0, n_scan, body, jnp.zeros((qb, scan), I32))
            return jnp.sum(acc, axis=1, keepdims=True)

        def bit_step(bi, thr):
            cand = thr + jnp.left_shift(jnp.int32(1), 31 - bi)
            cnt = count(lambda k, kidx: k >= cand)
            return jnp.where(cnt >= k_sel, cand, thr)

        thr = lax.fori_loop(0, 32, bit_step, jnp.full((qb, 1), INT_MIN, I32))
        n_gt = count(lambda k, kidx: k > thr)
        n_ge = count(lambda k, kidx: k >= thr)
        need = k_sel - n_gt

        def tie_limit():
            n_bits = max(1, (lp - 1).bit_length())

            def idx_step(bi, lo):
                cand = lo + jnp.left_shift(jnp.int32(1), n_bits - 1 - bi)
                cnt = count(lambda k, kidx: (k == thr) & (kidx <= cand))
                return jnp.where(cnt < need, cand, lo)

            lo = lax.fori_loop(0, n_bits, idx_step, jnp.full((qb, 1), -1, I32))
            return lo + 1

        tie_idx = lax.cond(jnp.max(n_ge - n_gt - need) > 0, tie_limit,
                           lambda: jnp.full((qb, 1), lp, I32))
    else:
        blk = lax.broadcasted_iota(I32, (qb, kmt_ref.shape[2]), 1)
        sel_blocks = []
        for g in range(N_KV):
            qg = q_ref[g]
            qsum = qg[0:qb]
            for j in range(1, GROUP):
                qsum = qsum + qg[j * qb:(j + 1) * qb]
            gate = jnp.dot(qsum, kmt_ref[g], preferred_element_type=F32,
                           precision=lax.Precision.HIGHEST)
            gate = jnp.where(blk < c_max, gate, -jnp.inf)
            sel = jnp.zeros(blk.shape, jnp.bool_)
            for _ in range(MOBA_TOPK):
                mx = jnp.max(gate, axis=1, keepdims=True)
                first = jnp.min(jnp.where(gate == mx, blk, blk.shape[1]), axis=1, keepdims=True)
                hit = blk == first
                sel = sel | (hit & (blk < c_max))
                gate = jnp.where(hit, -jnp.inf, gate)
            sel_blocks.append(sel.astype(I32))

    def attend(c, carry):
        k0 = pl.multiple_of(c * KEY_CHUNK, KEY_CHUNK)
        kidx = k0 + lane
        causal = kidx <= qpos
        off = pl.multiple_of(jnp.clip(STRIP_ORIGIN - (q0 - k0), 0, STRIP_ORIGIN), 128)
        if mode == "dsa":
            k = key_s[:, pl.ds(k0, KEY_CHUNK)]
            mask1 = ((k > thr) | ((k == thr) & (kidx <= tie_idx))) & causal
            mask = jnp.concatenate([mask1] * GROUP, axis=0)

        logits = [jnp.dot(qs_s[g], kt_ref[g, :, pl.ds(k0, KEY_CHUNK)], preferred_element_type=F32)
                  for g in range(N_KV)]
        probs, alphas = [], []
        for g in range(N_KV):
            if mode != "dsa":
                picked = jnp.sum(jnp.where(blk == c, sel_blocks[g], 0), axis=1, keepdims=True)
                mask1 = jnp.where(c == c_max, causal.astype(I32), picked) > 0
                mask = jnp.concatenate([mask1] * GROUP, axis=0)
            s = jnp.where(mask, logits[g] + strip_ref[g, :, pl.ds(off, KEY_CHUNK)], NEG)
            m_old = m_s[g]
            m_new = jnp.maximum(m_old, jnp.max(s, axis=1, keepdims=True))
            alpha = jnp.exp(m_old - m_new)
            p = jnp.exp(s - m_new)
            l_s[g] = alpha * l_s[g] + jnp.sum(p, axis=1, keepdims=True)
            m_s[g] = m_new
            probs.append(p.astype(BF16))
            alphas.append(alpha)
        vblk = v_ref[pl.ds(k0, KEY_CHUNK), :]
        for g in range(N_KV):
            acc_s[g] = alphas[g] * acc_s[g] + jnp.dot(probs[g], vblk, preferred_element_type=F32)
        return carry

    lax.fori_loop(0, n_chunks, attend, 0)
    for g in range(N_KV):
        o_ref[g] = acc_s[g][:, g * HEAD_DIM:(g + 1) * HEAD_DIM] / l_s[g]


def _sparse_attention(mode, q_st, kt, v, strip, extra, qb, q_pos0, k_sel=0):
    bsz, nq = q_st.shape[:2]
    lp = v.shape[1]
    rows = GROUP * qb
    per_q = lambda *tail: pl.BlockSpec((None, None) + tail, lambda b, i: (b, i) + (0,) * len(tail))
    per_b = lambda *tail: pl.BlockSpec((None,) + tail, lambda b, i: (b,) + (0,) * len(tail))
    in_specs = [per_q(N_KV, rows, HEAD_DIM), per_b(N_KV, HEAD_DIM, lp), per_b(lp, N_KV * HEAD_DIM),
                pl.BlockSpec((N_KV, rows, STRIP_W), lambda b, i: (0, 0, 0))]
    scratch = [pltpu.VMEM((N_KV, rows, HEAD_DIM), BF16),
               pltpu.VMEM((N_KV, rows, 1), F32),
               pltpu.VMEM((N_KV, rows, 1), F32),
               pltpu.VMEM((N_KV, rows, N_KV * HEAD_DIM), F32)]
    if mode == "dsa":
        qi_st, wi_st, kit = extra
        in_specs += [per_q(IDX_HEADS * qb, IDX_DIM), per_q(IDX_HEADS * qb, 1), per_b(IDX_DIM, lp)]
        scratch.append(pltpu.VMEM((qb, lp), I32))
    else:
        (kmt,) = extra
        in_specs.append(per_b(N_KV, HEAD_DIM, kmt.shape[3]))
    return pl.pallas_call(
        functools.partial(_attn_kernel, mode, qb, lp, q_pos0, k_sel),
        grid=(bsz, nq),
        in_specs=in_specs,
        out_specs=per_q(N_KV, rows, HEAD_DIM),
        out_shape=jax.ShapeDtypeStruct((bsz, nq, N_KV, rows, HEAD_DIM), F32),
        scratch_shapes=scratch,
        compiler_params=_cparams(("parallel", "arbitrary")),
        name=mode + "_attention",
    )(q_st, kt, v, strip, *extra)


QT = 128


def _attn_t_kernel(mode, lp, k_sel, *refs):
    if mode == "dsa":
        (qt_ref, k_ref, vt_ref, strip_ref, qit_ref, wit_ref, ki_ref, o_ref,
         qbd_s, m_s, l_s, acc_s, key_s) = refs
    else:
        (qt_ref, k_ref, vt_ref, strip_ref, km_ref, o_ref,
         qbd_s, m_s, l_s, acc_s) = refs
    kv_dim = N_KV * HEAD_DIM
    cols = GROUP * QT
    i = pl.program_id(1)
    q0 = i * QT
    c_max = (q0 + QT - 1) // KEY_CHUNK
    n_chunks = c_max + 1
    qpos = q0 + lax.broadcasted_iota(I32, (1, QT), 1)
    krow = lax.broadcasted_iota(I32, (KEY_CHUNK, 1), 0)
    scale = HEAD_DIM ** -0.5

    qbd_s[...] = jnp.zeros(qbd_s.shape, BF16)
    for g in range(N_KV):
        qbd_s[g * HEAD_DIM:(g + 1) * HEAD_DIM, g * cols:(g + 1) * cols] = (
            qt_ref[g * HEAD_DIM:(g + 1) * HEAD_DIM, :] * scale).astype(BF16)
    m_s[...] = jnp.full(m_s.shape, NEG, F32)
    l_s[...] = jnp.zeros(l_s.shape, F32)
    acc_s[...] = jnp.zeros(acc_s.shape, F32)

    if mode == "dsa":
        qit = qit_ref[...].astype(BF16)
        wit = wit_ref[...]

        def score_chunk(c, carry):
            k0 = pl.multiple_of(c * KEY_CHUNK, KEY_CHUNK)
            kib = ki_ref[pl.ds(k0, KEY_CHUNK), :]
            sc = jnp.zeros((KEY_CHUNK, QT), F32)
            for hp in range(IDX_HEADS // 2):
                lanes = slice(2 * hp * QT, 2 * (hp + 1) * QT)
                s = jnp.maximum(jnp.dot(kib, qit[:, lanes], preferred_element_type=F32), 0.0) * wit[:, lanes]
                sc = sc + s[:, :QT] + s[:, QT:]
            sc = jnp.where(k0 + krow <= qpos, sc, -jnp.inf)
            bits = pltpu.bitcast(sc, I32)
            bits = jnp.where(bits == INT_MIN, 0, bits)
            key_s[pl.ds(k0, KEY_CHUNK), :] = jnp.where(bits < 0, bits ^ 0x7FFFFFFF, bits)
            return carry

        lax.fori_loop(0, n_chunks, score_chunk, 0)

        def count(pred):
            def body(c, acc):
                k0 = pl.multiple_of(c * KEY_CHUNK, KEY_CHUNK)
                hit = pred(key_s[pl.ds(k0, KEY_CHUNK), :], k0).astype(I32)
                return acc + jnp.sum(hit.reshape(KEY_CHUNK // 8, 8, QT), axis=0)
            acc = lax.fori_loop(0, n_chunks, body, jnp.zeros((8, QT), I32))
            return jnp.sum(acc, axis=0, keepdims=True)

        def bit_step(bi, thr):
            cand = thr + jnp.left_shift(jnp.int32(1), 31 - bi)
            cnt = count(lambda k, k0: k >= cand)
            return jnp.where(cnt >= k_sel, cand, thr)

        thr = lax.fori_loop(0, 32, bit_step, jnp.full((1, QT), INT_MIN, I32))
        n_gt = count(lambda k, k0: k > thr)
        n_ge = count(lambda k, k0: k >= thr)
        need = k_sel - n_gt

        def tie_limit():
            n_bits = max(1, (lp - 1).bit_length())

            def idx_step(bi, lo):
                cand = lo + jnp.left_shift(jnp.int32(1), n_bits - 1 - bi)
                cnt = count(lambda k, k0: (k == thr) & (k0 + krow <= cand))
                return jnp.where(cnt < need, cand, lo)

            lo = lax.fori_loop(0, n_bits, idx_step, jnp.full((1, QT), -1, I32))
            return lo + 1

        tie_idx = lax.cond(jnp.max(n_ge - n_gt - need) > 0, tie_limit,
                           lambda: jnp.full((1, QT), lp, I32))
    else:
        nbp = km_ref.shape[1]
        blk = lax.broadcasted_iota(I32, (nbp, QT), 0)
        sel_blocks = []
        for g in range(N_KV):
            qg = qt_ref[g * HEAD_DIM:(g + 1) * HEAD_DIM, :]
            qsum = qg[:, 0:QT]
            for j in range(1, GROUP):
                qsum = qsum + qg[:, j * QT:(j + 1) * QT]
            gate = jnp.dot(km_ref[g], qsum, preferred_element_type=F32, precision=lax.Precision.HIGHEST)
            gate = jnp.where(blk < c_max, gate, -jnp.inf)
            sel = jnp.zeros(blk.shape, I32)
            for _ in range(MOBA_TOPK):
                mx = jnp.max(gate, axis=0, keepdims=True)
                first = jnp.min(jnp.where(gate == mx, blk, nbp), axis=0, keepdims=True)
                hit = blk == first
                sel = jnp.where(hit & (blk < c_max), 1, sel)
                gate = jnp.where(hit, -jnp.inf, gate)
            sel_blocks.append(sel)

    def attend(c, carry):
        k0 = pl.multiple_of(c * KEY_CHUNK, KEY_CHUNK)
        kidx = k0 + krow
        causal = kidx <= qpos
        off = pl.multiple_of(jnp.clip(STRIP_ORIGIN - (q0 - k0), 0, STRIP_ORIGIN), 128)
        if mode == "dsa":
            k = key_s[pl.ds(k0, KEY_CHUNK), :]
            mask = ((k > thr) | ((k == thr) & (kidx <= tie_idx))) & causal
        kblk = k_ref[pl.ds(k0, KEY_CHUNK), :]

        def logits(g):
            return jnp.dot(kblk, qbd_s[:, g * cols:(g + 1) * cols], preferred_element_type=F32)

        s_next = logits(0)
        for g in range(N_KV):
            if mode != "dsa":
                picked = jnp.sum(jnp.where(blk == c, sel_blocks[g], 0), axis=0, keepdims=True)
                mask = jnp.where(c == c_max, causal.astype(I32), picked) > 0
            s_all = s_next
            if g + 1 < N_KV:
                s_next = logits(g + 1)
            probs, alphas = [], []
            for j in range(GROUP):
                h = g * GROUP + j
                lanes = slice(j * QT, (j + 1) * QT)
                s = s_all[:, lanes] + strip_ref[g, pl.ds(off, KEY_CHUNK), lanes]
                s = jnp.where(mask, s, NEG)
                m_old = m_s[pl.ds(h, 1), :]
                m_new = jnp.maximum(m_old, jnp.max(s, axis=0, keepdims=True))
                alpha = jnp.exp(m_old - m_new)
                p = jnp.exp(s - m_new)
                l_s[pl.ds(h, 1), :] = alpha * l_s[pl.ds(h, 1), :] + jnp.sum(p, axis=0, keepdims=True)
                m_s[pl.ds(h, 1), :] = m_new
                probs.append(p.astype(BF16))
                alphas.append(alpha)
            pv = jnp.dot(vt_ref[g * HEAD_DIM:(g + 1) * HEAD_DIM, pl.ds(k0, KEY_CHUNK)],
                         jnp.concatenate(probs, axis=1), preferred_element_type=F32)
            acc_s[g] = jnp.concatenate(alphas, axis=1) * acc_s[g] + pv
        return carry

    lax.fori_loop(0, n_chunks, attend, 0)
    for g in range(N_KV):
        for j in range(0, GROUP, 2):
            h = g * GROUP + j
            pair = [acc_s[g][:, (j + i) * QT:(j + i + 1) * QT] * (1.0 / l_s[pl.ds(h + i, 1), :]) for i in range(2)]
            o_ref[:, h * HEAD_DIM:(h + 2) * HEAD_DIM] = jnp.concatenate(pair, axis=0).T


def _sparse_attention_t(mode, qt, k, vt, strip_t, extra, k_sel=0):
    bsz, nq = qt.shape[:2]
    lp = k.shape[1]
    kv_dim = N_KV * HEAD_DIM
    cols = GROUP * QT
    per_q = lambda *tail: pl.BlockSpec((None, None) + tail, lambda b, i: (b, i) + (0,) * len(tail))
    per_b = lambda *tail: pl.BlockSpec((None,) + tail, lambda b, i: (b,) + (0,) * len(tail))
    in_specs = [per_q(kv_dim, cols), per_b(lp, kv_dim), per_b(kv_dim, lp),
                pl.BlockSpec((N_KV, STRIP_W, cols), lambda b, i: (0, 0, 0))]
    scratch = [pltpu.VMEM((kv_dim, N_KV * cols), BF16),
               pltpu.VMEM((N_HEADS, QT), F32),
               pltpu.VMEM((N_HEADS, QT), F32),
               pltpu.VMEM((N_KV, HEAD_DIM, cols), F32)]
    if mode == "dsa":
        qit, wit, ki = extra
        in_specs += [per_q(IDX_DIM, IDX_HEADS * QT), per_q(1, IDX_HEADS * QT), per_b(lp, IDX_DIM)]
        scratch.append(pltpu.VMEM((lp, QT), I32))
    else:
        (km,) = extra
        in_specs.append(per_b(N_KV, km.shape[2], HEAD_DIM))
    return pl.pallas_call(
        functools.partial(_attn_t_kernel, mode, lp, k_sel),
        grid=(bsz, nq),
        in_specs=in_specs,
        out_specs=per_q(QT, N_HEADS * HEAD_DIM),
        out_shape=jax.ShapeDtypeStruct((bsz, nq, QT, N_HEADS * HEAD_DIM), F32),
        scratch_shapes=scratch,
        compiler_params=_cparams(("parallel", "arbitrary")),
        name=mode + "_attention_t",
    )(qt, k, vt, strip_t, *extra)


def _top_rows(s, k, payload=None, rank=None):
    rid = lax.broadcasted_iota(I32, s.shape, 0) if rank is None else rank
    vals, ids = [], []
    for _ in range(k):
        mx = jnp.max(s, axis=0, keepdims=True)
        first = jnp.min(jnp.where(s == mx, rid, jnp.iinfo(jnp.int32).max), axis=0, keepdims=True)
        hit = rid == first
        vals.append(mx)
        if payload is None:
            ids.append(first)
        else:
            ids.append(jnp.max(jnp.where(hit, payload, -1), axis=0, keepdims=True))
        s = jnp.where(hit, -jnp.inf, s)
    return jnp.concatenate(vals, axis=0), jnp.concatenate(ids, axis=0)


def _top_rows_paired(s, k):
    half = s.shape[0] // 2
    lo, hi = s[:half], s[half:]
    rid = lax.broadcasted_iota(I32, lo.shape, 0)
    first_lo = lo >= hi
    top = jnp.where(first_lo, lo, hi)
    rest = jnp.where(first_lo, hi, lo)
    top_id = jnp.where(first_lo, rid, rid + half)
    rest_id = jnp.where(first_lo, rid + half, rid)
    vals, ids = [], []
    for _ in range(k):
        mx = jnp.max(top, axis=0, keepdims=True)
        first = jnp.min(jnp.where(top == mx, top_id, jnp.iinfo(jnp.int32).max), axis=0, keepdims=True)
        hit = top_id == first
        vals.append(mx)
        ids.append(first)
        top = jnp.where(hit, rest, top)
        top_id = jnp.where(hit, rest_id, top_id)
        rest = jnp.where(hit, -jnp.inf, rest)
    return jnp.concatenate(vals, axis=0), jnp.concatenate(ids, axis=0)


def _peer_route_kernel(x_ref, wqt_ref, keys_ref, row_ref, par_ref, g_ref):
    xb = x_ref[...].astype(BF16)
    qt = lax.dot_general(wqt_ref[...], xb, (((1,), (1,)), ((), ())),
                         preferred_element_type=F32)
    n_t = x_ref.shape[0]
    piece = lax.broadcasted_iota(I32, (11 * 8, n_t), 0) // 8
    within = lax.broadcasted_iota(I32, (11 * 8, n_t), 0) % 8
    by_b = (piece >= 2) & (piece < 10)
    ca = jnp.where(piece < 2, 0, jnp.where(by_b, within, 8 + within))
    cb = jnp.where(piece == 0, within, jnp.where(piece == 1, 8 + within, jnp.where(by_b, piece - 2, 0)))
    cand_ok = ((ca + 1) * (cb + 1) <= PEER_TOPK) & jnp.logical_not(by_b & (within == 0))
    cand_rank = ca * PEER_TOPK + cb

    def pieces(first, second, combine):
        out = [combine(first[0:1], second[0:8]), combine(first[0:1], second[8:16])]
        out += [combine(first[0:8], second[b:b + 1]) for b in range(8)]
        out.append(combine(first[8:16], second[0:1]))
        return jnp.concatenate(out, axis=0)

    e_rows, g_rows = [], []
    for h in range(PEER_HEADS):
        sv, si = [], []
        for c in range(2):
            hc = 2 * h + c
            qhc = qt[hc * PEER_HALF:(hc + 1) * PEER_HALF].astype(BF16)
            s = jnp.dot(keys_ref[hc], qhc, preferred_element_type=F32)
            v_, i_ = _top_rows_paired(s, PEER_TOPK)
            sv.append(v_)
            si.append(i_)
        cand = jnp.where(cand_ok, pieces(sv[0], sv[1], lambda u, w: u + w), -jnp.inf)
        cidx = pieces(si[0], si[1], lambda u, w: u * PEER_NKEYS + w)
        gv, ge = _top_rows(cand, PEER_TOPK, payload=cidx, rank=cand_rank)
        ex = jnp.exp(gv - gv[0:1])
        g_rows.append(ex / jnp.sum(ex, axis=0, keepdims=True))
        e_rows.append(ge)
    e_t = jnp.concatenate(e_rows, axis=0).T
    row_ref[...] = jnp.right_shift(e_t, 1) * 8
    par_ref[...] = e_t & 1
    g_ref[...] = jnp.concatenate(g_rows, axis=0).T


def _peer_route(x, wqt_bf16, keys_bf16, tb):
    m, d = x.shape
    out = pl.BlockSpec((tb, PEER_PAIRS), lambda i: (i, 0))
    return pl.pallas_call(
        _peer_route_kernel,
        grid=(m // tb,),
        in_specs=[pl.BlockSpec((tb, d), lambda i: (i, 0)),
                  pl.BlockSpec(wqt_bf16.shape, lambda i: (0, 0)),
                  pl.BlockSpec(keys_bf16.shape, lambda i: (0, 0, 0))],
        out_specs=[out, out, out],
        out_shape=[jax.ShapeDtypeStruct((m, PEER_PAIRS), I32),
                   jax.ShapeDtypeStruct((m, PEER_PAIRS), I32),
                   jax.ShapeDtypeStruct((m, PEER_PAIRS), F32)],
        compiler_params=_cparams(("parallel",)),
        name="peer_route",
    )(x, wqt_bf16, keys_bf16)


FOLD_PAIRS = 32
HI_MASK = 0xFFFF0000
UP_SLOTS = 4
TILE_ROWS = 8


def _pack_table(w):
    n, d = w.shape
    bits = lax.bitcast_convert_type(w.astype(BF16), jnp.uint16).astype(jnp.uint32)
    bits = bits.reshape(n // 2, 2, d // 128, 128)
    return (bits[:, 0] | (bits[:, 1] << 16)).astype(jnp.uint32).reshape(n // 2 * (d // 128), 128)


def _load_tile(tbl_ref, row_ref, idx):
    return tbl_ref[pl.ds(pl.multiple_of(row_ref[idx], TILE_ROWS), TILE_ROWS), :]


def _peer_up_kernel(tb, row_ref, par_ref, x_ref, g_ref, fold_ref, tbl_ref, c_ref, prod_s):
    diag = (lax.broadcasted_iota(I32, (PEER_PAIRS, PEER_PAIRS), 0)
            == lax.broadcasted_iota(I32, (PEER_PAIRS, PEER_PAIRS), 1))

    def gather(t, slot):
        xt = x_ref[t]
        rows_t = row_ref.at[pl.ds(t * PEER_PAIRS, PEER_PAIRS)]
        for p in range(0, PEER_PAIRS, 2):
            w0 = _load_tile(tbl_ref, rows_t, p)
            w1 = _load_tile(tbl_ref, rows_t, p + 1)
            even = [pltpu.bitcast(jnp.left_shift(w, jnp.uint32(16)), F32) * xt for w in (w0, w1)]
            odd = [pltpu.bitcast(w & jnp.uint32(HI_MASK), F32) * xt for w in (w0, w1)]
            rows = pl.ds(p * TILE_ROWS, 2 * TILE_ROWS)
            prod_s[slot, rows, 0:128] = jnp.concatenate(even, axis=0).astype(BF16)
            prod_s[slot, rows, 128:256] = jnp.concatenate(odd, axis=0).astype(BF16)

    def reduce(t, slot):
        depth = FOLD_PAIRS * TILE_ROWS
        parts = [jnp.dot(fold_ref[...], prod_s[slot, pl.ds(i * depth, depth), :], preferred_element_type=F32)
                 for i in range(PEER_PAIRS // FOLD_PAIRS)]
        part = jnp.concatenate(parts, axis=0)
        rows = []
        for half in range(2):
            col = jnp.sum(part[:, half * 128:(half + 1) * 128], axis=1, keepdims=True)
            rows.append(jnp.sum(jnp.where(diag, col, 0.0), axis=0, keepdims=True))
        a = jnp.where(par_ref[pl.ds(t, 1), :] == 1, rows[1], rows[0])
        act = 0.5 * a * (1.0 + lax.erf(a * (2.0 ** -0.5)))
        c_ref[pl.ds(t, 1), :] = g_ref[pl.ds(t, 1), :] * act

    prod_s[UP_SLOTS:2 * UP_SLOTS] = jnp.zeros((UP_SLOTS,) + prod_s.shape[1:], BF16)

    def step(k, carry):
        t0 = 2 * UP_SLOTS * k
        for j in range(UP_SLOTS):
            gather(t0 + j, j)
            reduce(jnp.maximum(t0 - UP_SLOTS + j, 0), UP_SLOTS + j)
        for j in range(UP_SLOTS):
            gather(t0 + UP_SLOTS + j, UP_SLOTS + j)
            reduce(t0 + j, j)
        return carry

    lax.fori_loop(0, tb // (2 * UP_SLOTS), step, 0)
    for j in range(UP_SLOTS):
        reduce(tb - UP_SLOTS + j, UP_SLOTS + j)


DOWN_GROUP = 8


def _peer_down_kernel(tb, row_ref, par_ref, c_ref, spread_ref, tbl_ref, f_ref, tile_s, coef_s):
    width = PEER_PAIRS * 2 * TILE_ROWS
    lane = lax.broadcasted_iota(I32, (TILE_ROWS, width), 1)
    own_row = ((lane & (2 * TILE_ROWS - 1)) >> 1) == lax.broadcasted_iota(I32, (TILE_ROWS, width), 0)
    lane_par = (lax.broadcasted_iota(I32, (DOWN_GROUP, width), 1) & 1).astype(F32)

    def gather(t, slot):
        rows_t = row_ref.at[pl.ds(t * PEER_PAIRS, PEER_PAIRS)]
        for p in range(PEER_PAIRS):
            tile_s[slot, pl.ds(p * TILE_ROWS, TILE_ROWS), :] = _load_tile(tbl_ref, rows_t, p)

    def apply(t, j, slot):
        tiles = pltpu.bitcast(tile_s[slot], BF16)
        lhs = [jnp.where(own_row, coef_s[pl.ds(part * DOWN_GROUP + j, 1), :], 0.0) for part in range(2)]
        out = jnp.dot(jnp.concatenate(lhs, axis=0).astype(BF16), tiles, preferred_element_type=F32)
        f_ref[t] = out[0:TILE_ROWS] + out[TILE_ROWS:2 * TILE_ROWS]

    def group(gi, carry):
        t0 = pl.multiple_of(gi * DOWN_GROUP, DOWN_GROUP)
        gather(t0, 0)
        c = c_ref[pl.ds(t0, DOWN_GROUP), :]
        c_hi = c.astype(BF16)
        c_lo = (c - c_hi.astype(F32)).astype(BF16)
        par = par_ref[pl.ds(t0, DOWN_GROUP), :].astype(BF16)
        rep = jnp.dot(jnp.concatenate([c_hi, c_lo, par], axis=0), spread_ref[...],
                      preferred_element_type=F32)
        keep = rep[2 * DOWN_GROUP:] == lane_par
        coef_s[0:DOWN_GROUP, :] = jnp.where(keep, rep[0:DOWN_GROUP], 0.0)
        coef_s[DOWN_GROUP:2 * DOWN_GROUP, :] = jnp.where(keep, rep[DOWN_GROUP:2 * DOWN_GROUP], 0.0)
        for j in range(DOWN_GROUP):
            if j + 1 < DOWN_GROUP:
                gather(t0 + j + 1, (j + 1) % 2)
            apply(t0 + j, j, j % 2)
        return carry

    lax.fori_loop(0, tb // DOWN_GROUP, group, 0)


def _peer_experts(x, row, par, g, u_tbl, v_tbl, tb):
    m, d = x.shape
    x3 = x.reshape(m, d // 128, 128)
    flat = row.reshape(m * PEER_PAIRS)
    smem = pl.BlockSpec((tb * PEER_PAIRS,), lambda i: (i,), memory_space=pltpu.SMEM)
    vrow = pl.BlockSpec((tb, PEER_PAIRS), lambda i: (i, 0))
    tile = pl.BlockSpec((tb, d // 128, 128), lambda i: (i, 0, 0))
    resident = pl.BlockSpec(memory_space=pltpu.VMEM)
    const = lambda a: pl.BlockSpec(a.shape, lambda i: (0, 0))
    fold = jnp.asarray(np.arange(FOLD_PAIRS * TILE_ROWS)[None, :] // TILE_ROWS == np.arange(FOLD_PAIRS)[:, None], BF16)
    width = PEER_PAIRS * 2 * TILE_ROWS
    spread = jnp.asarray(np.arange(width)[None, :] // (2 * TILE_ROWS) == np.arange(PEER_PAIRS)[:, None], BF16)
    c = pl.pallas_call(
        functools.partial(_peer_up_kernel, tb),
        grid=(m // tb,),
        in_specs=[smem, vrow, tile, vrow, const(fold), resident],
        out_specs=vrow,
        out_shape=jax.ShapeDtypeStruct((m, PEER_PAIRS), F32),
        scratch_shapes=[pltpu.VMEM((2 * UP_SLOTS, PEER_PAIRS * TILE_ROWS, 256), BF16)],
        compiler_params=_cparams(("arbitrary",)),
        name="peer_up",
    )(flat, par, x3, g, fold, u_tbl)
    f = pl.pallas_call(
        functools.partial(_peer_down_kernel, tb),
        grid=(m // tb,),
        in_specs=[smem, vrow, vrow, const(spread), resident],
        out_specs=tile,
        out_shape=jax.ShapeDtypeStruct((m, d // 128, 128), F32),
        scratch_shapes=[pltpu.VMEM((2, PEER_PAIRS * TILE_ROWS, 128), jnp.uint32),
                        pltpu.VMEM((2 * DOWN_GROUP, width), F32)],
        compiler_params=_cparams(("arbitrary",)),
        name="peer_down",
    )(flat, par, c, spread, v_tbl)
    return f.reshape(m, d)


def _stack_heads(a, qb, n_lead):
    bsz, q, n, hd = a.shape
    n_in = n // n_lead
    a = a.reshape(bsz, q // qb, qb, n_lead, n_in, hd)
    return a.transpose(0, 1, 3, 4, 2, 5).reshape(bsz, q // qb, n_lead, n_in * qb, hd)


def _unstack_heads(o, qb):
    bsz, nq = o.shape[:2]
    o = o.reshape(bsz, nq, N_KV, GROUP, qb, HEAD_DIM).transpose(0, 1, 4, 2, 3, 5)
    return o.reshape(bsz * nq * qb, N_HEADS * HEAD_DIM)


def _run_trunk(x, p, past, prm, qb, tm, tb_route, tb_exp):
    bsz, q_len, d = x.shape
    depth = prm["w_o"].shape[0]
    m = bsz * q_len
    tm, tb_route, tb_exp = min(tm, m), min(tb_route, m), min(tb_exp, m)
    alpha = (2 * depth) ** 0.25
    attn_dim = N_HEADS * HEAD_DIM
    kv_dim = N_KV * HEAD_DIM
    if past is None:
        q_pos0 = 0
        n_pages = q_len // PAGE
        table = jnp.arange(bsz * n_pages, dtype=I32).reshape(bsz, n_pages)
    else:
        cache_k, cache_v, cache_ki, table = past
        q_pos0 = table.shape[1] * PAGE
    on_lanes = past is None and q_len % QT == 0
    if on_lanes:
        qb = QT
    nq = q_len // qb
    strip = _bias_strip(prm["rel_bias"], qb).reshape(N_KV, GROUP, qb, STRIP_W)
    if on_lanes:
        strip = strip.transpose(0, 3, 1, 2).reshape(N_KV, STRIP_W, GROUP * qb)
    else:
        strip = strip.reshape(N_KV, GROUP * qb, STRIP_W)
    xf = x.reshape(m, d)
    ks, vs, kis = [], [], []

    def relayout(new_rows, cache, layer, want):
        w = new_rows.shape[2]
        if past is None:
            return _relayout(new_rows.reshape(bsz * n_pages, PAGE, w), table, None, False, want)
        n_pool = cache.shape[1]
        pool = cache.reshape((cache.shape[0] * n_pool,) + cache.shape[2:])
        pool_t = jnp.moveaxis(pool, 1, -1).reshape(pool.shape[0], w, PAGE)
        pad = jnp.zeros((bsz, STEP_KEYS - q_len, w), F32)
        return _relayout(pool_t, table + layer * n_pool, jnp.concatenate([new_rows, pad], axis=1), True, want)

    for l in range(depth):
        dsa = l % 2 == 0
        w_in = prm["w_in_dsa"][l // 2] if dsa else prm["w_in_moba"][l // 2]
        if on_lanes:
            proj = _in_proj_lanes(xf, w_in, dsa, tm)
            q_in = proj[0].reshape(bsz, nq, kv_dim, GROUP * qb)
            k = proj[1].reshape(bsz, q_len, kv_dim)
            v = proj[2].reshape(bsz, q_len, kv_dim)
        else:
            n_in = w_in.shape[1]
            n_pad = -(-n_in // 128) * 128
            proj = _matmul(xf, jnp.pad(w_in, ((0, 0), (0, n_pad - n_in))).astype(BF16), tm)
            qh = proj[:, :attn_dim].reshape(bsz, q_len, N_HEADS, HEAD_DIM)
            k = proj[:, attn_dim:attn_dim + kv_dim].reshape(bsz, q_len, kv_dim)
            v = proj[:, attn_dim + kv_dim:attn_dim + 2 * kv_dim].reshape(bsz, q_len, kv_dim)
        ks.append(k.reshape(bsz, q_len, N_KV, HEAD_DIM))
        vs.append(v.reshape(bsz, q_len, N_KV, HEAD_DIM))
        k_lay = relayout(k, None if past is None else cache_k, l,
                         (("x",) if on_lanes else ("xt",)) + (() if dsa else ("sums",)))
        v_lay = relayout(v, None if past is None else cache_v, l, ("xt",) if on_lanes else ("x",))
        if on_lanes:
            kr, vt = k_lay["x"], v_lay["xt"]
        else:
            kt = k_lay["xt"].reshape(bsz, N_KV, HEAD_DIM, k_lay["xt"].shape[2])
            vr = v_lay["x"]
            q_in = _stack_heads(qh, qb, N_KV)
        if dsa:
            if on_lanes:
                ki = proj[3].reshape(bsz, q_len, IDX_DIM)
            else:
                off = attn_dim + 2 * kv_dim
                qi = proj[:, off:off + IDX_HEADS * IDX_DIM].reshape(bsz, q_len, IDX_HEADS, IDX_DIM)
                off += IDX_HEADS * IDX_DIM
                ki = proj[:, off:off + IDX_DIM].reshape(bsz, q_len, IDX_DIM)
                wi = proj[:, off + IDX_DIM:off + IDX_DIM + IDX_HEADS].reshape(bsz, q_len, IDX_HEADS, 1)
            kis.append(ki)
            ki_lay = relayout(ki, None if past is None else cache_ki, l // 2, ("x",) if on_lanes else ("xt",))
            k_sel = min(IDX_TOPK, (q_pos0 + q_len) // 4)
            if on_lanes:
                qi_t = proj[4].reshape(bsz, nq, IDX_DIM, IDX_HEADS * qb)
                wi_t = proj[5].reshape(bsz, nq, 1, IDX_HEADS * qb)
                o = _sparse_attention_t("dsa", q_in, kr, vt, strip, (qi_t, wi_t, ki_lay["x"]), k_sel)
            else:
                qi_st = _stack_heads(qi, qb, 1)[:, :, 0]
                wi_st = _stack_heads(wi, qb, 1)[:, :, 0]
                o = _sparse_attention("dsa", q_in, kt, vr, strip, (qi_st, wi_st, ki_lay["xt"]), qb, q_pos0, k_sel)
        else:
            ksum = k_lay["sums"]
            nb = ksum.shape[1]
            kmean = (ksum * (1.0 / MOBA_BLOCK)).reshape(bsz, nb, N_KV, HEAD_DIM)
            if on_lanes:
                km = jnp.pad(kmean.transpose(0, 2, 1, 3), ((0, 0), (0, 0), (0, -nb % 8), (0, 0)))
                o = _sparse_attention_t("moba", q_in, kr, vt, strip, (km,))
            else:
                kmt = jnp.pad(kmean.transpose(0, 2, 3, 1), ((0, 0), (0, 0), (0, 0), (0, -nb % 128)))
                o = _sparse_attention("moba", q_in, kt, vr, strip, (kmt,), qb, q_pos0)
        if on_lanes:
            o = o.reshape(m, attn_dim)
        else:
            o = _unstack_heads(o, qb)
        xf = _attn_out(o, prm["w_o"][l].astype(BF16), xf, prm["ln_g"][l, 0][None], prm["ln_b"][l, 0][None],
                       alpha, tm)
        row, par, g = _peer_route(xf, prm["peer_wq"][l].T.astype(BF16),
                                 prm["peer_keys"][l].reshape(2 * PEER_HEADS, PEER_NKEYS, PEER_HALF).astype(BF16),
                                 tb_route)
        f = _peer_experts(xf, row, par, g, _pack_table(prm["peer_u"][l]), _pack_table(prm["peer_v"][l]), tb_exp)
        xf = _layer_tail(xf, f, prm["ln_g"][l, 1][None], prm["ln_b"][l, 1][None],
                         prm["gate_w"][l].astype(BF16), prm["gate_b"][l][None],
                         p[l].reshape(m, -1), prm["ple_w"][l].astype(BF16), alpha, tm)
    return xf.reshape(bsz, q_len, d), jnp.stack(ks), jnp.stack(vs), jnp.stack(kis)


def kernel(x_prompt, x_sample, cache_k, cache_v, cache_ki, page_table, p_prompt, p_sample, w_in_dsa, w_in_moba, w_o, rel_bias, ln_g, ln_b, peer_wq, peer_keys, peer_u, peer_v, gate_w, gate_b, ple_w):
    prm = {
        "w_in_dsa": w_in_dsa, "w_in_moba": w_in_moba, "w_o": w_o, "rel_bias": rel_bias,
        "ln_g": ln_g, "ln_b": ln_b, "peer_wq": peer_wq, "peer_keys": peer_keys,
        "peer_u": peer_u, "peer_v": peer_v, "gate_w": gate_w, "gate_b": gate_b, "ple_w": ple_w,
    }
    q_sample = x_sample.shape[1]
    y_prompt, k_prompt, v_prompt, ki_prompt = _run_trunk(
        x_prompt, p_prompt, None, prm, qb=128, tm=512, tb_route=256, tb_exp=128)
    y_sample, k_sample, v_sample, ki_sample = _run_trunk(
        x_sample, p_sample, (cache_k, cache_v, cache_ki, page_table), prm,
        qb=q_sample, tm=256, tb_route=256, tb_exp=64)
    return (y_prompt, y_sample, k_prompt, v_prompt, ki_prompt, k_sample, v_sample, ki_sample)
```

```python
import functools
import math

import numpy as np
import jax
import jax.numpy as jnp
from jax import lax
from jax.experimental import pallas as pl
from jax.experimental.pallas import tpu as pltpu

F32 = jnp.float32
BF16 = jnp.bfloat16
I32 = jnp.int32

N_HEADS = 16
HEAD_DIM = 64
N_KV = 4
GROUP = 4
IDX_HEADS = 8
IDX_DIM = 64
IDX_TOPK = 256
MOBA_BLOCK = 256
MOBA_TOPK = 3
REL_BUCKETS = 32
REL_MAX_DIST = 128
PEER_HEADS = 8
PEER_NKEYS = 128
PEER_HALF = 128
PEER_TOPK = 16
PEER_PAIRS = PEER_HEADS * PEER_TOPK
LN_EPS = 1e-5
PAGE = 128

KEY_CHUNK = 256
ROWS_STEP = 2 * KEY_CHUNK
NEG = -1e30
INT_MIN = -2147483648
VMEM_LIMIT = 56 * 1024 * 1024


def _cparams(sem):
    return pltpu.CompilerParams(dimension_semantics=sem, vmem_limit_bytes=VMEM_LIMIT)


def _mm_kernel(a_ref, b_ref, o_ref):
    o_ref[...] = jnp.dot(a_ref[...].astype(BF16), b_ref[...], preferred_element_type=F32)


def _matmul(a, b_bf16, tm):
    m, k = a.shape
    n = b_bf16.shape[1]
    return pl.pallas_call(
        _mm_kernel,
        grid=(m // tm,),
        in_specs=[pl.BlockSpec((tm, k), lambda i: (i, 0)),
                  pl.BlockSpec((k, n), lambda i: (0, 0))],
        out_specs=pl.BlockSpec((tm, n), lambda i: (i, 0)),
        out_shape=jax.ShapeDtypeStruct((m, n), F32),
        compiler_params=_cparams(("parallel",)),
        name="in_proj",
    )(a, b_bf16)


def _in_proj_lanes_kernel(has_idx, *refs):
    if has_idx:
        x_ref, wqt_ref, wrow_ref, wqit_ref, wit_ref, qt_ref, k_ref, v_ref, ki_ref, qit_ref, wi_ref = refs
    else:
        x_ref, wqt_ref, wrow_ref, qt_ref, k_ref, v_ref = refs
    kv_dim = N_KV * HEAD_DIM
    xb = x_ref[...].astype(BF16)
    n_blk = xb.shape[0] // QT
    nt = (((1,), (1,)), ((), ()))
    rows = jnp.dot(xb, wrow_ref[...], preferred_element_type=F32)
    k_ref[...] = rows[:, 0:kv_dim]
    v_ref[...] = rows[:, kv_dim:2 * kv_dim]
    q_t = lax.dot_general(wqt_ref[...], xb, nt, preferred_element_type=F32)
    for i in range(n_blk):
        for g in range(N_KV):
            for j in range(GROUP):
                h = g * GROUP + j
                qt_ref[i, g * HEAD_DIM:(g + 1) * HEAD_DIM, j * QT:(j + 1) * QT] = (
                    q_t[h * HEAD_DIM:(h + 1) * HEAD_DIM, i * QT:(i + 1) * QT])
    if has_idx:
        ki_ref[...] = rows[:, 2 * kv_dim:2 * kv_dim + IDX_DIM]
        qi_t = lax.dot_general(wqit_ref[...], xb, nt, preferred_element_type=F32)
        wi_t = lax.dot_general(wit_ref[...], xb, nt, preferred_element_type=F32)
        for i in range(n_blk):
            for h in range(IDX_HEADS):
                qit_ref[i, :, h * QT:(h + 1) * QT] = qi_t[h * IDX_DIM:(h + 1) * IDX_DIM, i * QT:(i + 1) * QT]
                wi_ref[i, :, h * QT:(h + 1) * QT] = wi_t[h:h + 1, i * QT:(i + 1) * QT]


def _in_proj_lanes(x, w_in, has_idx, tm):
    m, d = x.shape
    attn_dim = N_HEADS * HEAD_DIM
    kv_dim = N_KV * HEAD_DIM
    n_blk = tm // QT
    w_rows = w_in[:, attn_dim:attn_dim + 2 * kv_dim]
    off = attn_dim + 2 * kv_dim
    if has_idx:
        w_ki = w_in[:, off + IDX_HEADS * IDX_DIM:off + IDX_HEADS * IDX_DIM + IDX_DIM]
        w_rows = jnp.concatenate([w_rows, w_ki, jnp.zeros((d, 128 - IDX_DIM), F32)], axis=1)
    args = [x, w_in[:, :attn_dim].T.astype(BF16), w_rows.astype(BF16)]
    whole = lambda a: pl.BlockSpec(a.shape, lambda i: (0, 0))
    in_specs = [pl.BlockSpec((tm, d), lambda i: (i, 0)), whole(args[1]), whole(args[2])]
    blk3 = lambda r, c: pl.BlockSpec((n_blk, r, c), lambda i: (i, 0, 0))
    rowb = lambda c: pl.BlockSpec((tm, c), lambda i: (i, 0))
    out_specs = [blk3(kv_dim, GROUP * QT), rowb(kv_dim), rowb(kv_dim)]
    out_shape = [jax.ShapeDtypeStruct((m // QT, kv_dim, GROUP * QT), F32),
                 jax.ShapeDtypeStruct((m, kv_dim), F32), jax.ShapeDtypeStruct((m, kv_dim), F32)]
    if has_idx:
        w_wi = w_in[:, off + IDX_HEADS * IDX_DIM + IDX_DIM:off + IDX_HEADS * IDX_DIM + IDX_DIM + IDX_HEADS]
        args += [w_in[:, off:off + IDX_HEADS * IDX_DIM].T.astype(BF16), w_wi.T.astype(BF16)]
        in_specs += [whole(args[3]), whole(args[4])]
        out_specs += [rowb(IDX_DIM), blk3(IDX_DIM, IDX_HEADS * QT), blk3(1, IDX_HEADS * QT)]
        out_shape += [jax.ShapeDtypeStruct((m, IDX_DIM), F32),
                      jax.ShapeDtypeStruct((m // QT, IDX_DIM, IDX_HEADS * QT), F32),
                      jax.ShapeDtypeStruct((m // QT, 1, IDX_HEADS * QT), F32)]
    return pl.pallas_call(
        functools.partial(_in_proj_lanes_kernel, has_idx),
        grid=(m // tm,),
        in_specs=in_specs,
        out_specs=out_specs,
        out_shape=out_shape,
        compiler_params=_cparams(("parallel",)),
        name="in_proj_lanes",
    )(*args)


def _layer_norm(y, g, b):
    mu = jnp.mean(y, axis=-1, keepdims=True)
    yc = y - mu
    var = jnp.mean(yc * yc, axis=-1, keepdims=True)
    return yc * lax.rsqrt(var + LN_EPS) * g + b


def _attn_out_kernel(alpha, o_ref, w_ref, x_ref, g_ref, b_ref, y_ref):
    y = alpha * x_ref[...] + jnp.dot(o_ref[...].astype(BF16), w_ref[...],
                                     preferred_element_type=F32)
    y_ref[...] = _layer_norm(y, g_ref[...], b_ref[...])


def _attn_out(o, w_bf16, x, g, b, alpha, tm):
    m, d = x.shape
    row = pl.BlockSpec((tm, d), lambda i: (i, 0))
    vec = pl.BlockSpec((1, d), lambda i: (0, 0))
    return pl.pallas_call(
        functools.partial(_attn_out_kernel, alpha),
        grid=(m // tm,),
        in_specs=[row, pl.BlockSpec((d, d), lambda i: (0, 0)), row, vec, vec],
        out_specs=row,
        out_shape=jax.ShapeDtypeStruct((m, d), F32),
        compiler_params=_cparams(("parallel",)),
        name="attn_out_ln",
    )(o, w_bf16, x, g, b)


def _layer_tail_kernel(alpha, x_ref, f_ref, g_ref, b_ref, gw_ref, gb_ref, p_ref, pw_ref, y_ref):
    x2 = _layer_norm(alpha * x_ref[...] + f_ref[...], g_ref[...], b_ref[...])
    z = jnp.dot(x2.astype(BF16), gw_ref[...], preferred_element_type=F32) + gb_ref[...]
    gate = 1.0 / (1.0 + jnp.exp(-z))
    e = jnp.dot(p_ref[...].astype(BF16), pw_ref[...], preferred_element_type=F32)
    y_ref[...] = x2 + gate * e


def _layer_tail(x, f, g, b, gw_bf16, gb, p, pw_bf16, alpha, tm):
    m, d = x.shape
    pd = p.shape[1]
    row = pl.BlockSpec((tm, d), lambda i: (i, 0))
    vec = pl.BlockSpec((1, d), lambda i: (0, 0))
    return pl.pallas_call(
        functools.partial(_layer_tail_kernel, alpha),
        grid=(m // tm,),
        in_specs=[row, row, vec, vec, pl.BlockSpec((d, d), lambda i: (0, 0)), vec,
                  pl.BlockSpec((tm, pd), lambda i: (i, 0)),
                  pl.BlockSpec((pd, d), lambda i: (0, 0))],
        out_specs=row,
        out_shape=jax.ShapeDtypeStruct((m, d), F32),
        compiler_params=_cparams(("parallel",)),
        name="layer_tail",
    )(x, f, g, b, gw_bf16, gb, p, pw_bf16)


STEP_PAGES = 8
STEP_KEYS = STEP_PAGES * PAGE
STEP_CHUNKS = STEP_KEYS // KEY_CHUNK


def _relayout_kernel(n_pool_steps, has_new, pages_t, want, *refs):
    page_refs = refs[1:1 + STEP_PAGES]
    new_ref = refs[1 + STEP_PAGES] if has_new else None
    outs = dict(zip(want, refs[len(refs) - len(want):]))
    j = pl.program_id(1)
    need_rows = "x" in want or "sums" in want

    def emit(rows, cols):
        if "xt" in want:
            outs["xt"][...] = (rows.T if cols is None else cols).astype(BF16)
        if need_rows and rows is None:
            rows = cols.T
        if "x" in want:
            outs["x"][...] = rows.astype(BF16)
        if "sums" in want:
            for i in range(STEP_CHUNKS):
                outs["sums"][pl.ds(i, 1), :] = jnp.sum(rows[i * KEY_CHUNK:(i + 1) * KEY_CHUNK], axis=0, keepdims=True)

    def from_pages():
        if pages_t:
            emit(None, jnp.concatenate([r[...] for r in page_refs], axis=1))
        else:
            emit(jnp.concatenate([r[...] for r in page_refs], axis=0), None)

    if has_new:
        pl.when(j < n_pool_steps)(from_pages)

        @pl.when(j >= n_pool_steps)
        def _():
            emit(new_ref[...], None)
    else:
        from_pages()


def _relayout(pool, table, new, pages_t, want):
    bsz, n_pages = table.shape
    w = pool.shape[1] if pages_t else pool.shape[2]
    n_pool_steps = n_pages // STEP_PAGES
    n_steps = n_pool_steps + (1 if new is not None else 0)
    lp = n_steps * STEP_KEYS
    last = n_pool_steps - 1

    def page_map(i):
        return lambda b, j, tbl: (tbl[b, STEP_PAGES * jnp.minimum(j, last) + i], 0, 0)

    page_block = (None, w, PAGE) if pages_t else (None, PAGE, w)
    in_specs = [pl.BlockSpec(page_block, page_map(i)) for i in range(STEP_PAGES)]
    args = [pool] * STEP_PAGES
    if new is not None:
        in_specs.append(pl.BlockSpec((None, STEP_KEYS, w), lambda b, j, tbl: (b, 0, 0)))
        args.append(new)
    specs = {"xt": (pl.BlockSpec((None, w, STEP_KEYS), lambda b, j, tbl: (b, 0, j)),
                    jax.ShapeDtypeStruct((bsz, w, lp), BF16)),
             "x": (pl.BlockSpec((None, STEP_KEYS, w), lambda b, j, tbl: (b, j, 0)),
                   jax.ShapeDtypeStruct((bsz, lp, w), BF16)),
             "sums": (pl.BlockSpec((None, None, STEP_CHUNKS, w), lambda b, j, tbl: (b, j, 0, 0)),
                      jax.ShapeDtypeStruct((bsz, n_steps, STEP_CHUNKS, w), F32))}
    outs = pl.pallas_call(
        functools.partial(_relayout_kernel, n_pool_steps, new is not None, pages_t, want),
        grid_spec=pltpu.PrefetchScalarGridSpec(
            num_scalar_prefetch=1,
            grid=(bsz, n_steps),
            in_specs=in_specs,
            out_specs=[specs[name][0] for name in want]),
        out_shape=[specs[name][1] for name in want],
        compiler_params=_cparams(("parallel", "arbitrary")),
        name="kv_relayout",
    )(table, *args)
    outs = dict(zip(want, outs))
    if "sums" in outs:
        outs["sums"] = outs["sums"].reshape(bsz, n_steps * STEP_CHUNKS, w)
    return outs


def _bucket_table():
    n = np.arange(REL_MAX_DIST + 1)
    exact = REL_BUCKETS // 2
    nf = np.maximum(n, 1).astype(np.float32)
    large = exact + (np.log(nf / np.float32(exact)) / np.float32(math.log(REL_MAX_DIST / exact))
                     * np.float32(REL_BUCKETS - exact)).astype(np.int32)
    return np.where(n < exact, n, np.minimum(large, REL_BUCKETS - 1)).astype(np.int32)


def _bias_strip_kernel(bk_ref, rb_ref, o_ref):
    bk = bk_ref[...]
    for h in range(N_HEADS):
        acc = jnp.zeros(bk.shape, F32)
        for bkt in range(REL_BUCKETS):
            acc = jnp.where(bk == bkt, rb_ref[bkt, h], acc)
        o_ref[h] = acc


def _strip_geometry(step):
    origin = -(-(step + REL_MAX_DIST - 1) // 128) * 128
    return origin, origin + step


def _strip_offset(step, delta):
    origin, _ = _strip_geometry(step)
    return pl.multiple_of(jnp.clip(origin - delta, 0, origin), 128)


def _bias_strip(rel_bias, qb, step):
    origin, width = _strip_geometry(step)
    qi = np.arange(qb)[:, None]
    z = np.arange(width)[None, :]
    dist = np.clip(qi - z + origin, 0, REL_MAX_DIST)
    bk = jnp.asarray(_bucket_table()[dist])
    out = pl.pallas_call(
        _bias_strip_kernel,
        in_specs=[pl.BlockSpec(memory_space=pltpu.VMEM), pl.BlockSpec(memory_space=pltpu.SMEM)],
        out_specs=pl.BlockSpec(memory_space=pltpu.VMEM),
        out_shape=jax.ShapeDtypeStruct((N_HEADS, qb, width), F32),
        name="bias_strip",
    )(bk, rel_bias)
    return out


def _attn_kernel(mode, qb, lp, q_pos0, k_sel, *refs):
    if mode == "dsa":
        (q_ref, kt_ref, v_ref, strip_ref, qi_ref, wi_ref, kit_ref, o_ref,
         qs_s, m_s, l_s, acc_s, key_s) = refs
    else:
        (q_ref, kt_ref, v_ref, strip_ref, kmt_ref, o_ref,
         qs_s, m_s, l_s, acc_s) = refs
    rows = GROUP * qb
    i = pl.program_id(1)
    q0 = q_pos0 + i * qb
    c_max = (q0 + qb - 1) // KEY_CHUNK
    n_chunks = c_max + 1
    qpos = q0 + lax.broadcasted_iota(I32, (qb, 1), 0)
    lane = lax.broadcasted_iota(I32, (1, KEY_CHUNK), 1)
    scale = HEAD_DIM ** -0.5

    for g in range(N_KV):
        qs_s[g] = (q_ref[g] * scale).astype(BF16)
    m_s[...] = jnp.full(m_s.shape, NEG, F32)
    l_s[...] = jnp.zeros(l_s.shape, F32)
    acc_s[...] = jnp.zeros(acc_s.shape, F32)

    if mode == "dsa":
        qi = qi_ref[...].astype(BF16)
        wi = wi_ref[...]

        def score_chunk(c, carry):
            k0 = pl.multiple_of(c * KEY_CHUNK, KEY_CHUNK)
            s = jnp.dot(qi, kit_ref[:, pl.ds(k0, KEY_CHUNK)], preferred_element_type=F32)
            s = jnp.maximum(s, 0.0) * wi
            sc = s[0:qb]
            for h in range(1, IDX_HEADS):
                sc = sc + s[h * qb:(h + 1) * qb]
            sc = jnp.where(k0 + lane <= qpos, sc, -jnp.inf)
            bits = pltpu.bitcast(sc, I32)
            bits = jnp.where(bits == INT_MIN, 0, bits)
            key_s[:, pl.ds(k0, KEY_CHUNK)] = jnp.where(bits < 0, bits ^ 0x7FFFFFFF, bits)
            return carry

        scan = STEP_KEYS if qb <= 32 else ROWS_STEP
        n_scan = (n_chunks * KEY_CHUNK + scan - 1) // scan
        lax.fori_loop(0, n_scan * (scan // KEY_CHUNK), score_chunk, 0)
        scan_lane = lax.broadcasted_iota(I32, (1, scan), 1)

        def count(pred):
            def body(c, acc):
                k0 = pl.multiple_of(c * scan, scan)
                return acc + pred(key_s[:, pl.ds(k0, scan)], k0 + scan_lane).astype(I32)
            acc = lax.fori_loop(0, n_scan, body, jnp.zeros((qb, scan), I32))
            return jnp.sum(acc, axis=1, keepdims=True)

        def bit_step(bi, thr):
            cand = thr + jnp.left_shift(jnp.int32(1), 31 - bi)
            cnt = count(lambda k, kidx: k >= cand)
            return jnp.where(cnt >= k_sel, cand, thr)

        thr = lax.fori_loop(0, 32, bit_step, jnp.full((qb, 1), INT_MIN, I32))
        n_gt = count(lambda k, kidx: k > thr)
        n_ge = count(lambda k, kidx: k >= thr)
        need = k_sel - n_gt

        def tie_limit():
            n_bits = max(1, (lp - 1).bit_length())

            def idx_step(bi, lo):
                cand = lo + jnp.left_shift(jnp.int32(1), n_bits - 1 - bi)
                cnt = count(lambda k, kidx: (k == thr) & (kidx <= cand))
                return jnp.where(cnt < need, cand, lo)

            lo = lax.fori_loop(0, n_bits, idx_step, jnp.full((qb, 1), -1, I32))
            return lo + 1

        tie_idx = lax.cond(jnp.max(n_ge - n_gt - need) > 0, tie_limit,
                           lambda: jnp.full((qb, 1), lp, I32))
    else:
        blk = lax.broadcasted_iota(I32, (qb, kmt_ref.shape[2]), 1)
        sel_blocks = []
        for g in range(N_KV):
            qg = q_ref[g]
            qsum = qg[0:qb]
            for j in range(1, GROUP):
                qsum = qsum + qg[j * qb:(j + 1) * qb]
            gate = jnp.dot(qsum, kmt_ref[g], preferred_element_type=F32,
                           precision=lax.Precision.HIGHEST)
            gate = jnp.where(blk < c_max, gate, -jnp.inf)
            sel = jnp.zeros(blk.shape, jnp.bool_)
            for _ in range(MOBA_TOPK):
                mx = jnp.max(gate, axis=1, keepdims=True)
                first = jnp.min(jnp.where(gate == mx, blk, blk.shape[1]), axis=1, keepdims=True)
                hit = blk == first
                sel = sel | (hit & (blk < c_max))
                gate = jnp.where(hit, -jnp.inf, gate)
            sel_blocks.append(sel.astype(I32))

    step_lane = lax.broadcasted_iota(I32, (1, ROWS_STEP), 1)

    def attend(c, carry):
        k0 = pl.multiple_of(c * ROWS_STEP, ROWS_STEP)
        kidx = k0 + step_lane
        causal = kidx <= qpos
        off = _strip_offset(ROWS_STEP, q0 - k0)
        if mode == "dsa":
            k = key_s[:, pl.ds(k0, ROWS_STEP)]
            mask1 = ((k > thr) | ((k == thr) & (kidx <= tie_idx))) & causal
            mask = jnp.concatenate([mask1] * GROUP, axis=0)

        logits = [jnp.dot(qs_s[g], kt_ref[g, :, pl.ds(k0, ROWS_STEP)], preferred_element_type=F32)
                  for g in range(N_KV)]
        probs, alphas = [], []
        for g in range(N_KV):
            if mode != "dsa":
                parts = []
                for i in range(ROWS_STEP // KEY_CHUNK):
                    cb = c * (ROWS_STEP // KEY_CHUNK) + i
                    picked = jnp.sum(jnp.where(blk == cb, sel_blocks[g], 0), axis=1, keepdims=True)
                    own = causal[:, i * KEY_CHUNK:(i + 1) * KEY_CHUNK].astype(I32)
                    parts.append(jnp.where(cb == c_max, own, picked))
                mask1 = jnp.concatenate(parts, axis=1) > 0
                mask = jnp.concatenate([mask1] * GROUP, axis=0)
            s = jnp.where(mask, logits[g] + strip_ref[g, :, pl.ds(off, ROWS_STEP)], NEG)
            m_old = m_s[g]
            m_new = jnp.maximum(m_old, jnp.max(s, axis=1, keepdims=True))
            alpha = jnp.exp(m_old - m_new)
            p = jnp.exp(s - m_new)
            l_s[g] = alpha * l_s[g] + jnp.sum(p, axis=1, keepdims=True)
            m_s[g] = m_new
            probs.append(p.astype(BF16))
            alphas.append(alpha)
        vblk = v_ref[pl.ds(k0, ROWS_STEP), :]
        for g in range(N_KV):
            acc_s[g] = alphas[g] * acc_s[g] + jnp.dot(probs[g], vblk, preferred_element_type=F32)
        return carry

    lax.fori_loop(0, (n_chunks * KEY_CHUNK + ROWS_STEP - 1) // ROWS_STEP, attend, 0)
    for g in range(N_KV):
        o_ref[g] = acc_s[g][:, g * HEAD_DIM:(g + 1) * HEAD_DIM] / l_s[g]


def _sparse_attention(mode, q_st, kt, v, strip, extra, qb, q_pos0, k_sel=0):
    bsz, nq = q_st.shape[:2]
    lp = v.shape[1]
    rows = GROUP * qb
    assert q_pos0 % ROWS_STEP == 0 and (nq == 1 or qb % ROWS_STEP == 0) and lp % ROWS_STEP == 0
    per_q = lambda *tail: pl.BlockSpec((None, None) + tail, lambda b, i: (b, i) + (0,) * len(tail))
    per_b = lambda *tail: pl.BlockSpec((None,) + tail, lambda b, i: (b,) + (0,) * len(tail))
    in_specs = [per_q(N_KV, rows, HEAD_DIM), per_b(N_KV, HEAD_DIM, lp), per_b(lp, N_KV * HEAD_DIM),
                pl.BlockSpec(strip.shape, lambda b, i: (0, 0, 0))]
    scratch = [pltpu.VMEM((N_KV, rows, HEAD_DIM), BF16),
               pltpu.VMEM((N_KV, rows, 1), F32),
               pltpu.VMEM((N_KV, rows, 1), F32),
               pltpu.VMEM((N_KV, rows, N_KV * HEAD_DIM), F32)]
    if mode == "dsa":
        qi_st, wi_st, kit = extra
        in_specs += [per_q(IDX_HEADS * qb, IDX_DIM), per_q(IDX_HEADS * qb, 1), per_b(IDX_DIM, lp)]
        scratch.append(pltpu.VMEM((qb, lp), I32))
    else:
        (kmt,) = extra
        in_specs.append(per_b(N_KV, HEAD_DIM, kmt.shape[3]))
    return pl.pallas_call(
        functools.partial(_attn_kernel, mode, qb, lp, q_pos0, k_sel),
        grid=(bsz, nq),
        in_specs=in_specs,
        out_specs=per_q(N_KV, rows, HEAD_DIM),
        out_shape=jax.ShapeDtypeStruct((bsz, nq, N_KV, rows, HEAD_DIM), F32),
        scratch_shapes=scratch,
        compiler_params=_cparams(("parallel", "arbitrary")),
        name=mode + "_attention",
    )(q_st, kt, v, strip, *extra)


QT = 128


def _attn_t_kernel(mode, lp, k_sel, *refs):
    if mode == "dsa":
        (qt_ref, k_ref, vt_ref, strip_ref, qit_ref, wit_ref, ki_ref, o_ref,
         qbd_s, m_s, l_s, acc_s, key_s) = refs
    else:
        (qt_ref, k_ref, vt_ref, strip_ref, km_ref, o_ref,
         qbd_s, m_s, l_s, acc_s) = refs
    kv_dim = N_KV * HEAD_DIM
    cols = GROUP * QT
    i = pl.program_id(1)
    q0 = i * QT
    c_max = (q0 + QT - 1) // KEY_CHUNK
    n_chunks = c_max + 1
    qpos = q0 + lax.broadcasted_iota(I32, (1, QT), 1)
    krow = lax.broadcasted_iota(I32, (KEY_CHUNK, 1), 0)
    scale = HEAD_DIM ** -0.5

    qbd_s[...] = jnp.zeros(qbd_s.shape, BF16)
    for g in range(N_KV):
        qbd_s[g * HEAD_DIM:(g + 1) * HEAD_DIM, g * cols:(g + 1) * cols] = (
            qt_ref[g * HEAD_DIM:(g + 1) * HEAD_DIM, :] * scale).astype(BF16)
    m_s[...] = jnp.full(m_s.shape, NEG, F32)
    l_s[...] = jnp.zeros(l_s.shape, F32)
    acc_s[...] = jnp.zeros(acc_s.shape, F32)

    if mode == "dsa":
        qit = qit_ref[...].astype(BF16)
        wit = wit_ref[...]

        def score_chunk(c, carry):
            k0 = pl.multiple_of(c * KEY_CHUNK, KEY_CHUNK)
            kib = ki_ref[pl.ds(k0, KEY_CHUNK), :]
            sc = jnp.zeros((KEY_CHUNK, QT), F32)
            for hp in range(IDX_HEADS // 2):
                lanes = slice(2 * hp * QT, 2 * (hp + 1) * QT)
                s = jnp.maximum(jnp.dot(kib, qit[:, lanes], preferred_element_type=F32), 0.0) * wit[:, lanes]
                sc = sc + s[:, :QT] + s[:, QT:]
            sc = jnp.where(k0 + krow <= qpos, sc, -jnp.inf)
            bits = pltpu.bitcast(sc, I32)
            bits = jnp.where(bits == INT_MIN, 0, bits)
            key_s[pl.ds(k0, KEY_CHUNK), :] = jnp.where(bits < 0, bits ^ 0x7FFFFFFF, bits)
            return carry

        lax.fori_loop(0, n_chunks, score_chunk, 0)

        def count(pred):
            def body(c, acc):
                k0 = pl.multiple_of(c * KEY_CHUNK, KEY_CHUNK)
                hit = pred(key_s[pl.ds(k0, KEY_CHUNK), :], k0).astype(I32)
                return acc + jnp.sum(hit.reshape(KEY_CHUNK // 8, 8, QT), axis=0)
            acc = lax.fori_loop(0, n_chunks, body, jnp.zeros((8, QT), I32))
            return jnp.sum(acc, axis=0, keepdims=True)

        def bit_step(bi, thr):
            cand = thr + jnp.left_shift(jnp.int32(1), 31 - bi)
            cnt = count(lambda k, k0: k >= cand)
            return jnp.where(cnt >= k_sel, cand, thr)

        thr = lax.fori_loop(0, 32, bit_step, jnp.full((1, QT), INT_MIN, I32))
        n_gt = count(lambda k, k0: k > thr)
        n_ge = count(lambda k, k0: k >= thr)
        need = k_sel - n_gt

        def tie_limit():
            n_bits = max(1, (lp - 1).bit_length())

            def idx_step(bi, lo):
                cand = lo + jnp.left_shift(jnp.int32(1), n_bits - 1 - bi)
                cnt = count(lambda k, k0: (k == thr) & (k0 + krow <= cand))
                return jnp.where(cnt < need, cand, lo)

            lo = lax.fori_loop(0, n_bits, idx_step, jnp.full((1, QT), -1, I32))
            return lo + 1

        tie_idx = lax.cond(jnp.max(n_ge - n_gt - need) > 0, tie_limit,
                           lambda: jnp.full((1, QT), lp, I32))
    else:
        nbp = km_ref.shape[1]
        blk = lax.broadcasted_iota(I32, (nbp, QT), 0)
        sel_blocks = []
        for g in range(N_KV):
            qg = qt_ref[g * HEAD_DIM:(g + 1) * HEAD_DIM, :]
            qsum = qg[:, 0:QT]
            for j in range(1, GROUP):
                qsum = qsum + qg[:, j * QT:(j + 1) * QT]
            gate = jnp.dot(km_ref[g], qsum, preferred_element_type=F32, precision=lax.Precision.HIGHEST)
            gate = jnp.where(blk < c_max, gate, -jnp.inf)
            sel = jnp.zeros(blk.shape, I32)
            for _ in range(MOBA_TOPK):
                mx = jnp.max(gate, axis=0, keepdims=True)
                first = jnp.min(jnp.where(gate == mx, blk, nbp), axis=0, keepdims=True)
                hit = blk == first
                sel = jnp.where(hit & (blk < c_max), 1, sel)
                gate = jnp.where(hit, -jnp.inf, gate)
            sel_blocks.append(sel)

    def attend(c, carry):
        k0 = pl.multiple_of(c * KEY_CHUNK, KEY_CHUNK)
        kidx = k0 + krow
        causal = kidx <= qpos
        off = _strip_offset(KEY_CHUNK, q0 - k0)
        if mode == "dsa":
            k = key_s[pl.ds(k0, KEY_CHUNK), :]
            mask = ((k > thr) | ((k == thr) & (kidx <= tie_idx))) & causal
        kblk = k_ref[pl.ds(k0, KEY_CHUNK), :]

        def logits(g):
            return jnp.dot(kblk, qbd_s[:, g * cols:(g + 1) * cols], preferred_element_type=F32)

        s_next = logits(0)
        for g in range(N_KV):
            if mode != "dsa":
                picked = jnp.sum(jnp.where(blk == c, sel_blocks[g], 0), axis=0, keepdims=True)
                mask = jnp.where(c == c_max, causal.astype(I32), picked) > 0
            s_all = s_next
            if g + 1 < N_KV:
                s_next = logits(g + 1)
            probs, alphas = [], []
            for j in range(GROUP):
                h = g * GROUP + j
                lanes = slice(j * QT, (j + 1) * QT)
                s = s_all[:, lanes] + strip_ref[g, pl.ds(off, KEY_CHUNK), lanes]
                s = jnp.where(mask, s, NEG)
                m_old = m_s[pl.ds(h, 1), :]
                m_new = jnp.maximum(m_old, jnp.max(s, axis=0, keepdims=True))
                alpha = jnp.exp(m_old - m_new)
                p = jnp.exp(s - m_new)
                l_s[pl.ds(h, 1), :] = alpha * l_s[pl.ds(h, 1), :] + jnp.sum(p, axis=0, keepdims=True)
                m_s[pl.ds(h, 1), :] = m_new
                probs.append(p.astype(BF16))
                alphas.append(alpha)
            pv = jnp.dot(vt_ref[g * HEAD_DIM:(g + 1) * HEAD_DIM, pl.ds(k0, KEY_CHUNK)],
                         jnp.concatenate(probs, axis=1), preferred_element_type=F32)
            acc_s[g] = jnp.concatenate(alphas, axis=1) * acc_s[g] + pv
        return carry

    lax.fori_loop(0, n_chunks, attend, 0)
    for g in range(N_KV):
        for j in range(0, GROUP, 2):
            h = g * GROUP + j
            pair = [acc_s[g][:, (j + i) * QT:(j + i + 1) * QT] * (1.0 / l_s[pl.ds(h + i, 1), :]) for i in range(2)]
            o_ref[:, h * HEAD_DIM:(h + 2) * HEAD_DIM] = jnp.concatenate(pair, axis=0).T


def _sparse_attention_t(mode, qt, k, vt, strip_t, extra, k_sel=0):
    bsz, nq = qt.shape[:2]
    lp = k.shape[1]
    kv_dim = N_KV * HEAD_DIM
    cols = GROUP * QT
    per_q = lambda *tail: pl.BlockSpec((None, None) + tail, lambda b, i: (b, i) + (0,) * len(tail))
    per_b = lambda *tail: pl.BlockSpec((None,) + tail, lambda b, i: (b,) + (0,) * len(tail))
    in_specs = [per_q(kv_dim, cols), per_b(lp, kv_dim), per_b(kv_dim, lp),
                pl.BlockSpec(strip_t.shape, lambda b, i: (0, 0, 0))]
    scratch = [pltpu.VMEM((kv_dim, N_KV * cols), BF16),
               pltpu.VMEM((N_HEADS, QT), F32),
               pltpu.VMEM((N_HEADS, QT), F32),
               pltpu.VMEM((N_KV, HEAD_DIM, cols), F32)]
    if mode == "dsa":
        qit, wit, ki = extra
        in_specs += [per_q(IDX_DIM, IDX_HEADS * QT), per_q(1, IDX_HEADS * QT), per_b(lp, IDX_DIM)]
        scratch.append(pltpu.VMEM((lp, QT), I32))
    else:
        (km,) = extra
        in_specs.append(per_b(N_KV, km.shape[2], HEAD_DIM))
    return pl.pallas_call(
        functools.partial(_attn_t_kernel, mode, lp, k_sel),
        grid=(bsz, nq),
        in_specs=in_specs,
        out_specs=per_q(QT, N_HEADS * HEAD_DIM),
        out_shape=jax.ShapeDtypeStruct((bsz, nq, QT, N_HEADS * HEAD_DIM), F32),
        scratch_shapes=scratch,
        compiler_params=_cparams(("parallel", "arbitrary")),
        name=mode + "_attention_t",
    )(qt, k, vt, strip_t, *extra)


def _top_rows(s, k, payload=None, rank=None):
    rid = lax.broadcasted_iota(I32, s.shape, 0) if rank is None else rank
    vals, ids = [], []
    for _ in range(k):
        mx = jnp.max(s, axis=0, keepdims=True)
        first = jnp.min(jnp.where(s == mx, rid, jnp.iinfo(jnp.int32).max), axis=0, keepdims=True)
        hit = rid == first
        vals.append(mx)
        if payload is None:
            ids.append(first)
        else:
            ids.append(jnp.max(jnp.where(hit, payload, -1), axis=0, keepdims=True))
        s = jnp.where(hit, -jnp.inf, s)
    return jnp.concatenate(vals, axis=0), jnp.concatenate(ids, axis=0)


def _top_rows_paired(s, k):
    half = s.shape[0] // 2
    lo, hi = s[:half], s[half:]
    rid = lax.broadcasted_iota(I32, lo.shape, 0)
    first_lo = lo >= hi
    top = jnp.where(first_lo, lo, hi)
    rest = jnp.where(first_lo, hi, lo)
    top_id = jnp.where(first_lo, rid, rid + half)
    rest_id = jnp.where(first_lo, rid + half, rid)
    vals, ids = [], []
    for _ in range(k):
        mx = jnp.max(top, axis=0, keepdims=True)
        first = jnp.min(jnp.where(top == mx, top_id, jnp.iinfo(jnp.int32).max), axis=0, keepdims=True)
        hit = top_id == first
        vals.append(mx)
        ids.append(first)
        top = jnp.where(hit, rest, top)
        top_id = jnp.where(hit, rest_id, top_id)
        rest = jnp.where(hit, -jnp.inf, rest)
    return jnp.concatenate(vals, axis=0), jnp.concatenate(ids, axis=0)


def _peer_route_kernel(x_ref, wqt_ref, keys_ref, row_ref, par_ref, g_ref):
    xb = x_ref[...].astype(BF16)
    qt = lax.dot_general(wqt_ref[...], xb, (((1,), (1,)), ((), ())),
                         preferred_element_type=F32)
    n_t = x_ref.shape[0]
    piece = lax.broadcasted_iota(I32, (11 * 8, n_t), 0) // 8
    within = lax.broadcasted_iota(I32, (11 * 8, n_t), 0) % 8
    by_b = (piece >= 2) & (piece < 10)
    ca = jnp.where(piece < 2, 0, jnp.where(by_b, within, 8 + within))
    cb = jnp.where(piece == 0, within, jnp.where(piece == 1, 8 + within, jnp.where(by_b, piece - 2, 0)))
    cand_ok = ((ca + 1) * (cb + 1) <= PEER_TOPK) & jnp.logical_not(by_b & (within == 0))
    cand_rank = ca * PEER_TOPK + cb

    def pieces(first, second, combine):
        out = [combine(first[0:1], second[0:8]), combine(first[0:1], second[8:16])]
        out += [combine(first[0:8], second[b:b + 1]) for b in range(8)]
        out.append(combine(first[8:16], second[0:1]))
        return jnp.concatenate(out, axis=0)

    e_rows, g_rows = [], []
    for h in range(PEER_HEADS):
        sv, si = [], []
        for c in range(2):
            hc = 2 * h + c
            qhc = qt[hc * PEER_HALF:(hc + 1) * PEER_HALF].astype(BF16)
            s = jnp.dot(keys_ref[hc], qhc, preferred_element_type=F32)
            v_, i_ = _top_rows_paired(s, PEER_TOPK)
            sv.append(v_)
            si.append(i_)
        cand = jnp.where(cand_ok, pieces(sv[0], sv[1], lambda u, w: u + w), -jnp.inf)
        cidx = pieces(si[0], si[1], lambda u, w: u * PEER_NKEYS + w)
        gv, ge = _top_rows(cand, PEER_TOPK, payload=cidx, rank=cand_rank)
        ex = jnp.exp(gv - gv[0:1])
        g_rows.append(ex / jnp.sum(ex, axis=0, keepdims=True))
        e_rows.append(ge)
    e_t = jnp.concatenate(e_rows, axis=0).T
    row_ref[...] = jnp.right_shift(e_t, 1) * 8
    par_ref[...] = e_t & 1
    g_ref[...] = jnp.concatenate(g_rows, axis=0).T


def _peer_route(x, wqt_bf16, keys_bf16, tb):
    m, d = x.shape
    out = pl.BlockSpec((tb, PEER_PAIRS), lambda i: (i, 0))
    return pl.pallas_call(
        _peer_route_kernel,
        grid=(m // tb,),
        in_specs=[pl.BlockSpec((tb, d), lambda i: (i, 0)),
                  pl.BlockSpec(wqt_bf16.shape, lambda i: (0, 0)),
                  pl.BlockSpec(keys_bf16.shape, lambda i: (0, 0, 0))],
        out_specs=[out, out, out],
        out_shape=[jax.ShapeDtypeStruct((m, PEER_PAIRS), I32),
                   jax.ShapeDtypeStruct((m, PEER_PAIRS), I32),
                   jax.ShapeDtypeStruct((m, PEER_PAIRS), F32)],
        compiler_params=_cparams(("parallel",)),
        name="peer_route",
    )(x, wqt_bf16, keys_bf16)


FOLD_PAIRS = 32
HI_MASK = 0xFFFF0000
UP_SLOTS = 4
TILE_ROWS = 8


def _pack_table(w):
    n, d = w.shape
    bits = lax.bitcast_convert_type(w.astype(BF16), jnp.uint16).astype(jnp.uint32)
    bits = bits.reshape(n // 2, 2, d // 128, 128)
    return (bits[:, 0] | (bits[:, 1] << 16)).astype(jnp.uint32).reshape(n // 2 * (d // 128), 128)


def _load_tile(tbl_ref, row_ref, idx):
    return tbl_ref[pl.ds(pl.multiple_of(row_ref[idx], TILE_ROWS), TILE_ROWS), :]


def _peer_up_kernel(tb, row_ref, par_ref, x_ref, g_ref, fold_ref, tbl_ref, c_ref, prod_s):
    diag = (lax.broadcasted_iota(I32, (PEER_PAIRS, PEER_PAIRS), 0)
            == lax.broadcasted_iota(I32, (PEER_PAIRS, PEER_PAIRS), 1))

    def gather(t, slot):
        xt = x_ref[t]
        rows_t = row_ref.at[pl.ds(t * PEER_PAIRS, PEER_PAIRS)]
        for p in range(0, PEER_PAIRS, 2):
            w0 = _load_tile(tbl_ref, rows_t, p)
            w1 = _load_tile(tbl_ref, rows_t, p + 1)
            even = [pltpu.bitcast(jnp.left_shift(w, jnp.uint32(16)), F32) * xt for w in (w0, w1)]
            odd = [pltpu.bitcast(w & jnp.uint32(HI_MASK), F32) * xt for w in (w0, w1)]
            rows = pl.ds(p * TILE_ROWS, 2 * TILE_ROWS)
            prod_s[slot, rows, 0:128] = jnp.concatenate(even, axis=0).astype(BF16)
            prod_s[slot, rows, 128:256] = jnp.concatenate(odd, axis=0).astype(BF16)

    def reduce(t, slot):
        depth = FOLD_PAIRS * TILE_ROWS
        parts = [jnp.dot(fold_ref[...], prod_s[slot, pl.ds(i * depth, depth), :], preferred_element_type=F32)
                 for i in range(PEER_PAIRS // FOLD_PAIRS)]
        part = jnp.concatenate(parts, axis=0)
        rows = []
        for half in range(2):
            col = jnp.sum(part[:, half * 128:(half + 1) * 128], axis=1, keepdims=True)
            rows.append(jnp.sum(jnp.where(diag, col, 0.0), axis=0, keepdims=True))
        a = jnp.where(par_ref[pl.ds(t, 1), :] == 1, rows[1], rows[0])
        act = 0.5 * a * (1.0 + lax.erf(a * (2.0 ** -0.5)))
        c_ref[pl.ds(t, 1), :] = g_ref[pl.ds(t, 1), :] * act

    prod_s[UP_SLOTS:2 * UP_SLOTS] = jnp.zeros((UP_SLOTS,) + prod_s.shape[1:], BF16)

    def step(k, carry):
        t0 = 2 * UP_SLOTS * k
        for j in range(UP_SLOTS):
            gather(t0 + j, j)
            reduce(jnp.maximum(t0 - UP_SLOTS + j, 0), UP_SLOTS + j)
        for j in range(UP_SLOTS):
            gather(t0 + UP_SLOTS + j, UP_SLOTS + j)
            reduce(t0 + j, j)
        return carry

    lax.fori_loop(0, tb // (2 * UP_SLOTS), step, 0)
    for j in range(UP_SLOTS):
        reduce(tb - UP_SLOTS + j, UP_SLOTS + j)


DOWN_GROUP = 8


def _peer_down_kernel(tb, row_ref, par_ref, c_ref, spread_ref, tbl_ref, f_ref, tile_s, coef_s):
    width = PEER_PAIRS * 2 * TILE_ROWS
    lane = lax.broadcasted_iota(I32, (TILE_ROWS, width), 1)
    own_row = ((lane & (2 * TILE_ROWS - 1)) >> 1) == lax.broadcasted_iota(I32, (TILE_ROWS, width), 0)
    lane_par = (lax.broadcasted_iota(I32, (DOWN_GROUP, width), 1) & 1).astype(F32)

    def gather(t, slot):
        rows_t = row_ref.at[pl.ds(t * PEER_PAIRS, PEER_PAIRS)]
        for p in range(PEER_PAIRS):
            tile_s[slot, pl.ds(p * TILE_ROWS, TILE_ROWS), :] = _load_tile(tbl_ref, rows_t, p)

    def apply(t, j, slot):
        tiles = pltpu.bitcast(tile_s[slot], BF16)
        lhs = [jnp.where(own_row, coef_s[pl.ds(part * DOWN_GROUP + j, 1), :], 0.0) for part in range(2)]
        out = jnp.dot(jnp.concatenate(lhs, axis=0).astype(BF16), tiles, preferred_element_type=F32)
        f_ref[t] = out[0:TILE_ROWS] + out[TILE_ROWS:2 * TILE_ROWS]

    def group(gi, carry):
        t0 = pl.multiple_of(gi * DOWN_GROUP, DOWN_GROUP)
        gather(t0, 0)
        c = c_ref[pl.ds(t0, DOWN_GROUP), :]
        c_hi = c.astype(BF16)
        c_lo = (c - c_hi.astype(F32)).astype(BF16)
        par = par_ref[pl.ds(t0, DOWN_GROUP), :].astype(BF16)
        rep = jnp.dot(jnp.concatenate([c_hi, c_lo, par], axis=0), spread_ref[...],
                      preferred_element_type=F32)
        keep = rep[2 * DOWN_GROUP:] == lane_par
        coef_s[0:DOWN_GROUP, :] = jnp.where(keep, rep[0:DOWN_GROUP], 0.0)
        coef_s[DOWN_GROUP:2 * DOWN_GROUP, :] = jnp.where(keep, rep[DOWN_GROUP:2 * DOWN_GROUP], 0.0)
        for j in range(DOWN_GROUP):
            if j + 1 < DOWN_GROUP:
                gather(t0 + j + 1, (j + 1) % 2)
            apply(t0 + j, j, j % 2)
        return carry

    lax.fori_loop(0, tb // DOWN_GROUP, group, 0)


def _peer_experts(x, row, par, g, u_tbl, v_tbl, tb):
    m, d = x.shape
    x3 = x.reshape(m, d // 128, 128)
    flat = row.reshape(m * PEER_PAIRS)
    smem = pl.BlockSpec((tb * PEER_PAIRS,), lambda i: (i,), memory_space=pltpu.SMEM)
    vrow = pl.BlockSpec((tb, PEER_PAIRS), lambda i: (i, 0))
    tile = pl.BlockSpec((tb, d // 128, 128), lambda i: (i, 0, 0))
    resident = pl.BlockSpec(memory_space=pltpu.VMEM)
    const = lambda a: pl.BlockSpec(a.shape, lambda i: (0, 0))
    fold = jnp.asarray(np.arange(FOLD_PAIRS * TILE_ROWS)[None, :] // TILE_ROWS == np.arange(FOLD_PAIRS)[:, None], BF16)
    width = PEER_PAIRS * 2 * TILE_ROWS
    spread = jnp.asarray(np.arange(width)[None, :] // (2 * TILE_ROWS) == np.arange(PEER_PAIRS)[:, None], BF16)
    c = pl.pallas_call(
        functools.partial(_peer_up_kernel, tb),
        grid=(m // tb,),
        in_specs=[smem, vrow, tile, vrow, const(fold), resident],
        out_specs=vrow,
        out_shape=jax.ShapeDtypeStruct((m, PEER_PAIRS), F32),
        scratch_shapes=[pltpu.VMEM((2 * UP_SLOTS, PEER_PAIRS * TILE_ROWS, 256), BF16)],
        compiler_params=_cparams(("arbitrary",)),
        name="peer_up",
    )(flat, par, x3, g, fold, u_tbl)
    f = pl.pallas_call(
        functools.partial(_peer_down_kernel, tb),
        grid=(m // tb,),
        in_specs=[smem, vrow, vrow, const(spread), resident],
        out_specs=tile,
        out_shape=jax.ShapeDtypeStruct((m, d // 128, 128), F32),
        scratch_shapes=[pltpu.VMEM((2, PEER_PAIRS * TILE_ROWS, 128), jnp.uint32),
                        pltpu.VMEM((2 * DOWN_GROUP, width), F32)],
        compiler_params=_cparams(("arbitrary",)),
        name="peer_down",
    )(flat, par, c, spread, v_tbl)
    return f.reshape(m, d)


def _stack_heads(a, qb, n_lead):
    bsz, q, n, hd = a.shape
    n_in = n // n_lead
    a = a.reshape(bsz, q // qb, qb, n_lead, n_in, hd)
    return a.transpose(0, 1, 3, 4, 2, 5).reshape(bsz, q // qb, n_lead, n_in * qb, hd)


def _unstack_heads(o, qb):
    bsz, nq = o.shape[:2]
    o = o.reshape(bsz, nq, N_KV, GROUP, qb, HEAD_DIM).transpose(0, 1, 4, 2, 3, 5)
    return o.reshape(bsz * nq * qb, N_HEADS * HEAD_DIM)


def _run_trunk(x, p, past, prm, qb, tm, tb_route, tb_exp):
    bsz, q_len, d = x.shape
    depth = prm["w_o"].shape[0]
    m = bsz * q_len
    tm, tb_route, tb_exp = min(tm, m), min(tb_route, m), min(tb_exp, m)
    alpha = (2 * depth) ** 0.25
    attn_dim = N_HEADS * HEAD_DIM
    kv_dim = N_KV * HEAD_DIM
    if past is None:
        q_pos0 = 0
        n_pages = q_len // PAGE
        table = jnp.arange(bsz * n_pages, dtype=I32).reshape(bsz, n_pages)
    else:
        cache_k, cache_v, cache_ki, table = past
        q_pos0 = table.shape[1] * PAGE
    on_lanes = past is None and q_len % QT == 0
    if on_lanes:
        qb = QT
    nq = q_len // qb
    strip = _bias_strip(prm["rel_bias"], qb, KEY_CHUNK if on_lanes else ROWS_STEP)
    strip = strip.reshape(N_KV, GROUP, qb, strip.shape[2])
    if on_lanes:
        strip = strip.transpose(0, 3, 1, 2).reshape(N_KV, strip.shape[3], GROUP * qb)
    else:
        strip = strip.reshape(N_KV, GROUP * qb, strip.shape[3])
    xf = x.reshape(m, d)
    ks, vs, kis = [], [], []

    def relayout(new_rows, cache, layer, want):
        w = new_rows.shape[2]
        if past is None:
            return _relayout(new_rows.reshape(bsz * n_pages, PAGE, w), table, None, False, want)
        n_pool = cache.shape[1]
        pool = cache.reshape((cache.shape[0] * n_pool,) + cache.shape[2:])
        pool_t = jnp.moveaxis(pool, 1, -1).reshape(pool.shape[0], w, PAGE)
        pad = jnp.zeros((bsz, STEP_KEYS - q_len, w), F32)
        return _relayout(pool_t, table + layer * n_pool, jnp.concatenate([new_rows, pad], axis=1), True, want)

    for l in range(depth):
        dsa = l % 2 == 0
        w_in = prm["w_in_dsa"][l // 2] if dsa else prm["w_in_moba"][l // 2]
        if on_lanes:
            proj = _in_proj_lanes(xf, w_in, dsa, tm)
            q_in = proj[0].reshape(bsz, nq, kv_dim, GROUP * qb)
            k = proj[1].reshape(bsz, q_len, kv_dim)
            v = proj[2].reshape(bsz, q_len, kv_dim)
        else:
            n_in = w_in.shape[1]
            n_pad = -(-n_in // 128) * 128
            proj = _matmul(xf, jnp.pad(w_in, ((0, 0), (0, n_pad - n_in))).astype(BF16), tm)
            qh = proj[:, :attn_dim].reshape(bsz, q_len, N_HEADS, HEAD_DIM)
            k = proj[:, attn_dim:attn_dim + kv_dim].reshape(bsz, q_len, kv_dim)
            v = proj[:, attn_dim + kv_dim:attn_dim + 2 * kv_dim].reshape(bsz, q_len, kv_dim)
        ks.append(k.reshape(bsz, q_len, N_KV, HEAD_DIM))
        vs.append(v.reshape(bsz, q_len, N_KV, HEAD_DIM))
        k_lay = relayout(k, None if past is None else cache_k, l,
                         (("x",) if on_lanes else ("xt",)) + (() if dsa else ("sums",)))
        v_lay = relayout(v, None if past is None else cache_v, l, ("xt",) if on_lanes else ("x",))
        if on_lanes:
            kr, vt = k_lay["x"], v_lay["xt"]
        else:
            kt = k_lay["xt"].reshape(bsz, N_KV, HEAD_DIM, k_lay["xt"].shape[2])
            vr = v_lay["x"]
            q_in = _stack_heads(qh, qb, N_KV)
        if dsa:
            if on_lanes:
                ki = proj[3].reshape(bsz, q_len, IDX_DIM)
            else:
                off = attn_dim + 2 * kv_dim
                qi = proj[:, off:off + IDX_HEADS * IDX_DIM].reshape(bsz, q_len, IDX_HEADS, IDX_DIM)
                off += IDX_HEADS * IDX_DIM
                ki = proj[:, off:off + IDX_DIM].reshape(bsz, q_len, IDX_DIM)
                wi = proj[:, off + IDX_DIM:off + IDX_DIM + IDX_HEADS].reshape(bsz, q_len, IDX_HEADS, 1)
            kis.append(ki)
            ki_lay = relayout(ki, None if past is None else cache_ki, l // 2, ("x",) if on_lanes else ("xt",))
            k_sel = min(IDX_TOPK, (q_pos0 + q_len) // 4)
            if on_lanes:
                qi_t = proj[4].reshape(bsz, nq, IDX_DIM, IDX_HEADS * qb)
                wi_t = proj[5].reshape(bsz, nq, 1, IDX_HEADS * qb)
                o = _sparse_attention_t("dsa", q_in, kr, vt, strip, (qi_t, wi_t, ki_lay["x"]), k_sel)
            else:
                qi_st = _stack_heads(qi, qb, 1)[:, :, 0]
                wi_st = _stack_heads(wi, qb, 1)[:, :, 0]
                o = _sparse_attention("dsa", q_in, kt, vr, strip, (qi_st, wi_st, ki_lay["xt"]), qb, q_pos0, k_sel)
        else:
            ksum = k_lay["sums"]
            nb = ksum.shape[1]
            kmean = (ksum * (1.0 / MOBA_BLOCK)).reshape(bsz, nb, N_KV, HEAD_DIM)
            if on_lanes:
                km = jnp.pad(kmean.transpose(0, 2, 1, 3), ((0, 0), (0, 0), (0, -nb % 8), (0, 0)))
                o = _sparse_attention_t("moba", q_in, kr, vt, strip, (km,))
            else:
                kmt = jnp.pad(kmean.transpose(0, 2, 3, 1), ((0, 0), (0, 0), (0, 0), (0, -nb % 128)))
                o = _sparse_attention("moba", q_in, kt, vr, strip, (kmt,), qb, q_pos0)
        if on_lanes:
            o = o.reshape(m, attn_dim)
        else:
            o = _unstack_heads(o, qb)
        xf = _attn_out(o, prm["w_o"][l].astype(BF16), xf, prm["ln_g"][l, 0][None], prm["ln_b"][l, 0][None],
                       alpha, tm)
        row, par, g = _peer_route(xf, prm["peer_wq"][l].T.astype(BF16),
                                 prm["peer_keys"][l].reshape(2 * PEER_HEADS, PEER_NKEYS, PEER_HALF).astype(BF16),
                                 tb_route)
        f = _peer_experts(xf, row, par, g, _pack_table(prm["peer_u"][l]), _pack_table(prm["peer_v"][l]), tb_exp)
        xf = _layer_tail(xf, f, prm["ln_g"][l, 1][None], prm["ln_b"][l, 1][None],
                         prm["gate_w"][l].astype(BF16), prm["gate_b"][l][None],
                         p[l].reshape(m, -1), prm["ple_w"][l].astype(BF16), alpha, tm)
    return xf.reshape(bsz, q_len, d), jnp.stack(ks), jnp.stack(vs), jnp.stack(kis)


def kernel(x_prompt, x_sample, cache_k, cache_v, cache_ki, page_table, p_prompt, p_sample, w_in_dsa, w_in_moba, w_o, rel_bias, ln_g, ln_b, peer_wq, peer_keys, peer_u, peer_v, gate_w, gate_b, ple_w):
    prm = {
        "w_in_dsa": w_in_dsa, "w_in_moba": w_in_moba, "w_o": w_o, "rel_bias": rel_bias,
        "ln_g": ln_g, "ln_b": ln_b, "peer_wq": peer_wq, "peer_keys": peer_keys,
        "peer_u": peer_u, "peer_v": peer_v, "gate_w": gate_w, "gate_b": gate_b, "ple_w": ple_w,
    }
    q_sample = x_sample.shape[1]
    y_prompt, k_prompt, v_prompt, ki_prompt = _run_trunk(
        x_prompt, p_prompt, None, prm, qb=128, tm=512, tb_route=256, tb_exp=128)
    y_sample, k_sample, v_sample, ki_sample = _run_trunk(
        x_sample, p_sample, (cache_k, cache_v, cache_ki, page_table), prm,
        qb=q_sample, tm=256, tb_route=256, tb_exp=64)
    return (y_prompt, y_sample, k_prompt, v_prompt, ki_prompt, k_sample, v_sample, ki_sample)
```

```python
import functools
import math

import numpy as np
import jax
import jax.numpy as jnp
from jax import lax
from jax.experimental import pallas as pl
from jax.experimental.pallas import tpu as pltpu

F32 = jnp.float32
BF16 = jnp.bfloat16
I32 = jnp.int32

N_HEADS = 16
HEAD_DIM = 64
N_KV = 4
GROUP = 4
IDX_HEADS = 8
IDX_DIM = 64
IDX_TOPK = 256
MOBA_BLOCK = 256
MOBA_TOPK = 3
REL_BUCKETS = 32
REL_MAX_DIST = 128
PEER_HEADS = 8
PEER_NKEYS = 128
PEER_HALF = 128
PEER_TOPK = 16
PEER_PAIRS = PEER_HEADS * PEER_TOPK
LN_EPS = 1e-5
PAGE = 128

KEY_CHUNK = 256
ROWS_STEP = 2 * KEY_CHUNK
NEG = -1e30
INT_MIN = -2147483648
VMEM_LIMIT = 56 * 1024 * 1024


def _cparams(sem):
    return pltpu.CompilerParams(dimension_semantics=sem, vmem_limit_bytes=VMEM_LIMIT)


def _mm_kernel(a_ref, b_ref, o_ref):
    o_ref[...] = jnp.dot(a_ref[...].astype(BF16), b_ref[...], preferred_element_type=F32)


def _matmul(a, b_bf16, tm):
    m, k = a.shape
    n = b_bf16.shape[1]
    return pl.pallas_call(
        _mm_kernel,
        grid=(m // tm,),
        in_specs=[pl.BlockSpec((tm, k), lambda i: (i, 0)),
                  pl.BlockSpec((k, n), lambda i: (0, 0))],
        out_specs=pl.BlockSpec((tm, n), lambda i: (i, 0)),
        out_shape=jax.ShapeDtypeStruct((m, n), F32),
        compiler_params=_cparams(("parallel",)),
        name="in_proj",
    )(a, b_bf16)


def _in_proj_lanes_kernel(has_idx, *refs):
    if has_idx:
        x_ref, wqt_ref, wrow_ref, wqit_ref, wit_ref, qt_ref, k_ref, v_ref, ki_ref, qit_ref, wi_ref = refs
    else:
        x_ref, wqt_ref, wrow_ref, qt_ref, k_ref, v_ref = refs
    kv_dim = N_KV * HEAD_DIM
    xb = x_ref[...].astype(BF16)
    n_blk = xb.shape[0] // QT
    nt = (((1,), (1,)), ((), ()))
    rows = jnp.dot(xb, wrow_ref[...], preferred_element_type=F32)
    k_ref[...] = rows[:, 0:kv_dim]
    v_ref[...] = rows[:, kv_dim:2 * kv_dim]
    q_t = lax.dot_general(wqt_ref[...], xb, nt, preferred_element_type=F32)
    for i in range(n_blk):
        for g in range(N_KV):
            for j in range(GROUP):
                h = g * GROUP + j
                qt_ref[i, g * HEAD_DIM:(g + 1) * HEAD_DIM, j * QT:(j + 1) * QT] = (
                    q_t[h * HEAD_DIM:(h + 1) * HEAD_DIM, i * QT:(i + 1) * QT])
    if has_idx:
        ki_ref[...] = rows[:, 2 * kv_dim:2 * kv_dim + IDX_DIM]
        qi_t = lax.dot_general(wqit_ref[...], xb, nt, preferred_element_type=F32)
        wi_t = lax.dot_general(wit_ref[...], xb, nt, preferred_element_type=F32)
        for i in range(n_blk):
            for h in range(IDX_HEADS):
                qit_ref[i, :, h * QT:(h + 1) * QT] = qi_t[h * IDX_DIM:(h + 1) * IDX_DIM, i * QT:(i + 1) * QT]
                wi_ref[i, :, h * QT:(h + 1) * QT] = wi_t[h:h + 1, i * QT:(i + 1) * QT]


def _in_proj_lanes(x, w_in, has_idx, tm):
    m, d = x.shape
    attn_dim = N_HEADS * HEAD_DIM
    kv_dim = N_KV * HEAD_DIM
    n_blk = tm // QT
    w_rows = w_in[:, attn_dim:attn_dim + 2 * kv_dim]
    off = attn_dim + 2 * kv_dim
    if has_idx:
        w_ki = w_in[:, off + IDX_HEADS * IDX_DIM:off + IDX_HEADS * IDX_DIM + IDX_DIM]
        w_rows = jnp.concatenate([w_rows, w_ki, jnp.zeros((d, 128 - IDX_DIM), F32)], axis=1)
    args = [x, w_in[:, :attn_dim].T.astype(BF16), w_rows.astype(BF16)]
    whole = lambda a: pl.BlockSpec(a.shape, lambda i: (0, 0))
    in_specs = [pl.BlockSpec((tm, d), lambda i: (i, 0)), whole(args[1]), whole(args[2])]
    blk3 = lambda r, c: pl.BlockSpec((n_blk, r, c), lambda i: (i, 0, 0))
    rowb = lambda c: pl.BlockSpec((tm, c), lambda i: (i, 0))
    out_specs = [blk3(kv_dim, GROUP * QT), rowb(kv_dim), rowb(kv_dim)]
    out_shape = [jax.ShapeDtypeStruct((m // QT, kv_dim, GROUP * QT), F32),
                 jax.ShapeDtypeStruct((m, kv_dim), F32), jax.ShapeDtypeStruct((m, kv_dim), F32)]
    if has_idx:
        w_wi = w_in[:, off + IDX_HEADS * IDX_DIM + IDX_DIM:off + IDX_HEADS * IDX_DIM + IDX_DIM + IDX_HEADS]
        args += [w_in[:, off:off + IDX_HEADS * IDX_DIM].T.astype(BF16), w_wi.T.astype(BF16)]
        in_specs += [whole(args[3]), whole(args[4])]
        out_specs += [rowb(IDX_DIM), blk3(IDX_DIM, IDX_HEADS * QT), blk3(1, IDX_HEADS * QT)]
        out_shape += [jax.ShapeDtypeStruct((m, IDX_DIM), F32),
                      jax.ShapeDtypeStruct((m // QT, IDX_DIM, IDX_HEADS * QT), F32),
                      jax.ShapeDtypeStruct((m // QT, 1, IDX_HEADS * QT), F32)]
    return pl.pallas_call(
        functools.partial(_in_proj_lanes_kernel, has_idx),
        grid=(m // tm,),
        in_specs=in_specs,
        out_specs=out_specs,
        out_shape=out_shape,
        compiler_params=_cparams(("parallel",)),
        name="in_proj_lanes",
    )(*args)


def _layer_norm(y, g, b):
    mu = jnp.mean(y, axis=-1, keepdims=True)
    yc = y - mu
    var = jnp.mean(yc * yc, axis=-1, keepdims=True)
    return yc * lax.rsqrt(var + LN_EPS) * g + b


def _attn_out_kernel(alpha, o_ref, w_ref, x_ref, g_ref, b_ref, y_ref):
    y = alpha * x_ref[...] + jnp.dot(o_ref[...].astype(BF16), w_ref[...],
                                     preferred_element_type=F32)
    y_ref[...] = _layer_norm(y, g_ref[...], b_ref[...])


def _attn_out(o, w_bf16, x, g, b, alpha, tm):
    m, d = x.shape
    row = pl.BlockSpec((tm, d), lambda i: (i, 0))
    vec = pl.BlockSpec((1, d), lambda i: (0, 0))
    return pl.pallas_call(
        functools.partial(_attn_out_kernel, alpha),
        grid=(m // tm,),
        in_specs=[row, pl.BlockSpec((d, d), lambda i: (0, 0)), row, vec, vec],
        out_specs=row,
        out_shape=jax.ShapeDtypeStruct((m, d), F32),
        compiler_params=_cparams(("parallel",)),
        name="attn_out_ln",
    )(o, w_bf16, x, g, b)


def _layer_tail_kernel(alpha, x_ref, f_ref, g_ref, b_ref, gw_ref, gb_ref, p_ref, pw_ref, y_ref):
    x2 = _layer_norm(alpha * x_ref[...] + f_ref[...], g_ref[...], b_ref[...])
    z = jnp.dot(x2.astype(BF16), gw_ref[...], preferred_element_type=F32) + gb_ref[...]
    gate = 1.0 / (1.0 + jnp.exp(-z))
    e = jnp.dot(p_ref[...].astype(BF16), pw_ref[...], preferred_element_type=F32)
    y_ref[...] = x2 + gate * e


def _layer_tail(x, f, g, b, gw_bf16, gb, p, pw_bf16, alpha, tm):
    m, d = x.shape
    pd = p.shape[1]
    row = pl.BlockSpec((tm, d), lambda i: (i, 0))
    vec = pl.BlockSpec((1, d), lambda i: (0, 0))
    return pl.pallas_call(
        functools.partial(_layer_tail_kernel, alpha),
        grid=(m // tm,),
        in_specs=[row, row, vec, vec, pl.BlockSpec((d, d), lambda i: (0, 0)), vec,
                  pl.BlockSpec((tm, pd), lambda i: (i, 0)),
                  pl.BlockSpec((pd, d), lambda i: (0, 0))],
        out_specs=row,
        out_shape=jax.ShapeDtypeStruct((m, d), F32),
        compiler_params=_cparams(("parallel",)),
        name="layer_tail",
    )(x, f, g, b, gw_bf16, gb, p, pw_bf16)


STEP_PAGES = 8
STEP_KEYS = STEP_PAGES * PAGE
STEP_CHUNKS = STEP_KEYS // KEY_CHUNK


def _relayout_kernel(n_pool_steps, has_new, pages_t, want, *refs):
    page_refs = refs[1:1 + STEP_PAGES]
    new_ref = refs[1 + STEP_PAGES] if has_new else None
    outs = dict(zip(want, refs[len(refs) - len(want):]))
    j = pl.program_id(1)
    need_rows = "x" in want or "sums" in want

    def emit(rows, cols):
        if "xt" in want:
            outs["xt"][...] = (rows.T if cols is None else cols).astype(BF16)
        if need_rows and rows is None:
            rows = cols.T
        if "x" in want:
            outs["x"][...] = rows.astype(BF16)
        if "sums" in want:
            for i in range(STEP_CHUNKS):
                outs["sums"][pl.ds(i, 1), :] = jnp.sum(rows[i * KEY_CHUNK:(i + 1) * KEY_CHUNK], axis=0, keepdims=True)

    def from_pages():
        if pages_t:
            emit(None, jnp.concatenate([r[...] for r in page_refs], axis=1))
        else:
            emit(jnp.concatenate([r[...] for r in page_refs], axis=0), None)

    if has_new:
        pl.when(j < n_pool_steps)(from_pages)

        @pl.when(j >= n_pool_steps)
        def _():
            emit(new_ref[...], None)
    else:
        from_pages()


def _relayout(pool, table, new, pages_t, want):
    bsz, n_pages = table.shape
    w = pool.shape[1] if pages_t else pool.shape[2]
    n_pool_steps = n_pages // STEP_PAGES
    n_steps = n_pool_steps + (1 if new is not None else 0)
    lp = n_steps * STEP_KEYS
    last = n_pool_steps - 1

    def page_map(i):
        return lambda b, j, tbl: (tbl[b, STEP_PAGES * jnp.minimum(j, last) + i], 0, 0)

    page_block = (None, w, PAGE) if pages_t else (None, PAGE, w)
    in_specs = [pl.BlockSpec(page_block, page_map(i)) for i in range(STEP_PAGES)]
    args = [pool] * STEP_PAGES
    if new is not None:
        in_specs.append(pl.BlockSpec((None, STEP_KEYS, w), lambda b, j, tbl: (b, 0, 0)))
        args.append(new)
    specs = {"xt": (pl.BlockSpec((None, w, STEP_KEYS), lambda b, j, tbl: (b, 0, j)),
                    jax.ShapeDtypeStruct((bsz, w, lp), BF16)),
             "x": (pl.BlockSpec((None, STEP_KEYS, w), lambda b, j, tbl: (b, j, 0)),
                   jax.ShapeDtypeStruct((bsz, lp, w), BF16)),
             "sums": (pl.BlockSpec((None, None, STEP_CHUNKS, w), lambda b, j, tbl: (b, j, 0, 0)),
                      jax.ShapeDtypeStruct((bsz, n_steps, STEP_CHUNKS, w), F32))}
    outs = pl.pallas_call(
        functools.partial(_relayout_kernel, n_pool_steps, new is not None, pages_t, want),
        grid_spec=pltpu.PrefetchScalarGridSpec(
            num_scalar_prefetch=1,
            grid=(bsz, n_steps),
            in_specs=in_specs,
            out_specs=[specs[name][0] for name in want]),
        out_shape=[specs[name][1] for name in want],
        compiler_params=_cparams(("parallel", "arbitrary")),
        name="kv_relayout",
    )(table, *args)
    outs = dict(zip(want, outs))
    if "sums" in outs:
        outs["sums"] = outs["sums"].reshape(bsz, n_steps * STEP_CHUNKS, w)
    return outs


def _bucket_table():
    n = np.arange(REL_MAX_DIST + 1)
    exact = REL_BUCKETS // 2
    nf = np.maximum(n, 1).astype(np.float32)
    large = exact + (np.log(nf / np.float32(exact)) / np.float32(math.log(REL_MAX_DIST / exact))
                     * np.float32(REL_BUCKETS - exact)).astype(np.int32)
    return np.where(n < exact, n, np.minimum(large, REL_BUCKETS - 1)).astype(np.int32)


def _bias_strip_kernel(bk_ref, rb_ref, o_ref):
    bk = bk_ref[...]
    for h in range(N_HEADS):
        acc = jnp.zeros(bk.shape, F32)
        for bkt in range(REL_BUCKETS):
            acc = jnp.where(bk == bkt, rb_ref[bkt, h], acc)
        o_ref[h] = acc


def _strip_geometry(step):
    origin = -(-(step + REL_MAX_DIST - 1) // 128) * 128
    return origin, origin + step


def _strip_offset(step, delta):
    origin, _ = _strip_geometry(step)
    return pl.multiple_of(jnp.clip(origin - delta, 0, origin), 128)


def _bias_strip(rel_bias, qb, step):
    origin, width = _strip_geometry(step)
    qi = np.arange(qb)[:, None]
    z = np.arange(width)[None, :]
    dist = np.clip(qi - z + origin, 0, REL_MAX_DIST)
    bk = jnp.asarray(_bucket_table()[dist])
    out = pl.pallas_call(
        _bias_strip_kernel,
        in_specs=[pl.BlockSpec(memory_space=pltpu.VMEM), pl.BlockSpec(memory_space=pltpu.SMEM)],
        out_specs=pl.BlockSpec(memory_space=pltpu.VMEM),
        out_shape=jax.ShapeDtypeStruct((N_HEADS, qb, width), F32),
        name="bias_strip",
    )(bk, rel_bias)
    return out


def _attn_kernel(mode, qb, lp, q_pos0, k_sel, *refs):
    if mode == "dsa":
        (q_ref, kt_ref, v_ref, strip_ref, qi_ref, wi_ref, kit_ref, o_ref,
         qs_s, m_s, l_s, acc_s, key_s) = refs
    else:
        (q_ref, kt_ref, v_ref, strip_ref, kmt_ref, o_ref,
         qs_s, m_s, l_s, acc_s) = refs
    rows = GROUP * qb
    i = pl.program_id(1)
    q0 = q_pos0 + i * qb
    c_max = (q0 + qb - 1) // KEY_CHUNK
    n_chunks = c_max + 1
    qpos = q0 + lax.broadcasted_iota(I32, (qb, 1), 0)
    lane = lax.broadcasted_iota(I32, (1, KEY_CHUNK), 1)
    scale = HEAD_DIM ** -0.5

    for g in range(N_KV):
        qs_s[g] = (q_ref[g] * scale).astype(BF16)
    m_s[...] = jnp.full(m_s.shape, NEG, F32)
    l_s[...] = jnp.zeros(l_s.shape, F32)
    acc_s[...] = jnp.zeros(acc_s.shape, F32)

    if mode == "dsa":
        qi = qi_ref[...].astype(BF16)
        wi = wi_ref[...]

        def score_chunk(c, carry):
            k0 = pl.multiple_of(c * KEY_CHUNK, KEY_CHUNK)
            s = jnp.dot(qi, kit_ref[:, pl.ds(k0, KEY_CHUNK)], preferred_element_type=F32)
            s = jnp.maximum(s, 0.0) * wi
            sc = s[0:qb]
            for h in range(1, IDX_HEADS):
                sc = sc + s[h * qb:(h + 1) * qb]
            sc = jnp.where(k0 + lane <= qpos, sc, -jnp.inf)
            bits = pltpu.bitcast(sc, I32)
            bits = jnp.where(bits == INT_MIN, 0, bits)
            key_s[:, pl.ds(k0, KEY_CHUNK)] = jnp.where(bits < 0, bits ^ 0x7FFFFFFF, bits)
            return carry

        scan = STEP_KEYS if qb <= 32 else ROWS_STEP
        n_scan = (n_chunks * KEY_CHUNK + scan - 1) // scan
        lax.fori_loop(0, n_scan * (scan // KEY_CHUNK), score_chunk, 0)
        scan_lane = lax.broadcasted_iota(I32, (1, scan), 1)

        def count(pred):
            def body(c, acc):
                k0 = pl.multiple_of(c * scan, scan)
                return acc + pred(key_s[:, pl.ds(k0, scan)], k0 + scan_lane).astype(I32)
            acc = lax.fori_loop(0, n_scan, body, jnp.zeros((qb, scan), I32))
            return jnp.sum(acc, axis=1, keepdims=True)

        def bit_step(bi, thr):
            cand = thr + jnp.left_shift(jnp.int32(1), 31 - bi)
            cnt = count(lambda k, kidx: k >= cand)
            return jnp.where(cnt >= k_sel, cand, thr)

        thr = lax.fori_loop(0, 32, bit_step, jnp.full((qb, 1), INT_MIN, I32))
        n_gt = count(lambda k, kidx: k > thr)
        n_ge = count(lambda k, kidx: k >= thr)
        need = k_sel - n_gt

        def tie_limit():
            n_bits = max(1, (lp - 1).bit_length())

            def idx_step(bi, lo):
                cand = lo + jnp.left_shift(jnp.int32(1), n_bits - 1 - bi)
                cnt = count(lambda k, kidx: (k == thr) & (kidx <= cand))
                return jnp.where(cnt < need, cand, lo)

            lo = lax.fori_loop(0, n_bits, idx_step, jnp.full((qb, 1), -1, I32))
            return lo + 1

        tie_idx = lax.cond(jnp.max(n_ge - n_gt - need) > 0, tie_limit,
                           lambda: jnp.full((qb, 1), lp, I32))
    else:
        blk = lax.broadcasted_iota(I32, (qb, kmt_ref.shape[2]), 1)
        sel_blocks = []
        for g in range(N_KV):
            qg = q_ref[g]
            qsum = qg[0:qb]
            for j in range(1, GROUP):
                qsum = qsum + qg[j * qb:(j + 1) * qb]
            gate = jnp.dot(qsum, kmt_ref[g], preferred_element_type=F32,
                           precision=lax.Precision.HIGHEST)
            gate = jnp.where(blk < c_max, gate, -jnp.inf)
            sel = jnp.zeros(blk.shape, jnp.bool_)
            for _ in range(MOBA_TOPK):
                mx = jnp.max(gate, axis=1, keepdims=True)
                first = jnp.min(jnp.where(gate == mx, blk, blk.shape[1]), axis=1, keepdims=True)
                hit = blk == first
                sel = sel | (hit & (blk < c_max))
                gate = jnp.where(hit, -jnp.inf, gate)
            sel_blocks.append(sel.astype(I32))

    step_lane = lax.broadcasted_iota(I32, (1, ROWS_STEP), 1)

    def attend(c, carry):
        k0 = pl.multiple_of(c * ROWS_STEP, ROWS_STEP)
        kidx = k0 + step_lane
        causal = kidx <= qpos
        off = _strip_offset(ROWS_STEP, q0 - k0)
        if mode == "dsa":
            k = key_s[:, pl.ds(k0, ROWS_STEP)]
            mask1 = ((k > thr) | ((k == thr) & (kidx <= tie_idx))) & causal
            mask = jnp.concatenate([mask1] * GROUP, axis=0)

        logits = [jnp.dot(qs_s[g], kt_ref[g, :, pl.ds(k0, ROWS_STEP)], preferred_element_type=F32)
                  for g in range(N_KV)]
        probs, alphas = [], []
        for g in range(N_KV):
            if mode != "dsa":
                parts = []
                for i in range(ROWS_STEP // KEY_CHUNK):
                    cb = c * (ROWS_STEP // KEY_CHUNK) + i
                    picked = jnp.sum(jnp.where(blk == cb, sel_blocks[g], 0), axis=1, keepdims=True)
                    own = causal[:, i * KEY_CHUNK:(i + 1) * KEY_CHUNK].astype(I32)
                    parts.append(jnp.where(cb == c_max, own, picked))
                mask1 = jnp.concatenate(parts, axis=1) > 0
                mask = jnp.concatenate([mask1] * GROUP, axis=0)
            s = jnp.where(mask, logits[g] + strip_ref[g, :, pl.ds(off, ROWS_STEP)], NEG)
            m_old = m_s[g]
            m_new = jnp.maximum(m_old, jnp.max(s, axis=1, keepdims=True))
            alpha = jnp.exp(m_old - m_new)
            p = jnp.exp(s - m_new)
            l_s[g] = alpha * l_s[g] + jnp.sum(p, axis=1, keepdims=True)
            m_s[g] = m_new
            probs.append(p.astype(BF16))
            alphas.append(alpha)
        vblk = v_ref[pl.ds(k0, ROWS_STEP), :]
        for g in range(N_KV):
            acc_s[g] = alphas[g] * acc_s[g] + jnp.dot(probs[g], vblk, preferred_element_type=F32)
        return carry

    lax.fori_loop(0, (n_chunks * KEY_CHUNK + ROWS_STEP - 1) // ROWS_STEP, attend, 0)
    for g in range(N_KV):
        o_ref[g] = acc_s[g][:, g * HEAD_DIM:(g + 1) * HEAD_DIM] / l_s[g]


def _sparse_attention(mode, q_st, kt, v, strip, extra, qb, q_pos0, k_sel=0):
    bsz, nq = q_st.shape[:2]
    lp = v.shape[1]
    rows = GROUP * qb
    assert q_pos0 % ROWS_STEP == 0 and (nq == 1 or qb % ROWS_STEP == 0) and lp % ROWS_STEP == 0
    per_q = lambda *tail: pl.BlockSpec((None, None) + tail, lambda b, i: (b, i) + (0,) * len(tail))
    per_b = lambda *tail: pl.BlockSpec((None,) + tail, lambda b, i: (b,) + (0,) * len(tail))
    in_specs = [per_q(N_KV, rows, HEAD_DIM), per_b(N_KV, HEAD_DIM, lp), per_b(lp, N_KV * HEAD_DIM),
                pl.BlockSpec(strip.shape, lambda b, i: (0, 0, 0))]
    scratch = [pltpu.VMEM((N_KV, rows, HEAD_DIM), BF16),
               pltpu.VMEM((N_KV, rows, 1), F32),
               pltpu.VMEM((N_KV, rows, 1), F32),
               pltpu.VMEM((N_KV, rows, N_KV * HEAD_DIM), F32)]
    if mode == "dsa":
        qi_st, wi_st, kit = extra
        in_specs += [per_q(IDX_HEADS * qb, IDX_DIM), per_q(IDX_HEADS * qb, 1), per_b(IDX_DIM, lp)]
        scratch.append(pltpu.VMEM((qb, lp), I32))
    else:
        (kmt,) = extra
        in_specs.append(per_b(N_KV, HEAD_DIM, kmt.shape[3]))
    return pl.pallas_call(
        functools.partial(_attn_kernel, mode, qb, lp, q_pos0, k_sel),
        grid=(bsz, nq),
        in_specs=in_specs,
        out_specs=per_q(N_KV, rows, HEAD_DIM),
        out_shape=jax.ShapeDtypeStruct((bsz, nq, N_KV, rows, HEAD_DIM), F32),
        scratch_shapes=scratch,
        compiler_params=_cparams(("parallel", "arbitrary")),
        name=mode + "_attention",
    )(q_st, kt, v, strip, *extra)


QT = 128


def _attn_t_kernel(mode, lp, k_sel, *refs):
    if mode == "dsa":
        (qt_ref, k_ref, vt_ref, strip_ref, qit_ref, wit_ref, ki_ref, o_ref,
         qbd_s, m_s, l_s, acc_s, key_s) = refs
    else:
        (qt_ref, k_ref, vt_ref, strip_ref, km_ref, o_ref,
         qbd_s, m_s, l_s, acc_s) = refs
    kv_dim = N_KV * HEAD_DIM
    cols = GROUP * QT
    i = pl.program_id(1)
    q0 = i * QT
    c_max = (q0 + QT - 1) // KEY_CHUNK
    n_chunks = c_max + 1
    qpos = q0 + lax.broadcasted_iota(I32, (1, QT), 1)
    krow = lax.broadcasted_iota(I32, (KEY_CHUNK, 1), 0)
    scale = HEAD_DIM ** -0.5

    qbd_s[...] = jnp.zeros(qbd_s.shape, BF16)
    for g in range(N_KV):
        qbd_s[g * HEAD_DIM:(g + 1) * HEAD_DIM, g * cols:(g + 1) * cols] = (
            qt_ref[g * HEAD_DIM:(g + 1) * HEAD_DIM, :] * scale).astype(BF16)
    m_s[...] = jnp.full(m_s.shape, NEG, F32)
    l_s[...] = jnp.zeros(l_s.shape, F32)
    acc_s[...] = jnp.zeros(acc_s.shape, F32)

    if mode == "dsa":
        qit = qit_ref[...].astype(BF16)
        wit = wit_ref[...]

        def score_chunk(c, carry):
            k0 = pl.multiple_of(c * KEY_CHUNK, KEY_CHUNK)
            kib = ki_ref[pl.ds(k0, KEY_CHUNK), :]
            sc = jnp.zeros((KEY_CHUNK, QT), F32)
            for hp in range(IDX_HEADS // 2):
                lanes = slice(2 * hp * QT, 2 * (hp + 1) * QT)
                s = jnp.maximum(jnp.dot(kib, qit[:, lanes], preferred_element_type=F32), 0.0) * wit[:, lanes]
                sc = sc + s[:, :QT] + s[:, QT:]
            sc = jnp.where(k0 + krow <= qpos, sc, -jnp.inf)
            bits = pltpu.bitcast(sc, I32)
            bits = jnp.where(bits == INT_MIN, 0, bits)
            key_s[pl.ds(k0, KEY_CHUNK), :] = jnp.where(bits < 0, bits ^ 0x7FFFFFFF, bits)
            return carry

        lax.fori_loop(0, n_chunks, score_chunk, 0)

        def count(pred):
            def body(c, acc):
                k0 = pl.multiple_of(c * KEY_CHUNK, KEY_CHUNK)
                hit = pred(key_s[pl.ds(k0, KEY_CHUNK), :], k0).astype(I32)
                return acc + jnp.sum(hit.reshape(KEY_CHUNK // 8, 8, QT), axis=0)
            acc = lax.fori_loop(0, n_chunks, body, jnp.zeros((8, QT), I32))
            return jnp.sum(acc, axis=0, keepdims=True)

        def bit_step(bi, thr):
            cand = thr + jnp.left_shift(jnp.int32(1), 31 - bi)
            cnt = count(lambda k, k0: k >= cand)
            return jnp.where(cnt >= k_sel, cand, thr)

        thr = lax.fori_loop(0, 32, bit_step, jnp.full((1, QT), INT_MIN, I32))
        n_gt = count(lambda k, k0: k > thr)
        n_ge = count(lambda k, k0: k >= thr)
        need = k_sel - n_gt

        def tie_limit():
            n_bits = max(1, (lp - 1).bit_length())

            def idx_step(bi, lo):
                cand = lo + jnp.left_shift(jnp.int32(1), n_bits - 1 - bi)
                cnt = count(lambda k, k0: (k == thr) & (k0 + krow <= cand))
                return jnp.where(cnt < need, cand, lo)

            lo = lax.fori_loop(0, n_bits, idx_step, jnp.full((1, QT), -1, I32))
            return lo + 1

        tie_idx = lax.cond(jnp.max(n_ge - n_gt - need) > 0, tie_limit,
                           lambda: jnp.full((1, QT), lp, I32))
    else:
        nbp = km_ref.shape[1]
        blk = lax.broadcasted_iota(I32, (nbp, QT), 0)
        sel_blocks = []
        for g in range(N_KV):
            qg = qt_ref[g * HEAD_DIM:(g + 1) * HEAD_DIM, :]
            qsum = qg[:, 0:QT]
            for j in range(1, GROUP):
                qsum = qsum + qg[:, j * QT:(j + 1) * QT]
            gate = jnp.dot(km_ref[g], qsum, preferred_element_type=F32, precision=lax.Precision.HIGHEST)
            gate = jnp.where(blk < c_max, gate, -jnp.inf)
            sel = jnp.zeros(blk.shape, I32)
            for _ in range(MOBA_TOPK):
                mx = jnp.max(gate, axis=0, keepdims=True)
                first = jnp.min(jnp.where(gate == mx, blk, nbp), axis=0, keepdims=True)
                hit = blk == first
                sel = jnp.where(hit & (blk < c_max), 1, sel)
                gate = jnp.where(hit, -jnp.inf, gate)
            sel_blocks.append(sel)

    def attend(c, carry):
        k0 = pl.multiple_of(c * KEY_CHUNK, KEY_CHUNK)
        kidx = k0 + krow
        causal = kidx <= qpos
        off = _strip_offset(KEY_CHUNK, q0 - k0)
        if mode == "dsa":
            k = key_s[pl.ds(k0, KEY_CHUNK), :]
            mask = ((k > thr) | ((k == thr) & (kidx <= tie_idx))) & causal
        kblk = k_ref[pl.ds(k0, KEY_CHUNK), :]

        def logits(g):
            return jnp.dot(kblk, qbd_s[:, g * cols:(g + 1) * cols], preferred_element_type=F32)

        s_next = logits(0)
        for g in range(N_KV):
            if mode != "dsa":
                picked = jnp.sum(jnp.where(blk == c, sel_blocks[g], 0), axis=0, keepdims=True)
                mask = jnp.where(c == c_max, causal.astype(I32), picked) > 0
            s_all = s_next
            if g + 1 < N_KV:
                s_next = logits(g + 1)
            probs, alphas = [], []
            for j in range(GROUP):
                h = g * GROUP + j
                lanes = slice(j * QT, (j + 1) * QT)
                s = s_all[:, lanes] + strip_ref[g, pl.ds(off, KEY_CHUNK), lanes]
                s = jnp.where(mask, s, NEG)
                m_old = m_s[pl.ds(h, 1), :]
                m_new = jnp.maximum(m_old, jnp.max(s, axis=0, keepdims=True))
                alpha = jnp.exp(m_old - m_new)
                p = jnp.exp(s - m_new)
                l_s[pl.ds(h, 1), :] = alpha * l_s[pl.ds(h, 1), :] + jnp.sum(p, axis=0, keepdims=True)
                m_s[pl.ds(h, 1), :] = m_new
                probs.append(p.astype(BF16))
                alphas.append(alpha)
            pv = jnp.dot(vt_ref[g * HEAD_DIM:(g + 1) * HEAD_DIM, pl.ds(k0, KEY_CHUNK)],
                         jnp.concatenate(probs, axis=1), preferred_element_type=F32)
            acc_s[g] = jnp.concatenate(alphas, axis=1) * acc_s[g] + pv
        return carry

    lax.fori_loop(0, n_chunks, attend, 0)
    for g in range(N_KV):
        for j in range(0, GROUP, 2):
            h = g * GROUP + j
            pair = [acc_s[g][:, (j + i) * QT:(j + i + 1) * QT] * (1.0 / l_s[pl.ds(h + i, 1), :]) for i in range(2)]
            o_ref[:, h * HEAD_DIM:(h + 2) * HEAD_DIM] = jnp.concatenate(pair, axis=0).T


def _sparse_attention_t(mode, qt, k, vt, strip_t, extra, k_sel=0):
    bsz, nq = qt.shape[:2]
    lp = k.shape[1]
    kv_dim = N_KV * HEAD_DIM
    cols = GROUP * QT
    per_q = lambda *tail: pl.BlockSpec((None, None) + tail, lambda b, i: (b, i) + (0,) * len(tail))
    per_b = lambda *tail: pl.BlockSpec((None,) + tail, lambda b, i: (b,) + (0,) * len(tail))
    in_specs = [per_q(kv_dim, cols), per_b(lp, kv_dim), per_b(kv_dim, lp),
                pl.BlockSpec(strip_t.shape, lambda b, i: (0, 0, 0))]
    scratch = [pltpu.VMEM((kv_dim, N_KV * cols), BF16),
               pltpu.VMEM((N_HEADS, QT), F32),
               pltpu.VMEM((N_HEADS, QT), F32),
               pltpu.VMEM((N_KV, HEAD_DIM, cols), F32)]
    if mode == "dsa":
        qit, wit, ki = extra
        in_specs += [per_q(IDX_DIM, IDX_HEADS * QT), per_q(1, IDX_HEADS * QT), per_b(lp, IDX_DIM)]
        scratch.append(pltpu.VMEM((lp, QT), I32))
    else:
        (km,) = extra
        in_specs.append(per_b(N_KV, km.shape[2], HEAD_DIM))
    return pl.pallas_call(
        functools.partial(_attn_t_kernel, mode, lp, k_sel),
        grid=(bsz, nq),
        in_specs=in_specs,
        out_specs=per_q(QT, N_HEADS * HEAD_DIM),
        out_shape=jax.ShapeDtypeStruct((bsz, nq, QT, N_HEADS * HEAD_DIM), F32),
        scratch_shapes=scratch,
        compiler_params=_cparams(("parallel", "arbitrary")),
        name=mode + "_attention_t",
    )(qt, k, vt, strip_t, *extra)


def _top_rows(s, k, payload=None, rank=None):
    rid = lax.broadcasted_iota(I32, s.shape, 0) if rank is None else rank
    vals, ids = [], []
    for _ in range(k):
        mx = jnp.max(s, axis=0, keepdims=True)
        first = jnp.min(jnp.where(s == mx, rid, jnp.iinfo(jnp.int32).max), axis=0, keepdims=True)
        hit = rid == first
        vals.append(mx)
        if payload is None:
            ids.append(first)
        else:
            ids.append(jnp.max(jnp.where(hit, payload, -1), axis=0, keepdims=True))
        s = jnp.where(hit, -jnp.inf, s)
    return jnp.concatenate(vals, axis=0), jnp.concatenate(ids, axis=0)


def _top_rows_paired(s, k):
    half = s.shape[0] // 2
    lo, hi = s[:half], s[half:]
    rid = lax.broadcasted_iota(I32, lo.shape, 0)
    first_lo = lo >= hi
    top = jnp.where(first_lo, lo, hi)
    rest = jnp.where(first_lo, hi, lo)
    top_id = jnp.where(first_lo, rid, rid + half)
    rest_id = jnp.where(first_lo, rid + half, rid)
    vals, ids = [], []
    for _ in range(k):
        mx = jnp.max(top, axis=0, keepdims=True)
        first = jnp.min(jnp.where(top == mx, top_id, jnp.iinfo(jnp.int32).max), axis=0, keepdims=True)
        hit = top_id == first
        vals.append(mx)
        ids.append(first)
        top = jnp.where(hit, rest, top)
        top_id = jnp.where(hit, rest_id, top_id)
        rest = jnp.where(hit, -jnp.inf, rest)
    return jnp.concatenate(vals, axis=0), jnp.concatenate(ids, axis=0)


def _peer_route_kernel(x_ref, wqt_ref, keys_ref, row_ref, par_ref, g_ref):
    xb = x_ref[...].astype(BF16)
    qt = lax.dot_general(wqt_ref[...], xb, (((1,), (1,)), ((), ())),
                         preferred_element_type=F32)
    n_t = x_ref.shape[0]
    piece = lax.broadcasted_iota(I32, (11 * 8, n_t), 0) // 8
    within = lax.broadcasted_iota(I32, (11 * 8, n_t), 0) % 8
    by_b = (piece >= 2) & (piece < 10)
    ca = jnp.where(piece < 2, 0, jnp.where(by_b, within, 8 + within))
    cb = jnp.where(piece == 0, within, jnp.where(piece == 1, 8 + within, jnp.where(by_b, piece - 2, 0)))
    cand_ok = ((ca + 1) * (cb + 1) <= PEER_TOPK) & jnp.logical_not(by_b & (within == 0))
    cand_rank = ca * PEER_TOPK + cb

    def pieces(first, second, combine):
        out = [combine(first[0:1], second[0:8]), combine(first[0:1], second[8:16])]
        out += [combine(first[0:8], second[b:b + 1]) for b in range(8)]
        out.append(combine(first[8:16], second[0:1]))
        return jnp.concatenate(out, axis=0)

    e_rows, g_rows = [], []
    for h in range(PEER_HEADS):
        sv, si = [], []
        for c in range(2):
            hc = 2 * h + c
            qhc = qt[hc * PEER_HALF:(hc + 1) * PEER_HALF].astype(BF16)
            s = jnp.dot(keys_ref[hc], qhc, preferred_element_type=F32)
            v_, i_ = _top_rows_paired(s, PEER_TOPK)
            sv.append(v_)
            si.append(i_)
        cand = jnp.where(cand_ok, pieces(sv[0], sv[1], lambda u, w: u + w), -jnp.inf)
        cidx = pieces(si[0], si[1], lambda u, w: u * PEER_NKEYS + w)
        gv, ge = _top_rows(cand, PEER_TOPK, payload=cidx, rank=cand_rank)
        ex = jnp.exp(gv - gv[0:1])
        g_rows.append(ex / jnp.sum(ex, axis=0, keepdims=True))
        e_rows.append(ge)
    e_t = jnp.concatenate(e_rows, axis=0).T
    row_ref[...] = jnp.right_shift(e_t, 1) * 8
    par_ref[...] = e_t & 1
    g_ref[...] = jnp.concatenate(g_rows, axis=0).T


def _peer_route(x, wqt_bf16, keys_bf16, tb):
    m, d = x.shape
    out = pl.BlockSpec((tb, PEER_PAIRS), lambda i: (i, 0))
    return pl.pallas_call(
        _peer_route_kernel,
        grid=(m // tb,),
        in_specs=[pl.BlockSpec((tb, d), lambda i: (i, 0)),
                  pl.BlockSpec(wqt_bf16.shape, lambda i: (0, 0)),
                  pl.BlockSpec(keys_bf16.shape, lambda i: (0, 0, 0))],
        out_specs=[out, out, out],
        out_shape=[jax.ShapeDtypeStruct((m, PEER_PAIRS), I32),
                   jax.ShapeDtypeStruct((m, PEER_PAIRS), I32),
                   jax.ShapeDtypeStruct((m, PEER_PAIRS), F32)],
        compiler_params=_cparams(("parallel",)),
        name="peer_route",
    )(x, wqt_bf16, keys_bf16)


FOLD_PAIRS = 32
HI_MASK = 0xFFFF0000
UP_SLOTS = 8
TILE_ROWS = 8


def _pack_table(w):
    n, d = w.shape
    bits = lax.bitcast_convert_type(w.astype(BF16), jnp.uint16).astype(jnp.uint32)
    bits = bits.reshape(n // 2, 2, d // 128, 128)
    return (bits[:, 0] | (bits[:, 1] << 16)).astype(jnp.uint32).reshape(n // 2 * (d // 128), 128)


def _load_tile(tbl_ref, row_ref, idx):
    return tbl_ref[pl.ds(pl.multiple_of(row_ref[idx], TILE_ROWS), TILE_ROWS), :]


def _peer_up_kernel(tb, row_ref, par_ref, x_ref, g_ref, fold_ref, tbl_ref, c_ref, prod_s):
    diag = (lax.broadcasted_iota(I32, (PEER_PAIRS, PEER_PAIRS), 0)
            == lax.broadcasted_iota(I32, (PEER_PAIRS, PEER_PAIRS), 1))

    def gather(t, slot):
        xt = x_ref[t]
        rows_t = row_ref.at[pl.ds(t * PEER_PAIRS, PEER_PAIRS)]
        for p in range(0, PEER_PAIRS, 2):
            w0 = _load_tile(tbl_ref, rows_t, p)
            w1 = _load_tile(tbl_ref, rows_t, p + 1)
            even = [pltpu.bitcast(jnp.left_shift(w, jnp.uint32(16)), F32) * xt for w in (w0, w1)]
            odd = [pltpu.bitcast(w & jnp.uint32(HI_MASK), F32) * xt for w in (w0, w1)]
            rows = pl.ds(p * TILE_ROWS, 2 * TILE_ROWS)
            prod_s[slot, rows, 0:128] = jnp.concatenate(even, axis=0).astype(BF16)
            prod_s[slot, rows, 128:256] = jnp.concatenate(odd, axis=0).astype(BF16)

    def reduce(t, slot):
        depth = FOLD_PAIRS * TILE_ROWS
        parts = [jnp.dot(fold_ref[...], prod_s[slot, pl.ds(i * depth, depth), :], preferred_element_type=F32)
                 for i in range(PEER_PAIRS // FOLD_PAIRS)]
        part = jnp.concatenate(parts, axis=0)
        rows = []
        for half in range(2):
            col = jnp.sum(part[:, half * 128:(half + 1) * 128], axis=1, keepdims=True)
            rows.append(jnp.sum(jnp.where(diag, col, 0.0), axis=0, keepdims=True))
        a = jnp.where(par_ref[pl.ds(t, 1), :] == 1, rows[1], rows[0])
        act = 0.5 * a * (1.0 + lax.erf(a * (2.0 ** -0.5)))
        c_ref[pl.ds(t, 1), :] = g_ref[pl.ds(t, 1), :] * act

    prod_s[UP_SLOTS:2 * UP_SLOTS] = jnp.zeros((UP_SLOTS,) + prod_s.shape[1:], BF16)

    def step(k, carry):
        t0 = 2 * UP_SLOTS * k
        for j in range(UP_SLOTS):
            gather(t0 + j, j)
            reduce(jnp.maximum(t0 - UP_SLOTS + j, 0), UP_SLOTS + j)
        for j in range(UP_SLOTS):
            gather(t0 + UP_SLOTS + j, UP_SLOTS + j)
            reduce(t0 + j, j)
        return carry

    lax.fori_loop(0, tb // (2 * UP_SLOTS), step, 0)
    for j in range(UP_SLOTS):
        reduce(tb - UP_SLOTS + j, UP_SLOTS + j)


DOWN_GROUP = 8


def _peer_down_kernel(tb, row_ref, par_ref, c_ref, spread_ref, tbl_ref, f_ref, tile_s, coef_s):
    width = PEER_PAIRS * 2 * TILE_ROWS
    lane = lax.broadcasted_iota(I32, (TILE_ROWS, width), 1)
    own_row = ((lane & (2 * TILE_ROWS - 1)) >> 1) == lax.broadcasted_iota(I32, (TILE_ROWS, width), 0)
    lane_par = (lax.broadcasted_iota(I32, (DOWN_GROUP, width), 1) & 1).astype(F32)

    def gather(t, slot):
        rows_t = row_ref.at[pl.ds(t * PEER_PAIRS, PEER_PAIRS)]
        for p in range(PEER_PAIRS):
            tile_s[slot, pl.ds(p * TILE_ROWS, TILE_ROWS), :] = _load_tile(tbl_ref, rows_t, p)

    def apply(t, j, slot):
        tiles = pltpu.bitcast(tile_s[slot], BF16)
        lhs = [jnp.where(own_row, coef_s[pl.ds(part * DOWN_GROUP + j, 1), :], 0.0) for part in range(2)]
        out = jnp.dot(jnp.concatenate(lhs, axis=0).astype(BF16), tiles, preferred_element_type=F32)
        f_ref[t] = out[0:TILE_ROWS] + out[TILE_ROWS:2 * TILE_ROWS]

    def group(gi, carry):
        t0 = pl.multiple_of(gi * DOWN_GROUP, DOWN_GROUP)
        gather(t0, 0)
        c = c_ref[pl.ds(t0, DOWN_GROUP), :]
        c_hi = c.astype(BF16)
        c_lo = (c - c_hi.astype(F32)).astype(BF16)
        par = par_ref[pl.ds(t0, DOWN_GROUP), :].astype(BF16)
        rep = jnp.dot(jnp.concatenate([c_hi, c_lo, par], axis=0), spread_ref[...],
                      preferred_element_type=F32)
        keep = rep[2 * DOWN_GROUP:] == lane_par
        coef_s[0:DOWN_GROUP, :] = jnp.where(keep, rep[0:DOWN_GROUP], 0.0)
        coef_s[DOWN_GROUP:2 * DOWN_GROUP, :] = jnp.where(keep, rep[DOWN_GROUP:2 * DOWN_GROUP], 0.0)
        for j in range(DOWN_GROUP):
            if j + 1 < DOWN_GROUP:
                gather(t0 + j + 1, (j + 1) % 2)
            apply(t0 + j, j, j % 2)
        return carry

    lax.fori_loop(0, tb // DOWN_GROUP, group, 0)


def _peer_experts(x, row, par, g, u_tbl, v_tbl, tb):
    m, d = x.shape
    assert m % tb == 0 and tb % (2 * UP_SLOTS) == 0 and tb % DOWN_GROUP == 0 and d == TILE_ROWS * 128
    x3 = x.reshape(m, d // 128, 128)
    flat = row.reshape(m * PEER_PAIRS)
    smem = pl.BlockSpec((tb * PEER_PAIRS,), lambda i: (i,), memory_space=pltpu.SMEM)
    vrow = pl.BlockSpec((tb, PEER_PAIRS), lambda i: (i, 0))
    tile = pl.BlockSpec((tb, d // 128, 128), lambda i: (i, 0, 0))
    resident = pl.BlockSpec(memory_space=pltpu.VMEM)
    const = lambda a: pl.BlockSpec(a.shape, lambda i: (0, 0))
    fold = jnp.asarray(np.arange(FOLD_PAIRS * TILE_ROWS)[None, :] // TILE_ROWS == np.arange(FOLD_PAIRS)[:, None], BF16)
    width = PEER_PAIRS * 2 * TILE_ROWS
    spread = jnp.asarray(np.arange(width)[None, :] // (2 * TILE_ROWS) == np.arange(PEER_PAIRS)[:, None], BF16)
    c = pl.pallas_call(
        functools.partial(_peer_up_kernel, tb),
        grid=(m // tb,),
        in_specs=[smem, vrow, tile, vrow, const(fold), resident],
        out_specs=vrow,
        out_shape=jax.ShapeDtypeStruct((m, PEER_PAIRS), F32),
        scratch_shapes=[pltpu.VMEM((2 * UP_SLOTS, PEER_PAIRS * TILE_ROWS, 256), BF16)],
        compiler_params=_cparams(("arbitrary",)),
        name="peer_up",
    )(flat, par, x3, g, fold, u_tbl)
    f = pl.pallas_call(
        functools.partial(_peer_down_kernel, tb),
        grid=(m // tb,),
        in_specs=[smem, vrow, vrow, const(spread), resident],
        out_specs=tile,
        out_shape=jax.ShapeDtypeStruct((m, d // 128, 128), F32),
        scratch_shapes=[pltpu.VMEM((2, PEER_PAIRS * TILE_ROWS, 128), jnp.uint32),
                        pltpu.VMEM((2 * DOWN_GROUP, width), F32)],
        compiler_params=_cparams(("arbitrary",)),
        name="peer_down",
    )(flat, par, c, spread, v_tbl)
    return f.reshape(m, d)


def _stack_heads(a, qb, n_lead):
    bsz, q, n, hd = a.shape
    n_in = n // n_lead
    a = a.reshape(bsz, q // qb, qb, n_lead, n_in, hd)
    return a.transpose(0, 1, 3, 4, 2, 5).reshape(bsz, q // qb, n_lead, n_in * qb, hd)


def _unstack_heads(o, qb):
    bsz, nq = o.shape[:2]
    o = o.reshape(bsz, nq, N_KV, GROUP, qb, HEAD_DIM).transpose(0, 1, 4, 2, 3, 5)
    return o.reshape(bsz * nq * qb, N_HEADS * HEAD_DIM)


def _run_trunk(x, p, past, prm, qb, tm, tb_route, tb_exp):
    bsz, q_len, d = x.shape
    depth = prm["w_o"].shape[0]
    m = bsz * q_len
    tm, tb_route, tb_exp = min(tm, m), min(tb_route, m), min(tb_exp, m)
    alpha = (2 * depth) ** 0.25
    attn_dim = N_HEADS * HEAD_DIM
    kv_dim = N_KV * HEAD_DIM
    if past is None:
        q_pos0 = 0
        n_pages = q_len // PAGE
        table = jnp.arange(bsz * n_pages, dtype=I32).reshape(bsz, n_pages)
    else:
        cache_k, cache_v, cache_ki, table = past
        q_pos0 = table.shape[1] * PAGE
    on_lanes = past is None and q_len % QT == 0
    if on_lanes:
        qb = QT
    nq = q_len // qb
    strip = _bias_strip(prm["rel_bias"], qb, KEY_CHUNK if on_lanes else ROWS_STEP)
    strip = strip.reshape(N_KV, GROUP, qb, strip.shape[2])
    if on_lanes:
        strip = strip.transpose(0, 3, 1, 2).reshape(N_KV, strip.shape[3], GROUP * qb)
    else:
        strip = strip.reshape(N_KV, GROUP * qb, strip.shape[3])
    xf = x.reshape(m, d)
    ks, vs, kis = [], [], []

    def relayout(new_rows, cache, layer, want):
        w = new_rows.shape[2]
        if past is None:
            return _relayout(new_rows.reshape(bsz * n_pages, PAGE, w), table, None, False, want)
        n_pool = cache.shape[1]
        pool = cache.reshape((cache.shape[0] * n_pool,) + cache.shape[2:])
        pool_t = jnp.moveaxis(pool, 1, -1).reshape(pool.shape[0], w, PAGE)
        pad = jnp.zeros((bsz, STEP_KEYS - q_len, w), F32)
        return _relayout(pool_t, table + layer * n_pool, jnp.concatenate([new_rows, pad], axis=1), True, want)

    for l in range(depth):
        dsa = l % 2 == 0
        w_in = prm["w_in_dsa"][l // 2] if dsa else prm["w_in_moba"][l // 2]
        if on_lanes:
            proj = _in_proj_lanes(xf, w_in, dsa, tm)
            q_in = proj[0].reshape(bsz, nq, kv_dim, GROUP * qb)
            k = proj[1].reshape(bsz, q_len, kv_dim)
            v = proj[2].reshape(bsz, q_len, kv_dim)
        else:
            n_in = w_in.shape[1]
            n_pad = -(-n_in // 128) * 128
            proj = _matmul(xf, jnp.pad(w_in, ((0, 0), (0, n_pad - n_in))).astype(BF16), tm)
            qh = proj[:, :attn_dim].reshape(bsz, q_len, N_HEADS, HEAD_DIM)
            k = proj[:, attn_dim:attn_dim + kv_dim].reshape(bsz, q_len, kv_dim)
            v = proj[:, attn_dim + kv_dim:attn_dim + 2 * kv_dim].reshape(bsz, q_len, kv_dim)
        ks.append(k.reshape(bsz, q_len, N_KV, HEAD_DIM))
        vs.append(v.reshape(bsz, q_len, N_KV, HEAD_DIM))
        k_lay = relayout(k, None if past is None else cache_k, l,
                         (("x",) if on_lanes else ("xt",)) + (() if dsa else ("sums",)))
        v_lay = relayout(v, None if past is None else cache_v, l, ("xt",) if on_lanes else ("x",))
        if on_lanes:
            kr, vt = k_lay["x"], v_lay["xt"]
        else:
            kt = k_lay["xt"].reshape(bsz, N_KV, HEAD_DIM, k_lay["xt"].shape[2])
            vr = v_lay["x"]
            q_in = _stack_heads(qh, qb, N_KV)
        if dsa:
            if on_lanes:
                ki = proj[3].reshape(bsz, q_len, IDX_DIM)
            else:
                off = attn_dim + 2 * kv_dim
                qi = proj[:, off:off + IDX_HEADS * IDX_DIM].reshape(bsz, q_len, IDX_HEADS, IDX_DIM)
                off += IDX_HEADS * IDX_DIM
                ki = proj[:, off:off + IDX_DIM].reshape(bsz, q_len, IDX_DIM)
                wi = proj[:, off + IDX_DIM:off + IDX_DIM + IDX_HEADS].reshape(bsz, q_len, IDX_HEADS, 1)
            kis.append(ki)
            ki_lay = relayout(ki, None if past is None else cache_ki, l // 2, ("x",) if on_lanes else ("xt",))
            k_sel = min(IDX_TOPK, (q_pos0 + q_len) // 4)
            if on_lanes:
                qi_t = proj[4].reshape(bsz, nq, IDX_DIM, IDX_HEADS * qb)
                wi_t = proj[5].reshape(bsz, nq, 1, IDX_HEADS * qb)
                o = _sparse_attention_t("dsa", q_in, kr, vt, strip, (qi_t, wi_t, ki_lay["x"]), k_sel)
            else:
                qi_st = _stack_heads(qi, qb, 1)[:, :, 0]
                wi_st = _stack_heads(wi, qb, 1)[:, :, 0]
                o = _sparse_attention("dsa", q_in, kt, vr, strip, (qi_st, wi_st, ki_lay["xt"]), qb, q_pos0, k_sel)
        else:
            ksum = k_lay["sums"]
            nb = ksum.shape[1]
            kmean = (ksum * (1.0 / MOBA_BLOCK)).reshape(bsz, nb, N_KV, HEAD_DIM)
            if on_lanes:
                km = jnp.pad(kmean.transpose(0, 2, 1, 3), ((0, 0), (0, 0), (0, -nb % 8), (0, 0)))
                o = _sparse_attention_t("moba", q_in, kr, vt, strip, (km,))
            else:
                kmt = jnp.pad(kmean.transpose(0, 2, 3, 1), ((0, 0), (0, 0), (0, 0), (0, -nb % 128)))
                o = _sparse_attention("moba", q_in, kt, vr, strip, (kmt,), qb, q_pos0)
        if on_lanes:
            o = o.reshape(m, attn_dim)
        else:
            o = _unstack_heads(o, qb)
        xf = _attn_out(o, prm["w_o"][l].astype(BF16), xf, prm["ln_g"][l, 0][None], prm["ln_b"][l, 0][None],
                       alpha, tm)
        row, par, g = _peer_route(xf, prm["peer_wq"][l].T.astype(BF16),
                                 prm["peer_keys"][l].reshape(2 * PEER_HEADS, PEER_NKEYS, PEER_HALF).astype(BF16),
                                 tb_route)
        f = _peer_experts(xf, row, par, g, _pack_table(prm["peer_u"][l]), _pack_table(prm["peer_v"][l]), tb_exp)
        xf = _layer_tail(xf, f, prm["ln_g"][l, 1][None], prm["ln_b"][l, 1][None],
                         prm["gate_w"][l].astype(BF16), prm["gate_b"][l][None],
                         p[l].reshape(m, -1), prm["ple_w"][l].astype(BF16), alpha, tm)
    return xf.reshape(bsz, q_len, d), jnp.stack(ks), jnp.stack(vs), jnp.stack(kis)


def kernel(x_prompt, x_sample, cache_k, cache_v, cache_ki, page_table, p_prompt, p_sample, w_in_dsa, w_in_moba, w_o, rel_bias, ln_g, ln_b, peer_wq, peer_keys, peer_u, peer_v, gate_w, gate_b, ple_w):
    prm = {
        "w_in_dsa": w_in_dsa, "w_in_moba": w_in_moba, "w_o": w_o, "rel_bias": rel_bias,
        "ln_g": ln_g, "ln_b": ln_b, "peer_wq": peer_wq, "peer_keys": peer_keys,
        "peer_u": peer_u, "peer_v": peer_v, "gate_w": gate_w, "gate_b": gate_b, "ple_w": ple_w,
    }
    q_sample = x_sample.shape[1]
    y_prompt, k_prompt, v_prompt, ki_prompt = _run_trunk(
        x_prompt, p_prompt, None, prm, qb=128, tm=512, tb_route=256, tb_exp=128)
    y_sample, k_sample, v_sample, ki_sample = _run_trunk(
        x_sample, p_sample, (cache_k, cache_v, cache_ki, page_table), prm,
        qb=q_sample, tm=256, tb_route=256, tb_exp=64)
    return (y_prompt, y_sample, k_prompt, v_prompt, ki_prompt, k_sample, v_sample, ki_sample)
```

```python
import functools
import math

import numpy as np
import jax
import jax.numpy as jnp
from jax import lax
from jax.experimental import pallas as pl
from jax.experimental.pallas import tpu as pltpu

F32 = jnp.float32
BF16 = jnp.bfloat16
I32 = jnp.int32

N_HEADS = 16
HEAD_DIM = 64
N_KV = 4
GROUP = 4
IDX_HEADS = 8
IDX_DIM = 64
IDX_TOPK = 256
MOBA_BLOCK = 256
MOBA_TOPK = 3
REL_BUCKETS = 32
REL_MAX_DIST = 128
PEER_HEADS = 8
PEER_NKEYS = 128
PEER_HALF = 128
PEER_TOPK = 16
PEER_PAIRS = PEER_HEADS * PEER_TOPK
LN_EPS = 1e-5
PAGE = 128

KEY_CHUNK = 256
ROWS_STEP = 2 * KEY_CHUNK
NEG = -1e30
INT_MIN = -2147483648
VMEM_LIMIT = 56 * 1024 * 1024


def _cparams(sem):
    return pltpu.CompilerParams(dimension_semantics=sem, vmem_limit_bytes=VMEM_LIMIT)


def _mm_kernel(a_ref, b_ref, o_ref):
    o_ref[...] = jnp.dot(a_ref[...].astype(BF16), b_ref[...], preferred_element_type=F32)


def _matmul(a, b_bf16, tm):
    m, k = a.shape
    n = b_bf16.shape[1]
    return pl.pallas_call(
        _mm_kernel,
        grid=(m // tm,),
        in_specs=[pl.BlockSpec((tm, k), lambda i: (i, 0)),
                  pl.BlockSpec((k, n), lambda i: (0, 0))],
        out_specs=pl.BlockSpec((tm, n), lambda i: (i, 0)),
        out_shape=jax.ShapeDtypeStruct((m, n), F32),
        compiler_params=_cparams(("parallel",)),
        name="in_proj",
    )(a, b_bf16)


def _in_proj_lanes_kernel(has_idx, *refs):
    if has_idx:
        x_ref, wqt_ref, wrow_ref, wqit_ref, wit_ref, qt_ref, k_ref, v_ref, ki_ref, qit_ref, wi_ref = refs
    else:
        x_ref, wqt_ref, wrow_ref, qt_ref, k_ref, v_ref = refs
    kv_dim = N_KV * HEAD_DIM
    xb = x_ref[...].astype(BF16)
    n_blk = xb.shape[0] // QT
    nt = (((1,), (1,)), ((), ()))
    rows = jnp.dot(xb, wrow_ref[...], preferred_element_type=F32)
    k_ref[...] = rows[:, 0:kv_dim]
    v_ref[...] = rows[:, kv_dim:2 * kv_dim]
    q_t = lax.dot_general(wqt_ref[...], xb, nt, preferred_element_type=F32)
    for i in range(n_blk):
        for g in range(N_KV):
            for j in range(GROUP):
                h = g * GROUP + j
                qt_ref[i, g * HEAD_DIM:(g + 1) * HEAD_DIM, j * QT:(j + 1) * QT] = (
                    q_t[h * HEAD_DIM:(h + 1) * HEAD_DIM, i * QT:(i + 1) * QT])
    if has_idx:
        ki_ref[...] = rows[:, 2 * kv_dim:2 * kv_dim + IDX_DIM]
        qi_t = lax.dot_general(wqit_ref[...], xb, nt, preferred_element_type=F32)
        wi_t = lax.dot_general(wit_ref[...], xb, nt, preferred_element_type=F32)
        for i in range(n_blk):
            for h in range(IDX_HEADS):
                qit_ref[i, :, h * QT:(h + 1) * QT] = qi_t[h * IDX_DIM:(h + 1) * IDX_DIM, i * QT:(i + 1) * QT]
                wi_ref[i, :, h * QT:(h + 1) * QT] = wi_t[h:h + 1, i * QT:(i + 1) * QT]


def _in_proj_lanes(x, w_in, has_idx, tm):
    m, d = x.shape
    attn_dim = N_HEADS * HEAD_DIM
    kv_dim = N_KV * HEAD_DIM
    n_blk = tm // QT
    w_rows = w_in[:, attn_dim:attn_dim + 2 * kv_dim]
    off = attn_dim + 2 * kv_dim
    if has_idx:
        w_ki = w_in[:, off + IDX_HEADS * IDX_DIM:off + IDX_HEADS * IDX_DIM + IDX_DIM]
        w_rows = jnp.concatenate([w_rows, w_ki, jnp.zeros((d, 128 - IDX_DIM), F32)], axis=1)
    args = [x, w_in[:, :attn_dim].T.astype(BF16), w_rows.astype(BF16)]
    whole = lambda a: pl.BlockSpec(a.shape, lambda i: (0, 0))
    in_specs = [pl.BlockSpec((tm, d), lambda i: (i, 0)), whole(args[1]), whole(args[2])]
    blk3 = lambda r, c: pl.BlockSpec((n_blk, r, c), lambda i: (i, 0, 0))
    rowb = lambda c: pl.BlockSpec((tm, c), lambda i: (i, 0))
    out_specs = [blk3(kv_dim, GROUP * QT), rowb(kv_dim), rowb(kv_dim)]
    out_shape = [jax.ShapeDtypeStruct((m // QT, kv_dim, GROUP * QT), F32),
                 jax.ShapeDtypeStruct((m, kv_dim), F32), jax.ShapeDtypeStruct((m, kv_dim), F32)]
    if has_idx:
        w_wi = w_in[:, off + IDX_HEADS * IDX_DIM + IDX_DIM:off + IDX_HEADS * IDX_DIM + IDX_DIM + IDX_HEADS]
        args += [w_in[:, off:off + IDX_HEADS * IDX_DIM].T.astype(BF16), w_wi.T.astype(BF16)]
        in_specs += [whole(args[3]), whole(args[4])]
        out_specs += [rowb(IDX_DIM), blk3(IDX_DIM, IDX_HEADS * QT), blk3(1, IDX_HEADS * QT)]
        out_shape += [jax.ShapeDtypeStruct((m, IDX_DIM), F32),
                      jax.ShapeDtypeStruct((m // QT, IDX_DIM, IDX_HEADS * QT), F32),
                      jax.ShapeDtypeStruct((m // QT, 1, IDX_HEADS * QT), F32)]
    return pl.pallas_call(
        functools.partial(_in_proj_lanes_kernel, has_idx),
        grid=(m // tm,),
        in_specs=in_specs,
        out_specs=out_specs,
        out_shape=out_shape,
        compiler_params=_cparams(("parallel",)),
        name="in_proj_lanes",
    )(*args)


def _layer_norm(y, g, b):
    mu = jnp.mean(y, axis=-1, keepdims=True)
    yc = y - mu
    var = jnp.mean(yc * yc, axis=-1, keepdims=True)
    return yc * lax.rsqrt(var + LN_EPS) * g + b


def _attn_out_kernel(alpha, o_ref, w_ref, x_ref, g_ref, b_ref, y_ref):
    y = alpha * x_ref[...] + jnp.dot(o_ref[...].astype(BF16), w_ref[...],
                                     preferred_element_type=F32)
    y_ref[...] = _layer_norm(y, g_ref[...], b_ref[...])


def _attn_out(o, w_bf16, x, g, b, alpha, tm):
    m, d = x.shape
    row = pl.BlockSpec((tm, d), lambda i: (i, 0))
    vec = pl.BlockSpec((1, d), lambda i: (0, 0))
    return pl.pallas_call(
        functools.partial(_attn_out_kernel, alpha),
        grid=(m // tm,),
        in_specs=[row, pl.BlockSpec((d, d), lambda i: (0, 0)), row, vec, vec],
        out_specs=row,
        out_shape=jax.ShapeDtypeStruct((m, d), F32),
        compiler_params=_cparams(("parallel",)),
        name="attn_out_ln",
    )(o, w_bf16, x, g, b)


def _layer_tail_kernel(alpha, x_ref, f_ref, g_ref, b_ref, gw_ref, gb_ref, p_ref, pw_ref, y_ref):
    x2 = _layer_norm(alpha * x_ref[...] + f_ref[...], g_ref[...], b_ref[...])
    z = jnp.dot(x2.astype(BF16), gw_ref[...], preferred_element_type=F32) + gb_ref[...]
    gate = 1.0 / (1.0 + jnp.exp(-z))
    e = jnp.dot(p_ref[...].astype(BF16), pw_ref[...], preferred_element_type=F32)
    y_ref[...] = x2 + gate * e


def _layer_tail(x, f, g, b, gw_bf16, gb, p, pw_bf16, alpha, tm):
    m, d = x.shape
    pd = p.shape[1]
    row = pl.BlockSpec((tm, d), lambda i: (i, 0))
    vec = pl.BlockSpec((1, d), lambda i: (0, 0))
    return pl.pallas_call(
        functools.partial(_layer_tail_kernel, alpha),
        grid=(m // tm,),
        in_specs=[row, row, vec, vec, pl.BlockSpec((d, d), lambda i: (0, 0)), vec,
                  pl.BlockSpec((tm, pd), lambda i: (i, 0)),
                  pl.BlockSpec((pd, d), lambda i: (0, 0))],
        out_specs=row,
        out_shape=jax.ShapeDtypeStruct((m, d), F32),
        compiler_params=_cparams(("parallel",)),
        name="layer_tail",
    )(x, f, g, b, gw_bf16, gb, p, pw_bf16)


STEP_PAGES = 8
STEP_KEYS = STEP_PAGES * PAGE
STEP_CHUNKS = STEP_KEYS // KEY_CHUNK


def _relayout_kernel(n_pool_steps, has_new, pages_t, want, *refs):
    page_refs = refs[1:1 + STEP_PAGES]
    new_ref = refs[1 + STEP_PAGES] if has_new else None
    outs = dict(zip(want, refs[len(refs) - len(want):]))
    j = pl.program_id(1)
    need_rows = "x" in want or "sums" in want

    def emit(rows, cols):
        if "xt" in want:
            outs["xt"][...] = (rows.T if cols is None else cols).astype(BF16)
        if need_rows and rows is None:
            rows = cols.T
        if "x" in want:
            outs["x"][...] = rows.astype(BF16)
        if "sums" in want:
            for i in range(STEP_CHUNKS):
                outs["sums"][pl.ds(i, 1), :] = jnp.sum(rows[i * KEY_CHUNK:(i + 1) * KEY_CHUNK], axis=0, keepdims=True)

    def from_pages():
        if pages_t:
            emit(None, jnp.concatenate([r[...] for r in page_refs], axis=1))
        else:
            emit(jnp.concatenate([r[...] for r in page_refs], axis=0), None)

    if has_new:
        pl.when(j < n_pool_steps)(from_pages)

        @pl.when(j >= n_pool_steps)
        def _():
            emit(new_ref[...], None)
    else:
        from_pages()


def _relayout(pool, table, new, pages_t, want):
    bsz, n_pages = table.shape
    w = pool.shape[1] if pages_t else pool.shape[2]
    n_pool_steps = n_pages // STEP_PAGES
    n_steps = n_pool_steps + (1 if new is not None else 0)
    lp = n_steps * STEP_KEYS
    last = n_pool_steps - 1

    def page_map(i):
        return lambda b, j, tbl: (tbl[b, STEP_PAGES * jnp.minimum(j, last) + i], 0, 0)

    page_block = (None, w, PAGE) if pages_t else (None, PAGE, w)
    in_specs = [pl.BlockSpec(page_block, page_map(i)) for i in range(STEP_PAGES)]
    args = [pool] * STEP_PAGES
    if new is not None:
        in_specs.append(pl.BlockSpec((None, STEP_KEYS, w), lambda b, j, tbl: (b, 0, 0)))
        args.append(new)
    specs = {"xt": (pl.BlockSpec((None, w, STEP_KEYS), lambda b, j, tbl: (b, 0, j)),
                    jax.ShapeDtypeStruct((bsz, w, lp), BF16)),
             "x": (pl.BlockSpec((None, STEP_KEYS, w), lambda b, j, tbl: (b, j, 0)),
                   jax.ShapeDtypeStruct((bsz, lp, w), BF16)),
             "sums": (pl.BlockSpec((None, None, STEP_CHUNKS, w), lambda b, j, tbl: (b, j, 0, 0)),
                      jax.ShapeDtypeStruct((bsz, n_steps, STEP_CHUNKS, w), F32))}
    outs = pl.pallas_call(
        functools.partial(_relayout_kernel, n_pool_steps, new is not None, pages_t, want),
        grid_spec=pltpu.PrefetchScalarGridSpec(
            num_scalar_prefetch=1,
            grid=(bsz, n_steps),
            in_specs=in_specs,
            out_specs=[specs[name][0] for name in want]),
        out_shape=[specs[name][1] for name in want],
        compiler_params=_cparams(("parallel", "arbitrary")),
        name="kv_relayout",
    )(table, *args)
    outs = dict(zip(want, outs))
    if "sums" in outs:
        outs["sums"] = outs["sums"].reshape(bsz, n_steps * STEP_CHUNKS, w)
    return outs


def _bucket_table():
    n = np.arange(REL_MAX_DIST + 1)
    exact = REL_BUCKETS // 2
    nf = np.maximum(n, 1).astype(np.float32)
    large = exact + (np.log(nf / np.float32(exact)) / np.float32(math.log(REL_MAX_DIST / exact))
                     * np.float32(REL_BUCKETS - exact)).astype(np.int32)
    return np.where(n < exact, n, np.minimum(large, REL_BUCKETS - 1)).astype(np.int32)


def _bias_strip_kernel(bk_ref, rb_ref, o_ref):
    bk = bk_ref[...]
    for h in range(N_HEADS):
        acc = jnp.zeros(bk.shape, F32)
        for bkt in range(REL_BUCKETS):
            acc = jnp.where(bk == bkt, rb_ref[bkt, h], acc)
        o_ref[h] = acc


def _strip_geometry(step):
    origin = -(-(step + REL_MAX_DIST - 1) // 128) * 128
    return origin, origin + step


def _strip_offset(step, delta):
    origin, _ = _strip_geometry(step)
    return pl.multiple_of(jnp.clip(origin - delta, 0, origin), 128)


def _bias_strip(rel_bias, qb, step):
    origin, width = _strip_geometry(step)
    qi = np.arange(qb)[:, None]
    z = np.arange(width)[None, :]
    dist = np.clip(qi - z + origin, 0, REL_MAX_DIST)
    bk = jnp.asarray(_bucket_table()[dist])
    out = pl.pallas_call(
        _bias_strip_kernel,
        in_specs=[pl.BlockSpec(memory_space=pltpu.VMEM), pl.BlockSpec(memory_space=pltpu.SMEM)],
        out_specs=pl.BlockSpec(memory_space=pltpu.VMEM),
        out_shape=jax.ShapeDtypeStruct((N_HEADS, qb, width), F32),
        name="bias_strip",
    )(bk, rel_bias)
    return out


def _attn_kernel(mode, qb, lp, q_pos0, k_sel, *refs):
    if mode == "dsa":
        (q_ref, kt_ref, v_ref, strip_ref, qi_ref, wi_ref, kit_ref, o_ref,
         qs_s, m_s, l_s, acc_s, key_s) = refs
    else:
        (q_ref, kt_ref, v_ref, strip_ref, kmt_ref, o_ref,
         qs_s, m_s, l_s, acc_s) = refs
    rows = GROUP * qb
    i = pl.program_id(1)
    q0 = q_pos0 + i * qb
    c_max = (q0 + qb - 1) // KEY_CHUNK
    n_chunks = c_max + 1
    qpos = q0 + lax.broadcasted_iota(I32, (qb, 1), 0)
    lane = lax.broadcasted_iota(I32, (1, KEY_CHUNK), 1)
    scale = HEAD_DIM ** -0.5

    for g in range(N_KV):
        qs_s[g] = (q_ref[g] * scale).astype(BF16)
    m_s[...] = jnp.full(m_s.shape, NEG, F32)
    l_s[...] = jnp.zeros(l_s.shape, F32)
    acc_s[...] = jnp.zeros(acc_s.shape, F32)

    if mode == "dsa":
        qi = qi_ref[...].astype(BF16)
        wi = wi_ref[...]

        def score_chunk(c, carry):
            k0 = pl.multiple_of(c * KEY_CHUNK, KEY_CHUNK)
            s = jnp.dot(qi, kit_ref[:, pl.ds(k0, KEY_CHUNK)], preferred_element_type=F32)
            s = jnp.maximum(s, 0.0) * wi
            sc = s[0:qb]
            for h in range(1, IDX_HEADS):
                sc = sc + s[h * qb:(h + 1) * qb]
            sc = jnp.where(k0 + lane <= qpos, sc, -jnp.inf)
            bits = pltpu.bitcast(sc, I32)
            bits = jnp.where(bits == INT_MIN, 0, bits)
            key_s[:, pl.ds(k0, KEY_CHUNK)] = jnp.where(bits < 0, bits ^ 0x7FFFFFFF, bits)
            return carry

        scan = STEP_KEYS if qb <= 32 else ROWS_STEP
        n_scan = (n_chunks * KEY_CHUNK + scan - 1) // scan
        lax.fori_loop(0, n_scan * (scan // KEY_CHUNK), score_chunk, 0)
        scan_lane = lax.broadcasted_iota(I32, (1, scan), 1)

        def count(pred):
            def body(c, acc):
                k0 = pl.multiple_of(c * scan, scan)
                return acc + pred(key_s[:, pl.ds(k0, scan)], k0 + scan_lane).astype(I32)
            acc = lax.fori_loop(0, n_scan, body, jnp.zeros((qb, scan), I32))
            return jnp.sum(acc, axis=1, keepdims=True)

        def bit_step(bi, thr):
            cand = thr + jnp.left_shift(jnp.int32(1), 31 - bi)
            cnt = count(lambda k, kidx: k >= cand)
            return jnp.where(cnt >= k_sel, cand, thr)

        thr = lax.fori_loop(0, 32, bit_step, jnp.full((qb, 1), INT_MIN, I32))
        n_gt = count(lambda k, kidx: k > thr)
        n_ge = count(lambda k, kidx: k >= thr)
        need = k_sel - n_gt

        def tie_limit():
            n_bits = max(1, (lp - 1).bit_length())

            def idx_step(bi, lo):
                cand = lo + jnp.left_shift(jnp.int32(1), n_bits - 1 - bi)
                cnt = count(lambda k, kidx: (k == thr) & (kidx <= cand))
                return jnp.where(cnt < need, cand, lo)

            lo = lax.fori_loop(0, n_bits, idx_step, jnp.full((qb, 1), -1, I32))
            return lo + 1

        tie_idx = lax.cond(jnp.max(n_ge - n_gt - need) > 0, tie_limit,
                           lambda: jnp.full((qb, 1), lp, I32))
    else:
        blk = lax.broadcasted_iota(I32, (qb, kmt_ref.shape[2]), 1)
        sel_blocks = []
        for g in range(N_KV):
            qg = q_ref[g]
            qsum = qg[0:qb]
            for j in range(1, GROUP):
                qsum = qsum + qg[j * qb:(j + 1) * qb]
            gate = jnp.dot(qsum, kmt_ref[g], preferred_element_type=F32,
                           precision=lax.Precision.HIGHEST)
            gate = jnp.where(blk < c_max, gate, -jnp.inf)
            sel = jnp.zeros(blk.shape, jnp.bool_)
            for _ in range(MOBA_TOPK):
                mx = jnp.max(gate, axis=1, keepdims=True)
                first = jnp.min(jnp.where(gate == mx, blk, blk.shape[1]), axis=1, keepdims=True)
                hit = blk == first
                sel = sel | (hit & (blk < c_max))
                gate = jnp.where(hit, -jnp.inf, gate)
            sel_blocks.append(sel.astype(I32))

    step_lane = lax.broadcasted_iota(I32, (1, ROWS_STEP), 1)

    def attend(c, carry):
        k0 = pl.multiple_of(c * ROWS_STEP, ROWS_STEP)
        kidx = k0 + step_lane
        causal = kidx <= qpos
        off = _strip_offset(ROWS_STEP, q0 - k0)
        if mode == "dsa":
            k = key_s[:, pl.ds(k0, ROWS_STEP)]
            mask1 = ((k > thr) | ((k == thr) & (kidx <= tie_idx))) & causal
            mask = jnp.concatenate([mask1] * GROUP, axis=0)

        logits = [jnp.dot(qs_s[g], kt_ref[g, :, pl.ds(k0, ROWS_STEP)], preferred_element_type=F32)
                  for g in range(N_KV)]
        probs, alphas = [], []
        for g in range(N_KV):
            if mode != "dsa":
                parts = []
                for i in range(ROWS_STEP // KEY_CHUNK):
                    cb = c * (ROWS_STEP // KEY_CHUNK) + i
                    picked = jnp.sum(jnp.where(blk == cb, sel_blocks[g], 0), axis=1, keepdims=True)
                    own = causal[:, i * KEY_CHUNK:(i + 1) * KEY_CHUNK].astype(I32)
                    parts.append(jnp.where(cb == c_max, own, picked))
                mask1 = jnp.concatenate(parts, axis=1) > 0
                mask = jnp.concatenate([mask1] * GROUP, axis=0)
            s = jnp.where(mask, logits[g] + strip_ref[g, :, pl.ds(off, ROWS_STEP)], NEG)
            m_old = m_s[g]
            m_new = jnp.maximum(m_old, jnp.max(s, axis=1, keepdims=True))
            alpha = jnp.exp(m_old - m_new)
            p = jnp.exp(s - m_new)
            l_s[g] = alpha * l_s[g] + jnp.sum(p, axis=1, keepdims=True)
            m_s[g] = m_new
            probs.append(p.astype(BF16))
            alphas.append(alpha)
        vblk = v_ref[pl.ds(k0, ROWS_STEP), :]
        for g in range(N_KV):
            acc_s[g] = alphas[g] * acc_s[g] + jnp.dot(probs[g], vblk, preferred_element_type=F32)
        return carry

    lax.fori_loop(0, (n_chunks * KEY_CHUNK + ROWS_STEP - 1) // ROWS_STEP, attend, 0)
    for g in range(N_KV):
        o_ref[g] = acc_s[g][:, g * HEAD_DIM:(g + 1) * HEAD_DIM] / l_s[g]


def _sparse_attention(mode, q_st, kt, v, strip, extra, qb, q_pos0, k_sel=0):
    bsz, nq = q_st.shape[:2]
    lp = v.shape[1]
    rows = GROUP * qb
    assert q_pos0 % ROWS_STEP == 0 and (nq == 1 or qb % ROWS_STEP == 0) and lp % ROWS_STEP == 0
    per_q = lambda *tail: pl.BlockSpec((None, None) + tail, lambda b, i: (b, i) + (0,) * len(tail))
    per_b = lambda *tail: pl.BlockSpec((None,) + tail, lambda b, i: (b,) + (0,) * len(tail))
    in_specs = [per_q(N_KV, rows, HEAD_DIM), per_b(N_KV, HEAD_DIM, lp), per_b(lp, N_KV * HEAD_DIM),
                pl.BlockSpec(strip.shape, lambda b, i: (0, 0, 0))]
    scratch = [pltpu.VMEM((N_KV, rows, HEAD_DIM), BF16),
               pltpu.VMEM((N_KV, rows, 1), F32),
               pltpu.VMEM((N_KV, rows, 1), F32),
               pltpu.VMEM((N_KV, rows, N_KV * HEAD_DIM), F32)]
    if mode == "dsa":
        qi_st, wi_st, kit = extra
        in_specs += [per_q(IDX_HEADS * qb, IDX_DIM), per_q(IDX_HEADS * qb, 1), per_b(IDX_DIM, lp)]
        scratch.append(pltpu.VMEM((qb, lp), I32))
    else:
        (kmt,) = extra
        in_specs.append(per_b(N_KV, HEAD_DIM, kmt.shape[3]))
    return pl.pallas_call(
        functools.partial(_attn_kernel, mode, qb, lp, q_pos0, k_sel),
        grid=(bsz, nq),
        in_specs=in_specs,
        out_specs=per_q(N_KV, rows, HEAD_DIM),
        out_shape=jax.ShapeDtypeStruct((bsz, nq, N_KV, rows, HEAD_DIM), F32),
        scratch_shapes=scratch,
        compiler_params=_cparams(("parallel", "arbitrary")),
        name=mode + "_attention",
    )(q_st, kt, v, strip, *extra)


QT = 128


def _attn_t_kernel(mode, lp, k_sel, *refs):
    if mode == "dsa":
        (qt_ref, k_ref, vt_ref, strip_ref, qit_ref, wit_ref, ki_ref, o_ref,
         qbd_s, m_s, l_s, acc_s, key_s) = refs
    else:
        (qt_ref, k_ref, vt_ref, strip_ref, km_ref, o_ref,
         qbd_s, m_s, l_s, acc_s) = refs
    kv_dim = N_KV * HEAD_DIM
    cols = GROUP * QT
    i = pl.program_id(1)
    q0 = i * QT
    c_max = (q0 + QT - 1) // KEY_CHUNK
    n_chunks = c_max + 1
    qpos = q0 + lax.broadcasted_iota(I32, (1, QT), 1)
    krow = lax.broadcasted_iota(I32, (KEY_CHUNK, 1), 0)
    scale = HEAD_DIM ** -0.5

    qbd_s[...] = jnp.zeros(qbd_s.shape, BF16)
    for g in range(N_KV):
        qbd_s[g * HEAD_DIM:(g + 1) * HEAD_DIM, g * cols:(g + 1) * cols] = (
            qt_ref[g * HEAD_DIM:(g + 1) * HEAD_DIM, :] * scale).astype(BF16)
    m_s[...] = jnp.full(m_s.shape, NEG, F32)
    l_s[...] = jnp.zeros(l_s.shape, F32)
    acc_s[...] = jnp.zeros(acc_s.shape, F32)

    if mode == "dsa":
        qit = qit_ref[...].astype(BF16)
        wit = wit_ref[...]

        def score_chunk(c, carry):
            k0 = pl.multiple_of(c * KEY_CHUNK, KEY_CHUNK)
            kib = ki_ref[pl.ds(k0, KEY_CHUNK), :]
            sc = jnp.zeros((KEY_CHUNK, QT), F32)
            for hp in range(IDX_HEADS // 2):
                lanes = slice(2 * hp * QT, 2 * (hp + 1) * QT)
                s = jnp.maximum(jnp.dot(kib, qit[:, lanes], preferred_element_type=F32), 0.0) * wit[:, lanes]
                sc = sc + s[:, :QT] + s[:, QT:]
            sc = jnp.where(k0 + krow <= qpos, sc, -jnp.inf)
            bits = pltpu.bitcast(sc, I32)
            bits = jnp.where(bits == INT_MIN, 0, bits)
            key_s[pl.ds(k0, KEY_CHUNK), :] = jnp.where(bits < 0, bits ^ 0x7FFFFFFF, bits)
            return carry

        lax.fori_loop(0, n_chunks, score_chunk, 0)

        def count(pred):
            def body(c, acc):
                k0 = pl.multiple_of(c * KEY_CHUNK, KEY_CHUNK)
                hit = pred(key_s[pl.ds(k0, KEY_CHUNK), :], k0).astype(I32)
                return acc + jnp.sum(hit.reshape(KEY_CHUNK // 8, 8, QT), axis=0)
            acc = lax.fori_loop(0, n_chunks, body, jnp.zeros((8, QT), I32))
            return jnp.sum(acc, axis=0, keepdims=True)

        def bit_step(bi, thr):
            cand = thr + jnp.left_shift(jnp.int32(1), 31 - bi)
            cnt = count(lambda k, k0: k >= cand)
            return jnp.where(cnt >= k_sel, cand, thr)

        thr = lax.fori_loop(0, 32, bit_step, jnp.full((1, QT), INT_MIN, I32))
        n_gt = count(lambda k, k0: k > thr)
        n_ge = count(lambda k, k0: k >= thr)
        need = k_sel - n_gt

        def tie_limit():
            n_bits = max(1, (lp - 1).bit_length())

            def idx_step(bi, lo):
                cand = lo + jnp.left_shift(jnp.int32(1), n_bits - 1 - bi)
                cnt = count(lambda k, k0: (k == thr) & (k0 + krow <= cand))
                return jnp.where(cnt < need, cand, lo)

            lo = lax.fori_loop(0, n_bits, idx_step, jnp.full((1, QT), -1, I32))
            return lo + 1

        tie_idx = lax.cond(jnp.max(n_ge - n_gt - need) > 0, tie_limit,
                           lambda: jnp.full((1, QT), lp, I32))
    else:
        nbp = km_ref.shape[1]
        blk = lax.broadcasted_iota(I32, (nbp, QT), 0)
        sel_blocks = []
        for g in range(N_KV):
            qg = qt_ref[g * HEAD_DIM:(g + 1) * HEAD_DIM, :]
            qsum = qg[:, 0:QT]
            for j in range(1, GROUP):
                qsum = qsum + qg[:, j * QT:(j + 1) * QT]
            gate = jnp.dot(km_ref[g], qsum, preferred_element_type=F32, precision=lax.Precision.HIGHEST)
            gate = jnp.where(blk < c_max, gate, -jnp.inf)
            sel = jnp.zeros(blk.shape, I32)
            for _ in range(MOBA_TOPK):
                mx = jnp.max(gate, axis=0, keepdims=True)
                first = jnp.min(jnp.where(gate == mx, blk, nbp), axis=0, keepdims=True)
                hit = blk == first
                sel = jnp.where(hit & (blk < c_max), 1, sel)
                gate = jnp.where(hit, -jnp.inf, gate)
            sel_blocks.append(sel)

    def attend(c, carry):
        k0 = pl.multiple_of(c * KEY_CHUNK, KEY_CHUNK)
        kidx = k0 + krow
        causal = kidx <= qpos
        off = _strip_offset(KEY_CHUNK, q0 - k0)
        if mode == "dsa":
            k = key_s[pl.ds(k0, KEY_CHUNK), :]
            mask = ((k > thr) | ((k == thr) & (kidx <= tie_idx))) & causal
        kblk = k_ref[pl.ds(k0, KEY_CHUNK), :]

        def logits(g):
            return jnp.dot(kblk, qbd_s[:, g * cols:(g + 1) * cols], preferred_element_type=F32)

        s_next = logits(0)
        for g in range(N_KV):
            if mode != "dsa":
                picked = jnp.sum(jnp.where(blk == c, sel_blocks[g], 0), axis=0, keepdims=True)
                mask = jnp.where(c == c_max, causal.astype(I32), picked) > 0
            s_all = s_next
            if g + 1 < N_KV:
                s_next = logits(g + 1)
            probs, alphas = [], []
            for j in range(GROUP):
                h = g * GROUP + j
                lanes = slice(j * QT, (j + 1) * QT)
                s = s_all[:, lanes] + strip_ref[g, pl.ds(off, KEY_CHUNK), lanes]
                s = jnp.where(mask, s, NEG)
                m_old = m_s[pl.ds(h, 1), :]
                m_new = jnp.maximum(m_old, jnp.max(s, axis=0, keepdims=True))
                alpha = jnp.exp(m_old - m_new)
                p = jnp.exp(s - m_new)
                l_s[pl.ds(h, 1), :] = alpha * l_s[pl.ds(h, 1), :] + jnp.sum(p, axis=0, keepdims=True)
                m_s[pl.ds(h, 1), :] = m_new
                probs.append(p.astype(BF16))
                alphas.append(alpha)
            pv = jnp.dot(vt_ref[g * HEAD_DIM:(g + 1) * HEAD_DIM, pl.ds(k0, KEY_CHUNK)],
                         jnp.concatenate(probs, axis=1), preferred_element_type=F32)
            acc_s[g] = jnp.concatenate(alphas, axis=1) * acc_s[g] + pv
        return carry

    lax.fori_loop(0, n_chunks, attend, 0)
    for g in range(N_KV):
        for j in range(0, GROUP, 2):
            h = g * GROUP + j
            pair = [acc_s[g][:, (j + i) * QT:(j + i + 1) * QT] * (1.0 / l_s[pl.ds(h + i, 1), :]) for i in range(2)]
            o_ref[:, h * HEAD_DIM:(h + 2) * HEAD_DIM] = jnp.concatenate(pair, axis=0).T


def _sparse_attention_t(mode, qt, k, vt, strip_t, extra, k_sel=0):
    bsz, nq = qt.shape[:2]
    lp = k.shape[1]
    kv_dim = N_KV * HEAD_DIM
    cols = GROUP * QT
    per_q = lambda *tail: pl.BlockSpec((None, None) + tail, lambda b, i: (b, i) + (0,) * len(tail))
    per_b = lambda *tail: pl.BlockSpec((None,) + tail, lambda b, i: (b,) + (0,) * len(tail))
    in_specs = [per_q(kv_dim, cols), per_b(lp, kv_dim), per_b(kv_dim, lp),
                pl.BlockSpec(strip_t.shape, lambda b, i: (0, 0, 0))]
    scratch = [pltpu.VMEM((kv_dim, N_KV * cols), BF16),
               pltpu.VMEM((N_HEADS, QT), F32),
               pltpu.VMEM((N_HEADS, QT), F32),
               pltpu.VMEM((N_KV, HEAD_DIM, cols), F32)]
    if mode == "dsa":
        qit, wit, ki = extra
        in_specs += [per_q(IDX_DIM, IDX_HEADS * QT), per_q(1, IDX_HEADS * QT), per_b(lp, IDX_DIM)]
        scratch.append(pltpu.VMEM((lp, QT), I32))
    else:
        (km,) = extra
        in_specs.append(per_b(N_KV, km.shape[2], HEAD_DIM))
    return pl.pallas_call(
        functools.partial(_attn_t_kernel, mode, lp, k_sel),
        grid=(bsz, nq),
        in_specs=in_specs,
        out_specs=per_q(QT, N_HEADS * HEAD_DIM),
        out_shape=jax.ShapeDtypeStruct((bsz, nq, QT, N_HEADS * HEAD_DIM), F32),
        scratch_shapes=scratch,
        compiler_params=_cparams(("parallel", "arbitrary")),
        name=mode + "_attention_t",
    )(qt, k, vt, strip_t, *extra)


def _top_rows(s, k, payload=None, rank=None):
    rid = lax.broadcasted_iota(I32, s.shape, 0) if rank is None else rank
    vals, ids = [], []
    for _ in range(k):
        mx = jnp.max(s, axis=0, keepdims=True)
        first = jnp.min(jnp.where(s == mx, rid, jnp.iinfo(jnp.int32).max), axis=0, keepdims=True)
        hit = rid == first
        vals.append(mx)
        if payload is None:
            ids.append(first)
        else:
            ids.append(jnp.max(jnp.where(hit, payload, -1), axis=0, keepdims=True))
        s = jnp.where(hit, -jnp.inf, s)
    return jnp.concatenate(vals, axis=0), jnp.concatenate(ids, axis=0)


def _top_rows_paired(s, k):
    half = s.shape[0] // 2
    lo, hi = s[:half], s[half:]
    rid = lax.broadcasted_iota(I32, lo.shape, 0)
    first_lo = lo >= hi
    top = jnp.where(first_lo, lo, hi)
    rest = jnp.where(first_lo, hi, lo)
    top_id = jnp.where(first_lo, rid, rid + half)
    rest_id = jnp.where(first_lo, rid + half, rid)
    vals, ids = [], []
    for _ in range(k):
        mx = jnp.max(top, axis=0, keepdims=True)
        first = jnp.min(jnp.where(top == mx, top_id, jnp.iinfo(jnp.int32).max), axis=0, keepdims=True)
        hit = top_id == first
        vals.append(mx)
        ids.append(first)
        top = jnp.where(hit, rest, top)
        top_id = jnp.where(hit, rest_id, top_id)
        rest = jnp.where(hit, -jnp.inf, rest)
    return jnp.concatenate(vals, axis=0), jnp.concatenate(ids, axis=0)


def _peer_route_kernel(x_ref, wqt_ref, keys_ref, row_ref, par_ref, g_ref):
    xb = x_ref[...].astype(BF16)
    qt = lax.dot_general(wqt_ref[...], xb, (((1,), (1,)), ((), ())),
                         preferred_element_type=F32)
    n_t = x_ref.shape[0]
    piece = lax.broadcasted_iota(I32, (11 * 8, n_t), 0) // 8
    within = lax.broadcasted_iota(I32, (11 * 8, n_t), 0) % 8
    by_b = (piece >= 2) & (piece < 10)
    ca = jnp.where(piece < 2, 0, jnp.where(by_b, within, 8 + within))
    cb = jnp.where(piece == 0, within, jnp.where(piece == 1, 8 + within, jnp.where(by_b, piece - 2, 0)))
    cand_ok = ((ca + 1) * (cb + 1) <= PEER_TOPK) & jnp.logical_not(by_b & (within == 0))
    cand_rank = ca * PEER_TOPK + cb

    def pieces(first, second, combine):
        out = [combine(first[0:1], second[0:8]), combine(first[0:1], second[8:16])]
        out += [combine(first[0:8], second[b:b + 1]) for b in range(8)]
        out.append(combine(first[8:16], second[0:1]))
        return jnp.concatenate(out, axis=0)

    e_rows, g_rows = [], []
    for h in range(PEER_HEADS):
        sv, si = [], []
        for c in range(2):
            hc = 2 * h + c
            qhc = qt[hc * PEER_HALF:(hc + 1) * PEER_HALF].astype(BF16)
            s = jnp.dot(keys_ref[hc], qhc, preferred_element_type=F32)
            v_, i_ = _top_rows_paired(s, PEER_TOPK)
            sv.append(v_)
            si.append(i_)
        cand = jnp.where(cand_ok, pieces(sv[0], sv[1], lambda u, w: u + w), -jnp.inf)
        cidx = pieces(si[0], si[1], lambda u, w: u * PEER_NKEYS + w)
        gv, ge = _top_rows(cand, PEER_TOPK, payload=cidx, rank=cand_rank)
        ex = jnp.exp(gv - gv[0:1])
        g_rows.append(ex / jnp.sum(ex, axis=0, keepdims=True))
        e_rows.append(ge)
    e_t = jnp.concatenate(e_rows, axis=0).T
    row_ref[...] = jnp.right_shift(e_t, 1) * 8
    par_ref[...] = e_t & 1
    g_ref[...] = jnp.concatenate(g_rows, axis=0).T


def _peer_route(x, wqt_bf16, keys_bf16, tb):
    m, d = x.shape
    out = pl.BlockSpec((tb, PEER_PAIRS), lambda i: (i, 0))
    return pl.pallas_call(
        _peer_route_kernel,
        grid=(m // tb,),
        in_specs=[pl.BlockSpec((tb, d), lambda i: (i, 0)),
                  pl.BlockSpec(wqt_bf16.shape, lambda i: (0, 0)),
                  pl.BlockSpec(keys_bf16.shape, lambda i: (0, 0, 0))],
        out_specs=[out, out, out],
        out_shape=[jax.ShapeDtypeStruct((m, PEER_PAIRS), I32),
                   jax.ShapeDtypeStruct((m, PEER_PAIRS), I32),
                   jax.ShapeDtypeStruct((m, PEER_PAIRS), F32)],
        compiler_params=_cparams(("parallel",)),
        name="peer_route",
    )(x, wqt_bf16, keys_bf16)


FOLD_PAIRS = 32
HI_MASK = 0xFFFF0000
UP_SLOTS = 8
TILE_ROWS = 8


def _pack_table(w):
    n, d = w.shape
    bits = lax.bitcast_convert_type(w.astype(BF16), jnp.uint16).astype(jnp.uint32)
    bits = bits.reshape(n // 2, 2, d // 128, 128)
    return (bits[:, 0] | (bits[:, 1] << 16)).astype(jnp.uint32).reshape(n // 2 * (d // 128), 128)


def _load_tile(tbl_ref, row_ref, idx):
    return tbl_ref[pl.ds(pl.multiple_of(row_ref[idx], TILE_ROWS), TILE_ROWS), :]


def _peer_up_kernel(tb, row_ref, par_ref, x_ref, g_ref, fold_ref, tbl_ref, c_ref, prod_s):
    diag = (lax.broadcasted_iota(I32, (PEER_PAIRS, PEER_PAIRS), 0)
            == lax.broadcasted_iota(I32, (PEER_PAIRS, PEER_PAIRS), 1))

    def gather(t, slot):
        xt = x_ref[t]
        rows_t = row_ref.at[pl.ds(t * PEER_PAIRS, PEER_PAIRS)]
        for p in range(0, PEER_PAIRS, 2):
            w0 = _load_tile(tbl_ref, rows_t, p)
            w1 = _load_tile(tbl_ref, rows_t, p + 1)
            even = [pltpu.bitcast(jnp.left_shift(w, jnp.uint32(16)), F32) * xt for w in (w0, w1)]
            odd = [pltpu.bitcast(w & jnp.uint32(HI_MASK), F32) * xt for w in (w0, w1)]
            rows = pl.ds(p * TILE_ROWS, 2 * TILE_ROWS)
            prod_s[slot, rows, 0:128] = jnp.concatenate(even, axis=0).astype(BF16)
            prod_s[slot, rows, 128:256] = jnp.concatenate(odd, axis=0).astype(BF16)

    def reduce(t, slot):
        depth = FOLD_PAIRS * TILE_ROWS
        parts = [jnp.dot(fold_ref[...], prod_s[slot, pl.ds(i * depth, depth), :], preferred_element_type=F32)
                 for i in range(PEER_PAIRS // FOLD_PAIRS)]
        part = jnp.concatenate(parts, axis=0)
        rows = []
        for half in range(2):
            col = jnp.sum(part[:, half * 128:(half + 1) * 128], axis=1, keepdims=True)
            rows.append(jnp.sum(jnp.where(diag, col, 0.0), axis=0, keepdims=True))
        a = jnp.where(par_ref[pl.ds(t, 1), :] == 1, rows[1], rows[0])
        act = 0.5 * a * (1.0 + lax.erf(a * (2.0 ** -0.5)))
        c_ref[pl.ds(t, 1), :] = g_ref[pl.ds(t, 1), :] * act

    prod_s[UP_SLOTS:2 * UP_SLOTS] = jnp.zeros((UP_SLOTS,) + prod_s.shape[1:], BF16)

    def step(k, carry):
        t0 = 2 * UP_SLOTS * k
        for j in range(UP_SLOTS):
            gather(t0 + j, j)
            reduce(jnp.maximum(t0 - UP_SLOTS + j, 0), UP_SLOTS + j)
        for j in range(UP_SLOTS):
            gather(t0 + UP_SLOTS + j, UP_SLOTS + j)
            reduce(t0 + j, j)
        return carry

    lax.fori_loop(0, tb // (2 * UP_SLOTS), step, 0)
    for j in range(UP_SLOTS):
        reduce(tb - UP_SLOTS + j, UP_SLOTS + j)


DOWN_GROUP = 32


def _peer_down_kernel(tb, row_ref, par_ref, c_ref, spread_ref, tbl_ref, f_ref, tile_s, coef_s):
    width = PEER_PAIRS * 2 * TILE_ROWS
    lane = lax.broadcasted_iota(I32, (TILE_ROWS, width), 1)
    own_row = ((lane & (2 * TILE_ROWS - 1)) >> 1) == lax.broadcasted_iota(I32, (TILE_ROWS, width), 0)
    lane_par = (lax.broadcasted_iota(I32, (DOWN_GROUP, width), 1) & 1).astype(F32)

    def gather(t, slot):
        rows_t = row_ref.at[pl.ds(t * PEER_PAIRS, PEER_PAIRS)]
        for p in range(PEER_PAIRS):
            tile_s[slot, pl.ds(p * TILE_ROWS, TILE_ROWS), :] = _load_tile(tbl_ref, rows_t, p)

    def apply(t, j, slot):
        tiles = pltpu.bitcast(tile_s[slot], BF16)
        lhs = [jnp.where(own_row, coef_s[pl.ds(part * DOWN_GROUP + j, 1), :], 0.0) for part in range(2)]
        out = jnp.dot(jnp.concatenate(lhs, axis=0).astype(BF16), tiles, preferred_element_type=F32)
        f_ref[t] = out[0:TILE_ROWS] + out[TILE_ROWS:2 * TILE_ROWS]

    def group(gi, carry):
        t0 = pl.multiple_of(gi * DOWN_GROUP, DOWN_GROUP)
        gather(t0, 0)
        c = c_ref[pl.ds(t0, DOWN_GROUP), :]
        c_hi = c.astype(BF16)
        c_lo = (c - c_hi.astype(F32)).astype(BF16)
        par = par_ref[pl.ds(t0, DOWN_GROUP), :].astype(BF16)
        rep = jnp.dot(jnp.concatenate([c_hi, c_lo, par], axis=0), spread_ref[...],
                      preferred_element_type=F32)
        keep = rep[2 * DOWN_GROUP:] == lane_par
        coef_s[0:DOWN_GROUP, :] = jnp.where(keep, rep[0:DOWN_GROUP], 0.0)
        coef_s[DOWN_GROUP:2 * DOWN_GROUP, :] = jnp.where(keep, rep[DOWN_GROUP:2 * DOWN_GROUP], 0.0)
        for j in range(DOWN_GROUP):
            if j + 1 < DOWN_GROUP:
                gather(t0 + j + 1, (j + 1) % 2)
            apply(t0 + j, j, j % 2)
        return carry

    lax.fori_loop(0, tb // DOWN_GROUP, group, 0)


def _peer_experts(x, row, par, g, u_tbl, v_tbl, tb):
    m, d = x.shape
    assert m % tb == 0 and tb % (2 * UP_SLOTS) == 0 and tb % DOWN_GROUP == 0 and d == TILE_ROWS * 128
    x3 = x.reshape(m, d // 128, 128)
    flat = row.reshape(m * PEER_PAIRS)
    smem = pl.BlockSpec((tb * PEER_PAIRS,), lambda i: (i,), memory_space=pltpu.SMEM)
    vrow = pl.BlockSpec((tb, PEER_PAIRS), lambda i: (i, 0))
    tile = pl.BlockSpec((tb, d // 128, 128), lambda i: (i, 0, 0))
    resident = pl.BlockSpec(memory_space=pltpu.VMEM)
    const = lambda a: pl.BlockSpec(a.shape, lambda i: (0, 0))
    fold = jnp.asarray(np.arange(FOLD_PAIRS * TILE_ROWS)[None, :] // TILE_ROWS == np.arange(FOLD_PAIRS)[:, None], BF16)
    width = PEER_PAIRS * 2 * TILE_ROWS
    spread = jnp.asarray(np.arange(width)[None, :] // (2 * TILE_ROWS) == np.arange(PEER_PAIRS)[:, None], BF16)
    c = pl.pallas_call(
        functools.partial(_peer_up_kernel, tb),
        grid=(m // tb,),
        in_specs=[smem, vrow, tile, vrow, const(fold), resident],
        out_specs=vrow,
        out_shape=jax.ShapeDtypeStruct((m, PEER_PAIRS), F32),
        scratch_shapes=[pltpu.VMEM((2 * UP_SLOTS, PEER_PAIRS * TILE_ROWS, 256), BF16)],
        compiler_params=_cparams(("arbitrary",)),
        name="peer_up",
    )(flat, par, x3, g, fold, u_tbl)
    f = pl.pallas_call(
        functools.partial(_peer_down_kernel, tb),
        grid=(m // tb,),
        in_specs=[smem, vrow, vrow, const(spread), resident],
        out_specs=tile,
        out_shape=jax.ShapeDtypeStruct((m, d // 128, 128), F32),
        scratch_shapes=[pltpu.VMEM((2, PEER_PAIRS * TILE_ROWS, 128), jnp.uint32),
                        pltpu.VMEM((2 * DOWN_GROUP, width), F32)],
        compiler_params=_cparams(("arbitrary",)),
        name="peer_down",
    )(flat, par, c, spread, v_tbl)
    return f.reshape(m, d)


def _stack_heads(a, qb, n_lead):
    bsz, q, n, hd = a.shape
    n_in = n // n_lead
    a = a.reshape(bsz, q // qb, qb, n_lead, n_in, hd)
    return a.transpose(0, 1, 3, 4, 2, 5).reshape(bsz, q // qb, n_lead, n_in * qb, hd)


def _unstack_heads(o, qb):
    bsz, nq = o.shape[:2]
    o = o.reshape(bsz, nq, N_KV, GROUP, qb, HEAD_DIM).transpose(0, 1, 4, 2, 3, 5)
    return o.reshape(bsz * nq * qb, N_HEADS * HEAD_DIM)


def _run_trunk(x, p, past, prm, qb, tm, tb_route, tb_exp):
    bsz, q_len, d = x.shape
    depth = prm["w_o"].shape[0]
    m = bsz * q_len
    tm, tb_route, tb_exp = min(tm, m), min(tb_route, m), min(tb_exp, m)
    alpha = (2 * depth) ** 0.25
    attn_dim = N_HEADS * HEAD_DIM
    kv_dim = N_KV * HEAD_DIM
    if past is None:
        q_pos0 = 0
        n_pages = q_len // PAGE
        table = jnp.arange(bsz * n_pages, dtype=I32).reshape(bsz, n_pages)
    else:
        cache_k, cache_v, cache_ki, table = past
        q_pos0 = table.shape[1] * PAGE
    on_lanes = past is None and q_len % QT == 0
    if on_lanes:
        qb = QT
    nq = q_len // qb
    strip = _bias_strip(prm["rel_bias"], qb, KEY_CHUNK if on_lanes else ROWS_STEP)
    strip = strip.reshape(N_KV, GROUP, qb, strip.shape[2])
    if on_lanes:
        strip = strip.transpose(0, 3, 1, 2).reshape(N_KV, strip.shape[3], GROUP * qb)
    else:
        strip = strip.reshape(N_KV, GROUP * qb, strip.shape[3])
    xf = x.reshape(m, d)
    ks, vs, kis = [], [], []

    def relayout(new_rows, cache, layer, want):
        w = new_rows.shape[2]
        if past is None:
            return _relayout(new_rows.reshape(bsz * n_pages, PAGE, w), table, None, False, want)
        n_pool = cache.shape[1]
        pool = cache.reshape((cache.shape[0] * n_pool,) + cache.shape[2:])
        pool_t = jnp.moveaxis(pool, 1, -1).reshape(pool.shape[0], w, PAGE)
        pad = jnp.zeros((bsz, STEP_KEYS - q_len, w), F32)
        return _relayout(pool_t, table + layer * n_pool, jnp.concatenate([new_rows, pad], axis=1), True, want)

    for l in range(depth):
        dsa = l % 2 == 0
        w_in = prm["w_in_dsa"][l // 2] if dsa else prm["w_in_moba"][l // 2]
        if on_lanes:
            proj = _in_proj_lanes(xf, w_in, dsa, tm)
            q_in = proj[0].reshape(bsz, nq, kv_dim, GROUP * qb)
            k = proj[1].reshape(bsz, q_len, kv_dim)
            v = proj[2].reshape(bsz, q_len, kv_dim)
        else:
            n_in = w_in.shape[1]
            n_pad = -(-n_in // 128) * 128
            proj = _matmul(xf, jnp.pad(w_in, ((0, 0), (0, n_pad - n_in))).astype(BF16), tm)
            qh = proj[:, :attn_dim].reshape(bsz, q_len, N_HEADS, HEAD_DIM)
            k = proj[:, attn_dim:attn_dim + kv_dim].reshape(bsz, q_len, kv_dim)
            v = proj[:, attn_dim + kv_dim:attn_dim + 2 * kv_dim].reshape(bsz, q_len, kv_dim)
        ks.append(k.reshape(bsz, q_len, N_KV, HEAD_DIM))
        vs.append(v.reshape(bsz, q_len, N_KV, HEAD_DIM))
        k_lay = relayout(k, None if past is None else cache_k, l,
                         (("x",) if on_lanes else ("xt",)) + (() if dsa else ("sums",)))
        v_lay = relayout(v, None if past is None else cache_v, l, ("xt",) if on_lanes else ("x",))
        if on_lanes:
            kr, vt = k_lay["x"], v_lay["xt"]
        else:
            kt = k_lay["xt"].reshape(bsz, N_KV, HEAD_DIM, k_lay["xt"].shape[2])
            vr = v_lay["x"]
            q_in = _stack_heads(qh, qb, N_KV)
        if dsa:
            if on_lanes:
                ki = proj[3].reshape(bsz, q_len, IDX_DIM)
            else:
                off = attn_dim + 2 * kv_dim
                qi = proj[:, off:off + IDX_HEADS * IDX_DIM].reshape(bsz, q_len, IDX_HEADS, IDX_DIM)
                off += IDX_HEADS * IDX_DIM
                ki = proj[:, off:off + IDX_DIM].reshape(bsz, q_len, IDX_DIM)
                wi = proj[:, off + IDX_DIM:off + IDX_DIM + IDX_HEADS].reshape(bsz, q_len, IDX_HEADS, 1)
            kis.append(ki)
            ki_lay = relayout(ki, None if past is None else cache_ki, l // 2, ("x",) if on_lanes else ("xt",))
            k_sel = min(IDX_TOPK, (q_pos0 + q_len) // 4)
            if on_lanes:
                qi_t = proj[4].reshape(bsz, nq, IDX_DIM, IDX_HEADS * qb)
                wi_t = proj[5].reshape(bsz, nq, 1, IDX_HEADS * qb)
                o = _sparse_attention_t("dsa", q_in, kr, vt, strip, (qi_t, wi_t, ki_lay["x"]), k_sel)
            else:
                qi_st = _stack_heads(qi, qb, 1)[:, :, 0]
                wi_st = _stack_heads(wi, qb, 1)[:, :, 0]
                o = _sparse_attention("dsa", q_in, kt, vr, strip, (qi_st, wi_st, ki_lay["xt"]), qb, q_pos0, k_sel)
        else:
            ksum = k_lay["sums"]
            nb = ksum.shape[1]
            kmean = (ksum * (1.0 / MOBA_BLOCK)).reshape(bsz, nb, N_KV, HEAD_DIM)
            if on_lanes:
                km = jnp.pad(kmean.transpose(0, 2, 1, 3), ((0, 0), (0, 0), (0, -nb % 8), (0, 0)))
                o = _sparse_attention_t("moba", q_in, kr, vt, strip, (km,))
            else:
                kmt = jnp.pad(kmean.transpose(0, 2, 3, 1), ((0, 0), (0, 0), (0, 0), (0, -nb % 128)))
                o = _sparse_attention("moba", q_in, kt, vr, strip, (kmt,), qb, q_pos0)
        if on_lanes:
            o = o.reshape(m, attn_dim)
        else:
            o = _unstack_heads(o, qb)
        xf = _attn_out(o, prm["w_o"][l].astype(BF16), xf, prm["ln_g"][l, 0][None], prm["ln_b"][l, 0][None],
                       alpha, tm)
        row, par, g = _peer_route(xf, prm["peer_wq"][l].T.astype(BF16),
                                 prm["peer_keys"][l].reshape(2 * PEER_HEADS, PEER_NKEYS, PEER_HALF).astype(BF16),
                                 tb_route)
        f = _peer_experts(xf, row, par, g, _pack_table(prm["peer_u"][l]), _pack_table(prm["peer_v"][l]), tb_exp)
        xf = _layer_tail(xf, f, prm["ln_g"][l, 1][None], prm["ln_b"][l, 1][None],
                         prm["gate_w"][l].astype(BF16), prm["gate_b"][l][None],
                         p[l].reshape(m, -1), prm["ple_w"][l].astype(BF16), alpha, tm)
    return xf.reshape(bsz, q_len, d), jnp.stack(ks), jnp.stack(vs), jnp.stack(kis)


def kernel(x_prompt, x_sample, cache_k, cache_v, cache_ki, page_table, p_prompt, p_sample, w_in_dsa, w_in_moba, w_o, rel_bias, ln_g, ln_b, peer_wq, peer_keys, peer_u, peer_v, gate_w, gate_b, ple_w):
    prm = {
        "w_in_dsa": w_in_dsa, "w_in_moba": w_in_moba, "w_o": w_o, "rel_bias": rel_bias,
        "ln_g": ln_g, "ln_b": ln_b, "peer_wq": peer_wq, "peer_keys": peer_keys,
        "peer_u": peer_u, "peer_v": peer_v, "gate_w": gate_w, "gate_b": gate_b, "ple_w": ple_w,
    }
    q_sample = x_sample.shape[1]
    y_prompt, k_prompt, v_prompt, ki_prompt = _run_trunk(
        x_prompt, p_prompt, None, prm, qb=128, tm=512, tb_route=256, tb_exp=128)
    y_sample, k_sample, v_sample, ki_sample = _run_trunk(
        x_sample, p_sample, (cache_k, cache_v, cache_ki, page_table), prm,
        qb=q_sample, tm=256, tb_route=256, tb_exp=64)
    return (y_prompt, y_sample, k_prompt, v_prompt, ki_prompt, k_sample, v_sample, ki_sample)
```

```python
import functools
import math

import numpy as np
import jax
import jax.numpy as jnp
from jax import lax
from jax.experimental import pallas as pl
from jax.experimental.pallas import tpu as pltpu

F32 = jnp.float32
BF16 = jnp.bfloat16
I32 = jnp.int32

N_HEADS = 16
HEAD_DIM = 64
N_KV = 4
GROUP = 4
IDX_HEADS = 8
IDX_DIM = 64
IDX_TOPK = 256
MOBA_BLOCK = 256
MOBA_TOPK = 3
REL_BUCKETS = 32
REL_MAX_DIST = 128
PEER_HEADS = 8
PEER_NKEYS = 128
PEER_HALF = 128
PEER_TOPK = 16
PEER_PAIRS = PEER_HEADS * PEER_TOPK
LN_EPS = 1e-5
PAGE = 128

KEY_CHUNK = 256
ROWS_STEP = 4 * KEY_CHUNK
NEG = -1e30
INT_MIN = -2147483648
VMEM_LIMIT = 56 * 1024 * 1024


def _cparams(sem):
    return pltpu.CompilerParams(dimension_semantics=sem, vmem_limit_bytes=VMEM_LIMIT)


def _mm_kernel(a_ref, b_ref, o_ref):
    o_ref[...] = jnp.dot(a_ref[...].astype(BF16), b_ref[...], preferred_element_type=F32)


def _matmul(a, b_bf16, tm):
    m, k = a.shape
    n = b_bf16.shape[1]
    return pl.pallas_call(
        _mm_kernel,
        grid=(m // tm,),
        in_specs=[pl.BlockSpec((tm, k), lambda i: (i, 0)),
                  pl.BlockSpec((k, n), lambda i: (0, 0))],
        out_specs=pl.BlockSpec((tm, n), lambda i: (i, 0)),
        out_shape=jax.ShapeDtypeStruct((m, n), F32),
        compiler_params=_cparams(("parallel",)),
        name="in_proj",
    )(a, b_bf16)


def _in_proj_lanes_kernel(has_idx, *refs):
    if has_idx:
        x_ref, wqt_ref, wrow_ref, wqit_ref, wit_ref, qt_ref, k_ref, v_ref, ki_ref, qit_ref, wi_ref = refs
    else:
        x_ref, wqt_ref, wrow_ref, qt_ref, k_ref, v_ref = refs
    kv_dim = N_KV * HEAD_DIM
    xb = x_ref[...].astype(BF16)
    n_blk = xb.shape[0] // QT
    nt = (((1,), (1,)), ((), ()))
    rows = jnp.dot(xb, wrow_ref[...], preferred_element_type=F32)
    k_ref[...] = rows[:, 0:kv_dim]
    v_ref[...] = rows[:, kv_dim:2 * kv_dim]
    q_t = lax.dot_general(wqt_ref[...], xb, nt, preferred_element_type=F32)
    for i in range(n_blk):
        for g in range(N_KV):
            for j in range(GROUP):
                h = g * GROUP + j
                qt_ref[i, g * HEAD_DIM:(g + 1) * HEAD_DIM, j * QT:(j + 1) * QT] = (
                    q_t[h * HEAD_DIM:(h + 1) * HEAD_DIM, i * QT:(i + 1) * QT])
    if has_idx:
        ki_ref[...] = rows[:, 2 * kv_dim:2 * kv_dim + IDX_DIM]
        qi_t = lax.dot_general(wqit_ref[...], xb, nt, preferred_element_type=F32)
        wi_t = lax.dot_general(wit_ref[...], xb, nt, preferred_element_type=F32)
        for i in range(n_blk):
            for h in range(IDX_HEADS):
                qit_ref[i, :, h * QT:(h + 1) * QT] = qi_t[h * IDX_DIM:(h + 1) * IDX_DIM, i * QT:(i + 1) * QT]
                wi_ref[i, :, h * QT:(h + 1) * QT] = wi_t[h:h + 1, i * QT:(i + 1) * QT]


def _in_proj_lanes(x, w_in, has_idx, tm):
    m, d = x.shape
    attn_dim = N_HEADS * HEAD_DIM
    kv_dim = N_KV * HEAD_DIM
    n_blk = tm // QT
    w_rows = w_in[:, attn_dim:attn_dim + 2 * kv_dim]
    off = attn_dim + 2 * kv_dim
    if has_idx:
        w_ki = w_in[:, off + IDX_HEADS * IDX_DIM:off + IDX_HEADS * IDX_DIM + IDX_DIM]
        w_rows = jnp.concatenate([w_rows, w_ki, jnp.zeros((d, 128 - IDX_DIM), F32)], axis=1)
    args = [x, w_in[:, :attn_dim].T.astype(BF16), w_rows.astype(BF16)]
    whole = lambda a: pl.BlockSpec(a.shape, lambda i: (0, 0))
    in_specs = [pl.BlockSpec((tm, d), lambda i: (i, 0)), whole(args[1]), whole(args[2])]
    blk3 = lambda r, c: pl.BlockSpec((n_blk, r, c), lambda i: (i, 0, 0))
    rowb = lambda c: pl.BlockSpec((tm, c), lambda i: (i, 0))
    out_specs = [blk3(kv_dim, GROUP * QT), rowb(kv_dim), rowb(kv_dim)]
    out_shape = [jax.ShapeDtypeStruct((m // QT, kv_dim, GROUP * QT), F32),
                 jax.ShapeDtypeStruct((m, kv_dim), F32), jax.ShapeDtypeStruct((m, kv_dim), F32)]
    if has_idx:
        w_wi = w_in[:, off + IDX_HEADS * IDX_DIM + IDX_DIM:off + IDX_HEADS * IDX_DIM + IDX_DIM + IDX_HEADS]
        args += [w_in[:, off:off + IDX_HEADS * IDX_DIM].T.astype(BF16), w_wi.T.astype(BF16)]
        in_specs += [whole(args[3]), whole(args[4])]
        out_specs += [rowb(IDX_DIM), blk3(IDX_DIM, IDX_HEADS * QT), blk3(1, IDX_HEADS * QT)]
        out_shape += [jax.ShapeDtypeStruct((m, IDX_DIM), F32),
                      jax.ShapeDtypeStruct((m // QT, IDX_DIM, IDX_HEADS * QT), F32),
                      jax.ShapeDtypeStruct((m // QT, 1, IDX_HEADS * QT), F32)]
    return pl.pallas_call(
        functools.partial(_in_proj_lanes_kernel, has_idx),
        grid=(m // tm,),
        in_specs=in_specs,
        out_specs=out_specs,
        out_shape=out_shape,
        compiler_params=_cparams(("parallel",)),
        name="in_proj_lanes",
    )(*args)


def _layer_norm(y, g, b):
    mu = jnp.mean(y, axis=-1, keepdims=True)
    yc = y - mu
    var = jnp.mean(yc * yc, axis=-1, keepdims=True)
    return yc * lax.rsqrt(var + LN_EPS) * g + b


def _attn_out_kernel(alpha, o_ref, w_ref, x_ref, g_ref, b_ref, y_ref):
    y = alpha * x_ref[...] + jnp.dot(o_ref[...].astype(BF16), w_ref[...],
                                     preferred_element_type=F32)
    y_ref[...] = _layer_norm(y, g_ref[...], b_ref[...])


def _attn_out(o, w_bf16, x, g, b, alpha, tm):
    m, d = x.shape
    row = pl.BlockSpec((tm, d), lambda i: (i, 0))
    vec = pl.BlockSpec((1, d), lambda i: (0, 0))
    return pl.pallas_call(
        functools.partial(_attn_out_kernel, alpha),
        grid=(m // tm,),
        in_specs=[row, pl.BlockSpec((d, d), lambda i: (0, 0)), row, vec, vec],
        out_specs=row,
        out_shape=jax.ShapeDtypeStruct((m, d), F32),
        compiler_params=_cparams(("parallel",)),
        name="attn_out_ln",
    )(o, w_bf16, x, g, b)


def _layer_tail_kernel(alpha, x_ref, f_ref, g_ref, b_ref, gw_ref, gb_ref, p_ref, pw_ref, y_ref):
    x2 = _layer_norm(alpha * x_ref[...] + f_ref[...], g_ref[...], b_ref[...])
    z = jnp.dot(x2.astype(BF16), gw_ref[...], preferred_element_type=F32) + gb_ref[...]
    gate = 1.0 / (1.0 + jnp.exp(-z))
    e = jnp.dot(p_ref[...].astype(BF16), pw_ref[...], preferred_element_type=F32)
    y_ref[...] = x2 + gate * e


def _layer_tail(x, f, g, b, gw_bf16, gb, p, pw_bf16, alpha, tm):
    m, d = x.shape
    pd = p.shape[1]
    row = pl.BlockSpec((tm, d), lambda i: (i, 0))
    vec = pl.BlockSpec((1, d), lambda i: (0, 0))
    return pl.pallas_call(
        functools.partial(_layer_tail_kernel, alpha),
        grid=(m // tm,),
        in_specs=[row, row, vec, vec, pl.BlockSpec((d, d), lambda i: (0, 0)), vec,
                  pl.BlockSpec((tm, pd), lambda i: (i, 0)),
                  pl.BlockSpec((pd, d), lambda i: (0, 0))],
        out_specs=row,
        out_shape=jax.ShapeDtypeStruct((m, d), F32),
        compiler_params=_cparams(("parallel",)),
        name="layer_tail",
    )(x, f, g, b, gw_bf16, gb, p, pw_bf16)


STEP_PAGES = 8
STEP_KEYS = STEP_PAGES * PAGE
STEP_CHUNKS = STEP_KEYS // KEY_CHUNK


def _relayout_kernel(n_pool_steps, has_new, pages_t, want, *refs):
    page_refs = refs[1:1 + STEP_PAGES]
    new_ref = refs[1 + STEP_PAGES] if has_new else None
    outs = dict(zip(want, refs[len(refs) - len(want):]))
    j = pl.program_id(1)
    need_rows = "x" in want or "sums" in want

    def emit(rows, cols):
        if "xt" in want:
            outs["xt"][...] = (rows.T if cols is None else cols).astype(BF16)
        if need_rows and rows is None:
            rows = cols.T
        if "x" in want:
            outs["x"][...] = rows.astype(BF16)
        if "sums" in want:
            for i in range(STEP_CHUNKS):
                outs["sums"][pl.ds(i, 1), :] = jnp.sum(rows[i * KEY_CHUNK:(i + 1) * KEY_CHUNK], axis=0, keepdims=True)

    def from_pages():
        if pages_t:
            emit(None, jnp.concatenate([r[...] for r in page_refs], axis=1))
        else:
            emit(jnp.concatenate([r[...] for r in page_refs], axis=0), None)

    if has_new:
        pl.when(j < n_pool_steps)(from_pages)

        @pl.when(j >= n_pool_steps)
        def _():
            emit(new_ref[...], None)
    else:
        from_pages()


def _relayout(pool, table, new, pages_t, want):
    bsz, n_pages = table.shape
    w = pool.shape[1] if pages_t else pool.shape[2]
    n_pool_steps = n_pages // STEP_PAGES
    n_steps = n_pool_steps + (1 if new is not None else 0)
    lp = n_steps * STEP_KEYS
    last = n_pool_steps - 1

    def page_map(i):
        return lambda b, j, tbl: (tbl[b, STEP_PAGES * jnp.minimum(j, last) + i], 0, 0)

    page_block = (None, w, PAGE) if pages_t else (None, PAGE, w)
    in_specs = [pl.BlockSpec(page_block, page_map(i)) for i in range(STEP_PAGES)]
    args = [pool] * STEP_PAGES
    if new is not None:
        in_specs.append(pl.BlockSpec((None, STEP_KEYS, w), lambda b, j, tbl: (b, 0, 0)))
        args.append(new)
    specs = {"xt": (pl.BlockSpec((None, w, STEP_KEYS), lambda b, j, tbl: (b, 0, j)),
                    jax.ShapeDtypeStruct((bsz, w, lp), BF16)),
             "x": (pl.BlockSpec((None, STEP_KEYS, w), lambda b, j, tbl: (b, j, 0)),
                   jax.ShapeDtypeStruct((bsz, lp, w), BF16)),
             "sums": (pl.BlockSpec((None, None, STEP_CHUNKS, w), lambda b, j, tbl: (b, j, 0, 0)),
                      jax.ShapeDtypeStruct((bsz, n_steps, STEP_CHUNKS, w), F32))}
    outs = pl.pallas_call(
        functools.partial(_relayout_kernel, n_pool_steps, new is not None, pages_t, want),
        grid_spec=pltpu.PrefetchScalarGridSpec(
            num_scalar_prefetch=1,
            grid=(bsz, n_steps),
            in_specs=in_specs,
            out_specs=[specs[name][0] for name in want]),
        out_shape=[specs[name][1] for name in want],
        compiler_params=_cparams(("parallel", "arbitrary")),
        name="kv_relayout",
    )(table, *args)
    outs = dict(zip(want, outs))
    if "sums" in outs:
        outs["sums"] = outs["sums"].reshape(bsz, n_steps * STEP_CHUNKS, w)
    return outs


def _bucket_table():
    n = np.arange(REL_MAX_DIST + 1)
    exact = REL_BUCKETS // 2
    nf = np.maximum(n, 1).astype(np.float32)
    large = exact + (np.log(nf / np.float32(exact)) / np.float32(math.log(REL_MAX_DIST / exact))
                     * np.float32(REL_BUCKETS - exact)).astype(np.int32)
    return np.where(n < exact, n, np.minimum(large, REL_BUCKETS - 1)).astype(np.int32)


def _bias_strip_kernel(bk_ref, rb_ref, o_ref):
    bk = bk_ref[...]
    for h in range(N_HEADS):
        acc = jnp.zeros(bk.shape, F32)
        for bkt in range(REL_BUCKETS):
            acc = jnp.where(bk == bkt, rb_ref[bkt, h], acc)
        o_ref[h] = acc


def _strip_geometry(step):
    origin = -(-(step + REL_MAX_DIST - 1) // 128) * 128
    return origin, origin + step


def _strip_offset(step, delta):
    origin, _ = _strip_geometry(step)
    return pl.multiple_of(jnp.clip(origin - delta, 0, origin), 128)


def _bias_strip(rel_bias, qb, step):
    origin, width = _strip_geometry(step)
    qi = np.arange(qb)[:, None]
    z = np.arange(width)[None, :]
    dist = np.clip(qi - z + origin, 0, REL_MAX_DIST)
    bk = jnp.asarray(_bucket_table()[dist])
    out = pl.pallas_call(
        _bias_strip_kernel,
        in_specs=[pl.BlockSpec(memory_space=pltpu.VMEM), pl.BlockSpec(memory_space=pltpu.SMEM)],
        out_specs=pl.BlockSpec(memory_space=pltpu.VMEM),
        out_shape=jax.ShapeDtypeStruct((N_HEADS, qb, width), F32),
        name="bias_strip",
    )(bk, rel_bias)
    return out


def _attn_kernel(mode, qb, lp, q_pos0, k_sel, *refs):
    if mode == "dsa":
        (q_ref, kt_ref, v_ref, strip_ref, qi_ref, wi_ref, kit_ref, o_ref,
         qs_s, m_s, l_s, acc_s, key_s) = refs
    else:
        (q_ref, kt_ref, v_ref, strip_ref, kmt_ref, o_ref,
         qs_s, m_s, l_s, acc_s) = refs
    rows = GROUP * qb
    i = pl.program_id(1)
    q0 = q_pos0 + i * qb
    c_max = (q0 + qb - 1) // KEY_CHUNK
    n_chunks = c_max + 1
    qpos = q0 + lax.broadcasted_iota(I32, (qb, 1), 0)
    lane = lax.broadcasted_iota(I32, (1, KEY_CHUNK), 1)
    scale = HEAD_DIM ** -0.5

    for g in range(N_KV):
        qs_s[g] = (q_ref[g] * scale).astype(BF16)
    m_s[...] = jnp.full(m_s.shape, NEG, F32)
    l_s[...] = jnp.zeros(l_s.shape, F32)
    acc_s[...] = jnp.zeros(acc_s.shape, F32)

    if mode == "dsa":
        qi = qi_ref[...].astype(BF16)
        wi = wi_ref[...]

        def score_chunk(c, carry):
            k0 = pl.multiple_of(c * KEY_CHUNK, KEY_CHUNK)
            s = jnp.dot(qi, kit_ref[:, pl.ds(k0, KEY_CHUNK)], preferred_element_type=F32)
            s = jnp.maximum(s, 0.0) * wi
            sc = s[0:qb]
            for h in range(1, IDX_HEADS):
                sc = sc + s[h * qb:(h + 1) * qb]
            sc = jnp.where(k0 + lane <= qpos, sc, -jnp.inf)
            bits = pltpu.bitcast(sc, I32)
            bits = jnp.where(bits == INT_MIN, 0, bits)
            key_s[:, pl.ds(k0, KEY_CHUNK)] = jnp.where(bits < 0, bits ^ 0x7FFFFFFF, bits)
            return carry

        scan = STEP_KEYS if qb <= 32 else ROWS_STEP
        n_scan = (n_chunks * KEY_CHUNK + scan - 1) // scan
        lax.fori_loop(0, n_scan * (scan // KEY_CHUNK), score_chunk, 0)
        scan_lane = lax.broadcasted_iota(I32, (1, scan), 1)

        def count(pred):
            def body(c, acc):
                k0 = pl.multiple_of(c * scan, scan)
                return acc + pred(key_s[:, pl.ds(k0, scan)], k0 + scan_lane).astype(I32)
            acc = lax.fori_loop(0, n_scan, body, jnp.zeros((qb, scan), I32))
            return jnp.sum(acc, axis=1, keepdims=True)

        def bit_step(bi, thr):
            cand = thr + jnp.left_shift(jnp.int32(1), 31 - bi)
            cnt = count(lambda k, kidx: k >= cand)
            return jnp.where(cnt >= k_sel, cand, thr)

        thr = lax.fori_loop(0, 32, bit_step, jnp.full((qb, 1), INT_MIN, I32))
        n_gt = count(lambda k, kidx: k > thr)
        n_ge = count(lambda k, kidx: k >= thr)
        need = k_sel - n_gt

        def tie_limit():
            n_bits = max(1, (lp - 1).bit_length())

            def idx_step(bi, lo):
                cand = lo + jnp.left_shift(jnp.int32(1), n_bits - 1 - bi)
                cnt = count(lambda k, kidx: (k == thr) & (kidx <= cand))
                return jnp.where(cnt < need, cand, lo)

            lo = lax.fori_loop(0, n_bits, idx_step, jnp.full((qb, 1), -1, I32))
            return lo + 1

        tie_idx = lax.cond(jnp.max(n_ge - n_gt - need) > 0, tie_limit,
                           lambda: jnp.full((qb, 1), lp, I32))
    else:
        blk = lax.broadcasted_iota(I32, (qb, kmt_ref.shape[2]), 1)
        sel_blocks = []
        for g in range(N_KV):
            qg = q_ref[g]
            qsum = qg[0:qb]
            for j in range(1, GROUP):
                qsum = qsum + qg[j * qb:(j + 1) * qb]
            gate = jnp.dot(qsum, kmt_ref[g], preferred_element_type=F32,
                           precision=lax.Precision.HIGHEST)
            gate = jnp.where(blk < c_max, gate, -jnp.inf)
            sel = jnp.zeros(blk.shape, jnp.bool_)
            for _ in range(MOBA_TOPK):
                mx = jnp.max(gate, axis=1, keepdims=True)
                first = jnp.min(jnp.where(gate == mx, blk, blk.shape[1]), axis=1, keepdims=True)
                hit = blk == first
                sel = sel | (hit & (blk < c_max))
                gate = jnp.where(hit, -jnp.inf, gate)
            sel_blocks.append(sel.astype(I32))

    step_lane = lax.broadcasted_iota(I32, (1, ROWS_STEP), 1)

    def attend(c, carry):
        k0 = pl.multiple_of(c * ROWS_STEP, ROWS_STEP)
        kidx = k0 + step_lane
        causal = kidx <= qpos
        off = _strip_offset(ROWS_STEP, q0 - k0)
        if mode == "dsa":
            k = key_s[:, pl.ds(k0, ROWS_STEP)]
            mask1 = ((k > thr) | ((k == thr) & (kidx <= tie_idx))) & causal
            mask = jnp.concatenate([mask1] * GROUP, axis=0)

        logits = [jnp.dot(qs_s[g], kt_ref[g, :, pl.ds(k0, ROWS_STEP)], preferred_element_type=F32)
                  for g in range(N_KV)]
        probs, alphas = [], []
        for g in range(N_KV):
            if mode != "dsa":
                parts = []
                for i in range(ROWS_STEP // KEY_CHUNK):
                    cb = c * (ROWS_STEP // KEY_CHUNK) + i
                    picked = jnp.sum(jnp.where(blk == cb, sel_blocks[g], 0), axis=1, keepdims=True)
                    own = causal[:, i * KEY_CHUNK:(i + 1) * KEY_CHUNK].astype(I32)
                    parts.append(jnp.where(cb == c_max, own, picked))
                mask1 = jnp.concatenate(parts, axis=1) > 0
                mask = jnp.concatenate([mask1] * GROUP, axis=0)
            s = jnp.where(mask, logits[g] + strip_ref[g, :, pl.ds(off, ROWS_STEP)], NEG)
            m_old = m_s[g]
            m_new = jnp.maximum(m_old, jnp.max(s, axis=1, keepdims=True))
            alpha = jnp.exp(m_old - m_new)
            p = jnp.exp(s - m_new)
            l_s[g] = alpha * l_s[g] + jnp.sum(p, axis=1, keepdims=True)
            m_s[g] = m_new
            probs.append(p.astype(BF16))
            alphas.append(alpha)
        vblk = v_ref[pl.ds(k0, ROWS_STEP), :]
        for g in range(N_KV):
            acc_s[g] = alphas[g] * acc_s[g] + jnp.dot(probs[g], vblk, preferred_element_type=F32)
        return carry

    lax.fori_loop(0, (n_chunks * KEY_CHUNK + ROWS_STEP - 1) // ROWS_STEP, attend, 0)
    for g in range(N_KV):
        o_ref[g] = acc_s[g][:, g * HEAD_DIM:(g + 1) * HEAD_DIM] / l_s[g]


def _sparse_attention(mode, q_st, kt, v, strip, extra, qb, q_pos0, k_sel=0):
    bsz, nq = q_st.shape[:2]
    lp = v.shape[1]
    rows = GROUP * qb
    assert q_pos0 % ROWS_STEP == 0 and (nq == 1 or qb % ROWS_STEP == 0) and lp % ROWS_STEP == 0
    per_q = lambda *tail: pl.BlockSpec((None, None) + tail, lambda b, i: (b, i) + (0,) * len(tail))
    per_b = lambda *tail: pl.BlockSpec((None,) + tail, lambda b, i: (b,) + (0,) * len(tail))
    in_specs = [per_q(N_KV, rows, HEAD_DIM), per_b(N_KV, HEAD_DIM, lp), per_b(lp, N_KV * HEAD_DIM),
                pl.BlockSpec(strip.shape, lambda b, i: (0, 0, 0))]
    scratch = [pltpu.VMEM((N_KV, rows, HEAD_DIM), BF16),
               pltpu.VMEM((N_KV, rows, 1), F32),
               pltpu.VMEM((N_KV, rows, 1), F32),
               pltpu.VMEM((N_KV, rows, N_KV * HEAD_DIM), F32)]
    if mode == "dsa":
        qi_st, wi_st, kit = extra
        in_specs += [per_q(IDX_HEADS * qb, IDX_DIM), per_q(IDX_HEADS * qb, 1), per_b(IDX_DIM, lp)]
        scratch.append(pltpu.VMEM((qb, lp), I32))
    else:
        (kmt,) = extra
        in_specs.append(per_b(N_KV, HEAD_DIM, kmt.shape[3]))
    return pl.pallas_call(
        functools.partial(_attn_kernel, mode, qb, lp, q_pos0, k_sel),
        grid=(bsz, nq),
        in_specs=in_specs,
        out_specs=per_q(N_KV, rows, HEAD_DIM),
        out_shape=jax.ShapeDtypeStruct((bsz, nq, N_KV, rows, HEAD_DIM), F32),
        scratch_shapes=scratch,
        compiler_params=_cparams(("parallel", "arbitrary")),
        name=mode + "_attention",
    )(q_st, kt, v, strip, *extra)


QT = 128


def _attn_t_kernel(mode, lp, k_sel, *refs):
    if mode == "dsa":
        (qt_ref, k_ref, vt_ref, strip_ref, qit_ref, wit_ref, ki_ref, o_ref,
         qbd_s, m_s, l_s, acc_s, key_s) = refs
    else:
        (qt_ref, k_ref, vt_ref, strip_ref, km_ref, o_ref,
         qbd_s, m_s, l_s, acc_s) = refs
    kv_dim = N_KV * HEAD_DIM
    cols = GROUP * QT
    i = pl.program_id(1)
    q0 = i * QT
    c_max = (q0 + QT - 1) // KEY_CHUNK
    n_chunks = c_max + 1
    qpos = q0 + lax.broadcasted_iota(I32, (1, QT), 1)
    krow = lax.broadcasted_iota(I32, (KEY_CHUNK, 1), 0)
    scale = HEAD_DIM ** -0.5

    qbd_s[...] = jnp.zeros(qbd_s.shape, BF16)
    for g in range(N_KV):
        qbd_s[g * HEAD_DIM:(g + 1) * HEAD_DIM, g * cols:(g + 1) * cols] = (
            qt_ref[g * HEAD_DIM:(g + 1) * HEAD_DIM, :] * scale).astype(BF16)
    m_s[...] = jnp.full(m_s.shape, NEG, F32)
    l_s[...] = jnp.zeros(l_s.shape, F32)
    acc_s[...] = jnp.zeros(acc_s.shape, F32)

    if mode == "dsa":
        qit = qit_ref[...].astype(BF16)
        wit = wit_ref[...]

        def score_chunk(c, carry):
            k0 = pl.multiple_of(c * KEY_CHUNK, KEY_CHUNK)
            kib = ki_ref[pl.ds(k0, KEY_CHUNK), :]
            sc = jnp.zeros((KEY_CHUNK, QT), F32)
            for hp in range(IDX_HEADS // 2):
                lanes = slice(2 * hp * QT, 2 * (hp + 1) * QT)
                s = jnp.maximum(jnp.dot(kib, qit[:, lanes], preferred_element_type=F32), 0.0) * wit[:, lanes]
                sc = sc + s[:, :QT] + s[:, QT:]
            sc = jnp.where(k0 + krow <= qpos, sc, -jnp.inf)
            bits = pltpu.bitcast(sc, I32)
            bits = jnp.where(bits == INT_MIN, 0, bits)
            key_s[pl.ds(k0, KEY_CHUNK), :] = jnp.where(bits < 0, bits ^ 0x7FFFFFFF, bits)
            return carry

        lax.fori_loop(0, n_chunks, score_chunk, 0)

        def count(pred):
            def body(c, acc):
                k0 = pl.multiple_of(c * KEY_CHUNK, KEY_CHUNK)
                hit = pred(key_s[pl.ds(k0, KEY_CHUNK), :], k0).astype(I32)
                return acc + jnp.sum(hit.reshape(KEY_CHUNK // 8, 8, QT), axis=0)
            acc = lax.fori_loop(0, n_chunks, body, jnp.zeros((8, QT), I32))
            return jnp.sum(acc, axis=0, keepdims=True)

        def bit_step(bi, thr):
            cand = thr + jnp.left_shift(jnp.int32(1), 31 - bi)
            cnt = count(lambda k, k0: k >= cand)
            return jnp.where(cnt >= k_sel, cand, thr)

        thr = lax.fori_loop(0, 32, bit_step, jnp.full((1, QT), INT_MIN, I32))
        n_gt = count(lambda k, k0: k > thr)
        n_ge = count(lambda k, k0: k >= thr)
        need = k_sel - n_gt

        def tie_limit():
            n_bits = max(1, (lp - 1).bit_length())

            def idx_step(bi, lo):
                cand = lo + jnp.left_shift(jnp.int32(1), n_bits - 1 - bi)
                cnt = count(lambda k, k0: (k == thr) & (k0 + krow <= cand))
                return jnp.where(cnt < need, cand, lo)

            lo = lax.fori_loop(0, n_bits, idx_step, jnp.full((1, QT), -1, I32))
            return lo + 1

        tie_idx = lax.cond(jnp.max(n_ge - n_gt - need) > 0, tie_limit,
                           lambda: jnp.full((1, QT), lp, I32))
    else:
        nbp = km_ref.shape[1]
        blk = lax.broadcasted_iota(I32, (nbp, QT), 0)
        sel_blocks = []
        for g in range(N_KV):
            qg = qt_ref[g * HEAD_DIM:(g + 1) * HEAD_DIM, :]
            qsum = qg[:, 0:QT]
            for j in range(1, GROUP):
                qsum = qsum + qg[:, j * QT:(j + 1) * QT]
            gate = jnp.dot(km_ref[g], qsum, preferred_element_type=F32, precision=lax.Precision.HIGHEST)
            gate = jnp.where(blk < c_max, gate, -jnp.inf)
            sel = jnp.zeros(blk.shape, I32)
            for _ in range(MOBA_TOPK):
                mx = jnp.max(gate, axis=0, keepdims=True)
                first = jnp.min(jnp.where(gate == mx, blk, nbp), axis=0, keepdims=True)
                hit = blk == first
                sel = jnp.where(hit & (blk < c_max), 1, sel)
                gate = jnp.where(hit, -jnp.inf, gate)
            sel_blocks.append(sel)

    def attend(c, carry):
        k0 = pl.multiple_of(c * KEY_CHUNK, KEY_CHUNK)
        kidx = k0 + krow
        causal = kidx <= qpos
        off = _strip_offset(KEY_CHUNK, q0 - k0)
        if mode == "dsa":
            k = key_s[pl.ds(k0, KEY_CHUNK), :]
            mask = ((k > thr) | ((k == thr) & (kidx <= tie_idx))) & causal
        kblk = k_ref[pl.ds(k0, KEY_CHUNK), :]

        def logits(g):
            return jnp.dot(kblk, qbd_s[:, g * cols:(g + 1) * cols], preferred_element_type=F32)

        s_next = logits(0)
        for g in range(N_KV):
            if mode != "dsa":
                picked = jnp.sum(jnp.where(blk == c, sel_blocks[g], 0), axis=0, keepdims=True)
                mask = jnp.where(c == c_max, causal.astype(I32), picked) > 0
            s_all = s_next
            if g + 1 < N_KV:
                s_next = logits(g + 1)
            probs, alphas = [], []
            for j in range(GROUP):
                h = g * GROUP + j
                lanes = slice(j * QT, (j + 1) * QT)
                s = s_all[:, lanes] + strip_ref[g, pl.ds(off, KEY_CHUNK), lanes]
                s = jnp.where(mask, s, NEG)
                m_old = m_s[pl.ds(h, 1), :]
                m_new = jnp.maximum(m_old, jnp.max(s, axis=0, keepdims=True))
                alpha = jnp.exp(m_old - m_new)
                p = jnp.exp(s - m_new)
                l_s[pl.ds(h, 1), :] = alpha * l_s[pl.ds(h, 1), :] + jnp.sum(p, axis=0, keepdims=True)
                m_s[pl.ds(h, 1), :] = m_new
                probs.append(p.astype(BF16))
                alphas.append(alpha)
            pv = jnp.dot(vt_ref[g * HEAD_DIM:(g + 1) * HEAD_DIM, pl.ds(k0, KEY_CHUNK)],
                         jnp.concatenate(probs, axis=1), preferred_element_type=F32)
            acc_s[g] = jnp.concatenate(alphas, axis=1) * acc_s[g] + pv
        return carry

    lax.fori_loop(0, n_chunks, attend, 0)
    for g in range(N_KV):
        for j in range(0, GROUP, 2):
            h = g * GROUP + j
            pair = [acc_s[g][:, (j + i) * QT:(j + i + 1) * QT] * (1.0 / l_s[pl.ds(h + i, 1), :]) for i in range(2)]
            o_ref[:, h * HEAD_DIM:(h + 2) * HEAD_DIM] = jnp.concatenate(pair, axis=0).T


def _sparse_attention_t(mode, qt, k, vt, strip_t, extra, k_sel=0):
    bsz, nq = qt.shape[:2]
    lp = k.shape[1]
    kv_dim = N_KV * HEAD_DIM
    cols = GROUP * QT
    per_q = lambda *tail: pl.BlockSpec((None, None) + tail, lambda b, i: (b, i) + (0,) * len(tail))
    per_b = lambda *tail: pl.BlockSpec((None,) + tail, lambda b, i: (b,) + (0,) * len(tail))
    in_specs = [per_q(kv_dim, cols), per_b(lp, kv_dim), per_b(kv_dim, lp),
                pl.BlockSpec(strip_t.shape, lambda b, i: (0, 0, 0))]
    scratch = [pltpu.VMEM((kv_dim, N_KV * cols), BF16),
               pltpu.VMEM((N_HEADS, QT), F32),
               pltpu.VMEM((N_HEADS, QT), F32),
               pltpu.VMEM((N_KV, HEAD_DIM, cols), F32)]
    if mode == "dsa":
        qit, wit, ki = extra
        in_specs += [per_q(IDX_DIM, IDX_HEADS * QT), per_q(1, IDX_HEADS * QT), per_b(lp, IDX_DIM)]
        scratch.append(pltpu.VMEM((lp, QT), I32))
    else:
        (km,) = extra
        in_specs.append(per_b(N_KV, km.shape[2], HEAD_DIM))
    return pl.pallas_call(
        functools.partial(_attn_t_kernel, mode, lp, k_sel),
        grid=(bsz, nq),
        in_specs=in_specs,
        out_specs=per_q(QT, N_HEADS * HEAD_DIM),
        out_shape=jax.ShapeDtypeStruct((bsz, nq, QT, N_HEADS * HEAD_DIM), F32),
        scratch_shapes=scratch,
        compiler_params=_cparams(("parallel", "arbitrary")),
        name=mode + "_attention_t",
    )(qt, k, vt, strip_t, *extra)


def _top_rows(s, k, payload=None, rank=None):
    rid = lax.broadcasted_iota(I32, s.shape, 0) if rank is None else rank
    vals, ids = [], []
    for _ in range(k):
        mx = jnp.max(s, axis=0, keepdims=True)
        first = jnp.min(jnp.where(s == mx, rid, jnp.iinfo(jnp.int32).max), axis=0, keepdims=True)
        hit = rid == first
        vals.append(mx)
        if payload is None:
            ids.append(first)
        else:
            ids.append(jnp.max(jnp.where(hit, payload, -1), axis=0, keepdims=True))
        s = jnp.where(hit, -jnp.inf, s)
    return jnp.concatenate(vals, axis=0), jnp.concatenate(ids, axis=0)


def _top_rows_paired(s, k):
    half = s.shape[0] // 2
    lo, hi = s[:half], s[half:]
    rid = lax.broadcasted_iota(I32, lo.shape, 0)
    first_lo = lo >= hi
    top = jnp.where(first_lo, lo, hi)
    rest = jnp.where(first_lo, hi, lo)
    top_id = jnp.where(first_lo, rid, rid + half)
    rest_id = jnp.where(first_lo, rid + half, rid)
    vals, ids = [], []
    for _ in range(k):
        mx = jnp.max(top, axis=0, keepdims=True)
        first = jnp.min(jnp.where(top == mx, top_id, jnp.iinfo(jnp.int32).max), axis=0, keepdims=True)
        hit = top_id == first
        vals.append(mx)
        ids.append(first)
        top = jnp.where(hit, rest, top)
        top_id = jnp.where(hit, rest_id, top_id)
        rest = jnp.where(hit, -jnp.inf, rest)
    return jnp.concatenate(vals, axis=0), jnp.concatenate(ids, axis=0)


def _peer_route_kernel(x_ref, wqt_ref, keys_ref, row_ref, par_ref, g_ref):
    xb = x_ref[...].astype(BF16)
    qt = lax.dot_general(wqt_ref[...], xb, (((1,), (1,)), ((), ())),
                         preferred_element_type=F32)
    n_t = x_ref.shape[0]
    piece = lax.broadcasted_iota(I32, (11 * 8, n_t), 0) // 8
    within = lax.broadcasted_iota(I32, (11 * 8, n_t), 0) % 8
    by_b = (piece >= 2) & (piece < 10)
    ca = jnp.where(piece < 2, 0, jnp.where(by_b, within, 8 + within))
    cb = jnp.where(piece == 0, within, jnp.where(piece == 1, 8 + within, jnp.where(by_b, piece - 2, 0)))
    cand_ok = ((ca + 1) * (cb + 1) <= PEER_TOPK) & jnp.logical_not(by_b & (within == 0))
    cand_rank = ca * PEER_TOPK + cb

    def pieces(first, second, combine):
        out = [combine(first[0:1], second[0:8]), combine(first[0:1], second[8:16])]
        out += [combine(first[0:8], second[b:b + 1]) for b in range(8)]
        out.append(combine(first[8:16], second[0:1]))
        return jnp.concatenate(out, axis=0)

    e_rows, g_rows = [], []
    for h in range(PEER_HEADS):
        sv, si = [], []
        for c in range(2):
            hc = 2 * h + c
            qhc = qt[hc * PEER_HALF:(hc + 1) * PEER_HALF].astype(BF16)
            s = jnp.dot(keys_ref[hc], qhc, preferred_element_type=F32)
            v_, i_ = _top_rows_paired(s, PEER_TOPK)
            sv.append(v_)
            si.append(i_)
        cand = jnp.where(cand_ok, pieces(sv[0], sv[1], lambda u, w: u + w), -jnp.inf)
        cidx = pieces(si[0], si[1], lambda u, w: u * PEER_NKEYS + w)
        gv, ge = _top_rows(cand, PEER_TOPK, payload=cidx, rank=cand_rank)
        ex = jnp.exp(gv - gv[0:1])
        g_rows.append(ex / jnp.sum(ex, axis=0, keepdims=True))
        e_rows.append(ge)
    e_t = jnp.concatenate(e_rows, axis=0).T
    row_ref[...] = jnp.right_shift(e_t, 1) * 8
    par_ref[...] = e_t & 1
    g_ref[...] = jnp.concatenate(g_rows, axis=0).T


def _peer_route(x, wqt_bf16, keys_bf16, tb):
    m, d = x.shape
    out = pl.BlockSpec((tb, PEER_PAIRS), lambda i: (i, 0))
    return pl.pallas_call(
        _peer_route_kernel,
        grid=(m // tb,),
        in_specs=[pl.BlockSpec((tb, d), lambda i: (i, 0)),
                  pl.BlockSpec(wqt_bf16.shape, lambda i: (0, 0)),
                  pl.BlockSpec(keys_bf16.shape, lambda i: (0, 0, 0))],
        out_specs=[out, out, out],
        out_shape=[jax.ShapeDtypeStruct((m, PEER_PAIRS), I32),
                   jax.ShapeDtypeStruct((m, PEER_PAIRS), I32),
                   jax.ShapeDtypeStruct((m, PEER_PAIRS), F32)],
        compiler_params=_cparams(("parallel",)),
        name="peer_route",
    )(x, wqt_bf16, keys_bf16)


FOLD_PAIRS = 32
HI_MASK = 0xFFFF0000
UP_SLOTS = 8
TILE_ROWS = 8


def _pack_table(w):
    n, d = w.shape
    bits = lax.bitcast_convert_type(w.astype(BF16), jnp.uint16).astype(jnp.uint32)
    bits = bits.reshape(n // 2, 2, d // 128, 128)
    return (bits[:, 0] | (bits[:, 1] << 16)).astype(jnp.uint32).reshape(n // 2 * (d // 128), 128)


def _load_tile(tbl_ref, row_ref, idx):
    return tbl_ref[pl.ds(pl.multiple_of(row_ref[idx], TILE_ROWS), TILE_ROWS), :]


def _peer_up_kernel(tb, row_ref, par_ref, x_ref, g_ref, fold_ref, tbl_ref, c_ref, prod_s):
    diag = (lax.broadcasted_iota(I32, (PEER_PAIRS, PEER_PAIRS), 0)
            == lax.broadcasted_iota(I32, (PEER_PAIRS, PEER_PAIRS), 1))

    def gather(t, slot):
        xt = x_ref[t]
        rows_t = row_ref.at[pl.ds(t * PEER_PAIRS, PEER_PAIRS)]
        for p in range(0, PEER_PAIRS, 2):
            w0 = _load_tile(tbl_ref, rows_t, p)
            w1 = _load_tile(tbl_ref, rows_t, p + 1)
            even = [pltpu.bitcast(jnp.left_shift(w, jnp.uint32(16)), F32) * xt for w in (w0, w1)]
            odd = [pltpu.bitcast(w & jnp.uint32(HI_MASK), F32) * xt for w in (w0, w1)]
            rows = pl.ds(p * TILE_ROWS, 2 * TILE_ROWS)
            prod_s[slot, rows, 0:128] = jnp.concatenate(even, axis=0).astype(BF16)
            prod_s[slot, rows, 128:256] = jnp.concatenate(odd, axis=0).astype(BF16)

    def reduce(t, slot):
        depth = FOLD_PAIRS * TILE_ROWS
        parts = [jnp.dot(fold_ref[...], prod_s[slot, pl.ds(i * depth, depth), :], preferred_element_type=F32)
                 for i in range(PEER_PAIRS // FOLD_PAIRS)]
        part = jnp.concatenate(parts, axis=0)
        rows = []
        for half in range(2):
            col = jnp.sum(part[:, half * 128:(half + 1) * 128], axis=1, keepdims=True)
            rows.append(jnp.sum(jnp.where(diag, col, 0.0), axis=0, keepdims=True))
        a = jnp.where(par_ref[pl.ds(t, 1), :] == 1, rows[1], rows[0])
        act = 0.5 * a * (1.0 + lax.erf(a * (2.0 ** -0.5)))
        c_ref[pl.ds(t, 1), :] = g_ref[pl.ds(t, 1), :] * act

    prod_s[UP_SLOTS:2 * UP_SLOTS] = jnp.zeros((UP_SLOTS,) + prod_s.shape[1:], BF16)

    def step(k, carry):
        t0 = 2 * UP_SLOTS * k
        for j in range(UP_SLOTS):
            gather(t0 + j, j)
            reduce(jnp.maximum(t0 - UP_SLOTS + j, 0), UP_SLOTS + j)
        for j in range(UP_SLOTS):
            gather(t0 + UP_SLOTS + j, UP_SLOTS + j)
            reduce(t0 + j, j)
        return carry

    lax.fori_loop(0, tb // (2 * UP_SLOTS), step, 0)
    for j in range(UP_SLOTS):
        reduce(tb - UP_SLOTS + j, UP_SLOTS + j)


DOWN_GROUP = 32


def _peer_down_kernel(tb, row_ref, par_ref, c_ref, spread_ref, tbl_ref, f_ref, tile_s, coef_s):
    width = PEER_PAIRS * 2 * TILE_ROWS
    lane = lax.broadcasted_iota(I32, (TILE_ROWS, width), 1)
    own_row = ((lane & (2 * TILE_ROWS - 1)) >> 1) == lax.broadcasted_iota(I32, (TILE_ROWS, width), 0)
    lane_par = (lax.broadcasted_iota(I32, (DOWN_GROUP, width), 1) & 1).astype(F32)

    def gather(t, slot):
        rows_t = row_ref.at[pl.ds(t * PEER_PAIRS, PEER_PAIRS)]
        for p in range(PEER_PAIRS):
            tile_s[slot, pl.ds(p * TILE_ROWS, TILE_ROWS), :] = _load_tile(tbl_ref, rows_t, p)

    def apply(t, j, slot):
        tiles = pltpu.bitcast(tile_s[slot], BF16)
        lhs = [jnp.where(own_row, coef_s[pl.ds(part * DOWN_GROUP + j, 1), :], 0.0) for part in range(2)]
        out = jnp.dot(jnp.concatenate(lhs, axis=0).astype(BF16), tiles, preferred_element_type=F32)
        f_ref[t] = out[0:TILE_ROWS] + out[TILE_ROWS:2 * TILE_ROWS]

    def group(gi, carry):
        t0 = pl.multiple_of(gi * DOWN_GROUP, DOWN_GROUP)
        gather(t0, 0)
        c = c_ref[pl.ds(t0, DOWN_GROUP), :]
        c_hi = c.astype(BF16)
        c_lo = (c - c_hi.astype(F32)).astype(BF16)
        par = par_ref[pl.ds(t0, DOWN_GROUP), :].astype(BF16)
        rep = jnp.dot(jnp.concatenate([c_hi, c_lo, par], axis=0), spread_ref[...],
                      preferred_element_type=F32)
        keep = rep[2 * DOWN_GROUP:] == lane_par
        coef_s[0:DOWN_GROUP, :] = jnp.where(keep, rep[0:DOWN_GROUP], 0.0)
        coef_s[DOWN_GROUP:2 * DOWN_GROUP, :] = jnp.where(keep, rep[DOWN_GROUP:2 * DOWN_GROUP], 0.0)
        for j in range(DOWN_GROUP):
            if j + 1 < DOWN_GROUP:
                gather(t0 + j + 1, (j + 1) % 2)
            apply(t0 + j, j, j % 2)
        return carry

    lax.fori_loop(0, tb // DOWN_GROUP, group, 0)


def _peer_experts(x, row, par, g, u_tbl, v_tbl, tb):
    m, d = x.shape
    assert m % tb == 0 and tb % (2 * UP_SLOTS) == 0 and tb % DOWN_GROUP == 0 and d == TILE_ROWS * 128
    x3 = x.reshape(m, d // 128, 128)
    flat = row.reshape(m * PEER_PAIRS)
    smem = pl.BlockSpec((tb * PEER_PAIRS,), lambda i: (i,), memory_space=pltpu.SMEM)
    vrow = pl.BlockSpec((tb, PEER_PAIRS), lambda i: (i, 0))
    tile = pl.BlockSpec((tb, d // 128, 128), lambda i: (i, 0, 0))
    resident = pl.BlockSpec(memory_space=pltpu.VMEM)
    const = lambda a: pl.BlockSpec(a.shape, lambda i: (0, 0))
    fold = jnp.asarray(np.arange(FOLD_PAIRS * TILE_ROWS)[None, :] // TILE_ROWS == np.arange(FOLD_PAIRS)[:, None], BF16)
    width = PEER_PAIRS * 2 * TILE_ROWS
    spread = jnp.asarray(np.arange(width)[None, :] // (2 * TILE_ROWS) == np.arange(PEER_PAIRS)[:, None], BF16)
    c = pl.pallas_call(
        functools.partial(_peer_up_kernel, tb),
        grid=(m // tb,),
        in_specs=[smem, vrow, tile, vrow, const(fold), resident],
        out_specs=vrow,
        out_shape=jax.ShapeDtypeStruct((m, PEER_PAIRS), F32),
        scratch_shapes=[pltpu.VMEM((2 * UP_SLOTS, PEER_PAIRS * TILE_ROWS, 256), BF16)],
        compiler_params=_cparams(("arbitrary",)),
        name="peer_up",
    )(flat, par, x3, g, fold, u_tbl)
    f = pl.pallas_call(
        functools.partial(_peer_down_kernel, tb),
        grid=(m // tb,),
        in_specs=[smem, vrow, vrow, const(spread), resident],
        out_specs=tile,
        out_shape=jax.ShapeDtypeStruct((m, d // 128, 128), F32),
        scratch_shapes=[pltpu.VMEM((2, PEER_PAIRS * TILE_ROWS, 128), jnp.uint32),
                        pltpu.VMEM((2 * DOWN_GROUP, width), F32)],
        compiler_params=_cparams(("arbitrary",)),
        name="peer_down",
    )(flat, par, c, spread, v_tbl)
    return f.reshape(m, d)


def _stack_heads(a, qb, n_lead):
    bsz, q, n, hd = a.shape
    n_in = n // n_lead
    a = a.reshape(bsz, q // qb, qb, n_lead, n_in, hd)
    return a.transpose(0, 1, 3, 4, 2, 5).reshape(bsz, q // qb, n_lead, n_in * qb, hd)


def _unstack_heads(o, qb):
    bsz, nq = o.shape[:2]
    o = o.reshape(bsz, nq, N_KV, GROUP, qb, HEAD_DIM).transpose(0, 1, 4, 2, 3, 5)
    return o.reshape(bsz * nq * qb, N_HEADS * HEAD_DIM)


def _run_trunk(x, p, past, prm, qb, tm, tb_route, tb_exp):
    bsz, q_len, d = x.shape
    depth = prm["w_o"].shape[0]
    m = bsz * q_len
    tm, tb_route, tb_exp = min(tm, m), min(tb_route, m), min(tb_exp, m)
    alpha = (2 * depth) ** 0.25
    attn_dim = N_HEADS * HEAD_DIM
    kv_dim = N_KV * HEAD_DIM
    if past is None:
        q_pos0 = 0
        n_pages = q_len // PAGE
        table = jnp.arange(bsz * n_pages, dtype=I32).reshape(bsz, n_pages)
    else:
        cache_k, cache_v, cache_ki, table = past
        q_pos0 = table.shape[1] * PAGE
    on_lanes = past is None and q_len % QT == 0
    if on_lanes:
        qb = QT
    nq = q_len // qb
    strip = _bias_strip(prm["rel_bias"], qb, KEY_CHUNK if on_lanes else ROWS_STEP)
    strip = strip.reshape(N_KV, GROUP, qb, strip.shape[2])
    if on_lanes:
        strip = strip.transpose(0, 3, 1, 2).reshape(N_KV, strip.shape[3], GROUP * qb)
    else:
        strip = strip.reshape(N_KV, GROUP * qb, strip.shape[3])
    xf = x.reshape(m, d)
    ks, vs, kis = [], [], []

    def relayout(new_rows, cache, layer, want):
        w = new_rows.shape[2]
        if past is None:
            return _relayout(new_rows.reshape(bsz * n_pages, PAGE, w), table, None, False, want)
        n_pool = cache.shape[1]
        pool = cache.reshape((cache.shape[0] * n_pool,) + cache.shape[2:])
        pool_t = jnp.moveaxis(pool, 1, -1).reshape(pool.shape[0], w, PAGE)
        pad = jnp.zeros((bsz, STEP_KEYS - q_len, w), F32)
        return _relayout(pool_t, table + layer * n_pool, jnp.concatenate([new_rows, pad], axis=1), True, want)

    for l in range(depth):
        dsa = l % 2 == 0
        w_in = prm["w_in_dsa"][l // 2] if dsa else prm["w_in_moba"][l // 2]
        if on_lanes:
            proj = _in_proj_lanes(xf, w_in, dsa, tm)
            q_in = proj[0].reshape(bsz, nq, kv_dim, GROUP * qb)
            k = proj[1].reshape(bsz, q_len, kv_dim)
            v = proj[2].reshape(bsz, q_len, kv_dim)
        else:
            n_in = w_in.shape[1]
            n_pad = -(-n_in // 128) * 128
            proj = _matmul(xf, jnp.pad(w_in, ((0, 0), (0, n_pad - n_in))).astype(BF16), tm)
            qh = proj[:, :attn_dim].reshape(bsz, q_len, N_HEADS, HEAD_DIM)
            k = proj[:, attn_dim:attn_dim + kv_dim].reshape(bsz, q_len, kv_dim)
            v = proj[:, attn_dim + kv_dim:attn_dim + 2 * kv_dim].reshape(bsz, q_len, kv_dim)
        ks.append(k.reshape(bsz, q_len, N_KV, HEAD_DIM))
        vs.append(v.reshape(bsz, q_len, N_KV, HEAD_DIM))
        k_lay = relayout(k, None if past is None else cache_k, l,
                         (("x",) if on_lanes else ("xt",)) + (() if dsa else ("sums",)))
        v_lay = relayout(v, None if past is None else cache_v, l, ("xt",) if on_lanes else ("x",))
        if on_lanes:
            kr, vt = k_lay["x"], v_lay["xt"]
        else:
            kt = k_lay["xt"].reshape(bsz, N_KV, HEAD_DIM, k_lay["xt"].shape[2])
            vr = v_lay["x"]
            q_in = _stack_heads(qh, qb, N_KV)
        if dsa:
            if on_lanes:
                ki = proj[3].reshape(bsz, q_len, IDX_DIM)
            else:
                off = attn_dim + 2 * kv_dim
                qi = proj[:, off:off + IDX_HEADS * IDX_DIM].reshape(bsz, q_len, IDX_HEADS, IDX_DIM)
                off += IDX_HEADS * IDX_DIM
                ki = proj[:, off:off + IDX_DIM].reshape(bsz, q_len, IDX_DIM)
                wi = proj[:, off + IDX_DIM:off + IDX_DIM + IDX_HEADS].reshape(bsz, q_len, IDX_HEADS, 1)
            kis.append(ki)
            ki_lay = relayout(ki, None if past is None else cache_ki, l // 2, ("x",) if on_lanes else ("xt",))
            k_sel = min(IDX_TOPK, (q_pos0 + q_len) // 4)
            if on_lanes:
                qi_t = proj[4].reshape(bsz, nq, IDX_DIM, IDX_HEADS * qb)
                wi_t = proj[5].reshape(bsz, nq, 1, IDX_HEADS * qb)
                o = _sparse_attention_t("dsa", q_in, kr, vt, strip, (qi_t, wi_t, ki_lay["x"]), k_sel)
            else:
                qi_st = _stack_heads(qi, qb, 1)[:, :, 0]
                wi_st = _stack_heads(wi, qb, 1)[:, :, 0]
                o = _sparse_attention("dsa", q_in, kt, vr, strip, (qi_st, wi_st, ki_lay["xt"]), qb, q_pos0, k_sel)
        else:
            ksum = k_lay["sums"]
            nb = ksum.shape[1]
            kmean = (ksum * (1.0 / MOBA_BLOCK)).reshape(bsz, nb, N_KV, HEAD_DIM)
            if on_lanes:
                km = jnp.pad(kmean.transpose(0, 2, 1, 3), ((0, 0), (0, 0), (0, -nb % 8), (0, 0)))
                o = _sparse_attention_t("moba", q_in, kr, vt, strip, (km,))
            else:
                kmt = jnp.pad(kmean.transpose(0, 2, 3, 1), ((0, 0), (0, 0), (0, 0), (0, -nb % 128)))
                o = _sparse_attention("moba", q_in, kt, vr, strip, (kmt,), qb, q_pos0)
        if on_lanes:
            o = o.reshape(m, attn_dim)
        else:
            o = _unstack_heads(o, qb)
        xf = _attn_out(o, prm["w_o"][l].astype(BF16), xf, prm["ln_g"][l, 0][None], prm["ln_b"][l, 0][None],
                       alpha, tm)
        row, par, g = _peer_route(xf, prm["peer_wq"][l].T.astype(BF16),
                                 prm["peer_keys"][l].reshape(2 * PEER_HEADS, PEER_NKEYS, PEER_HALF).astype(BF16),
                                 tb_route)
        f = _peer_experts(xf, row, par, g, _pack_table(prm["peer_u"][l]), _pack_table(prm["peer_v"][l]), tb_exp)
        xf = _layer_tail(xf, f, prm["ln_g"][l, 1][None], prm["ln_b"][l, 1][None],
                         prm["gate_w"][l].astype(BF16), prm["gate_b"][l][None],
                         p[l].reshape(m, -1), prm["ple_w"][l].astype(BF16), alpha, tm)
    return xf.reshape(bsz, q_len, d), jnp.stack(ks), jnp.stack(vs), jnp.stack(kis)


def kernel(x_prompt, x_sample, cache_k, cache_v, cache_ki, page_table, p_prompt, p_sample, w_in_dsa, w_in_moba, w_o, rel_bias, ln_g, ln_b, peer_wq, peer_keys, peer_u, peer_v, gate_w, gate_b, ple_w):
    prm = {
        "w_in_dsa": w_in_dsa, "w_in_moba": w_in_moba, "w_o": w_o, "rel_bias": rel_bias,
        "ln_g": ln_g, "ln_b": ln_b, "peer_wq": peer_wq, "peer_keys": peer_keys,
        "peer_u": peer_u, "peer_v": peer_v, "gate_w": gate_w, "gate_b": gate_b, "ple_w": ple_w,
    }
    q_sample = x_sample.shape[1]
    y_prompt, k_prompt, v_prompt, ki_prompt = _run_trunk(
        x_prompt, p_prompt, None, prm, qb=128, tm=512, tb_route=256, tb_exp=128)
    y_sample, k_sample, v_sample, ki_sample = _run_trunk(
        x_sample, p_sample, (cache_k, cache_v, cache_ki, page_table), prm,
        qb=q_sample, tm=256, tb_route=256, tb_exp=64)
    return (y_prompt, y_sample, k_prompt, v_prompt, ki_prompt, k_sample, v_sample, ki_sample)
```
